```python
import math
import jax, jax.numpy as jnp
from jax import lax
import numpy as np

D_MODEL = 1024
BATCH = 8
SEQ = 2048
DEPTH = 2

MIX_WIDTH = D_MODEL
HEAD_DIM = 64
A_WIDTH = MIX_WIDTH // 2
B_WIDTH = MIX_WIDTH - A_WIDTH
A_GROUPS = A_WIDTH // HEAD_DIM
B_HEADS = B_WIDTH // HEAD_DIM
IN_COLS = 2 * A_WIDTH + 3 * B_WIDTH
CHUNK = 128
DILATED_CONFIGS = ((128, 1), (512, 4), (2048, 16))
ATTN_BLOCK = 128
REL_BUCKETS = 32
REL_MAX_EXACT = REL_BUCKETS // 2
REL_MAX_DISTANCE = 2048
N_EXPERTS = 16
N_EXPERT_GROUPS = 4
EXPERTS_PER_GROUP = N_EXPERTS // N_EXPERT_GROUPS
TOP_K = 2
D_EXPERT = D_MODEL // 2
N_MOD = 6
EPS = 1e-6
NEG_INF = -1e30

kernel_name = "hybrid_gmlp_dilated_attn_grouped_moe"


def rms_norm(x, g):
    xf = x.astype(jnp.float32)
    y = xf * lax.rsqrt(jnp.mean(xf * xf, axis=-1, keepdims=True) + EPS)
    return (y * g.astype(jnp.float32)).astype(x.dtype)


def layer_norm(x, g, b):
    xf = x.astype(jnp.float32)
    mu = jnp.mean(xf, axis=-1, keepdims=True)
    xc = xf - mu
    y = xc * lax.rsqrt(jnp.mean(xc * xc, axis=-1, keepdims=True) + EPS)
    return (y * g.astype(jnp.float32) + b.astype(jnp.float32)).astype(x.dtype)


def t5_bucket_np(dist):
    dist = np.maximum(dist, 0)
    ratio = np.log(np.maximum(dist, 1) / REL_MAX_EXACT) / np.log(REL_MAX_DISTANCE / REL_MAX_EXACT)
    large = REL_MAX_EXACT + np.floor(ratio * (REL_BUCKETS - REL_MAX_EXACT)).astype(np.int64)
    large = np.minimum(large, REL_BUCKETS - 1)
    return np.where(dist < REL_MAX_EXACT, dist, large).astype(np.int32)


def chunked_spatial_gating(u, v, ln_g, ln_b, w_s, b_s):
    B, S, _ = u.shape
    v = layer_norm(v, ln_g, ln_b)
    vc = v.reshape(B, S // CHUNK, CHUNK, A_GROUPS, HEAD_DIM)
    causal = jnp.tril(jnp.ones((CHUNK, CHUNK), dtype=w_s.dtype))
    s = jnp.einsum('gts,bnsge->bntge', w_s * causal, vc) + b_s.T[None, None, :, :, None]
    return u * s.reshape(B, S, A_WIDTH)


def dilated_window_attention(q, k, v, rel_bias, window, dilation):
    B, S, H, Dh = q.shape
    d = dilation
    L = S // d
    span = window // d
    blk = ATTN_BLOCK
    nb = -(-L // blk)
    Lp = nb * blk

    def gather(t):
        t = t.reshape(B, L, d, H, Dh).transpose(0, 2, 1, 3, 4)
        return jnp.pad(t, ((0, 0), (0, 0), (0, Lp - L), (0, 0), (0, 0)))

    def band(t):
        tb = t.reshape(B, d, nb, blk, H, Dh)
        prev = jnp.pad(tb, ((0, 0), (0, 0), (1, 0), (0, 0), (0, 0), (0, 0)))[:, :, :-1]
        return jnp.concatenate([prev, tb], axis=3)

    qb = gather(q).reshape(B, d, nb, blk, H, Dh).astype(jnp.float32)
    kb = band(gather(k)).astype(jnp.float32)
    vb = band(gather(v)).astype(jnp.float32)
    scores = jnp.einsum('brnqhe,brnkhe->brnhqk', qb, kb)

    qi = np.arange(blk)[:, None]
    kj = np.arange(2 * blk)[None, :]
    rel = qi + blk - kj
    bucket = t5_bucket_np(np.clip(rel, 0, span) * d)
    bias = jnp.transpose(rel_bias.astype(jnp.float32)[bucket], (2, 0, 1))
    key_pos = np.arange(nb)[:, None, None] * blk + kj[None] - blk
    valid = ((rel >= 0) & (rel <= span))[None] & (key_pos >= 0)
    scores = jnp.where(valid[None, None, :, None], scores + bias, NEG_INF)

    lse = jax.nn.logsumexp(scores, axis=-1)
    p = jnp.exp(scores - lse[..., None])
    o = jnp.einsum('brnhqk,brnkhe->brnqhe', p, vb)
    o = o.reshape(B, d, Lp, H, Dh)[:, :, :L].transpose(0, 2, 1, 3, 4).reshape(B, S, H, Dh)
    lse = lse.transpose(0, 1, 2, 4, 3).reshape(B, d, Lp, H)[:, :, :L]
    lse = lse.transpose(0, 2, 1, 3).reshape(B, S, H)
    return o, lse


def dilated_mixture_attention(q, k, v, rel_bias):
    B, S, _ = q.shape
    q = q.reshape(B, S, B_HEADS, HEAD_DIM) * (HEAD_DIM ** -0.5)
    k = k.reshape(B, S, B_HEADS, HEAD_DIM)
    v = v.reshape(B, S, B_HEADS, HEAD_DIM)
    outs, lses = [], []
    for window, dilation in DILATED_CONFIGS:
        o, lse = dilated_window_attention(q, k, v, rel_bias, window, dilation)
        outs.append(o)
        lses.append(lse)
    w = jax.nn.softmax(jnp.stack(lses, axis=0), axis=0)
    o = jnp.sum(w[..., None] * jnp.stack(outs, axis=0), axis=0)
    return o.reshape(B, S, B_WIDTH).astype(v.dtype)


def grouped_top2_moe(h, router_w, router_b, w_gate, w_up, w_down):
    B, S, D = h.shape
    t = h.reshape(-1, D)
    probs = jax.nn.softmax((t @ router_w).astype(jnp.float32), axis=-1)
    sel = probs + router_b.astype(jnp.float32)
    grp = sel.reshape(-1, N_EXPERT_GROUPS, EXPERTS_PER_GROUP)
    grp_score = jnp.sum(lax.top_k(grp, TOP_K)[0], axis=-1)
    best = jnp.argmax(grp_score, axis=-1)
    in_group = (jnp.arange(N_EXPERTS) // EXPERTS_PER_GROUP)[None, :] == best[:, None]
    _, idx = lax.top_k(jnp.where(in_group, sel, NEG_INF), TOP_K)
    g = jnp.take_along_axis(probs, idx, axis=-1)
    g = g / jnp.sum(g, axis=-1, keepdims=True)
    gates = jnp.sum(jax.nn.one_hot(idx, N_EXPERTS, dtype=jnp.float32) * g[..., None], axis=1)
    hg = jnp.einsum('td,edf->tef', t, w_gate)
    hu = jnp.einsum('td,edf->tef', t, w_up)
    act = jax.nn.silu(hg) * hu * gates[:, :, None].astype(hu.dtype)
    y = jnp.einsum('tef,efd->td', act, w_down)
    return y.reshape(B, S, D)


def setup_inputs(seed: int = 0) -> dict:
    key = jax.random.key(seed)
    ks = jax.random.split(key, 24)
    f32 = jnp.float32
    nrm = lambda k, shape, s: (jax.random.normal(k, shape, f32) * s)
    return {
        "x": nrm(ks[0], (BATCH, SEQ, D_MODEL), 1.0),
        "c": nrm(ks[1], (BATCH, D_MODEL), 1.0),
        "rel_bias": nrm(ks[2], (REL_BUCKETS, B_HEADS), 0.5),
        "router_w": nrm(ks[3], (D_MODEL, N_EXPERTS), D_MODEL ** -0.5),
        "router_b": nrm(ks[4], (N_EXPERTS,), 0.01),
        "mod_w": nrm(ks[5], (DEPTH, D_MODEL, N_MOD * D_MODEL), 0.5 * D_MODEL ** -0.5),
        "mod_b": nrm(ks[6], (DEPTH, N_MOD * D_MODEL), 0.02),
        "norm1_g": 1.0 + nrm(ks[7], (DEPTH, D_MODEL), 0.05),
        "w_in": nrm(ks[8], (DEPTH, D_MODEL, IN_COLS), D_MODEL ** -0.5),
        "gmlp_ln_g": 1.0 + nrm(ks[9], (DEPTH, A_WIDTH), 0.05),
        "gmlp_ln_b": nrm(ks[10], (DEPTH, A_WIDTH), 0.02),
        "gmlp_ws": nrm(ks[11], (DEPTH, A_GROUPS, CHUNK, CHUNK), CHUNK ** -0.5),
        "gmlp_bs": 1.0 + nrm(ks[12], (DEPTH, A_GROUPS, CHUNK), 0.1),
        "out_norm_a_g": 1.0 + nrm(ks[13], (DEPTH, A_WIDTH), 0.05),
        "out_norm_b_g": 1.0 + nrm(ks[14], (DEPTH, B_WIDTH), 0.05),
        "w_out": nrm(ks[15], (DEPTH, MIX_WIDTH, D_MODEL), MIX_WIDTH ** -0.5),
        "norm2_g": 1.0 + nrm(ks[16], (DEPTH, D_MODEL), 0.05),
        "moe_w_gate": nrm(ks[17], (DEPTH, N_EXPERTS, D_MODEL, D_EXPERT), D_MODEL ** -0.5),
        "moe_w_up": nrm(ks[18], (DEPTH, N_EXPERTS, D_MODEL, D_EXPERT), D_MODEL ** -0.5),
        "moe_w_down": nrm(ks[19], (DEPTH, N_EXPERTS, D_EXPERT, D_MODEL), D_EXPERT ** -0.5),
        "final_g": 1.0 + nrm(ks[20], (D_MODEL,), 0.05),
    }


def reference(x, c, rel_bias, router_w, router_b, mod_w, mod_b, norm1_g, w_in,
              gmlp_ln_g, gmlp_ln_b, gmlp_ws, gmlp_bs, out_norm_a_g, out_norm_b_g,
              w_out, norm2_g, moe_w_gate, moe_w_up, moe_w_down, final_g):
    split_at = [A_WIDTH, 2 * A_WIDTH, 2 * A_WIDTH + B_WIDTH, 2 * A_WIDTH + 2 * B_WIDTH]
    c_act = jax.nn.silu(c)
    for l in range(DEPTH):
        mod = (c_act @ mod_w[l] + mod_b[l])[:, None, :]
        sh1, sc1, g1, sh2, sc2, g2 = jnp.split(mod, N_MOD, axis=-1)

        h = rms_norm(x, norm1_g[l]) * (1.0 + sc1) + sh1
        proj = h @ w_in[l]
        u, va, q, k, vb = jnp.split(proj, split_at, axis=-1)
        out_a = chunked_spatial_gating(jax.nn.gelu(u, approximate=False),
                                       jax.nn.gelu(va, approximate=False),
                                       gmlp_ln_g[l], gmlp_ln_b[l], gmlp_ws[l], gmlp_bs[l])
        out_b = dilated_mixture_attention(q, k, vb, rel_bias)
        mixed = jnp.concatenate([rms_norm(out_a, out_norm_a_g[l]),
                                 rms_norm(out_b, out_norm_b_g[l])], axis=-1) @ w_out[l]
        x = x + g1 * mixed

        h = rms_norm(x, norm2_g[l]) * (1.0 + sc2) + sh2
        x = x + g2 * grouped_top2_moe(h, router_w, router_b,
                                      moe_w_gate[l], moe_w_up[l], moe_w_down[l])
    return rms_norm(x, final_g)
```

```python
import functools
import math

import numpy as np
import jax
import jax.numpy as jnp
from jax import lax
from jax.experimental import pallas as pl
from jax.experimental.pallas import tpu as pltpu

D_MODEL = 1024
BATCH = 8
SEQ = 2048
DEPTH = 2
TOKENS = BATCH * SEQ
HEAD_DIM = 64
A_WIDTH = 512
B_WIDTH = 512
A_GROUPS = 8
IN_COLS = 2 * A_WIDTH + 3 * B_WIDTH
CHUNK = 128
DILATED_CONFIGS = ((128, 1), (512, 4), (2048, 16))
ATTN_BLOCK = 128
REL_BUCKETS = 32
REL_MAX_EXACT = REL_BUCKETS // 2
REL_MAX_DISTANCE = 2048
N_EXPERTS = 16
N_EXPERT_GROUPS = 4
EXPERTS_PER_GROUP = 4
D_EXPERT = 512
N_MOD = 6
EPS = 1e-6
NEG_INF = -1e30

LANES = 128
HEAD_PAIR = 2 * HEAD_DIM
N_PAIRS = B_WIDTH // HEAD_PAIR

F32 = jnp.float32
BF16 = jnp.bfloat16

VMEM_LIMIT = 56 * 1024 * 1024


def _cparams(sem):
    return pltpu.CompilerParams(dimension_semantics=sem, vmem_limit_bytes=VMEM_LIMIT)


def _gelu(x):
    return 0.5 * x * (1.0 + lax.erf(x * math.sqrt(0.5)))


MOD_TN = 1024


def _mod_kernel(c_ref, w_ref, b_ref, o_ref):
    ca = jax.nn.silu(c_ref[...])
    o_ref[...] = jnp.dot(ca.astype(BF16), w_ref[...].astype(BF16),
                         preferred_element_type=F32) + b_ref[...]


def _modulation(c, mod_w, mod_b):
    n_cols = N_MOD * D_MODEL
    return pl.pallas_call(
        _mod_kernel,
        grid=(DEPTH, n_cols // MOD_TN),
        in_specs=[
            pl.BlockSpec((BATCH, D_MODEL), lambda l, j: (0, 0)),
            pl.BlockSpec((None, D_MODEL, MOD_TN), lambda l, j: (l, 0, j)),
            pl.BlockSpec((None, 1, MOD_TN), lambda l, j: (l, 0, j)),
        ],
        out_specs=pl.BlockSpec((None, BATCH, MOD_TN), lambda l, j: (l, 0, j)),
        out_shape=jax.ShapeDtypeStruct((DEPTH, BATCH, n_cols), F32),
        compiler_params=_cparams(("arbitrary", "arbitrary")),
        name="modulation",
    )(c, mod_w, mod_b.reshape(DEPTH, 1, n_cols))


K1_TM = 512


def _front_kernel(*refs, has_moe):
    if has_moe:
        (x_ref, y_ref, modp_ref, mod_ref, n1g_ref, win_ref, lng_ref, lnb_ref, ws_ref,
         bs_ref, ga_ref, xo_ref, a_ref, q_ref, k_ref, v_ref) = refs
        x = x_ref[...] + modp_ref[5:6, :] * y_ref[...]
        xo_ref[...] = x
    else:
        (x_ref, mod_ref, n1g_ref, win_ref, lng_ref, lnb_ref, ws_ref,
         bs_ref, ga_ref, a_ref, q_ref, k_ref, v_ref) = refs
        x = x_ref[...]
    tm = x.shape[0]
    h = x * lax.rsqrt(jnp.mean(x * x, axis=-1, keepdims=True) + EPS) * n1g_ref[...]
    h = h * (1.0 + mod_ref[1:2, :]) + mod_ref[0:1, :]
    proj = jnp.dot(h.astype(BF16), win_ref[...], preferred_element_type=F32)

    q_ref[...] = proj[:, 2 * A_WIDTH:2 * A_WIDTH + B_WIDTH] * (HEAD_DIM ** -0.5)
    k_ref[...] = proj[:, 2 * A_WIDTH + B_WIDTH:2 * A_WIDTH + 2 * B_WIDTH]
    v_ref[...] = proj[:, 2 * A_WIDTH + 2 * B_WIDTH:]

    u = _gelu(proj[:, :A_WIDTH])
    va = _gelu(proj[:, A_WIDTH:2 * A_WIDTH])
    mu = jnp.mean(va, axis=-1, keepdims=True)
    vc = va - mu
    vln = vc * lax.rsqrt(jnp.mean(vc * vc, axis=-1, keepdims=True) + EPS)
    vln = (vln * lng_ref[...] + lnb_ref[...]).astype(BF16)

    row = lax.broadcasted_iota(jnp.int32, (2 * CHUNK, CHUNK), 0)
    col = lax.broadcasted_iota(jnp.int32, (2 * CHUNK, CHUNK), 1)
    causal = (row % CHUNK) >= col
    first_group = lax.broadcasted_iota(jnp.int32, (CHUNK, LANES), 1) < HEAD_DIM
    wmix = [jnp.where(causal, ws_ref[p], jnp.zeros((), BF16)) for p in range(A_GROUPS // 2)]
    for c in range(tm // CHUNK):
        rows = slice(c * CHUNK, (c + 1) * CHUNK)
        parts = []
        for p in range(A_GROUPS // 2):
            vp = vln[rows, p * LANES:(p + 1) * LANES]
            r = jnp.dot(wmix[p], vp, preferred_element_type=F32)
            parts.append(jnp.where(first_group, r[:CHUNK], r[CHUNK:]))
        s = jnp.concatenate(parts, axis=-1) + bs_ref[...]
        oa = u[rows] * s
        oa = oa * lax.rsqrt(jnp.mean(oa * oa, axis=-1, keepdims=True) + EPS) * ga_ref[...]
        a_ref[rows, :] = oa.astype(BF16)


def _front(x, y_prev, mod_prev, mod_l, n1g, win, lng, lnb, ws2, bs_exp, ga):
    tm = K1_TM
    has_moe = y_prev is not None
    tiles_per_seq = SEQ // tm
    tok = pl.BlockSpec((tm, D_MODEL), lambda i: (i, 0))
    modspec = pl.BlockSpec((None, N_MOD, D_MODEL), lambda i: (i // tiles_per_seq, 0, 0))
    row1024 = pl.BlockSpec((1, D_MODEL), lambda i: (0, 0))
    row512 = pl.BlockSpec((1, A_WIDTH), lambda i: (0, 0))
    half = pl.BlockSpec((tm, A_WIDTH), lambda i: (i, 0))
    in_specs = [tok]
    args = [x]
    if has_moe:
        in_specs += [tok, modspec]
        args += [y_prev, mod_prev]
    in_specs += [
        modspec, row1024,
        pl.BlockSpec((D_MODEL, IN_COLS), lambda i: (0, 0)),
        row512, row512,
        pl.BlockSpec((A_GROUPS // 2, 2 * CHUNK, CHUNK), lambda i: (0, 0, 0)),
        pl.BlockSpec((CHUNK, A_WIDTH), lambda i: (0, 0)),
        row512,
    ]
    args += [mod_l, n1g, win, lng, lnb, ws2, bs_exp, ga]
    out_specs = [half, half, half, half]
    out_shape = [jax.ShapeDtypeStruct((TOKENS, A_WIDTH), BF16)] + \
                [jax.ShapeDtypeStruct((TOKENS, B_WIDTH), F32)] * 3
    if has_moe:
        out_specs = [tok] + out_specs
        out_shape = [jax.ShapeDtypeStruct((TOKENS, D_MODEL), F32)] + out_shape
    outs = pl.pallas_call(
        functools.partial(_front_kernel, has_moe=has_moe),
        grid=(TOKENS // tm,),
        in_specs=in_specs,
        out_specs=out_specs,
        out_shape=out_shape,
        compiler_params=_cparams(("arbitrary",)),
        name="front",
    )(*args)
    if has_moe:
        return outs
    return (x, *outs)


def _t5_bucket_np(dist):
    dist = np.maximum(dist, 0)
    ratio = np.log(np.maximum(dist, 1) / REL_MAX_EXACT) / np.log(REL_MAX_DISTANCE / REL_MAX_EXACT)
    large = REL_MAX_EXACT + np.floor(ratio * (REL_BUCKETS - REL_MAX_EXACT)).astype(np.int64)
    large = np.minimum(large, REL_BUCKETS - 1)
    return np.where(dist < REL_MAX_EXACT, dist, large).astype(np.int32)


def _bias_tables(rel_bias):
    blk = ATTN_BLOCK
    qi = np.arange(blk)[:, None]
    kj = np.arange(2 * blk)[None, :]
    rel = qi + blk - kj
    tables = []
    for window, d in DILATED_CONFIGS:
        span = window // d
        bucket = _t5_bucket_np(np.clip(rel, 0, span) * d)
        bias = jnp.transpose(rel_bias.astype(F32)[bucket], (2, 0, 1))
        valid = (rel >= 0) & (rel <= span)
        valid_first = valid & (kj >= blk)
        tables.append(jnp.stack([jnp.where(valid[None], bias, NEG_INF),
                                 jnp.where(valid_first[None], bias, NEG_INF)], axis=0))
    return jnp.stack(tables, axis=0)


def _attn_kernel(q_ref, k_ref, v_ref, bias_ref, o_ref, m_sc, l_sc, acc_sc):
    blk = ATTN_BLOCK
    lane = lax.broadcasted_iota(jnp.int32, (blk, LANES), 1)
    head0 = lane < HEAD_DIM

    def merge_heads(t):
        return jnp.where(head0, t[:blk], t[blk:])

    for ci, (window, d) in enumerate(DILATED_CONFIGS):
        seq_d = SEQ // d
        nb = seq_d // blk

        def block(j, carry, ci=ci, d=d, nb=nb):
            r = j // nb
            n = j % nb
            rows = pl.ds(r + n * (blk * d), blk, stride=d) if d > 1 else \
                pl.ds(pl.multiple_of(n * blk, blk), blk)
            q = q_ref[rows, :]
            kc = k_ref[rows, :]
            vc = v_ref[rows, :]
            if nb > 1:
                pn = jnp.maximum(n - 1, 0)
                prow = pl.ds(r + pn * (blk * d), blk, stride=d) if d > 1 else \
                    pl.ds(pl.multiple_of(pn * blk, blk), blk)
                kcat = jnp.concatenate([k_ref[prow, :], kc], axis=0).astype(BF16)
                vcat = jnp.concatenate([v_ref[prow, :], vc], axis=0).astype(BF16)
                first = (n == 0).astype(jnp.int32)
                bias = jnp.concatenate([bias_ref[ci, first, 0], bias_ref[ci, first, 1]], axis=0)
            else:
                kcat = kc.astype(BF16)
                vcat = vc.astype(BF16)
                bias = jnp.concatenate([bias_ref[ci, 0, 0, :, blk:], bias_ref[ci, 0, 1, :, blk:]],
                                       axis=0)
            zero = jnp.zeros_like(q)
            qs = jnp.concatenate([jnp.where(head0, q, zero), jnp.where(head0, zero, q)],
                                 axis=0).astype(BF16)
            s = lax.dot_general(qs, kcat, (((1,), (1,)), ((), ())),
                                preferred_element_type=F32) + bias
            m = jnp.max(s, axis=-1, keepdims=True)
            p = jnp.exp(s - m)
            l = jnp.sum(p, axis=-1, keepdims=True)
            pv = jnp.dot(p.astype(BF16), vcat, preferred_element_type=F32)
            o_c = merge_heads(pv)
            m_c = merge_heads(jnp.broadcast_to(m, (2 * blk, LANES)))
            l_c = merge_heads(jnp.broadcast_to(l, (2 * blk, LANES)))
            if ci == 0:
                m_sc[rows, :] = m_c
                l_sc[rows, :] = l_c
                acc_sc[rows, :] = o_c
            else:
                m_r = m_sc[rows, :]
                m_n = jnp.maximum(m_r, m_c)
                a = jnp.exp(m_r - m_n)
                b = jnp.exp(m_c - m_n)
                m_sc[rows, :] = m_n
                l_sc[rows, :] = a * l_sc[rows, :] + b * l_c
                acc_sc[rows, :] = a * acc_sc[rows, :] + b * o_c
            return carry

        lax.fori_loop(0, d * nb, block, 0)

    o_ref[...] = acc_sc[...] / l_sc[...]


def _attention(q, k, v, bias_tab):
    n_cfg = len(DILATED_CONFIGS)
    blk = ATTN_BLOCK
    seq_spec = pl.BlockSpec((SEQ, HEAD_PAIR), lambda b, p: (b, p))
    return pl.pallas_call(
        _attn_kernel,
        grid=(BATCH, N_PAIRS),
        in_specs=[seq_spec, seq_spec, seq_spec,
                  pl.BlockSpec((n_cfg, 2, 2, blk, 2 * blk), lambda b, p: (0, 0, p, 0, 0))],
        out_specs=seq_spec,
        out_shape=jax.ShapeDtypeStruct((TOKENS, B_WIDTH), F32),
        scratch_shapes=[pltpu.VMEM((SEQ, HEAD_PAIR), F32)] * 3,
        compiler_params=_cparams(("arbitrary", "arbitrary")),
        name="attention",
    )(q, k, v, bias_tab)


K4_TM = 512


def _top2_sum(a, b, c, d):
    hi1, lo1 = jnp.maximum(a, b), jnp.minimum(a, b)
    hi2, lo2 = jnp.maximum(c, d), jnp.minimum(c, d)
    return jnp.maximum(hi1, hi2) + jnp.maximum(jnp.minimum(hi1, hi2), jnp.maximum(lo1, lo2))


def _route(logits_t, rb_col):
    m = jnp.max(logits_t, axis=0, keepdims=True)
    e = jnp.exp(logits_t - m)
    probs = e / jnp.sum(e, axis=0, keepdims=True)
    sel = probs + rb_col
    sel_rows = [sel[i:i + 1, :] for i in range(N_EXPERTS)]
    prob_rows = [probs[i:i + 1, :] for i in range(N_EXPERTS)]
    gsz = EXPERTS_PER_GROUP
    score = [_top2_sum(*sel_rows[g * gsz:(g + 1) * gsz]) for g in range(N_EXPERT_GROUPS)]
    chosen = []
    for g in range(N_EXPERT_GROUPS):
        best = None
        for g2 in range(N_EXPERT_GROUPS):
            if g2 == g:
                continue
            c = (score[g] > score[g2]) if g2 < g else (score[g] >= score[g2])
            best = c if best is None else jnp.logical_and(best, c)
        for i in range(gsz):
            ei = g * gsz + i
            rank = jnp.zeros_like(sel_rows[ei])
            for j in range(gsz):
                if j == i:
                    continue
                ej = g * gsz + j
                ahead = (sel_rows[ej] >= sel_rows[ei]) if j < i else (sel_rows[ej] > sel_rows[ei])
                rank = rank + jnp.where(ahead, 1.0, 0.0)
            chosen.append(jnp.logical_and(best, rank < float(2)))
    picked = [jnp.where(chosen[i], prob_rows[i], 0.0) for i in range(N_EXPERTS)]
    denom = picked[0]
    for i in range(1, N_EXPERTS):
        denom = denom + picked[i]
    return jnp.concatenate([pk / denom for pk in picked], axis=0)


def _mid_kernel(a_ref, ob_ref, x_ref, mod_ref, gb_ref, wout_ref, n2g_ref, rwt_ref, rb_ref,
                x1_ref, h2_ref, gates_ref):
    ob = ob_ref[...]
    bn = ob * lax.rsqrt(jnp.mean(ob * ob, axis=-1, keepdims=True) + EPS) * gb_ref[...]
    mixed = jnp.dot(a_ref[...], wout_ref[:A_WIDTH, :], preferred_element_type=F32)
    mixed = mixed + jnp.dot(bn.astype(BF16), wout_ref[A_WIDTH:, :], preferred_element_type=F32)
    x1 = x_ref[...] + mod_ref[2:3, :] * mixed
    x1_ref[...] = x1
    h2 = x1 * lax.rsqrt(jnp.mean(x1 * x1, axis=-1, keepdims=True) + EPS) * n2g_ref[...]
    h2 = (h2 * (1.0 + mod_ref[4:5, :]) + mod_ref[3:4, :]).astype(BF16)
    h2_ref[...] = h2
    logits_t = lax.dot_general(rwt_ref[...], h2, (((1,), (1,)), ((), ())),
                               preferred_element_type=F32)
    gates_t = _route(logits_t, rb_ref[...])
    tm = gates_t.shape[1]
    padded = jnp.concatenate([gates_t, jnp.zeros((LANES - N_EXPERTS, tm), F32)], axis=0)
    gates_ref[...] = padded.T


def _mid(out_a, out_b, x, mod_l, gb, wout, n2g, rwt, rb_col):
    tm = K4_TM
    tiles_per_seq = SEQ // tm
    tok = pl.BlockSpec((tm, D_MODEL), lambda i: (i, 0))
    half = pl.BlockSpec((tm, A_WIDTH), lambda i: (i, 0))
    return pl.pallas_call(
        _mid_kernel,
        grid=(TOKENS // tm,),
        in_specs=[
            half, half, tok,
            pl.BlockSpec((None, N_MOD, D_MODEL), lambda i: (i // tiles_per_seq, 0, 0)),
            pl.BlockSpec((1, B_WIDTH), lambda i: (0, 0)),
            pl.BlockSpec((D_MODEL, D_MODEL), lambda i: (0, 0)),
            pl.BlockSpec((1, D_MODEL), lambda i: (0, 0)),
            pl.BlockSpec((N_EXPERTS, D_MODEL), lambda i: (0, 0)),
            pl.BlockSpec((N_EXPERTS, 1), lambda i: (0, 0)),
        ],
        out_specs=[tok, tok, pl.BlockSpec((tm, LANES), lambda i: (i, 0))],
        out_shape=[jax.ShapeDtypeStruct((TOKENS, D_MODEL), F32),
                   jax.ShapeDtypeStruct((TOKENS, D_MODEL), BF16),
                   jax.ShapeDtypeStruct((TOKENS, LANES), F32)],
        compiler_params=_cparams(("arbitrary",)),
        name="mid",
    )(out_a, out_b, x, mod_l, gb, wout, n2g, rwt, rb_col)


MOE_TM = 512


def _moe_kernel(h_ref, gates_ref, wg_ref, wu_ref, wd_ref, y_ref):
    e = pl.program_id(1)
    h = h_ref[...]
    lane = lax.broadcasted_iota(jnp.int32, gates_ref.shape, 1)
    gate = jnp.sum(jnp.where(lane == e, gates_ref[...], 0.0), axis=-1, keepdims=True)
    hg = jnp.dot(h, wg_ref[...], preferred_element_type=F32)
    hu = jnp.dot(h, wu_ref[...], preferred_element_type=F32)
    act = (jax.nn.silu(hg) * hu * gate).astype(BF16)
    y = jnp.dot(act, wd_ref[...], preferred_element_type=F32)

    @pl.when(e == 0)
    def _():
        y_ref[...] = y

    @pl.when(e > 0)
    def _():
        y_ref[...] += y


def _moe(h2, gates, wg, wu, wd):
    tm = MOE_TM
    return pl.pallas_call(
        _moe_kernel,
        grid=(TOKENS // tm, N_EXPERTS),
        in_specs=[
            pl.BlockSpec((tm, D_MODEL), lambda i, e: (i, 0)),
            pl.BlockSpec((tm, LANES), lambda i, e: (i, 0)),
            pl.BlockSpec((None, D_MODEL, D_EXPERT), lambda i, e: (e, 0, 0)),
            pl.BlockSpec((None, D_MODEL, D_EXPERT), lambda i, e: (e, 0, 0)),
            pl.BlockSpec((None, D_EXPERT, D_MODEL), lambda i, e: (e, 0, 0)),
        ],
        out_specs=pl.BlockSpec((tm, D_MODEL), lambda i, e: (i, 0)),
        out_shape=jax.ShapeDtypeStruct((TOKENS, D_MODEL), F32),
        compiler_params=_cparams(("arbitrary", "arbitrary")),
        name="moe",
    )(h2, gates, wg, wu, wd)


FIN_TM = 512


def _final_kernel(x_ref, y_ref, mod_ref, g_ref, o_ref):
    x = x_ref[...] + mod_ref[5:6, :] * y_ref[...]
    o_ref[...] = x * lax.rsqrt(jnp.mean(x * x, axis=-1, keepdims=True) + EPS) * g_ref[...]


def _final(x, y, mod_l, g):
    tm = FIN_TM
    tiles_per_seq = SEQ // tm
    tok = pl.BlockSpec((tm, D_MODEL), lambda i: (i, 0))
    return pl.pallas_call(
        _final_kernel,
        grid=(TOKENS // tm,),
        in_specs=[tok, tok,
                  pl.BlockSpec((None, N_MOD, D_MODEL), lambda i: (i // tiles_per_seq, 0, 0)),
                  pl.BlockSpec((1, D_MODEL), lambda i: (0, 0))],
        out_specs=tok,
        out_shape=jax.ShapeDtypeStruct((TOKENS, D_MODEL), F32),
        compiler_params=_cparams(("arbitrary",)),
        name="final",
    )(x, y, mod_l, g)


def kernel(x, c, rel_bias, router_w, router_b, mod_w, mod_b, norm1_g, w_in, gmlp_ln_g, gmlp_ln_b,
           gmlp_ws, gmlp_bs, out_norm_a_g, out_norm_b_g, w_out, norm2_g, moe_w_gate, moe_w_up,
           moe_w_down, final_g):
    mod = _modulation(c, mod_w, mod_b).reshape(DEPTH, BATCH, N_MOD, D_MODEL)
    bias_tab = _bias_tables(rel_bias)
    rwt = router_w.T.astype(BF16)
    rb_col = router_b.reshape(N_EXPERTS, 1)
    xt = x.reshape(TOKENS, D_MODEL)
    y = None
    for l in range(DEPTH):
        ws2 = gmlp_ws[l].astype(BF16).reshape(A_GROUPS // 2, 2 * CHUNK, CHUNK)
        bs_exp = jnp.repeat(gmlp_bs[l].T, HEAD_DIM, axis=1)
        xt, out_a, q, k, v = _front(
            xt, y, mod[l - 1] if l else None, mod[l], norm1_g[l].reshape(1, -1),
            w_in[l].astype(BF16), gmlp_ln_g[l].reshape(1, -1), gmlp_ln_b[l].reshape(1, -1),
            ws2, bs_exp, out_norm_a_g[l].reshape(1, -1))
        out_b = _attention(q, k, v, bias_tab)
        xt, h2, gates = _mid(out_a, out_b, xt, mod[l], out_norm_b_g[l].reshape(1, -1),
                             w_out[l].astype(BF16), norm2_g[l].reshape(1, -1), rwt, rb_col)
        y = _moe(h2, gates, moe_w_gate[l].astype(BF16), moe_w_up[l].astype(BF16),
                 moe_w_down[l].astype(BF16))
    out = _final(xt, y, mod[DEPTH - 1], final_g.reshape(1, -1))
    return out.reshape(BATCH, SEQ, D_MODEL)
```

```python
import functools
import math

import numpy as np
import jax
import jax.numpy as jnp
from jax import lax
from jax.experimental import pallas as pl
from jax.experimental.pallas import tpu as pltpu

D_MODEL = 1024
BATCH = 8
SEQ = 2048
DEPTH = 2
TOKENS = BATCH * SEQ
HEAD_DIM = 64
A_WIDTH = 512
B_WIDTH = 512
A_GROUPS = 8
IN_COLS = 2 * A_WIDTH + 3 * B_WIDTH
CHUNK = 128
DILATED_CONFIGS = ((128, 1), (512, 4), (2048, 16))
ATTN_BLOCK = 128
REL_BUCKETS = 32
REL_MAX_EXACT = REL_BUCKETS // 2
REL_MAX_DISTANCE = 2048
N_EXPERTS = 16
N_EXPERT_GROUPS = 4
EXPERTS_PER_GROUP = 4
D_EXPERT = 512
N_MOD = 6
EPS = 1e-6
NEG_INF = -1e30

LANES = 128
HEAD_PAIR = 2 * HEAD_DIM
N_PAIRS = B_WIDTH // HEAD_PAIR

F32 = jnp.float32
BF16 = jnp.bfloat16

VMEM_LIMIT = 56 * 1024 * 1024


def _cparams(sem):
    return pltpu.CompilerParams(dimension_semantics=sem, vmem_limit_bytes=VMEM_LIMIT)


def _gelu(x):
    return 0.5 * x * (1.0 + lax.erf(x * math.sqrt(0.5)))


MOD_TN = 1024


def _mod_kernel(c_ref, w_ref, b_ref, o_ref):
    ca = jax.nn.silu(c_ref[...])
    o_ref[...] = jnp.dot(ca.astype(BF16), w_ref[...].astype(BF16),
                         preferred_element_type=F32) + b_ref[...]


def _modulation(c, mod_w, mod_b):
    n_cols = N_MOD * D_MODEL
    return pl.pallas_call(
        _mod_kernel,
        grid=(DEPTH, n_cols // MOD_TN),
        in_specs=[
            pl.BlockSpec((BATCH, D_MODEL), lambda l, j: (0, 0)),
            pl.BlockSpec((None, D_MODEL, MOD_TN), lambda l, j: (l, 0, j)),
            pl.BlockSpec((None, 1, MOD_TN), lambda l, j: (l, 0, j)),
        ],
        out_specs=pl.BlockSpec((None, BATCH, MOD_TN), lambda l, j: (l, 0, j)),
        out_shape=jax.ShapeDtypeStruct((DEPTH, BATCH, n_cols), F32),
        compiler_params=_cparams(("arbitrary", "arbitrary")),
        name="modulation",
    )(c, mod_w, mod_b.reshape(DEPTH, 1, n_cols))


K1_TM = 512


def _front_kernel(*refs, has_moe):
    if has_moe:
        (x_ref, y_ref, modp_ref, mod_ref, n1g_ref, win_ref, lng_ref, lnb_ref, ws_ref,
         bs_ref, ga_ref, xo_ref, a_ref, q_ref, k_ref, v_ref) = refs
        x = x_ref[...] + modp_ref[5:6, :] * y_ref[...]
        xo_ref[...] = x
    else:
        (x_ref, mod_ref, n1g_ref, win_ref, lng_ref, lnb_ref, ws_ref,
         bs_ref, ga_ref, a_ref, q_ref, k_ref, v_ref) = refs
        x = x_ref[...]
    tm = x.shape[0]
    h = x * lax.rsqrt(jnp.mean(x * x, axis=-1, keepdims=True) + EPS) * n1g_ref[...]
    h = h * (1.0 + mod_ref[1:2, :]) + mod_ref[0:1, :]
    proj = jnp.dot(h.astype(BF16), win_ref[...], preferred_element_type=F32)

    q_ref[...] = proj[:, 2 * A_WIDTH:2 * A_WIDTH + B_WIDTH] * (HEAD_DIM ** -0.5)
    k_ref[...] = proj[:, 2 * A_WIDTH + B_WIDTH:2 * A_WIDTH + 2 * B_WIDTH]
    v_ref[...] = proj[:, 2 * A_WIDTH + 2 * B_WIDTH:]

    u = _gelu(proj[:, :A_WIDTH])
    va = _gelu(proj[:, A_WIDTH:2 * A_WIDTH])
    mu = jnp.mean(va, axis=-1, keepdims=True)
    vc = va - mu
    vln = vc * lax.rsqrt(jnp.mean(vc * vc, axis=-1, keepdims=True) + EPS)
    vln = (vln * lng_ref[...] + lnb_ref[...]).astype(BF16)

    row = lax.broadcasted_iota(jnp.int32, (2 * CHUNK, CHUNK), 0)
    col = lax.broadcasted_iota(jnp.int32, (2 * CHUNK, CHUNK), 1)
    causal = (row % CHUNK) >= col
    first_group = lax.broadcasted_iota(jnp.int32, (CHUNK, LANES), 1) < HEAD_DIM
    wmix = [jnp.where(causal, ws_ref[p], jnp.zeros((), BF16)) for p in range(A_GROUPS // 2)]
    for c in range(tm // CHUNK):
        rows = slice(c * CHUNK, (c + 1) * CHUNK)
        parts = []
        for p in range(A_GROUPS // 2):
            vp = vln[rows, p * LANES:(p + 1) * LANES]
            r = jnp.dot(wmix[p], vp, preferred_element_type=F32)
            parts.append(jnp.where(first_group, r[:CHUNK], r[CHUNK:]))
        s = jnp.concatenate(parts, axis=-1) + bs_ref[...]
        oa = u[rows] * s
        oa = oa * lax.rsqrt(jnp.mean(oa * oa, axis=-1, keepdims=True) + EPS) * ga_ref[...]
        a_ref[rows, :] = oa.astype(BF16)


def _front(x, y_prev, mod_prev, mod_l, n1g, win, lng, lnb, ws2, bs_exp, ga):
    tm = K1_TM
    has_moe = y_prev is not None
    tiles_per_seq = SEQ // tm
    tok = pl.BlockSpec((tm, D_MODEL), lambda i: (i, 0))
    modspec = pl.BlockSpec((None, N_MOD, D_MODEL), lambda i: (i // tiles_per_seq, 0, 0))
    row1024 = pl.BlockSpec((1, D_MODEL), lambda i: (0, 0))
    row512 = pl.BlockSpec((1, A_WIDTH), lambda i: (0, 0))
    half = pl.BlockSpec((tm, A_WIDTH), lambda i: (i, 0))
    in_specs = [tok]
    args = [x]
    if has_moe:
        in_specs += [tok, modspec]
        args += [y_prev, mod_prev]
    in_specs += [
        modspec, row1024,
        pl.BlockSpec((D_MODEL, IN_COLS), lambda i: (0, 0)),
        row512, row512,
        pl.BlockSpec((A_GROUPS // 2, 2 * CHUNK, CHUNK), lambda i: (0, 0, 0)),
        pl.BlockSpec((CHUNK, A_WIDTH), lambda i: (0, 0)),
        row512,
    ]
    args += [mod_l, n1g, win, lng, lnb, ws2, bs_exp, ga]
    out_specs = [half, half, half, half]
    out_shape = [jax.ShapeDtypeStruct((TOKENS, A_WIDTH), BF16)] + \
                [jax.ShapeDtypeStruct((TOKENS, B_WIDTH), F32)] * 3
    if has_moe:
        out_specs = [tok] + out_specs
        out_shape = [jax.ShapeDtypeStruct((TOKENS, D_MODEL), F32)] + out_shape
    outs = pl.pallas_call(
        functools.partial(_front_kernel, has_moe=has_moe),
        grid=(TOKENS // tm,),
        in_specs=in_specs,
        out_specs=out_specs,
        out_shape=out_shape,
        compiler_params=_cparams(("arbitrary",)),
        name="front",
    )(*args)
    if has_moe:
        return outs
    return (x, *outs)


def _t5_bucket_np(dist):
    dist = np.maximum(dist, 0)
    ratio = np.log(np.maximum(dist, 1) / REL_MAX_EXACT) / np.log(REL_MAX_DISTANCE / REL_MAX_EXACT)
    large = REL_MAX_EXACT + np.floor(ratio * (REL_BUCKETS - REL_MAX_EXACT)).astype(np.int64)
    large = np.minimum(large, REL_BUCKETS - 1)
    return np.where(dist < REL_MAX_EXACT, dist, large).astype(np.int32)


def _bias_tables(rel_bias):
    blk = ATTN_BLOCK
    n_rel = 3 * blk
    rel = 2 * blk - 1 - np.arange(n_rel)
    first = jnp.asarray(np.arange(2 * blk)[None, :] >= blk)
    tables = []
    for window, d in DILATED_CONFIGS:
        span = window // d
        bucket = _t5_bucket_np(np.clip(rel, 0, span) * d)
        valid = jnp.asarray((rel >= 0) & (rel <= span))
        w = jnp.where(valid[None, :], rel_bias.astype(F32)[bucket].T, NEG_INF)
        flat = jnp.tile(w, (1, blk))
        skew = flat[:, blk - 1:blk - 1 + blk * (n_rel - 1)].reshape(-1, blk, n_rel - 1)
        tab = skew[:, :, :2 * blk]
        tables.append(jnp.stack([tab, jnp.where(first[None], tab, NEG_INF)], axis=0))
    return jnp.stack(tables, axis=0)


ATTN_GROUP = 4


def _attn_kernel(q_ref, k_ref, v_ref, bias_ref, o_ref, m_sc, l_sc, acc_sc):
    blk = ATTN_BLOCK
    grp = ATTN_GROUP
    lane = lax.broadcasted_iota(jnp.int32, (blk, LANES), 1)
    head0 = lane < HEAD_DIM
    ones = jnp.ones((2 * blk, LANES), BF16)

    def merge_heads(t):
        return jnp.where(head0, t[:blk], t[blk:])

    def scores_block(q, kcat, vcat, bias):
        nk = kcat.shape[0]
        zero = jnp.zeros_like(q)
        qs = jnp.concatenate([jnp.where(head0, q, zero), jnp.where(head0, zero, q)],
                             axis=0).astype(BF16)
        s = lax.dot_general(qs, kcat, (((1,), (1,)), ((), ())), preferred_element_type=F32) + bias
        m = jnp.max(s, axis=-1, keepdims=True)
        p = jnp.exp(s - m).astype(BF16)
        pv = jnp.dot(p, jnp.concatenate([vcat, ones[:nk]], axis=1), preferred_element_type=F32)
        return (merge_heads(jnp.broadcast_to(m, (2 * blk, LANES))),
                merge_heads(pv[:, LANES:]), merge_heads(pv[:, :LANES]))

    def update(ci, rows_list, results):
        if ci == 0:
            for rows, (m_c, l_c, o_c) in zip(rows_list, results):
                m_sc[rows, :] = m_c
                l_sc[rows, :] = l_c
                acc_sc[rows, :] = o_c
            return
        olds = [(m_sc[rows, :], l_sc[rows, :], acc_sc[rows, :]) for rows in rows_list]
        news = []
        for (m_r, l_r, a_r), (m_c, l_c, o_c) in zip(olds, results):
            m_n = jnp.maximum(m_r, m_c)
            a = jnp.exp(m_r - m_n)
            b = jnp.exp(m_c - m_n)
            news.append((m_n, a * l_r + b * l_c, a * a_r + b * o_c))
        for rows, (m_n, l_n, a_n) in zip(rows_list, news):
            m_sc[rows, :] = m_n
            l_sc[rows, :] = l_n
            acc_sc[rows, :] = a_n

    def head_bias(ci, first, nk):
        if nk == 2 * blk:
            return jnp.concatenate([bias_ref[ci, first, 0], bias_ref[ci, first, 1]], axis=0)
        return jnp.concatenate([bias_ref[ci, 0, 0, :, blk:], bias_ref[ci, 0, 1, :, blk:]], axis=0)

    for ci, (window, d) in enumerate(DILATED_CONFIGS):
        nb = SEQ // d // blk

        def rows_of(r, n, d=d):
            if d == 1:
                return pl.ds(pl.multiple_of(n * blk, blk), blk)
            return pl.ds(r + n * (blk * d), blk, stride=d)

        if nb >= grp:
            def trip(j, carry, ci=ci, nb=nb, rows_of=rows_of):
                r = j // (nb // grp)
                n0 = (j % (nb // grp)) * grp
                k_prev = v_prev = None
                if nb > grp:
                    prev_rows = rows_of(r, jnp.maximum(n0 - 1, 0))
                    k_prev = k_ref[prev_rows, :].astype(BF16)
                    v_prev = v_ref[prev_rows, :].astype(BF16)
                first = (n0 == 0).astype(jnp.int32)
                rows_list, results = [], []
                for g in range(grp):
                    rows = rows_of(r, n0 + g)
                    k_cur = k_ref[rows, :].astype(BF16)
                    v_cur = v_ref[rows, :].astype(BF16)
                    if k_prev is None:
                        results.append(scores_block(q_ref[rows, :], k_cur, v_cur,
                                                    head_bias(ci, 0, blk)))
                    else:
                        results.append(scores_block(
                            q_ref[rows, :], jnp.concatenate([k_prev, k_cur], axis=0),
                            jnp.concatenate([v_prev, v_cur], axis=0),
                            head_bias(ci, first if g == 0 else 0, 2 * blk)))
                    rows_list.append(rows)
                    k_prev, v_prev = k_cur, v_cur
                update(ci, rows_list, results)
                return carry

            lax.fori_loop(0, d * nb // grp, trip, 0)
        else:
            assert nb == 1

            def trip(j, carry, ci=ci, rows_of=rows_of):
                rows_list, results = [], []
                for g in range(grp):
                    rows = rows_of(j * grp + g, 0)
                    results.append(scores_block(q_ref[rows, :], k_ref[rows, :].astype(BF16),
                                                v_ref[rows, :].astype(BF16), head_bias(ci, 0, blk)))
                    rows_list.append(rows)
                update(ci, rows_list, results)
                return carry

            lax.fori_loop(0, d // grp, trip, 0)

    o_ref[...] = acc_sc[...] / l_sc[...]


def _attention(q, k, v, bias_tab):
    n_cfg = len(DILATED_CONFIGS)
    blk = ATTN_BLOCK
    seq_spec = pl.BlockSpec((SEQ, HEAD_PAIR), lambda b, p: (b, p))
    return pl.pallas_call(
        _attn_kernel,
        grid=(BATCH, N_PAIRS),
        in_specs=[seq_spec, seq_spec, seq_spec,
                  pl.BlockSpec((n_cfg, 2, 2, blk, 2 * blk), lambda b, p: (0, 0, p, 0, 0))],
        out_specs=seq_spec,
        out_shape=jax.ShapeDtypeStruct((TOKENS, B_WIDTH), F32),
        scratch_shapes=[pltpu.VMEM((SEQ, HEAD_PAIR), F32)] * 3,
        compiler_params=_cparams(("arbitrary", "arbitrary")),
        name="attention",
    )(q, k, v, bias_tab)


K4_TM = 512


def _top2_sum(a, b, c, d):
    hi1, lo1 = jnp.maximum(a, b), jnp.minimum(a, b)
    hi2, lo2 = jnp.maximum(c, d), jnp.minimum(c, d)
    return jnp.maximum(hi1, hi2) + jnp.maximum(jnp.minimum(hi1, hi2), jnp.maximum(lo1, lo2))


def _route(logits_t, rb_col):
    m = jnp.max(logits_t, axis=0, keepdims=True)
    e = jnp.exp(logits_t - m)
    probs = e / jnp.sum(e, axis=0, keepdims=True)
    sel = probs + rb_col
    sel_rows = [sel[i:i + 1, :] for i in range(N_EXPERTS)]
    prob_rows = [probs[i:i + 1, :] for i in range(N_EXPERTS)]
    gsz = EXPERTS_PER_GROUP
    score = [_top2_sum(*sel_rows[g * gsz:(g + 1) * gsz]) for g in range(N_EXPERT_GROUPS)]
    chosen = []
    for g in range(N_EXPERT_GROUPS):
        best = None
        for g2 in range(N_EXPERT_GROUPS):
            if g2 == g:
                continue
            c = (score[g] > score[g2]) if g2 < g else (score[g] >= score[g2])
            best = c if best is None else jnp.logical_and(best, c)
        for i in range(gsz):
            ei = g * gsz + i
            rank = jnp.zeros_like(sel_rows[ei])
            for j in range(gsz):
                if j == i:
                    continue
                ej = g * gsz + j
                ahead = (sel_rows[ej] >= sel_rows[ei]) if j < i else (sel_rows[ej] > sel_rows[ei])
                rank = rank + jnp.where(ahead, 1.0, 0.0)
            chosen.append(jnp.logical_and(best, rank < float(2)))
    picked = [jnp.where(chosen[i], prob_rows[i], 0.0) for i in range(N_EXPERTS)]
    denom = picked[0]
    for i in range(1, N_EXPERTS):
        denom = denom + picked[i]
    return jnp.concatenate([pk / denom for pk in picked], axis=0)


def _mid_kernel(a_ref, ob_ref, x_ref, mod_ref, gb_ref, wout_ref, n2g_ref, rwt_ref, rb_ref,
                x1_ref, h2_ref, gates_ref):
    ob = ob_ref[...]
    bn = ob * lax.rsqrt(jnp.mean(ob * ob, axis=-1, keepdims=True) + EPS) * gb_ref[...]
    mixed = jnp.dot(a_ref[...], wout_ref[:A_WIDTH, :], preferred_element_type=F32)
    mixed = mixed + jnp.dot(bn.astype(BF16), wout_ref[A_WIDTH:, :], preferred_element_type=F32)
    x1 = x_ref[...] + mod_ref[2:3, :] * mixed
    x1_ref[...] = x1
    h2 = x1 * lax.rsqrt(jnp.mean(x1 * x1, axis=-1, keepdims=True) + EPS) * n2g_ref[...]
    h2 = (h2 * (1.0 + mod_ref[4:5, :]) + mod_ref[3:4, :]).astype(BF16)
    h2_ref[...] = h2
    logits_t = lax.dot_general(rwt_ref[...], h2, (((1,), (1,)), ((), ())),
                               preferred_element_type=F32)
    gates_t = _route(logits_t, rb_ref[...])
    tm = gates_t.shape[1]
    padded = jnp.concatenate([gates_t, jnp.zeros((LANES - N_EXPERTS, tm), F32)], axis=0)
    gates_ref[...] = padded.T


def _mid(out_a, out_b, x, mod_l, gb, wout, n2g, rwt, rb_col):
    tm = K4_TM
    tiles_per_seq = SEQ // tm
    tok = pl.BlockSpec((tm, D_MODEL), lambda i: (i, 0))
    half = pl.BlockSpec((tm, A_WIDTH), lambda i: (i, 0))
    return pl.pallas_call(
        _mid_kernel,
        grid=(TOKENS // tm,),
        in_specs=[
            half, half, tok,
            pl.BlockSpec((None, N_MOD, D_MODEL), lambda i: (i // tiles_per_seq, 0, 0)),
            pl.BlockSpec((1, B_WIDTH), lambda i: (0, 0)),
            pl.BlockSpec((D_MODEL, D_MODEL), lambda i: (0, 0)),
            pl.BlockSpec((1, D_MODEL), lambda i: (0, 0)),
            pl.BlockSpec((N_EXPERTS, D_MODEL), lambda i: (0, 0)),
            pl.BlockSpec((N_EXPERTS, 1), lambda i: (0, 0)),
        ],
        out_specs=[tok, tok, pl.BlockSpec((tm, LANES), lambda i: (i, 0))],
        out_shape=[jax.ShapeDtypeStruct((TOKENS, D_MODEL), F32),
                   jax.ShapeDtypeStruct((TOKENS, D_MODEL), BF16),
                   jax.ShapeDtypeStruct((TOKENS, LANES), F32)],
        compiler_params=_cparams(("arbitrary",)),
        name="mid",
    )(out_a, out_b, x, mod_l, gb, wout, n2g, rwt, rb_col)


MOE_TM = 512


def _moe_kernel(h_ref, gates_ref, wg_ref, wu_ref, wd_ref, y_ref):
    e = pl.program_id(1)
    h = h_ref[...]
    lane = lax.broadcasted_iota(jnp.int32, gates_ref.shape, 1)
    gate = jnp.sum(jnp.where(lane == e, gates_ref[...], 0.0), axis=-1, keepdims=True)
    hg = jnp.dot(h, wg_ref[...], preferred_element_type=F32)
    hu = jnp.dot(h, wu_ref[...], preferred_element_type=F32)
    act = (jax.nn.silu(hg) * hu * gate).astype(BF16)
    y = jnp.dot(act, wd_ref[...], preferred_element_type=F32)

    @pl.when(e == 0)
    def _():
        y_ref[...] = y

    @pl.when(e > 0)
    def _():
        y_ref[...] += y


def _moe(h2, gates, wg, wu, wd):
    tm = MOE_TM
    return pl.pallas_call(
        _moe_kernel,
        grid=(TOKENS // tm, N_EXPERTS),
        in_specs=[
            pl.BlockSpec((tm, D_MODEL), lambda i, e: (i, 0)),
            pl.BlockSpec((tm, LANES), lambda i, e: (i, 0)),
            pl.BlockSpec((None, D_MODEL, D_EXPERT), lambda i, e: (e, 0, 0)),
            pl.BlockSpec((None, D_MODEL, D_EXPERT), lambda i, e: (e, 0, 0)),
            pl.BlockSpec((None, D_EXPERT, D_MODEL), lambda i, e: (e, 0, 0)),
        ],
        out_specs=pl.BlockSpec((tm, D_MODEL), lambda i, e: (i, 0)),
        out_shape=jax.ShapeDtypeStruct((TOKENS, D_MODEL), F32),
        compiler_params=_cparams(("arbitrary", "arbitrary")),
        name="moe",
    )(h2, gates, wg, wu, wd)


FIN_TM = 512


def _final_kernel(x_ref, y_ref, mod_ref, g_ref, o_ref):
    x = x_ref[...] + mod_ref[5:6, :] * y_ref[...]
    o_ref[...] = x * lax.rsqrt(jnp.mean(x * x, axis=-1, keepdims=True) + EPS) * g_ref[...]


def _final(x, y, mod_l, g):
    tm = FIN_TM
    tiles_per_seq = SEQ // tm
    tok = pl.BlockSpec((tm, D_MODEL), lambda i: (i, 0))
    return pl.pallas_call(
        _final_kernel,
        grid=(TOKENS // tm,),
        in_specs=[tok, tok,
                  pl.BlockSpec((None, N_MOD, D_MODEL), lambda i: (i // tiles_per_seq, 0, 0)),
                  pl.BlockSpec((1, D_MODEL), lambda i: (0, 0))],
        out_specs=tok,
        out_shape=jax.ShapeDtypeStruct((TOKENS, D_MODEL), F32),
        compiler_params=_cparams(("arbitrary",)),
        name="final",
    )(x, y, mod_l, g)


def kernel(x, c, rel_bias, router_w, router_b, mod_w, mod_b, norm1_g, w_in, gmlp_ln_g, gmlp_ln_b,
           gmlp_ws, gmlp_bs, out_norm_a_g, out_norm_b_g, w_out, norm2_g, moe_w_gate, moe_w_up,
           moe_w_down, final_g):
    mod = _modulation(c, mod_w, mod_b).reshape(DEPTH, BATCH, N_MOD, D_MODEL)
    bias_tab = _bias_tables(rel_bias)
    rwt = router_w.T.astype(BF16)
    rb_col = router_b.reshape(N_EXPERTS, 1)
    xt = x.reshape(TOKENS, D_MODEL)
    y = None
    for l in range(DEPTH):
        ws2 = gmlp_ws[l].astype(BF16).reshape(A_GROUPS // 2, 2 * CHUNK, CHUNK)
        bs_exp = jnp.repeat(gmlp_bs[l].T, HEAD_DIM, axis=1)
        xt, out_a, q, k, v = _front(
            xt, y, mod[l - 1] if l else None, mod[l], norm1_g[l].reshape(1, -1),
            w_in[l].astype(BF16), gmlp_ln_g[l].reshape(1, -1), gmlp_ln_b[l].reshape(1, -1),
            ws2, bs_exp, out_norm_a_g[l].reshape(1, -1))
        out_b = _attention(q, k, v, bias_tab)
        xt, h2, gates = _mid(out_a, out_b, xt, mod[l], out_norm_b_g[l].reshape(1, -1),
                             w_out[l].astype(BF16), norm2_g[l].reshape(1, -1), rwt, rb_col)
        y = _moe(h2, gates, moe_w_gate[l].astype(BF16), moe_w_up[l].astype(BF16),
                 moe_w_down[l].astype(BF16))
    out = _final(xt, y, mod[DEPTH - 1], final_g.reshape(1, -1))
    return out.reshape(BATCH, SEQ, D_MODEL)
```

```python
import functools
import math

import numpy as np
import jax
import jax.numpy as jnp
from jax import lax
from jax.experimental import pallas as pl
from jax.experimental.pallas import tpu as pltpu

D_MODEL = 1024
BATCH = 8
SEQ = 2048
DEPTH = 2
TOKENS = BATCH * SEQ
HEAD_DIM = 64
A_WIDTH = 512
B_WIDTH = 512
A_GROUPS = 8
IN_COLS = 2 * A_WIDTH + 3 * B_WIDTH
CHUNK = 128
DILATED_CONFIGS = ((128, 1), (512, 4), (2048, 16))
ATTN_BLOCK = 128
REL_BUCKETS = 32
REL_MAX_EXACT = REL_BUCKETS // 2
REL_MAX_DISTANCE = 2048
N_EXPERTS = 16
N_EXPERT_GROUPS = 4
EXPERTS_PER_GROUP = 4
D_EXPERT = 512
N_MOD = 6
EPS = 1e-6
NEG_INF = -1e30

LANES = 128
HEAD_PAIR = 2 * HEAD_DIM
N_PAIRS = B_WIDTH // HEAD_PAIR

F32 = jnp.float32
BF16 = jnp.bfloat16

VMEM_LIMIT = 56 * 1024 * 1024


def _cparams(sem):
    return pltpu.CompilerParams(dimension_semantics=sem, vmem_limit_bytes=VMEM_LIMIT)


def _gelu(x):
    return 0.5 * x * (1.0 + lax.erf(x * math.sqrt(0.5)))


MOD_TN = 1024


def _mod_kernel(c_ref, w_ref, b_ref, o_ref):
    ca = jax.nn.silu(c_ref[...])
    o_ref[...] = jnp.dot(ca.astype(BF16), w_ref[...].astype(BF16),
                         preferred_element_type=F32) + b_ref[...]


def _modulation(c, mod_w, mod_b):
    n_cols = N_MOD * D_MODEL
    return pl.pallas_call(
        _mod_kernel,
        grid=(DEPTH, n_cols // MOD_TN),
        in_specs=[
            pl.BlockSpec((BATCH, D_MODEL), lambda l, j: (0, 0)),
            pl.BlockSpec((None, D_MODEL, MOD_TN), lambda l, j: (l, 0, j)),
            pl.BlockSpec((None, 1, MOD_TN), lambda l, j: (l, 0, j)),
        ],
        out_specs=pl.BlockSpec((None, BATCH, MOD_TN), lambda l, j: (l, 0, j)),
        out_shape=jax.ShapeDtypeStruct((DEPTH, BATCH, n_cols), F32),
        compiler_params=_cparams(("arbitrary", "arbitrary")),
        name="modulation",
    )(c, mod_w, mod_b.reshape(DEPTH, 1, n_cols))


K1_TM = 512


def _front_kernel(*refs, has_moe):
    if has_moe:
        (x_ref, y_ref, modp_ref, mod_ref, n1g_ref, win_ref, lng_ref, lnb_ref, ws_ref,
         bs_ref, ga_ref, xo_ref, a_ref, q_ref, k_ref, v_ref) = refs
        x = x_ref[...] + modp_ref[5:6, :] * y_ref[...]
        xo_ref[...] = x
    else:
        (x_ref, mod_ref, n1g_ref, win_ref, lng_ref, lnb_ref, ws_ref,
         bs_ref, ga_ref, a_ref, q_ref, k_ref, v_ref) = refs
        x = x_ref[...]
    tm = x.shape[0]
    h = x * lax.rsqrt(jnp.mean(x * x, axis=-1, keepdims=True) + EPS) * n1g_ref[...]
    h = h * (1.0 + mod_ref[1:2, :]) + mod_ref[0:1, :]
    proj = jnp.dot(h.astype(BF16), win_ref[...], preferred_element_type=F32)

    q_ref[...] = proj[:, 2 * A_WIDTH:2 * A_WIDTH + B_WIDTH] * (HEAD_DIM ** -0.5)
    k_ref[...] = proj[:, 2 * A_WIDTH + B_WIDTH:2 * A_WIDTH + 2 * B_WIDTH]
    v_ref[...] = proj[:, 2 * A_WIDTH + 2 * B_WIDTH:]

    u = _gelu(proj[:, :A_WIDTH])
    va = _gelu(proj[:, A_WIDTH:2 * A_WIDTH])
    mu = jnp.mean(va, axis=-1, keepdims=True)
    vc = va - mu
    vln = vc * lax.rsqrt(jnp.mean(vc * vc, axis=-1, keepdims=True) + EPS)
    vln = (vln * lng_ref[...] + lnb_ref[...]).astype(BF16)

    row = lax.broadcasted_iota(jnp.int32, (2 * CHUNK, CHUNK), 0)
    col = lax.broadcasted_iota(jnp.int32, (2 * CHUNK, CHUNK), 1)
    causal = (row % CHUNK) >= col
    first_group = lax.broadcasted_iota(jnp.int32, (CHUNK, LANES), 1) < HEAD_DIM
    wmix = [jnp.where(causal, ws_ref[p], jnp.zeros((), BF16)) for p in range(A_GROUPS // 2)]
    for c in range(tm // CHUNK):
        rows = slice(c * CHUNK, (c + 1) * CHUNK)
        parts = []
        for p in range(A_GROUPS // 2):
            vp = vln[rows, p * LANES:(p + 1) * LANES]
            r = jnp.dot(wmix[p], vp, preferred_element_type=F32)
            parts.append(jnp.where(first_group, r[:CHUNK], r[CHUNK:]))
        s = jnp.concatenate(parts, axis=-1) + bs_ref[...]
        oa = u[rows] * s
        oa = oa * lax.rsqrt(jnp.mean(oa * oa, axis=-1, keepdims=True) + EPS) * ga_ref[...]
        a_ref[rows, :] = oa.astype(BF16)


def _front(x, y_prev, mod_prev, mod_l, n1g, win, lng, lnb, ws2, bs_exp, ga):
    tm = K1_TM
    has_moe = y_prev is not None
    tiles_per_seq = SEQ // tm
    tok = pl.BlockSpec((tm, D_MODEL), lambda i: (i, 0))
    modspec = pl.BlockSpec((None, N_MOD, D_MODEL), lambda i: (i // tiles_per_seq, 0, 0))
    row1024 = pl.BlockSpec((1, D_MODEL), lambda i: (0, 0))
    row512 = pl.BlockSpec((1, A_WIDTH), lambda i: (0, 0))
    half = pl.BlockSpec((tm, A_WIDTH), lambda i: (i, 0))
    in_specs = [tok]
    args = [x]
    if has_moe:
        in_specs += [tok, modspec]
        args += [y_prev, mod_prev]
    in_specs += [
        modspec, row1024,
        pl.BlockSpec((D_MODEL, IN_COLS), lambda i: (0, 0)),
        row512, row512,
        pl.BlockSpec((A_GROUPS // 2, 2 * CHUNK, CHUNK), lambda i: (0, 0, 0)),
        pl.BlockSpec((CHUNK, A_WIDTH), lambda i: (0, 0)),
        row512,
    ]
    args += [mod_l, n1g, win, lng, lnb, ws2, bs_exp, ga]
    out_specs = [half, half, half, half]
    out_shape = [jax.ShapeDtypeStruct((TOKENS, A_WIDTH), BF16)] + \
                [jax.ShapeDtypeStruct((TOKENS, B_WIDTH), F32)] * 3
    if has_moe:
        out_specs = [tok] + out_specs
        out_shape = [jax.ShapeDtypeStruct((TOKENS, D_MODEL), F32)] + out_shape
    outs = pl.pallas_call(
        functools.partial(_front_kernel, has_moe=has_moe),
        grid=(TOKENS // tm,),
        in_specs=in_specs,
        out_specs=out_specs,
        out_shape=out_shape,
        compiler_params=_cparams(("arbitrary",)),
        name="front",
    )(*args)
    if has_moe:
        return outs
    return (x, *outs)


def _t5_bucket_np(dist):
    dist = np.maximum(dist, 0)
    ratio = np.log(np.maximum(dist, 1) / REL_MAX_EXACT) / np.log(REL_MAX_DISTANCE / REL_MAX_EXACT)
    large = REL_MAX_EXACT + np.floor(ratio * (REL_BUCKETS - REL_MAX_EXACT)).astype(np.int64)
    large = np.minimum(large, REL_BUCKETS - 1)
    return np.where(dist < REL_MAX_EXACT, dist, large).astype(np.int32)


def _bias_tables(rel_bias):
    blk = ATTN_BLOCK
    n_rel = 3 * blk
    rel = 2 * blk - 1 - np.arange(n_rel)
    first = jnp.asarray(np.arange(2 * blk)[None, :] >= blk)
    tables = []
    for window, d in DILATED_CONFIGS:
        span = window // d
        bucket = _t5_bucket_np(np.clip(rel, 0, span) * d)
        valid = jnp.asarray((rel >= 0) & (rel <= span))
        w = jnp.where(valid[None, :], rel_bias.astype(F32)[bucket].T, NEG_INF)
        flat = jnp.tile(w, (1, blk))
        skew = flat[:, blk - 1:blk - 1 + blk * (n_rel - 1)].reshape(-1, blk, n_rel - 1)
        tab = skew[:, :, :2 * blk]
        tables.append(jnp.stack([tab, jnp.where(first[None], tab, NEG_INF)], axis=0))
    return jnp.stack(tables, axis=0)


ATTN_GROUP = 4


def _attn_kernel(q_ref, k_ref, v_ref, bias_ref, o_ref, m_sc, l_sc, acc_sc):
    blk = ATTN_BLOCK
    grp = ATTN_GROUP
    lane = lax.broadcasted_iota(jnp.int32, (blk, LANES), 1)
    head0 = lane < HEAD_DIM
    ones = jnp.ones((2 * blk, LANES), BF16)

    def merge_heads(t):
        return jnp.where(head0, t[:blk], t[blk:])

    def scores_block(q, kcat, vcat, bias):
        nk = kcat.shape[0]
        zero = jnp.zeros_like(q)
        qs = jnp.concatenate([jnp.where(head0, q, zero), jnp.where(head0, zero, q)],
                             axis=0).astype(BF16)
        s = lax.dot_general(qs, kcat, (((1,), (1,)), ((), ())), preferred_element_type=F32) + bias
        m = jnp.max(s, axis=-1, keepdims=True)
        p = jnp.exp(s - m).astype(BF16)
        pv = jnp.dot(p, jnp.concatenate([vcat, ones[:nk]], axis=1), preferred_element_type=F32)
        return (merge_heads(jnp.broadcast_to(m, (2 * blk, LANES))),
                merge_heads(pv[:, LANES:]), merge_heads(pv[:, :LANES]))

    def update(ci, rows_list, results):
        if ci == 0:
            for rows, (m_c, l_c, o_c) in zip(rows_list, results):
                m_sc[rows, :] = m_c
                l_sc[rows, :] = l_c
                acc_sc[rows, :] = o_c
            return
        olds = [(m_sc[rows, :], l_sc[rows, :], acc_sc[rows, :]) for rows in rows_list]
        news = []
        for (m_r, l_r, a_r), (m_c, l_c, o_c) in zip(olds, results):
            m_n = jnp.maximum(m_r, m_c)
            a = jnp.exp(m_r - m_n)
            b = jnp.exp(m_c - m_n)
            news.append((m_n, a * l_r + b * l_c, a * a_r + b * o_c))
        for rows, (m_n, l_n, a_n) in zip(rows_list, news):
            m_sc[rows, :] = m_n
            l_sc[rows, :] = l_n
            acc_sc[rows, :] = a_n

    def head_bias(ci, first, nk):
        if nk == 2 * blk:
            return jnp.concatenate([bias_ref[ci, first, 0], bias_ref[ci, first, 1]], axis=0)
        return jnp.concatenate([bias_ref[ci, 0, 0, :, blk:], bias_ref[ci, 0, 1, :, blk:]], axis=0)

    for ci, (window, d) in enumerate(DILATED_CONFIGS):
        nb = SEQ // d // blk

        def rows_of(r, n, d=d):
            if d == 1:
                return pl.ds(pl.multiple_of(n * blk, blk), blk)
            return pl.ds(r + n * (blk * d), blk, stride=d)

        if nb >= grp:
            def trip(j, carry, ci=ci, nb=nb, rows_of=rows_of):
                r = j // (nb // grp)
                n0 = (j % (nb // grp)) * grp
                k_prev = v_prev = None
                if nb > grp:
                    prev_rows = rows_of(r, jnp.maximum(n0 - 1, 0))
                    k_prev = k_ref[prev_rows, :].astype(BF16)
                    v_prev = v_ref[prev_rows, :].astype(BF16)
                first = jnp.where(n0 == 0, 1, 0)
                rows_list, results = [], []
                for g in range(grp):
                    rows = rows_of(r, n0 + g)
                    k_cur = k_ref[rows, :].astype(BF16)
                    v_cur = v_ref[rows, :].astype(BF16)
                    if k_prev is None:
                        results.append(scores_block(q_ref[rows, :], k_cur, v_cur,
                                                    head_bias(ci, 0, blk)))
                    else:
                        results.append(scores_block(
                            q_ref[rows, :], jnp.concatenate([k_prev, k_cur], axis=0),
                            jnp.concatenate([v_prev, v_cur], axis=0),
                            head_bias(ci, first if g == 0 else 0, 2 * blk)))
                    rows_list.append(rows)
                    k_prev, v_prev = k_cur, v_cur
                update(ci, rows_list, results)
                return carry

            lax.fori_loop(0, d * nb // grp, trip, 0)
        else:
            assert nb == 1

            def trip(j, carry, ci=ci, rows_of=rows_of):
                rows_list, results = [], []
                for g in range(grp):
                    rows = rows_of(j * grp + g, 0)
                    results.append(scores_block(q_ref[rows, :], k_ref[rows, :].astype(BF16),
                                                v_ref[rows, :].astype(BF16), head_bias(ci, 0, blk)))
                    rows_list.append(rows)
                update(ci, rows_list, results)
                return carry

            lax.fori_loop(0, d // grp, trip, 0)

    o_ref[...] = acc_sc[...] / l_sc[...]


def _attention(q, k, v, bias_tab):
    n_cfg = len(DILATED_CONFIGS)
    blk = ATTN_BLOCK
    seq_spec = pl.BlockSpec((SEQ, HEAD_PAIR), lambda b, p: (b, p))
    return pl.pallas_call(
        _attn_kernel,
        grid=(BATCH, N_PAIRS),
        in_specs=[seq_spec, seq_spec, seq_spec,
                  pl.BlockSpec((n_cfg, 2, 2, blk, 2 * blk), lambda b, p: (0, 0, p, 0, 0))],
        out_specs=seq_spec,
        out_shape=jax.ShapeDtypeStruct((TOKENS, B_WIDTH), F32),
        scratch_shapes=[pltpu.VMEM((SEQ, HEAD_PAIR), F32)] * 3,
        compiler_params=_cparams(("arbitrary", "arbitrary")),
        name="attention",
    )(q, k, v, bias_tab)


K4_TM = 512


def _top2_sum(a, b, c, d):
    hi1, lo1 = jnp.maximum(a, b), jnp.minimum(a, b)
    hi2, lo2 = jnp.maximum(c, d), jnp.minimum(c, d)
    return jnp.maximum(hi1, hi2) + jnp.maximum(jnp.minimum(hi1, hi2), jnp.maximum(lo1, lo2))


def _route(logits_t, rb_col):
    m = jnp.max(logits_t, axis=0, keepdims=True)
    e = jnp.exp(logits_t - m)
    probs = e / jnp.sum(e, axis=0, keepdims=True)
    sel = probs + rb_col
    sel_rows = [sel[i:i + 1, :] for i in range(N_EXPERTS)]
    prob_rows = [probs[i:i + 1, :] for i in range(N_EXPERTS)]
    gsz = EXPERTS_PER_GROUP
    score = [_top2_sum(*sel_rows[g * gsz:(g + 1) * gsz]) for g in range(N_EXPERT_GROUPS)]
    chosen = []
    for g in range(N_EXPERT_GROUPS):
        best = None
        for g2 in range(N_EXPERT_GROUPS):
            if g2 == g:
                continue
            c = (score[g] > score[g2]) if g2 < g else (score[g] >= score[g2])
            best = c if best is None else jnp.logical_and(best, c)
        for i in range(gsz):
            ei = g * gsz + i
            rank = jnp.zeros_like(sel_rows[ei])
            for j in range(gsz):
                if j == i:
                    continue
                ej = g * gsz + j
                ahead = (sel_rows[ej] >= sel_rows[ei]) if j < i else (sel_rows[ej] > sel_rows[ei])
                rank = rank + jnp.where(ahead, 1.0, 0.0)
            chosen.append(jnp.logical_and(best, rank < float(2)))
    picked = [jnp.where(chosen[i], prob_rows[i], 0.0) for i in range(N_EXPERTS)]
    denom = picked[0]
    for i in range(1, N_EXPERTS):
        denom = denom + picked[i]
    gates = jnp.concatenate([pk / denom for pk in picked], axis=0)
    onehot = jnp.concatenate([jnp.where(ch, 1.0, 0.0) for ch in chosen], axis=0)
    return gates, onehot


def _mid_kernel(a_ref, ob_ref, x_ref, mod_ref, gb_ref, wout_ref, n2g_ref, rwt_ref, rb_ref,
                x1_ref, h2_ref, gates_ref, onehot_ref, count_ref):
    ob = ob_ref[...]
    bn = ob * lax.rsqrt(jnp.mean(ob * ob, axis=-1, keepdims=True) + EPS) * gb_ref[...]
    mixed = jnp.dot(a_ref[...], wout_ref[:A_WIDTH, :], preferred_element_type=F32)
    mixed = mixed + jnp.dot(bn.astype(BF16), wout_ref[A_WIDTH:, :], preferred_element_type=F32)
    x1 = x_ref[...] + mod_ref[2:3, :] * mixed
    x1_ref[...] = x1
    h2 = x1 * lax.rsqrt(jnp.mean(x1 * x1, axis=-1, keepdims=True) + EPS) * n2g_ref[...]
    h2 = (h2 * (1.0 + mod_ref[4:5, :]) + mod_ref[3:4, :]).astype(BF16)
    h2_ref[...] = h2
    logits_t = lax.dot_general(rwt_ref[...], h2, (((1,), (1,)), ((), ())),
                               preferred_element_type=F32)
    gates_t, onehot_t = _route(logits_t, rb_ref[...])
    gates_ref[...] = gates_t
    onehot_ref[...] = onehot_t
    for s in range(count_ref.shape[0]):
        count_ref[s] = jnp.sum(onehot_t[:, s * MOE_TOK:(s + 1) * MOE_TOK], axis=1, keepdims=True)


def _mid(out_a, out_b, x, mod_l, gb, wout, n2g, rwt, rb_col):
    tm = K4_TM
    tiles_per_seq = SEQ // tm
    tok = pl.BlockSpec((tm, D_MODEL), lambda i: (i, 0))
    half = pl.BlockSpec((tm, A_WIDTH), lambda i: (i, 0))
    route = pl.BlockSpec((N_EXPERTS, tm), lambda i: (0, i))
    return pl.pallas_call(
        _mid_kernel,
        grid=(TOKENS // tm,),
        in_specs=[
            half, half, tok,
            pl.BlockSpec((None, N_MOD, D_MODEL), lambda i: (i // tiles_per_seq, 0, 0)),
            pl.BlockSpec((1, B_WIDTH), lambda i: (0, 0)),
            pl.BlockSpec((D_MODEL, D_MODEL), lambda i: (0, 0)),
            pl.BlockSpec((1, D_MODEL), lambda i: (0, 0)),
            pl.BlockSpec((N_EXPERTS, D_MODEL), lambda i: (0, 0)),
            pl.BlockSpec((N_EXPERTS, 1), lambda i: (0, 0)),
        ],
        out_specs=[tok, tok, route, route,
                   pl.BlockSpec((tm // MOE_TOK, N_EXPERTS, 1), lambda i: (i, 0, 0))],
        out_shape=[jax.ShapeDtypeStruct((TOKENS, D_MODEL), F32),
                   jax.ShapeDtypeStruct((TOKENS, D_MODEL), BF16),
                   jax.ShapeDtypeStruct((N_EXPERTS, TOKENS), F32),
                   jax.ShapeDtypeStruct((N_EXPERTS, TOKENS), F32),
                   jax.ShapeDtypeStruct((TOKENS // MOE_TOK, N_EXPERTS, 1), F32)],
        compiler_params=_cparams(("arbitrary",)),
        name="mid",
    )(out_a, out_b, x, mod_l, gb, wout, n2g, rwt, rb_col)


MOE_TOK = 256
ROW_ALIGN = 8
MOE_LB = 2 * MOE_TOK + N_EXPERTS * ROW_ALIGN
FFN_TM = 512
N_TOK_TILES = TOKENS // MOE_TOK
ROWS_MAX = -(-(2 * TOKENS + N_TOK_TILES * N_EXPERTS * (ROW_ALIGN - 1)
               + N_EXPERTS * (FFN_TM - ROW_ALIGN)) // FFN_TM) * FFN_TM
RUN_SIZES = tuple(MOE_TOK >> s for s in range((MOE_TOK // ROW_ALIGN).bit_length()))


def _moe_plan(count):
    n = count.reshape(N_TOK_TILES, N_EXPERTS).astype(jnp.int32)
    n8 = (n + (ROW_ALIGN - 1)) // ROW_ALIGN * ROW_ALIGN
    local = jnp.cumsum(n8, axis=1) - n8
    seg = (jnp.sum(n8, axis=0) + (FFN_TM - 1)) // FFN_TM * FFN_TM
    seg_end = jnp.cumsum(seg)
    glob = (seg_end - seg)[None, :] + jnp.cumsum(n8, axis=0) - n8
    n_ffn = ROWS_MAX // FFN_TM
    n_active = seg_end[-1] // FFN_TM
    tile_row = jnp.arange(n_ffn, dtype=jnp.int32) * FFN_TM
    texp = jnp.sum((tile_row[:, None] >= seg_end[None, :]).astype(jnp.int32), axis=1)
    texp = jnp.minimum(texp, N_EXPERTS - 1)
    texp = jnp.where(jnp.arange(n_ffn) < n_active, texp, texp[n_active - 1])
    return dict(n8=n8.reshape(-1), local=local.reshape(-1), glob=glob.reshape(-1),
                local_col=local.astype(F32).reshape(N_TOK_TILES, N_EXPERTS, 1),
                texp=texp.astype(jnp.int32), n_active=n_active.reshape(1).astype(jnp.int32))


def _run_copies(tile, n8_ref, local_ref, glob_ref, make_copy, start):
    for e in range(N_EXPERTS):
        idx = tile * N_EXPERTS + e
        n = n8_ref[idx]
        lo = local_ref[idx]
        go = glob_ref[idx]
        done = jnp.int32(0)
        for size in RUN_SIZES:
            hit = (n & size) != 0

            @pl.when(hit)
            def _(done=done, size=size, lo=lo, go=go):
                cp = make_copy(pl.multiple_of(lo + done, ROW_ALIGN),
                               pl.multiple_of(go + done, ROW_ALIGN), size)
                if start:
                    cp.start()
                else:
                    cp.wait()

            done = done + jnp.where(hit, size, 0)


def _sorted_rows(onehot, local_col):
    tm = onehot.shape[1]
    src = lax.broadcasted_iota(jnp.int32, (tm, tm), 0)
    dst = lax.broadcasted_iota(jnp.int32, (tm, tm), 1)
    before = jnp.where(src < dst, 1.0, 0.0).astype(BF16)
    rank = jnp.dot(onehot.astype(BF16), before, preferred_element_type=F32)
    pos = local_col + rank
    chosen = onehot > 0.5
    lo = jnp.min(jnp.where(chosen, pos, float(MOE_LB)), axis=0, keepdims=True)
    hi = jnp.max(jnp.where(chosen, pos, -1.0), axis=0, keepdims=True)
    return pos, chosen, lo, hi


def _dispatch_kernel(n8_ref, local_ref, glob_ref, h_ref, oh_ref, lcol_ref, xs_ref, lbuf, sem):
    j = pl.program_id(0)
    last = pl.num_programs(0) - 1
    slot = j % 2
    _, _, lo, hi = _sorted_rows(oh_ref[...], lcol_ref[...])
    row = lax.broadcasted_iota(jnp.int32, (MOE_LB, MOE_TOK), 0).astype(F32)
    perm = (jnp.where(row == lo, 1.0, 0.0) + jnp.where(row == hi, 1.0, 0.0)).astype(BF16)
    lbuf[slot] = jnp.dot(perm, h_ref[...], preferred_element_type=F32)

    def copy_from(slot_):
        def make(lo_, go_, size):
            return pltpu.make_async_copy(lbuf.at[slot_, pl.ds(lo_, size), :],
                                         xs_ref.at[pl.ds(go_, size), :], sem.at[slot_])
        return make

    refs = (n8_ref, local_ref, glob_ref)
    _run_copies(j, *refs, copy_from(slot), True)

    @pl.when(j > 0)
    def _():
        _run_copies(j - 1, *refs, copy_from(1 - slot), False)

    @pl.when(j == last)
    def _():
        _run_copies(j, *refs, copy_from(slot), False)


def _dispatch(plan, h2, onehot):
    grid_spec = pltpu.PrefetchScalarGridSpec(
        num_scalar_prefetch=3,
        grid=(N_TOK_TILES,),
        in_specs=[
            pl.BlockSpec((MOE_TOK, D_MODEL), lambda j, *_: (j, 0)),
            pl.BlockSpec((N_EXPERTS, MOE_TOK), lambda j, *_: (0, j)),
            pl.BlockSpec((None, N_EXPERTS, 1), lambda j, *_: (j, 0, 0)),
        ],
        out_specs=pl.BlockSpec(memory_space=pl.ANY),
        scratch_shapes=[pltpu.VMEM((2, MOE_LB, D_MODEL), F32), pltpu.SemaphoreType.DMA((2,))],
    )
    return pl.pallas_call(
        _dispatch_kernel,
        grid_spec=grid_spec,
        out_shape=jax.ShapeDtypeStruct((ROWS_MAX, D_MODEL), F32),
        compiler_params=_cparams(("arbitrary",)),
        name="dispatch",
    )(plan["n8"], plan["local"], plan["glob"], h2, onehot, plan["local_col"])


def _experts_kernel(texp_ref, nact_ref, xs_ref, wg_ref, wu_ref, wd_ref, ys_ref, wg_b, wu_b, wd_b):
    i = pl.program_id(0)
    active = i < nact_ref[0]
    new_expert = jnp.logical_or(i == 0, texp_ref[i] != texp_ref[jnp.maximum(i - 1, 0)])

    @pl.when(jnp.logical_and(active, new_expert))
    def _():
        wg_b[...] = wg_ref[...].astype(BF16)
        wu_b[...] = wu_ref[...].astype(BF16)
        wd_b[...] = wd_ref[...].astype(BF16)

    @pl.when(active)
    def _():
        x = xs_ref[...].astype(BF16)
        hg = jnp.dot(x, wg_b[...], preferred_element_type=F32)
        hu = jnp.dot(x, wu_b[...], preferred_element_type=F32)
        act = (jax.nn.silu(hg) * hu).astype(BF16)
        ys_ref[...] = jnp.dot(act, wd_b[...], preferred_element_type=F32)


def _experts(plan, xs, wg, wu, wd):
    def rows(i, texp, nact):
        return (jnp.minimum(i, nact[0] - 1), 0)

    def expert(i, texp, nact):
        return (texp[i], 0, 0)

    grid_spec = pltpu.PrefetchScalarGridSpec(
        num_scalar_prefetch=2,
        grid=(ROWS_MAX // FFN_TM,),
        in_specs=[
            pl.BlockSpec((FFN_TM, D_MODEL), rows),
            pl.BlockSpec((None, D_MODEL, D_EXPERT), expert),
            pl.BlockSpec((None, D_MODEL, D_EXPERT), expert),
            pl.BlockSpec((None, D_EXPERT, D_MODEL), expert),
        ],
        out_specs=pl.BlockSpec((FFN_TM, D_MODEL), rows),
        scratch_shapes=[pltpu.VMEM((D_MODEL, D_EXPERT), BF16), pltpu.VMEM((D_MODEL, D_EXPERT), BF16),
                        pltpu.VMEM((D_EXPERT, D_MODEL), BF16)],
    )
    return pl.pallas_call(
        _experts_kernel,
        grid_spec=grid_spec,
        out_shape=jax.ShapeDtypeStruct((ROWS_MAX, D_MODEL), F32),
        compiler_params=_cparams(("arbitrary",)),
        name="experts",
    )(plan["texp"], plan["n_active"], xs, wg, wu, wd)


def _combine_kernel(n8_ref, local_ref, glob_ref, oh_ref, g_ref, lcol_ref, ys_ref, y_ref, ybuf, sem):
    j = pl.program_id(0)
    n_tiles = pl.num_programs(0)
    slot = j % 2
    refs = (n8_ref, local_ref, glob_ref)

    def copy_to(slot_):
        def make(lo_, go_, size):
            return pltpu.make_async_copy(ys_ref.at[pl.ds(go_, size), :],
                                         ybuf.at[slot_, pl.ds(lo_, size), :], sem.at[slot_])
        return make

    @pl.when(j == 0)
    def _():
        ybuf[...] = jnp.zeros(ybuf.shape, F32)
        _run_copies(0, *refs, copy_to(0), True)

    @pl.when(j + 1 < n_tiles)
    def _():
        _run_copies(j + 1, *refs, copy_to(1 - slot), True)

    pos, chosen, lo, hi = _sorted_rows(oh_ref[...], lcol_ref[...])
    gate = jnp.where(chosen, g_ref[...], 0.0)
    g_lo = jnp.sum(jnp.where(pos == lo, gate, 0.0), axis=0, keepdims=True)
    g_hi = jnp.sum(jnp.where(pos == hi, gate, 0.0), axis=0, keepdims=True)
    row = lax.broadcasted_iota(jnp.int32, (MOE_LB, MOE_TOK), 0).astype(F32)
    perm = (jnp.where(row == lo, g_lo, 0.0) + jnp.where(row == hi, g_hi, 0.0)).astype(BF16)

    _run_copies(j, *refs, copy_to(slot), False)
    y_ref[...] = lax.dot_general(perm, ybuf[slot].astype(BF16), (((0,), (0,)), ((), ())),
                                 preferred_element_type=F32)


def _combine(plan, onehot, gates, ys):
    route = pl.BlockSpec((N_EXPERTS, MOE_TOK), lambda j, *_: (0, j))
    grid_spec = pltpu.PrefetchScalarGridSpec(
        num_scalar_prefetch=3,
        grid=(N_TOK_TILES,),
        in_specs=[
            route, route,
            pl.BlockSpec((None, N_EXPERTS, 1), lambda j, *_: (j, 0, 0)),
            pl.BlockSpec(memory_space=pl.ANY),
        ],
        out_specs=pl.BlockSpec((MOE_TOK, D_MODEL), lambda j, *_: (j, 0)),
        scratch_shapes=[pltpu.VMEM((2, MOE_LB, D_MODEL), F32), pltpu.SemaphoreType.DMA((2,))],
    )
    return pl.pallas_call(
        _combine_kernel,
        grid_spec=grid_spec,
        out_shape=jax.ShapeDtypeStruct((TOKENS, D_MODEL), F32),
        compiler_params=_cparams(("arbitrary",)),
        name="combine",
    )(plan["n8"], plan["local"], plan["glob"], onehot, gates, plan["local_col"], ys)


FIN_TM = 512


def _final_kernel(x_ref, y_ref, mod_ref, g_ref, o_ref):
    x = x_ref[...] + mod_ref[5:6, :] * y_ref[...]
    o_ref[...] = x * lax.rsqrt(jnp.mean(x * x, axis=-1, keepdims=True) + EPS) * g_ref[...]


def _final(x, y, mod_l, g):
    tm = FIN_TM
    tiles_per_seq = SEQ // tm
    tok = pl.BlockSpec((tm, D_MODEL), lambda i: (i, 0))
    return pl.pallas_call(
        _final_kernel,
        grid=(TOKENS // tm,),
        in_specs=[tok, tok,
                  pl.BlockSpec((None, N_MOD, D_MODEL), lambda i: (i // tiles_per_seq, 0, 0)),
                  pl.BlockSpec((1, D_MODEL), lambda i: (0, 0))],
        out_specs=tok,
        out_shape=jax.ShapeDtypeStruct((TOKENS, D_MODEL), F32),
        compiler_params=_cparams(("arbitrary",)),
        name="final",
    )(x, y, mod_l, g)


def kernel(x, c, rel_bias, router_w, router_b, mod_w, mod_b, norm1_g, w_in, gmlp_ln_g, gmlp_ln_b,
           gmlp_ws, gmlp_bs, out_norm_a_g, out_norm_b_g, w_out, norm2_g, moe_w_gate, moe_w_up,
           moe_w_down, final_g):
    mod = _modulation(c, mod_w, mod_b).reshape(DEPTH, BATCH, N_MOD, D_MODEL)
    bias_tab = _bias_tables(rel_bias)
    rwt = router_w.T.astype(BF16)
    rb_col = router_b.reshape(N_EXPERTS, 1)
    xt = x.reshape(TOKENS, D_MODEL)
    y = None
    for l in range(DEPTH):
        ws2 = gmlp_ws[l].astype(BF16).reshape(A_GROUPS // 2, 2 * CHUNK, CHUNK)
        bs_exp = jnp.repeat(gmlp_bs[l].T, HEAD_DIM, axis=1)
        xt, out_a, q, k, v = _front(
            xt, y, mod[l - 1] if l else None, mod[l], norm1_g[l].reshape(1, -1),
            w_in[l].astype(BF16), gmlp_ln_g[l].reshape(1, -1), gmlp_ln_b[l].reshape(1, -1),
            ws2, bs_exp, out_norm_a_g[l].reshape(1, -1))
        out_b = _attention(q, k, v, bias_tab)
        xt, h2, gates, onehot, count = _mid(
            out_a, out_b, xt, mod[l], out_norm_b_g[l].reshape(1, -1),
            w_out[l].astype(BF16), norm2_g[l].reshape(1, -1), rwt, rb_col)
        plan = _moe_plan(count)
        xs = _dispatch(plan, h2, onehot)
        ys = _experts(plan, xs, moe_w_gate[l], moe_w_up[l], moe_w_down[l])
        y = _combine(plan, onehot, gates, ys)
    out = _final(xt, y, mod[DEPTH - 1], final_g.reshape(1, -1))
    return out.reshape(BATCH, SEQ, D_MODEL)
```

```python
import functools
import math

import numpy as np
import jax
import jax.numpy as jnp
from jax import lax
from jax.experimental import pallas as pl
from jax.experimental.pallas import tpu as pltpu

D_MODEL = 1024
BATCH = 8
SEQ = 2048
DEPTH = 2
TOKENS = BATCH * SEQ
HEAD_DIM = 64
A_WIDTH = 512
B_WIDTH = 512
A_GROUPS = 8
IN_COLS = 2 * A_WIDTH + 3 * B_WIDTH
CHUNK = 128
DILATED_CONFIGS = ((128, 1), (512, 4), (2048, 16))
ATTN_BLOCK = 128
REL_BUCKETS = 32
REL_MAX_EXACT = REL_BUCKETS // 2
REL_MAX_DISTANCE = 2048
N_EXPERTS = 16
N_EXPERT_GROUPS = 4
EXPERTS_PER_GROUP = 4
D_EXPERT = 512
N_MOD = 6
EPS = 1e-6
NEG_INF = -1e30

LANES = 128
HEAD_PAIR = 2 * HEAD_DIM
N_PAIRS = B_WIDTH // HEAD_PAIR

F32 = jnp.float32
BF16 = jnp.bfloat16

VMEM_LIMIT = 56 * 1024 * 1024


def _cparams(sem):
    return pltpu.CompilerParams(dimension_semantics=sem, vmem_limit_bytes=VMEM_LIMIT)


def _gelu(x):
    return 0.5 * x * (1.0 + lax.erf(x * math.sqrt(0.5)))


MOD_TN = 1024


def _mod_kernel(c_ref, w_ref, b_ref, o_ref):
    ca = jax.nn.silu(c_ref[...])
    o_ref[...] = jnp.dot(ca.astype(BF16), w_ref[...].astype(BF16),
                         preferred_element_type=F32) + b_ref[...]


def _modulation(c, mod_w, mod_b):
    n_cols = N_MOD * D_MODEL
    return pl.pallas_call(
        _mod_kernel,
        grid=(DEPTH, n_cols // MOD_TN),
        in_specs=[
            pl.BlockSpec((BATCH, D_MODEL), lambda l, j: (0, 0)),
            pl.BlockSpec((None, D_MODEL, MOD_TN), lambda l, j: (l, 0, j)),
            pl.BlockSpec((None, 1, MOD_TN), lambda l, j: (l, 0, j)),
        ],
        out_specs=pl.BlockSpec((None, BATCH, MOD_TN), lambda l, j: (l, 0, j)),
        out_shape=jax.ShapeDtypeStruct((DEPTH, BATCH, n_cols), F32),
        compiler_params=_cparams(("arbitrary", "arbitrary")),
        name="modulation",
    )(c, mod_w, mod_b.reshape(DEPTH, 1, n_cols))


K1_TM = 512


def _front_kernel(x_ref, mod_ref, n1g_ref, win_ref, lng_ref, lnb_ref, ws_ref, bs_ref, ga_ref,
                  a_ref, q_ref, k_ref, v_ref):
    x = x_ref[...]
    tm = x.shape[0]
    h = x * lax.rsqrt(jnp.mean(x * x, axis=-1, keepdims=True) + EPS) * n1g_ref[...]
    h = h * (1.0 + mod_ref[1:2, :]) + mod_ref[0:1, :]
    proj = jnp.dot(h.astype(BF16), win_ref[...], preferred_element_type=F32)

    q_ref[...] = proj[:, 2 * A_WIDTH:2 * A_WIDTH + B_WIDTH] * (HEAD_DIM ** -0.5)
    k_ref[...] = proj[:, 2 * A_WIDTH + B_WIDTH:2 * A_WIDTH + 2 * B_WIDTH]
    v_ref[...] = proj[:, 2 * A_WIDTH + 2 * B_WIDTH:]

    u = _gelu(proj[:, :A_WIDTH])
    va = _gelu(proj[:, A_WIDTH:2 * A_WIDTH])
    mu = jnp.mean(va, axis=-1, keepdims=True)
    vc = va - mu
    vln = vc * lax.rsqrt(jnp.mean(vc * vc, axis=-1, keepdims=True) + EPS)
    vln = (vln * lng_ref[...] + lnb_ref[...]).astype(BF16)

    row = lax.broadcasted_iota(jnp.int32, (2 * CHUNK, CHUNK), 0)
    col = lax.broadcasted_iota(jnp.int32, (2 * CHUNK, CHUNK), 1)
    causal = (row % CHUNK) >= col
    first_group = lax.broadcasted_iota(jnp.int32, (CHUNK, LANES), 1) < HEAD_DIM
    wmix = [jnp.where(causal, ws_ref[p], jnp.zeros((), BF16)) for p in range(A_GROUPS // 2)]
    for c in range(tm // CHUNK):
        rows = slice(c * CHUNK, (c + 1) * CHUNK)
        parts = []
        for p in range(A_GROUPS // 2):
            vp = vln[rows, p * LANES:(p + 1) * LANES]
            r = jnp.dot(wmix[p], vp, preferred_element_type=F32)
            parts.append(jnp.where(first_group, r[:CHUNK], r[CHUNK:]))
        s = jnp.concatenate(parts, axis=-1) + bs_ref[...]
        oa = u[rows] * s
        oa = oa * lax.rsqrt(jnp.mean(oa * oa, axis=-1, keepdims=True) + EPS) * ga_ref[...]
        a_ref[rows, :] = oa.astype(BF16)


def _front(x, mod_l, n1g, win, lng, lnb, ws2, bs_exp, ga):
    tm = K1_TM
    tiles_per_seq = SEQ // tm
    row1024 = pl.BlockSpec((1, D_MODEL), lambda i: (0, 0))
    row512 = pl.BlockSpec((1, A_WIDTH), lambda i: (0, 0))
    half = pl.BlockSpec((tm, A_WIDTH), lambda i: (i, 0))
    return pl.pallas_call(
        _front_kernel,
        grid=(TOKENS // tm,),
        in_specs=[
            pl.BlockSpec((tm, D_MODEL), lambda i: (i, 0)),
            pl.BlockSpec((None, N_MOD, D_MODEL), lambda i: (i // tiles_per_seq, 0, 0)),
            row1024,
            pl.BlockSpec((D_MODEL, IN_COLS), lambda i: (0, 0)),
            row512, row512,
            pl.BlockSpec((A_GROUPS // 2, 2 * CHUNK, CHUNK), lambda i: (0, 0, 0)),
            pl.BlockSpec((CHUNK, A_WIDTH), lambda i: (0, 0)),
            row512,
        ],
        out_specs=[half, half, half, half],
        out_shape=[jax.ShapeDtypeStruct((TOKENS, A_WIDTH), BF16)] +
                  [jax.ShapeDtypeStruct((TOKENS, B_WIDTH), F32)] * 3,
        compiler_params=_cparams(("arbitrary",)),
        name="front",
    )(x, mod_l, n1g, win, lng, lnb, ws2, bs_exp, ga)


def _t5_bucket_np(dist):
    dist = np.maximum(dist, 0)
    ratio = np.log(np.maximum(dist, 1) / REL_MAX_EXACT) / np.log(REL_MAX_DISTANCE / REL_MAX_EXACT)
    large = REL_MAX_EXACT + np.floor(ratio * (REL_BUCKETS - REL_MAX_EXACT)).astype(np.int64)
    large = np.minimum(large, REL_BUCKETS - 1)
    return np.where(dist < REL_MAX_EXACT, dist, large).astype(np.int32)


def _bias_tables(rel_bias):
    blk = ATTN_BLOCK
    n_rel = 3 * blk
    rel = 2 * blk - 1 - np.arange(n_rel)
    first = jnp.asarray(np.arange(2 * blk)[None, :] >= blk)
    tables = []
    for window, d in DILATED_CONFIGS:
        span = window // d
        bucket = _t5_bucket_np(np.clip(rel, 0, span) * d)
        valid = jnp.asarray((rel >= 0) & (rel <= span))
        w = jnp.where(valid[None, :], rel_bias.astype(F32)[bucket].T, NEG_INF)
        flat = jnp.tile(w, (1, blk))
        skew = flat[:, blk - 1:blk - 1 + blk * (n_rel - 1)].reshape(-1, blk, n_rel - 1)
        tab = skew[:, :, :2 * blk]
        tables.append(jnp.stack([tab, jnp.where(first[None], tab, NEG_INF)], axis=0))
    return jnp.stack(tables, axis=0)


ATTN_GROUP = 4


def _attn_kernel(q_ref, k_ref, v_ref, bias_ref, o_ref, m_sc, l_sc, acc_sc):
    blk = ATTN_BLOCK
    grp = ATTN_GROUP
    lane = lax.broadcasted_iota(jnp.int32, (blk, LANES), 1)
    head0 = lane < HEAD_DIM
    ones = jnp.ones((2 * blk, LANES), BF16)

    def merge_heads(t):
        return jnp.where(head0, t[:blk], t[blk:])

    def scores_block(q, kcat, vcat, bias):
        nk = kcat.shape[0]
        zero = jnp.zeros_like(q)
        qs = jnp.concatenate([jnp.where(head0, q, zero), jnp.where(head0, zero, q)],
                             axis=0).astype(BF16)
        s = lax.dot_general(qs, kcat, (((1,), (1,)), ((), ())), preferred_element_type=F32) + bias
        m = jnp.max(s, axis=-1, keepdims=True)
        p = jnp.exp(s - m).astype(BF16)
        pv = jnp.dot(p, jnp.concatenate([vcat, ones[:nk]], axis=1), preferred_element_type=F32)
        return (merge_heads(jnp.broadcast_to(m, (2 * blk, LANES))),
                merge_heads(pv[:, LANES:]), merge_heads(pv[:, :LANES]))

    def update(ci, rows_list, results):
        if ci == 0:
            for rows, (m_c, l_c, o_c) in zip(rows_list, results):
                m_sc[rows, :] = m_c
                l_sc[rows, :] = l_c
                acc_sc[rows, :] = o_c
            return
        olds = [(m_sc[rows, :], l_sc[rows, :], acc_sc[rows, :]) for rows in rows_list]
        news = []
        for (m_r, l_r, a_r), (m_c, l_c, o_c) in zip(olds, results):
            m_n = jnp.maximum(m_r, m_c)
            a = jnp.exp(m_r - m_n)
            b = jnp.exp(m_c - m_n)
            news.append((m_n, a * l_r + b * l_c, a * a_r + b * o_c))
        for rows, (m_n, l_n, a_n) in zip(rows_list, news):
            m_sc[rows, :] = m_n
            l_sc[rows, :] = l_n
            acc_sc[rows, :] = a_n

    def head_bias(ci, first, nk):
        if nk == 2 * blk:
            return jnp.concatenate([bias_ref[ci, first, 0], bias_ref[ci, first, 1]], axis=0)
        return jnp.concatenate([bias_ref[ci, 0, 0, :, blk:], bias_ref[ci, 0, 1, :, blk:]], axis=0)

    for ci, (window, d) in enumerate(DILATED_CONFIGS):
        nb = SEQ // d // blk

        def rows_of(r, n, d=d):
            if d == 1:
                return pl.ds(pl.multiple_of(n * blk, blk), blk)
            return pl.ds(r + n * (blk * d), blk, stride=d)

        if nb >= grp:
            def trip(j, carry, ci=ci, nb=nb, rows_of=rows_of):
                r = j // (nb // grp)
                n0 = (j % (nb // grp)) * grp
                k_prev = v_prev = None
                if nb > grp:
                    prev_rows = rows_of(r, jnp.maximum(n0 - 1, 0))
                    k_prev = k_ref[prev_rows, :].astype(BF16)
                    v_prev = v_ref[prev_rows, :].astype(BF16)
                first = jnp.where(n0 == 0, 1, 0)
                rows_list, results = [], []
                for g in range(grp):
                    rows = rows_of(r, n0 + g)
                    k_cur = k_ref[rows, :].astype(BF16)
                    v_cur = v_ref[rows, :].astype(BF16)
                    if k_prev is None:
                        results.append(scores_block(q_ref[rows, :], k_cur, v_cur,
                                                    head_bias(ci, 0, blk)))
                    else:
                        results.append(scores_block(
                            q_ref[rows, :], jnp.concatenate([k_prev, k_cur], axis=0),
                            jnp.concatenate([v_prev, v_cur], axis=0),
                            head_bias(ci, first if g == 0 else 0, 2 * blk)))
                    rows_list.append(rows)
                    k_prev, v_prev = k_cur, v_cur
                update(ci, rows_list, results)
                return carry

            lax.fori_loop(0, d * nb // grp, trip, 0)
        else:
            assert nb == 1

            def trip(j, carry, ci=ci, rows_of=rows_of):
                rows_list, results = [], []
                for g in range(grp):
                    rows = rows_of(j * grp + g, 0)
                    results.append(scores_block(q_ref[rows, :], k_ref[rows, :].astype(BF16),
                                                v_ref[rows, :].astype(BF16), head_bias(ci, 0, blk)))
                    rows_list.append(rows)
                update(ci, rows_list, results)
                return carry

            lax.fori_loop(0, d // grp, trip, 0)

    o_ref[...] = acc_sc[...] / l_sc[...]


def _attention(q, k, v, bias_tab):
    n_cfg = len(DILATED_CONFIGS)
    blk = ATTN_BLOCK
    seq_spec = pl.BlockSpec((SEQ, HEAD_PAIR), lambda b, p: (b, p))
    return pl.pallas_call(
        _attn_kernel,
        grid=(BATCH, N_PAIRS),
        in_specs=[seq_spec, seq_spec, seq_spec,
                  pl.BlockSpec((n_cfg, 2, 2, blk, 2 * blk), lambda b, p: (0, 0, p, 0, 0))],
        out_specs=seq_spec,
        out_shape=jax.ShapeDtypeStruct((TOKENS, B_WIDTH), F32),
        scratch_shapes=[pltpu.VMEM((SEQ, HEAD_PAIR), F32)] * 3,
        compiler_params=_cparams(("arbitrary", "arbitrary")),
        name="attention",
    )(q, k, v, bias_tab)


K4_TM = 512


def _top2_sum(a, b, c, d):
    hi1, lo1 = jnp.maximum(a, b), jnp.minimum(a, b)
    hi2, lo2 = jnp.maximum(c, d), jnp.minimum(c, d)
    return jnp.maximum(hi1, hi2) + jnp.maximum(jnp.minimum(hi1, hi2), jnp.maximum(lo1, lo2))


def _route(logits_t, rb_col):
    m = jnp.max(logits_t, axis=0, keepdims=True)
    e = jnp.exp(logits_t - m)
    probs = e / jnp.sum(e, axis=0, keepdims=True)
    sel = probs + rb_col
    sel_rows = [sel[i:i + 1, :] for i in range(N_EXPERTS)]
    prob_rows = [probs[i:i + 1, :] for i in range(N_EXPERTS)]
    gsz = EXPERTS_PER_GROUP
    score = [_top2_sum(*sel_rows[g * gsz:(g + 1) * gsz]) for g in range(N_EXPERT_GROUPS)]
    chosen = []
    for g in range(N_EXPERT_GROUPS):
        best = None
        for g2 in range(N_EXPERT_GROUPS):
            if g2 == g:
                continue
            c = (score[g] > score[g2]) if g2 < g else (score[g] >= score[g2])
            best = c if best is None else jnp.logical_and(best, c)
        for i in range(gsz):
            ei = g * gsz + i
            rank = jnp.zeros_like(sel_rows[ei])
            for j in range(gsz):
                if j == i:
                    continue
                ej = g * gsz + j
                ahead = (sel_rows[ej] >= sel_rows[ei]) if j < i else (sel_rows[ej] > sel_rows[ei])
                rank = rank + jnp.where(ahead, 1.0, 0.0)
            chosen.append(jnp.logical_and(best, rank < float(2)))
    picked = [jnp.where(chosen[i], prob_rows[i], 0.0) for i in range(N_EXPERTS)]
    denom = picked[0]
    for i in range(1, N_EXPERTS):
        denom = denom + picked[i]
    gates = jnp.concatenate([pk / denom for pk in picked], axis=0)
    onehot = jnp.concatenate([jnp.where(ch, 1.0, 0.0) for ch in chosen], axis=0)
    return gates, onehot


def _mid_kernel(a_ref, ob_ref, x_ref, mod_ref, gb_ref, wout_ref, n2g_ref, rwt_ref, rb_ref,
                x1_ref, h2_ref, gates_ref, onehot_ref, count_ref):
    ob = ob_ref[...]
    bn = ob * lax.rsqrt(jnp.mean(ob * ob, axis=-1, keepdims=True) + EPS) * gb_ref[...]
    mixed = jnp.dot(a_ref[...], wout_ref[:A_WIDTH, :], preferred_element_type=F32)
    mixed = mixed + jnp.dot(bn.astype(BF16), wout_ref[A_WIDTH:, :], preferred_element_type=F32)
    x1 = x_ref[...] + mod_ref[2:3, :] * mixed
    x1_ref[...] = x1
    h2 = x1 * lax.rsqrt(jnp.mean(x1 * x1, axis=-1, keepdims=True) + EPS) * n2g_ref[...]
    h2 = (h2 * (1.0 + mod_ref[4:5, :]) + mod_ref[3:4, :]).astype(BF16)
    h2_ref[...] = h2
    logits_t = lax.dot_general(rwt_ref[...], h2, (((1,), (1,)), ((), ())),
                               preferred_element_type=F32)
    gates_t, onehot_t = _route(logits_t, rb_ref[...])
    gates_ref[...] = gates_t
    onehot_ref[...] = onehot_t
    for s in range(count_ref.shape[0]):
        count_ref[s] = jnp.sum(onehot_t[:, s * MOE_TOK:(s + 1) * MOE_TOK], axis=1, keepdims=True)


def _mid(out_a, out_b, x, mod_l, gb, wout, n2g, rwt, rb_col):
    tm = K4_TM
    tiles_per_seq = SEQ // tm
    tok = pl.BlockSpec((tm, D_MODEL), lambda i: (i, 0))
    half = pl.BlockSpec((tm, A_WIDTH), lambda i: (i, 0))
    route = pl.BlockSpec((N_EXPERTS, tm), lambda i: (0, i))
    return pl.pallas_call(
        _mid_kernel,
        grid=(TOKENS // tm,),
        in_specs=[
            half, half, tok,
            pl.BlockSpec((None, N_MOD, D_MODEL), lambda i: (i // tiles_per_seq, 0, 0)),
            pl.BlockSpec((1, B_WIDTH), lambda i: (0, 0)),
            pl.BlockSpec((D_MODEL, D_MODEL), lambda i: (0, 0)),
            pl.BlockSpec((1, D_MODEL), lambda i: (0, 0)),
            pl.BlockSpec((N_EXPERTS, D_MODEL), lambda i: (0, 0)),
            pl.BlockSpec((N_EXPERTS, 1), lambda i: (0, 0)),
        ],
        out_specs=[tok, tok, route, route,
                   pl.BlockSpec((tm // MOE_TOK, N_EXPERTS, 1), lambda i: (i, 0, 0))],
        out_shape=[jax.ShapeDtypeStruct((TOKENS, D_MODEL), F32),
                   jax.ShapeDtypeStruct((TOKENS, D_MODEL), BF16),
                   jax.ShapeDtypeStruct((N_EXPERTS, TOKENS), F32),
                   jax.ShapeDtypeStruct((N_EXPERTS, TOKENS), F32),
                   jax.ShapeDtypeStruct((TOKENS // MOE_TOK, N_EXPERTS, 1), F32)],
        compiler_params=_cparams(("arbitrary",)),
        name="mid",
    )(out_a, out_b, x, mod_l, gb, wout, n2g, rwt, rb_col)


MOE_TOK = 256
ROW_ALIGN = 8
MOE_LB = 2 * MOE_TOK + N_EXPERTS * ROW_ALIGN
FFN_TM = 512
N_TOK_TILES = TOKENS // MOE_TOK
ROWS_MAX = -(-(2 * TOKENS + N_TOK_TILES * N_EXPERTS * (ROW_ALIGN - 1)
               + N_EXPERTS * (FFN_TM - ROW_ALIGN)) // FFN_TM) * FFN_TM
RUN_SIZES = tuple(MOE_TOK >> s for s in range((MOE_TOK // ROW_ALIGN).bit_length()))


def _moe_plan(count):
    n = count.reshape(N_TOK_TILES, N_EXPERTS).astype(jnp.int32)
    n8 = (n + (ROW_ALIGN - 1)) // ROW_ALIGN * ROW_ALIGN
    local = jnp.cumsum(n8, axis=1) - n8
    seg = (jnp.sum(n8, axis=0) + (FFN_TM - 1)) // FFN_TM * FFN_TM
    seg_end = jnp.cumsum(seg)
    glob = (seg_end - seg)[None, :] + jnp.cumsum(n8, axis=0) - n8
    n_ffn = ROWS_MAX // FFN_TM
    n_active = seg_end[-1] // FFN_TM
    tile_row = jnp.arange(n_ffn, dtype=jnp.int32) * FFN_TM
    texp = jnp.sum((tile_row[:, None] >= seg_end[None, :]).astype(jnp.int32), axis=1)
    texp = jnp.minimum(texp, N_EXPERTS - 1)
    texp = jnp.where(jnp.arange(n_ffn) < n_active, texp, texp[n_active - 1])
    return dict(n8=n8.reshape(-1), local=local.reshape(-1), glob=glob.reshape(-1),
                local_col=local.astype(F32).reshape(N_TOK_TILES, N_EXPERTS, 1),
                texp=texp.astype(jnp.int32), n_active=n_active.reshape(1).astype(jnp.int32))


def _run_copies(tile, n8_ref, local_ref, glob_ref, make_copy, start):
    for e in range(N_EXPERTS):
        idx = tile * N_EXPERTS + e
        n = n8_ref[idx]
        lo = local_ref[idx]
        go = glob_ref[idx]
        done = jnp.int32(0)
        for size in RUN_SIZES:
            hit = (n & size) != 0

            @pl.when(hit)
            def _(done=done, size=size, lo=lo, go=go):
                cp = make_copy(pl.multiple_of(lo + done, ROW_ALIGN),
                               pl.multiple_of(go + done, ROW_ALIGN), size)
                if start:
                    cp.start()
                else:
                    cp.wait()

            done = done + jnp.where(hit, size, 0)


def _sorted_rows(onehot, local_col):
    tm = onehot.shape[1]
    src = lax.broadcasted_iota(jnp.int32, (tm, tm), 0)
    dst = lax.broadcasted_iota(jnp.int32, (tm, tm), 1)
    before = jnp.where(src < dst, 1.0, 0.0).astype(BF16)
    rank = jnp.dot(onehot.astype(BF16), before, preferred_element_type=F32)
    pos = local_col + rank
    chosen = onehot > 0.5
    lo = jnp.min(jnp.where(chosen, pos, float(MOE_LB)), axis=0, keepdims=True)
    hi = jnp.max(jnp.where(chosen, pos, -1.0), axis=0, keepdims=True)
    return pos, chosen, lo, hi


def _dispatch_kernel(n8_ref, local_ref, glob_ref, h_ref, oh_ref, lcol_ref, xs_ref, lbuf, sem):
    j = pl.program_id(0)
    last = pl.num_programs(0) - 1
    slot = j % 2
    _, _, lo, hi = _sorted_rows(oh_ref[...], lcol_ref[...])
    row = lax.broadcasted_iota(jnp.int32, (MOE_LB, MOE_TOK), 0).astype(F32)
    perm = (jnp.where(row == lo, 1.0, 0.0) + jnp.where(row == hi, 1.0, 0.0)).astype(BF16)
    lbuf[slot] = jnp.dot(perm, h_ref[...], preferred_element_type=F32)

    def copy_from(slot_):
        def make(lo_, go_, size):
            return pltpu.make_async_copy(lbuf.at[slot_, pl.ds(lo_, size), :],
                                         xs_ref.at[pl.ds(go_, size), :], sem.at[slot_])
        return make

    refs = (n8_ref, local_ref, glob_ref)
    _run_copies(j, *refs, copy_from(slot), True)

    @pl.when(j > 0)
    def _():
        _run_copies(j - 1, *refs, copy_from(1 - slot), False)

    @pl.when(j == last)
    def _():
        _run_copies(j, *refs, copy_from(slot), False)


def _dispatch(plan, h2, onehot):
    grid_spec = pltpu.PrefetchScalarGridSpec(
        num_scalar_prefetch=3,
        grid=(N_TOK_TILES,),
        in_specs=[
            pl.BlockSpec((MOE_TOK, D_MODEL), lambda j, *_: (j, 0)),
            pl.BlockSpec((N_EXPERTS, MOE_TOK), lambda j, *_: (0, j)),
            pl.BlockSpec((None, N_EXPERTS, 1), lambda j, *_: (j, 0, 0)),
        ],
        out_specs=pl.BlockSpec(memory_space=pl.ANY),
        scratch_shapes=[pltpu.VMEM((2, MOE_LB, D_MODEL), F32), pltpu.SemaphoreType.DMA((2,))],
    )
    return pl.pallas_call(
        _dispatch_kernel,
        grid_spec=grid_spec,
        out_shape=jax.ShapeDtypeStruct((ROWS_MAX, D_MODEL), F32),
        compiler_params=_cparams(("arbitrary",)),
        name="dispatch",
    )(plan["n8"], plan["local"], plan["glob"], h2, onehot, plan["local_col"])


def _experts_kernel(texp_ref, nact_ref, xs_ref, wg_ref, wu_ref, wd_ref, ys_ref, wg_b, wu_b, wd_b):
    i = pl.program_id(0)
    active = i < nact_ref[0]
    new_expert = jnp.logical_or(i == 0, texp_ref[i] != texp_ref[jnp.maximum(i - 1, 0)])

    @pl.when(jnp.logical_and(active, new_expert))
    def _():
        wg_b[...] = wg_ref[...].astype(BF16)
        wu_b[...] = wu_ref[...].astype(BF16)
        wd_b[...] = wd_ref[...].astype(BF16)

    @pl.when(active)
    def _():
        x = xs_ref[...].astype(BF16)
        hg = jnp.dot(x, wg_b[...], preferred_element_type=F32)
        hu = jnp.dot(x, wu_b[...], preferred_element_type=F32)
        act = (jax.nn.silu(hg) * hu).astype(BF16)
        ys_ref[...] = jnp.dot(act, wd_b[...], preferred_element_type=F32)


def _experts(plan, xs, layer, wg, wu, wd):
    def rows(i, texp, nact):
        return (jnp.minimum(i, nact[0] - 1), 0)

    def expert(i, texp, nact):
        return (layer, texp[i], 0, 0)

    grid_spec = pltpu.PrefetchScalarGridSpec(
        num_scalar_prefetch=2,
        grid=(ROWS_MAX // FFN_TM,),
        in_specs=[
            pl.BlockSpec((FFN_TM, D_MODEL), rows),
            pl.BlockSpec((None, None, D_MODEL, D_EXPERT), expert),
            pl.BlockSpec((None, None, D_MODEL, D_EXPERT), expert),
            pl.BlockSpec((None, None, D_EXPERT, D_MODEL), expert),
        ],
        out_specs=pl.BlockSpec((FFN_TM, D_MODEL), rows),
        scratch_shapes=[pltpu.VMEM((D_MODEL, D_EXPERT), BF16), pltpu.VMEM((D_MODEL, D_EXPERT), BF16),
                        pltpu.VMEM((D_EXPERT, D_MODEL), BF16)],
    )
    return pl.pallas_call(
        _experts_kernel,
        grid_spec=grid_spec,
        out_shape=jax.ShapeDtypeStruct((ROWS_MAX, D_MODEL), F32),
        compiler_params=_cparams(("arbitrary",)),
        name="experts",
    )(plan["texp"], plan["n_active"], xs, wg, wu, wd)


def _combine_kernel(n8_ref, local_ref, glob_ref, oh_ref, g_ref, lcol_ref, x_ref, mod_ref, fg_ref,
                    ys_ref, o_ref, ybuf, sem, *, final_norm):
    j = pl.program_id(0)
    n_tiles = pl.num_programs(0)
    slot = j % 2
    refs = (n8_ref, local_ref, glob_ref)

    def copy_to(slot_):
        def make(lo_, go_, size):
            return pltpu.make_async_copy(ys_ref.at[pl.ds(go_, size), :],
                                         ybuf.at[slot_, pl.ds(lo_, size), :], sem.at[slot_])
        return make

    @pl.when(j == 0)
    def _():
        ybuf[...] = jnp.zeros(ybuf.shape, F32)
        _run_copies(0, *refs, copy_to(0), True)

    @pl.when(j + 1 < n_tiles)
    def _():
        _run_copies(j + 1, *refs, copy_to(1 - slot), True)

    pos, chosen, lo, hi = _sorted_rows(oh_ref[...], lcol_ref[...])
    gate = jnp.where(chosen, g_ref[...], 0.0)
    g_lo = jnp.sum(jnp.where(pos == lo, gate, 0.0), axis=0, keepdims=True)
    g_hi = jnp.sum(jnp.where(pos == hi, gate, 0.0), axis=0, keepdims=True)
    row = lax.broadcasted_iota(jnp.int32, (MOE_LB, MOE_TOK), 0).astype(F32)
    perm = (jnp.where(row == lo, g_lo, 0.0) + jnp.where(row == hi, g_hi, 0.0)).astype(BF16)

    _run_copies(j, *refs, copy_to(slot), False)
    y = lax.dot_general(perm, ybuf[slot].astype(BF16), (((0,), (0,)), ((), ())),
                        preferred_element_type=F32)
    x = x_ref[...] + mod_ref[5:6, :] * y
    if final_norm:
        x = x * lax.rsqrt(jnp.mean(x * x, axis=-1, keepdims=True) + EPS) * fg_ref[...]
    o_ref[...] = x


def _combine(plan, onehot, gates, ys, x, mod_l, final_g, final_norm):
    tiles_per_seq = SEQ // MOE_TOK
    route = pl.BlockSpec((N_EXPERTS, MOE_TOK), lambda j, *_: (0, j))
    tok = pl.BlockSpec((MOE_TOK, D_MODEL), lambda j, *_: (j, 0))
    grid_spec = pltpu.PrefetchScalarGridSpec(
        num_scalar_prefetch=3,
        grid=(N_TOK_TILES,),
        in_specs=[
            route, route,
            pl.BlockSpec((None, N_EXPERTS, 1), lambda j, *_: (j, 0, 0)),
            tok,
            pl.BlockSpec((None, N_MOD, D_MODEL), lambda j, *_: (j // tiles_per_seq, 0, 0)),
            pl.BlockSpec((1, D_MODEL), lambda j, *_: (0, 0)),
            pl.BlockSpec(memory_space=pl.ANY),
        ],
        out_specs=tok,
        scratch_shapes=[pltpu.VMEM((2, MOE_LB, D_MODEL), F32), pltpu.SemaphoreType.DMA((2,))],
    )
    return pl.pallas_call(
        functools.partial(_combine_kernel, final_norm=final_norm),
        grid_spec=grid_spec,
        out_shape=jax.ShapeDtypeStruct((TOKENS, D_MODEL), F32),
        compiler_params=_cparams(("arbitrary",)),
        name="combine",
    )(plan["n8"], plan["local"], plan["glob"], onehot, gates, plan["local_col"], x, mod_l,
      final_g, ys)


def kernel(x, c, rel_bias, router_w, router_b, mod_w, mod_b, norm1_g, w_in, gmlp_ln_g, gmlp_ln_b,
           gmlp_ws, gmlp_bs, out_norm_a_g, out_norm_b_g, w_out, norm2_g, moe_w_gate, moe_w_up,
           moe_w_down, final_g):
    mod = _modulation(c, mod_w, mod_b).reshape(DEPTH, BATCH, N_MOD, D_MODEL)
    bias_tab = _bias_tables(rel_bias)
    rwt = router_w.T.astype(BF16)
    rb_col = router_b.reshape(N_EXPERTS, 1)
    xt = x.reshape(TOKENS, D_MODEL)
    win_b = w_in.astype(BF16)
    wout_b = w_out.astype(BF16)
    for l in range(DEPTH):
        ws2 = gmlp_ws[l].astype(BF16).reshape(A_GROUPS // 2, 2 * CHUNK, CHUNK)
        bs_exp = jnp.repeat(gmlp_bs[l].T, HEAD_DIM, axis=1)
        out_a, q, k, v = _front(
            xt, mod[l], norm1_g[l].reshape(1, -1), win_b[l], gmlp_ln_g[l].reshape(1, -1),
            gmlp_ln_b[l].reshape(1, -1), ws2, bs_exp, out_norm_a_g[l].reshape(1, -1))
        out_b = _attention(q, k, v, bias_tab)
        xt, h2, gates, onehot, count = _mid(
            out_a, out_b, xt, mod[l], out_norm_b_g[l].reshape(1, -1),
            wout_b[l], norm2_g[l].reshape(1, -1), rwt, rb_col)
        plan = _moe_plan(count)
        xs = _dispatch(plan, h2, onehot)
        ys = _experts(plan, xs, l, moe_w_gate, moe_w_up, moe_w_down)
        xt = _combine(plan, onehot, gates, ys, xt, mod[l], final_g.reshape(1, -1),
                      final_norm=(l == DEPTH - 1))
    return xt.reshape(BATCH, SEQ, D_MODEL)
```

```python
import functools
import math

import numpy as np
import jax
import jax.numpy as jnp
from jax import lax
from jax.experimental import pallas as pl
from jax.experimental.pallas import tpu as pltpu

D_MODEL = 1024
BATCH = 8
SEQ = 2048
DEPTH = 2
TOKENS = BATCH * SEQ
HEAD_DIM = 64
A_WIDTH = 512
B_WIDTH = 512
A_GROUPS = 8
IN_COLS = 2 * A_WIDTH + 3 * B_WIDTH
CHUNK = 128
DILATED_CONFIGS = ((128, 1), (512, 4), (2048, 16))
ATTN_BLOCK = 128
REL_BUCKETS = 32
REL_MAX_EXACT = REL_BUCKETS // 2
REL_MAX_DISTANCE = 2048
N_EXPERTS = 16
N_EXPERT_GROUPS = 4
EXPERTS_PER_GROUP = 4
D_EXPERT = 512
N_MOD = 6
EPS = 1e-6
NEG_INF = -1e30

LANES = 128
HEAD_PAIR = 2 * HEAD_DIM
N_PAIRS = B_WIDTH // HEAD_PAIR

F32 = jnp.float32
BF16 = jnp.bfloat16

VMEM_LIMIT = 56 * 1024 * 1024


def _cparams(sem):
    return pltpu.CompilerParams(dimension_semantics=sem, vmem_limit_bytes=VMEM_LIMIT)


def _gelu(x):
    return 0.5 * x * (1.0 + lax.erf(x * math.sqrt(0.5)))


MOD_TN = 1024


def _mod_kernel(c_ref, w_ref, b_ref, o_ref):
    ca = jax.nn.silu(c_ref[...])
    o_ref[...] = jnp.dot(ca.astype(BF16), w_ref[...].astype(BF16),
                         preferred_element_type=F32) + b_ref[...]


def _modulation(c, mod_w, mod_b):
    n_cols = N_MOD * D_MODEL
    return pl.pallas_call(
        _mod_kernel,
        grid=(DEPTH, n_cols // MOD_TN),
        in_specs=[
            pl.BlockSpec((BATCH, D_MODEL), lambda l, j: (0, 0)),
            pl.BlockSpec((None, D_MODEL, MOD_TN), lambda l, j: (l, 0, j)),
            pl.BlockSpec((None, 1, MOD_TN), lambda l, j: (l, 0, j)),
        ],
        out_specs=pl.BlockSpec((None, BATCH, MOD_TN), lambda l, j: (l, 0, j)),
        out_shape=jax.ShapeDtypeStruct((DEPTH, BATCH, n_cols), F32),
        compiler_params=_cparams(("arbitrary", "arbitrary")),
        name="modulation",
    )(c, mod_w, mod_b.reshape(DEPTH, 1, n_cols))


K1_TM = 512


def _front_kernel(x_ref, mod_ref, n1g_ref, win_ref, lng_ref, lnb_ref, ws_ref, bs_ref, ga_ref,
                  a_ref, q_ref, k_ref, v_ref):
    x = x_ref[...]
    tm = x.shape[0]
    h = x * lax.rsqrt(jnp.mean(x * x, axis=-1, keepdims=True) + EPS) * n1g_ref[...]
    h = h * (1.0 + mod_ref[1:2, :]) + mod_ref[0:1, :]
    proj = jnp.dot(h.astype(BF16), win_ref[...], preferred_element_type=F32)

    q_ref[...] = proj[:, 2 * A_WIDTH:2 * A_WIDTH + B_WIDTH] * (HEAD_DIM ** -0.5)
    k_ref[...] = proj[:, 2 * A_WIDTH + B_WIDTH:2 * A_WIDTH + 2 * B_WIDTH]
    v_ref[...] = proj[:, 2 * A_WIDTH + 2 * B_WIDTH:]

    u = _gelu(proj[:, :A_WIDTH])
    va = _gelu(proj[:, A_WIDTH:2 * A_WIDTH])
    mu = jnp.mean(va, axis=-1, keepdims=True)
    vc = va - mu
    vln = vc * lax.rsqrt(jnp.mean(vc * vc, axis=-1, keepdims=True) + EPS)
    vln = (vln * lng_ref[...] + lnb_ref[...]).astype(BF16)

    row = lax.broadcasted_iota(jnp.int32, (2 * CHUNK, CHUNK), 0)
    col = lax.broadcasted_iota(jnp.int32, (2 * CHUNK, CHUNK), 1)
    causal = (row % CHUNK) >= col
    first_group = lax.broadcasted_iota(jnp.int32, (CHUNK, LANES), 1) < HEAD_DIM
    wmix = [jnp.where(causal, ws_ref[p], jnp.zeros((), BF16)) for p in range(A_GROUPS // 2)]
    for c in range(tm // CHUNK):
        rows = slice(c * CHUNK, (c + 1) * CHUNK)
        parts = []
        for p in range(A_GROUPS // 2):
            vp = vln[rows, p * LANES:(p + 1) * LANES]
            r = jnp.dot(wmix[p], vp, preferred_element_type=F32)
            parts.append(jnp.where(first_group, r[:CHUNK], r[CHUNK:]))
        s = jnp.concatenate(parts, axis=-1) + bs_ref[...]
        oa = u[rows] * s
        oa = oa * lax.rsqrt(jnp.mean(oa * oa, axis=-1, keepdims=True) + EPS) * ga_ref[...]
        a_ref[rows, :] = oa.astype(BF16)


def _front(x, mod_l, n1g, win, lng, lnb, ws2, bs_exp, ga):
    tm = K1_TM
    tiles_per_seq = SEQ // tm
    row1024 = pl.BlockSpec((1, D_MODEL), lambda i: (0, 0))
    row512 = pl.BlockSpec((1, A_WIDTH), lambda i: (0, 0))
    half = pl.BlockSpec((tm, A_WIDTH), lambda i: (i, 0))
    return pl.pallas_call(
        _front_kernel,
        grid=(TOKENS // tm,),
        in_specs=[
            pl.BlockSpec((tm, D_MODEL), lambda i: (i, 0)),
            pl.BlockSpec((None, N_MOD, D_MODEL), lambda i: (i // tiles_per_seq, 0, 0)),
            row1024,
            pl.BlockSpec((D_MODEL, IN_COLS), lambda i: (0, 0)),
            row512, row512,
            pl.BlockSpec((A_GROUPS // 2, 2 * CHUNK, CHUNK), lambda i: (0, 0, 0)),
            pl.BlockSpec((CHUNK, A_WIDTH), lambda i: (0, 0)),
            row512,
        ],
        out_specs=[half, half, half, half],
        out_shape=[jax.ShapeDtypeStruct((TOKENS, A_WIDTH), BF16)] +
                  [jax.ShapeDtypeStruct((TOKENS, B_WIDTH), F32)] * 3,
        compiler_params=_cparams(("arbitrary",)),
        name="front",
    )(x, mod_l, n1g, win, lng, lnb, ws2, bs_exp, ga)


def _t5_bucket_np(dist):
    dist = np.maximum(dist, 0)
    ratio = np.log(np.maximum(dist, 1) / REL_MAX_EXACT) / np.log(REL_MAX_DISTANCE / REL_MAX_EXACT)
    large = REL_MAX_EXACT + np.floor(ratio * (REL_BUCKETS - REL_MAX_EXACT)).astype(np.int64)
    large = np.minimum(large, REL_BUCKETS - 1)
    return np.where(dist < REL_MAX_EXACT, dist, large).astype(np.int32)


def _bias_tables(rel_bias):
    blk = ATTN_BLOCK
    n_rel = 3 * blk
    rel = 2 * blk - 1 - np.arange(n_rel)
    first = jnp.asarray(np.arange(2 * blk)[None, :] >= blk)
    tables = []
    for window, d in DILATED_CONFIGS:
        span = window // d
        bucket = _t5_bucket_np(np.clip(rel, 0, span) * d)
        valid = jnp.asarray((rel >= 0) & (rel <= span))
        w = jnp.where(valid[None, :], rel_bias.astype(F32)[bucket].T, NEG_INF)
        flat = jnp.tile(w, (1, blk))
        skew = flat[:, blk - 1:blk - 1 + blk * (n_rel - 1)].reshape(-1, blk, n_rel - 1)
        tab = skew[:, :, :2 * blk]
        tables.append(jnp.stack([tab, jnp.where(first[None], tab, NEG_INF)], axis=0))
    return jnp.stack(tables, axis=0)


ATTN_GROUP = 4


def _attn_kernel(q_ref, k_ref, v_ref, bias_ref, o_ref, m_sc, l_sc, acc_sc):
    blk = ATTN_BLOCK
    grp = ATTN_GROUP
    lane = lax.broadcasted_iota(jnp.int32, (blk, LANES), 1)
    head0 = lane < HEAD_DIM
    ones = jnp.ones((2 * blk, LANES), BF16)

    def merge_heads(t):
        return jnp.where(head0, t[:blk], t[blk:])

    def scores_block(q, kcat, vcat, bias):
        nk = kcat.shape[0]
        zero = jnp.zeros_like(q)
        qs = jnp.concatenate([jnp.where(head0, q, zero), jnp.where(head0, zero, q)],
                             axis=0).astype(BF16)
        s = lax.dot_general(qs, kcat, (((1,), (1,)), ((), ())), preferred_element_type=F32) + bias
        m = jnp.max(s, axis=-1, keepdims=True)
        p = jnp.exp(s - m).astype(BF16)
        pv = jnp.dot(p, jnp.concatenate([vcat, ones[:nk]], axis=1), preferred_element_type=F32)
        return (merge_heads(jnp.broadcast_to(m, (2 * blk, LANES))),
                merge_heads(pv[:, LANES:]), merge_heads(pv[:, :LANES]))

    def update(ci, rows_list, results):
        if ci == 0:
            for rows, (m_c, l_c, o_c) in zip(rows_list, results):
                m_sc[rows, :] = m_c
                l_sc[rows, :] = l_c
                acc_sc[rows, :] = o_c
            return
        olds = [(m_sc[rows, :], l_sc[rows, :], acc_sc[rows, :]) for rows in rows_list]
        news = []
        for (m_r, l_r, a_r), (m_c, l_c, o_c) in zip(olds, results):
            m_n = jnp.maximum(m_r, m_c)
            a = jnp.exp(m_r - m_n)
            b = jnp.exp(m_c - m_n)
            news.append((m_n, a * l_r + b * l_c, a * a_r + b * o_c))
        for rows, (m_n, l_n, a_n) in zip(rows_list, news):
            m_sc[rows, :] = m_n
            l_sc[rows, :] = l_n
            acc_sc[rows, :] = a_n

    def head_bias(ci, first, nk):
        if nk == 2 * blk:
            return jnp.concatenate([bias_ref[ci, first, 0], bias_ref[ci, first, 1]], axis=0)
        return jnp.concatenate([bias_ref[ci, 0, 0, :, blk:], bias_ref[ci, 0, 1, :, blk:]], axis=0)

    for ci, (window, d) in enumerate(DILATED_CONFIGS):
        nb = SEQ // d // blk

        def rows_of(r, n, d=d):
            if d == 1:
                return pl.ds(pl.multiple_of(n * blk, blk), blk)
            return pl.ds(r + n * (blk * d), blk, stride=d)

        if nb >= grp:
            def trip(j, carry, ci=ci, nb=nb, rows_of=rows_of):
                r = j // (nb // grp)
                n0 = (j % (nb // grp)) * grp
                k_prev = v_prev = None
                if nb > grp:
                    prev_rows = rows_of(r, jnp.maximum(n0 - 1, 0))
                    k_prev = k_ref[prev_rows, :].astype(BF16)
                    v_prev = v_ref[prev_rows, :].astype(BF16)
                first = jnp.where(n0 == 0, 1, 0)
                rows_list, results = [], []
                for g in range(grp):
                    rows = rows_of(r, n0 + g)
                    k_cur = k_ref[rows, :].astype(BF16)
                    v_cur = v_ref[rows, :].astype(BF16)
                    if k_prev is None:
                        results.append(scores_block(q_ref[rows, :], k_cur, v_cur,
                                                    head_bias(ci, 0, blk)))
                    else:
                        results.append(scores_block(
                            q_ref[rows, :], jnp.concatenate([k_prev, k_cur], axis=0),
                            jnp.concatenate([v_prev, v_cur], axis=0),
                            head_bias(ci, first if g == 0 else 0, 2 * blk)))
                    rows_list.append(rows)
                    k_prev, v_prev = k_cur, v_cur
                update(ci, rows_list, results)
                return carry

            lax.fori_loop(0, d * nb // grp, trip, 0)
        else:
            assert nb == 1

            def trip(j, carry, ci=ci, rows_of=rows_of):
                rows_list, results = [], []
                for g in range(grp):
                    rows = rows_of(j * grp + g, 0)
                    results.append(scores_block(q_ref[rows, :], k_ref[rows, :].astype(BF16),
                                                v_ref[rows, :].astype(BF16), head_bias(ci, 0, blk)))
                    rows_list.append(rows)
                update(ci, rows_list, results)
                return carry

            lax.fori_loop(0, d // grp, trip, 0)

    o_ref[...] = acc_sc[...] / l_sc[...]


def _attention(q, k, v, bias_tab):
    n_cfg = len(DILATED_CONFIGS)
    blk = ATTN_BLOCK
    seq_spec = pl.BlockSpec((SEQ, HEAD_PAIR), lambda b, p: (b, p))
    return pl.pallas_call(
        _attn_kernel,
        grid=(BATCH, N_PAIRS),
        in_specs=[seq_spec, seq_spec, seq_spec,
                  pl.BlockSpec((n_cfg, 2, 2, blk, 2 * blk), lambda b, p: (0, 0, p, 0, 0))],
        out_specs=seq_spec,
        out_shape=jax.ShapeDtypeStruct((TOKENS, B_WIDTH), F32),
        scratch_shapes=[pltpu.VMEM((SEQ, HEAD_PAIR), F32)] * 3,
        compiler_params=_cparams(("arbitrary", "arbitrary")),
        name="attention",
    )(q, k, v, bias_tab)


K4_TM = 512


def _top2_sum(a, b, c, d):
    hi1, lo1 = jnp.maximum(a, b), jnp.minimum(a, b)
    hi2, lo2 = jnp.maximum(c, d), jnp.minimum(c, d)
    return jnp.maximum(hi1, hi2) + jnp.maximum(jnp.minimum(hi1, hi2), jnp.maximum(lo1, lo2))


def _route(logits_t, rb_col):
    m = jnp.max(logits_t, axis=0, keepdims=True)
    e = jnp.exp(logits_t - m)
    probs = e / jnp.sum(e, axis=0, keepdims=True)
    sel = probs + rb_col
    sel_rows = [sel[i:i + 1, :] for i in range(N_EXPERTS)]
    prob_rows = [probs[i:i + 1, :] for i in range(N_EXPERTS)]
    gsz = EXPERTS_PER_GROUP
    score = [_top2_sum(*sel_rows[g * gsz:(g + 1) * gsz]) for g in range(N_EXPERT_GROUPS)]
    chosen = []
    for g in range(N_EXPERT_GROUPS):
        best = None
        for g2 in range(N_EXPERT_GROUPS):
            if g2 == g:
                continue
            c = (score[g] > score[g2]) if g2 < g else (score[g] >= score[g2])
            best = c if best is None else jnp.logical_and(best, c)
        for i in range(gsz):
            ei = g * gsz + i
            rank = jnp.zeros_like(sel_rows[ei])
            for j in range(gsz):
                if j == i:
                    continue
                ej = g * gsz + j
                ahead = (sel_rows[ej] >= sel_rows[ei]) if j < i else (sel_rows[ej] > sel_rows[ei])
                rank = rank + jnp.where(ahead, 1.0, 0.0)
            chosen.append(jnp.logical_and(best, rank < float(2)))
    picked = [jnp.where(chosen[i], prob_rows[i], 0.0) for i in range(N_EXPERTS)]
    denom = picked[0]
    for i in range(1, N_EXPERTS):
        denom = denom + picked[i]
    gates = jnp.concatenate([pk / denom for pk in picked], axis=0)
    onehot = jnp.concatenate([jnp.where(ch, 1.0, 0.0) for ch in chosen], axis=0)
    return gates, onehot


def _mid_kernel(a_ref, ob_ref, x_ref, mod_ref, gb_ref, wout_ref, n2g_ref, rwt_ref, rb_ref,
                x1_ref, h2_ref, gates_ref, onehot_ref, count_ref):
    ob = ob_ref[...]
    bn = ob * lax.rsqrt(jnp.mean(ob * ob, axis=-1, keepdims=True) + EPS) * gb_ref[...]
    mixed = jnp.dot(a_ref[...], wout_ref[:A_WIDTH, :], preferred_element_type=F32)
    mixed = mixed + jnp.dot(bn.astype(BF16), wout_ref[A_WIDTH:, :], preferred_element_type=F32)
    x1 = x_ref[...] + mod_ref[2:3, :] * mixed
    x1_ref[...] = x1
    h2 = x1 * lax.rsqrt(jnp.mean(x1 * x1, axis=-1, keepdims=True) + EPS) * n2g_ref[...]
    h2 = (h2 * (1.0 + mod_ref[4:5, :]) + mod_ref[3:4, :]).astype(BF16)
    h2_ref[...] = h2
    logits_t = lax.dot_general(rwt_ref[...], h2, (((1,), (1,)), ((), ())),
                               preferred_element_type=F32)
    gates_t, onehot_t = _route(logits_t, rb_ref[...])
    gates_ref[...] = gates_t
    onehot_ref[...] = onehot_t
    for s in range(count_ref.shape[0]):
        count_ref[s] = jnp.sum(onehot_t[:, s * MOE_TOK:(s + 1) * MOE_TOK], axis=1, keepdims=True)


def _mid(out_a, out_b, x, mod_l, gb, wout, n2g, rwt, rb_col):
    tm = K4_TM
    tiles_per_seq = SEQ // tm
    tok = pl.BlockSpec((tm, D_MODEL), lambda i: (i, 0))
    half = pl.BlockSpec((tm, A_WIDTH), lambda i: (i, 0))
    route = pl.BlockSpec((N_EXPERTS, tm), lambda i: (0, i))
    return pl.pallas_call(
        _mid_kernel,
        grid=(TOKENS // tm,),
        in_specs=[
            half, half, tok,
            pl.BlockSpec((None, N_MOD, D_MODEL), lambda i: (i // tiles_per_seq, 0, 0)),
            pl.BlockSpec((1, B_WIDTH), lambda i: (0, 0)),
            pl.BlockSpec((D_MODEL, D_MODEL), lambda i: (0, 0)),
            pl.BlockSpec((1, D_MODEL), lambda i: (0, 0)),
            pl.BlockSpec((N_EXPERTS, D_MODEL), lambda i: (0, 0)),
            pl.BlockSpec((N_EXPERTS, 1), lambda i: (0, 0)),
        ],
        out_specs=[tok, tok, route, route,
                   pl.BlockSpec((tm // MOE_TOK, N_EXPERTS, 1), lambda i: (i, 0, 0))],
        out_shape=[jax.ShapeDtypeStruct((TOKENS, D_MODEL), F32),
                   jax.ShapeDtypeStruct((TOKENS, D_MODEL), BF16),
                   jax.ShapeDtypeStruct((N_EXPERTS, TOKENS), F32),
                   jax.ShapeDtypeStruct((N_EXPERTS, TOKENS), F32),
                   jax.ShapeDtypeStruct((TOKENS // MOE_TOK, N_EXPERTS, 1), F32)],
        compiler_params=_cparams(("arbitrary",)),
        name="mid",
    )(out_a, out_b, x, mod_l, gb, wout, n2g, rwt, rb_col)


MOE_TOK = 256
ROW_ALIGN = 8
MOE_LB = 2 * MOE_TOK + N_EXPERTS * ROW_ALIGN
FFN_TM = 512
N_TOK_TILES = TOKENS // MOE_TOK
ROWS_MAX = -(-(2 * TOKENS + N_TOK_TILES * N_EXPERTS * (ROW_ALIGN - 1)
               + N_EXPERTS * (FFN_TM - ROW_ALIGN)) // FFN_TM) * FFN_TM


def _moe_plan(count):
    n = count.reshape(N_TOK_TILES, N_EXPERTS).astype(jnp.int32)
    n8 = (n + (ROW_ALIGN - 1)) // ROW_ALIGN * ROW_ALIGN
    local = jnp.cumsum(n8, axis=1) - n8
    seg = (jnp.sum(n8, axis=0) + (FFN_TM - 1)) // FFN_TM * FFN_TM
    seg_end = jnp.cumsum(seg)
    glob = (seg_end - seg)[None, :] + jnp.cumsum(n8, axis=0) - n8
    n_ffn = ROWS_MAX // FFN_TM
    n_active = seg_end[-1] // FFN_TM
    tile_row = jnp.arange(n_ffn, dtype=jnp.int32) * FFN_TM
    texp = jnp.sum((tile_row[:, None] >= seg_end[None, :]).astype(jnp.int32), axis=1)
    texp = jnp.minimum(texp, N_EXPERTS - 1)
    texp = jnp.where(jnp.arange(n_ffn) < n_active, texp, texp[n_active - 1])
    return dict(n8=n8.reshape(-1), local=local.reshape(-1), glob=glob.reshape(-1),
                local_col=local.astype(F32).reshape(N_TOK_TILES, N_EXPERTS, 1),
                texp=texp.astype(jnp.int32), n_active=n_active.reshape(1).astype(jnp.int32))


def _run_copies(tile, n8_ref, local_ref, glob_ref, make_copy, start):
    for e in range(N_EXPERTS):
        idx = tile * N_EXPERTS + e
        n = n8_ref[idx]

        @pl.when(n > 0)
        def _(idx=idx, n=n):
            cp = make_copy(pl.multiple_of(local_ref[idx], ROW_ALIGN),
                           pl.multiple_of(glob_ref[idx], ROW_ALIGN),
                           pl.multiple_of(n, ROW_ALIGN))
            if start:
                cp.start()
            else:
                cp.wait()


def _sorted_rows(onehot, local_col):
    tm = onehot.shape[1]
    src = lax.broadcasted_iota(jnp.int32, (tm, tm), 0)
    dst = lax.broadcasted_iota(jnp.int32, (tm, tm), 1)
    before = jnp.where(src < dst, 1.0, 0.0).astype(BF16)
    rank = jnp.dot(onehot.astype(BF16), before, preferred_element_type=F32)
    pos = local_col + rank
    chosen = onehot > 0.5
    lo = jnp.min(jnp.where(chosen, pos, float(MOE_LB)), axis=0, keepdims=True)
    hi = jnp.max(jnp.where(chosen, pos, -1.0), axis=0, keepdims=True)
    return pos, chosen, lo, hi


def _dispatch_kernel(n8_ref, local_ref, glob_ref, h_ref, oh_ref, lcol_ref, xs_ref, lbuf, sem):
    j = pl.program_id(0)
    last = pl.num_programs(0) - 1
    slot = j % 2
    _, _, lo, hi = _sorted_rows(oh_ref[...], lcol_ref[...])
    row = lax.broadcasted_iota(jnp.int32, (MOE_LB, MOE_TOK), 0).astype(F32)
    perm = (jnp.where(row == lo, 1.0, 0.0) + jnp.where(row == hi, 1.0, 0.0)).astype(BF16)
    lbuf[slot] = jnp.dot(perm, h_ref[...], preferred_element_type=F32)

    def copy_from(slot_):
        def make(lo_, go_, size):
            return pltpu.make_async_copy(lbuf.at[slot_, pl.ds(lo_, size), :],
                                         xs_ref.at[pl.ds(go_, size), :], sem.at[slot_])
        return make

    refs = (n8_ref, local_ref, glob_ref)
    _run_copies(j, *refs, copy_from(slot), True)

    @pl.when(j > 0)
    def _():
        _run_copies(j - 1, *refs, copy_from(1 - slot), False)

    @pl.when(j == last)
    def _():
        _run_copies(j, *refs, copy_from(slot), False)


def _dispatch(plan, h2, onehot):
    grid_spec = pltpu.PrefetchScalarGridSpec(
        num_scalar_prefetch=3,
        grid=(N_TOK_TILES,),
        in_specs=[
            pl.BlockSpec((MOE_TOK, D_MODEL), lambda j, *_: (j, 0)),
            pl.BlockSpec((N_EXPERTS, MOE_TOK), lambda j, *_: (0, j)),
            pl.BlockSpec((None, N_EXPERTS, 1), lambda j, *_: (j, 0, 0)),
        ],
        out_specs=pl.BlockSpec(memory_space=pl.ANY),
        scratch_shapes=[pltpu.VMEM((2, MOE_LB, D_MODEL), F32), pltpu.SemaphoreType.DMA((2,))],
    )
    return pl.pallas_call(
        _dispatch_kernel,
        grid_spec=grid_spec,
        out_shape=jax.ShapeDtypeStruct((ROWS_MAX, D_MODEL), F32),
        compiler_params=_cparams(("arbitrary",)),
        name="dispatch",
    )(plan["n8"], plan["local"], plan["glob"], h2, onehot, plan["local_col"])


def _experts_kernel(texp_ref, nact_ref, xs_ref, wg_ref, wu_ref, wd_ref, ys_ref, wg_b, wu_b, wd_b):
    i = pl.program_id(0)
    active = i < nact_ref[0]
    new_expert = jnp.logical_or(i == 0, texp_ref[i] != texp_ref[jnp.maximum(i - 1, 0)])

    @pl.when(jnp.logical_and(active, new_expert))
    def _():
        wg_b[...] = wg_ref[...].astype(BF16)
        wu_b[...] = wu_ref[...].astype(BF16)
        wd_b[...] = wd_ref[...].astype(BF16)

    @pl.when(active)
    def _():
        x = xs_ref[...].astype(BF16)
        hg = jnp.dot(x, wg_b[...], preferred_element_type=F32)
        hu = jnp.dot(x, wu_b[...], preferred_element_type=F32)
        act = (jax.nn.silu(hg) * hu).astype(BF16)
        ys_ref[...] = jnp.dot(act, wd_b[...], preferred_element_type=F32)


def _experts(plan, xs, layer, wg, wu, wd):
    def rows(i, texp, nact):
        return (jnp.minimum(i, nact[0] - 1), 0)

    def expert(i, texp, nact):
        return (layer, texp[i], 0, 0)

    grid_spec = pltpu.PrefetchScalarGridSpec(
        num_scalar_prefetch=2,
        grid=(ROWS_MAX // FFN_TM,),
        in_specs=[
            pl.BlockSpec((FFN_TM, D_MODEL), rows),
            pl.BlockSpec((None, None, D_MODEL, D_EXPERT), expert),
            pl.BlockSpec((None, None, D_MODEL, D_EXPERT), expert),
            pl.BlockSpec((None, None, D_EXPERT, D_MODEL), expert),
        ],
        out_specs=pl.BlockSpec((FFN_TM, D_MODEL), rows),
        scratch_shapes=[pltpu.VMEM((D_MODEL, D_EXPERT), BF16), pltpu.VMEM((D_MODEL, D_EXPERT), BF16),
                        pltpu.VMEM((D_EXPERT, D_MODEL), BF16)],
    )
    return pl.pallas_call(
        _experts_kernel,
        grid_spec=grid_spec,
        out_shape=jax.ShapeDtypeStruct((ROWS_MAX, D_MODEL), F32),
        compiler_params=_cparams(("arbitrary",)),
        name="experts",
    )(plan["texp"], plan["n_active"], xs, wg, wu, wd)


def _combine_kernel(n8_ref, local_ref, glob_ref, oh_ref, g_ref, lcol_ref, x_ref, mod_ref, fg_ref,
                    ys_ref, o_ref, ybuf, sem, *, final_norm):
    j = pl.program_id(0)
    n_tiles = pl.num_programs(0)
    slot = j % 2
    refs = (n8_ref, local_ref, glob_ref)

    def copy_to(slot_):
        def make(lo_, go_, size):
            return pltpu.make_async_copy(ys_ref.at[pl.ds(go_, size), :],
                                         ybuf.at[slot_, pl.ds(lo_, size), :], sem.at[slot_])
        return make

    @pl.when(j == 0)
    def _():
        ybuf[...] = jnp.zeros(ybuf.shape, F32)
        _run_copies(0, *refs, copy_to(0), True)

    @pl.when(j + 1 < n_tiles)
    def _():
        _run_copies(j + 1, *refs, copy_to(1 - slot), True)

    pos, chosen, lo, hi = _sorted_rows(oh_ref[...], lcol_ref[...])
    gate = jnp.where(chosen, g_ref[...], 0.0)
    g_lo = jnp.sum(jnp.where(pos == lo, gate, 0.0), axis=0, keepdims=True)
    g_hi = jnp.sum(jnp.where(pos == hi, gate, 0.0), axis=0, keepdims=True)
    row = lax.broadcasted_iota(jnp.int32, (MOE_LB, MOE_TOK), 0).astype(F32)
    perm = (jnp.where(row == lo, g_lo, 0.0) + jnp.where(row == hi, g_hi, 0.0)).astype(BF16)

    _run_copies(j, *refs, copy_to(slot), False)
    y = lax.dot_general(perm, ybuf[slot].astype(BF16), (((0,), (0,)), ((), ())),
                        preferred_element_type=F32)
    x = x_ref[...] + mod_ref[5:6, :] * y
    if final_norm:
        x = x * lax.rsqrt(jnp.mean(x * x, axis=-1, keepdims=True) + EPS) * fg_ref[...]
    o_ref[...] = x


def _combine(plan, onehot, gates, ys, x, mod_l, final_g, final_norm):
    tiles_per_seq = SEQ // MOE_TOK
    route = pl.BlockSpec((N_EXPERTS, MOE_TOK), lambda j, *_: (0, j))
    tok = pl.BlockSpec((MOE_TOK, D_MODEL), lambda j, *_: (j, 0))
    grid_spec = pltpu.PrefetchScalarGridSpec(
        num_scalar_prefetch=3,
        grid=(N_TOK_TILES,),
        in_specs=[
            route, route,
            pl.BlockSpec((None, N_EXPERTS, 1), lambda j, *_: (j, 0, 0)),
            tok,
            pl.BlockSpec((None, N_MOD, D_MODEL), lambda j, *_: (j // tiles_per_seq, 0, 0)),
            pl.BlockSpec((1, D_MODEL), lambda j, *_: (0, 0)),
            pl.BlockSpec(memory_space=pl.ANY),
        ],
        out_specs=tok,
        scratch_shapes=[pltpu.VMEM((2, MOE_LB, D_MODEL), F32), pltpu.SemaphoreType.DMA((2,))],
    )
    return pl.pallas_call(
        functools.partial(_combine_kernel, final_norm=final_norm),
        grid_spec=grid_spec,
        out_shape=jax.ShapeDtypeStruct((TOKENS, D_MODEL), F32),
        compiler_params=_cparams(("arbitrary",)),
        name="combine",
    )(plan["n8"], plan["local"], plan["glob"], onehot, gates, plan["local_col"], x, mod_l,
      final_g, ys)


def kernel(x, c, rel_bias, router_w, router_b, mod_w, mod_b, norm1_g, w_in, gmlp_ln_g, gmlp_ln_b,
           gmlp_ws, gmlp_bs, out_norm_a_g, out_norm_b_g, w_out, norm2_g, moe_w_gate, moe_w_up,
           moe_w_down, final_g):
    mod = _modulation(c, mod_w, mod_b).reshape(DEPTH, BATCH, N_MOD, D_MODEL)
    bias_tab = _bias_tables(rel_bias)
    rwt = router_w.T.astype(BF16)
    rb_col = router_b.reshape(N_EXPERTS, 1)
    xt = x.reshape(TOKENS, D_MODEL)
    win_b = w_in.astype(BF16)
    wout_b = w_out.astype(BF16)
    for l in range(DEPTH):
        ws2 = gmlp_ws[l].astype(BF16).reshape(A_GROUPS // 2, 2 * CHUNK, CHUNK)
        bs_exp = jnp.repeat(gmlp_bs[l].T, HEAD_DIM, axis=1)
        out_a, q, k, v = _front(
            xt, mod[l], norm1_g[l].reshape(1, -1), win_b[l], gmlp_ln_g[l].reshape(1, -1),
            gmlp_ln_b[l].reshape(1, -1), ws2, bs_exp, out_norm_a_g[l].reshape(1, -1))
        out_b = _attention(q, k, v, bias_tab)
        xt, h2, gates, onehot, count = _mid(
            out_a, out_b, xt, mod[l], out_norm_b_g[l].reshape(1, -1),
            wout_b[l], norm2_g[l].reshape(1, -1), rwt, rb_col)
        plan = _moe_plan(count)
        xs = _dispatch(plan, h2, onehot)
        ys = _experts(plan, xs, l, moe_w_gate, moe_w_up, moe_w_down)
        xt = _combine(plan, onehot, gates, ys, xt, mod[l], final_g.reshape(1, -1),
                      final_norm=(l == DEPTH - 1))
    return xt.reshape(BATCH, SEQ, D_MODEL)
```

```python
import functools
import math

import numpy as np
import jax
import jax.numpy as jnp
from jax import lax
from jax.experimental import pallas as pl
from jax.experimental.pallas import tpu as pltpu

D_MODEL = 1024
BATCH = 8
SEQ = 2048
DEPTH = 2
TOKENS = BATCH * SEQ
HEAD_DIM = 64
A_WIDTH = 512
B_WIDTH = 512
A_GROUPS = 8
IN_COLS = 2 * A_WIDTH + 3 * B_WIDTH
CHUNK = 128
DILATED_CONFIGS = ((128, 1), (512, 4), (2048, 16))
ATTN_BLOCK = 128
REL_BUCKETS = 32
REL_MAX_EXACT = REL_BUCKETS // 2
REL_MAX_DISTANCE = 2048
N_EXPERTS = 16
N_EXPERT_GROUPS = 4
EXPERTS_PER_GROUP = 4
D_EXPERT = 512
N_MOD = 6
EPS = 1e-6
NEG_INF = -1e30

LANES = 128
HEAD_PAIR = 2 * HEAD_DIM
N_PAIRS = B_WIDTH // HEAD_PAIR
RESIDUES = max(d for _, d in DILATED_CONFIGS)
LOG2E = math.log2(math.e)

F32 = jnp.float32
BF16 = jnp.bfloat16

VMEM_LIMIT = 56 * 1024 * 1024


def _cparams(sem):
    return pltpu.CompilerParams(dimension_semantics=sem, vmem_limit_bytes=VMEM_LIMIT)


def _gelu(x):
    return 0.5 * x * (1.0 + lax.erf(x * math.sqrt(0.5)))


MOD_TN = 1024


def _mod_kernel(c_ref, w_ref, b_ref, o_ref):
    ca = jax.nn.silu(c_ref[...])
    o_ref[...] = jnp.dot(ca.astype(BF16), w_ref[...].astype(BF16),
                         preferred_element_type=F32) + b_ref[...]


def _modulation(c, mod_w, mod_b):
    n_cols = N_MOD * D_MODEL
    return pl.pallas_call(
        _mod_kernel,
        grid=(DEPTH, n_cols // MOD_TN),
        in_specs=[
            pl.BlockSpec((BATCH, D_MODEL), lambda l, j: (0, 0)),
            pl.BlockSpec((None, D_MODEL, MOD_TN), lambda l, j: (l, 0, j)),
            pl.BlockSpec((None, 1, MOD_TN), lambda l, j: (l, 0, j)),
        ],
        out_specs=pl.BlockSpec((None, BATCH, MOD_TN), lambda l, j: (l, 0, j)),
        out_shape=jax.ShapeDtypeStruct((DEPTH, BATCH, n_cols), F32),
        compiler_params=_cparams(("arbitrary", "arbitrary")),
        name="modulation",
    )(c, mod_w, mod_b.reshape(DEPTH, 1, n_cols))


K1_TM = 512


def _front_kernel(x_ref, mod_ref, n1g_ref, win_ref, lng_ref, lnb_ref, ws_ref, bs_ref, ga_ref,
                  a_ref, q_ref, k_ref, v_ref):
    x = x_ref[...]
    tm = x.shape[0]
    h = x * lax.rsqrt(jnp.mean(x * x, axis=-1, keepdims=True) + EPS) * n1g_ref[...]
    h = h * (1.0 + mod_ref[1:2, :]) + mod_ref[0:1, :]
    proj = jnp.dot(h.astype(BF16), win_ref[...], preferred_element_type=F32)

    q_ref[...] = proj[:, 2 * A_WIDTH:2 * A_WIDTH + B_WIDTH] * (HEAD_DIM ** -0.5 * LOG2E)
    k_ref[...] = proj[:, 2 * A_WIDTH + B_WIDTH:2 * A_WIDTH + 2 * B_WIDTH]
    v_ref[...] = proj[:, 2 * A_WIDTH + 2 * B_WIDTH:]

    u = _gelu(proj[:, :A_WIDTH])
    va = _gelu(proj[:, A_WIDTH:2 * A_WIDTH])
    mu = jnp.mean(va, axis=-1, keepdims=True)
    vc = va - mu
    vln = vc * lax.rsqrt(jnp.mean(vc * vc, axis=-1, keepdims=True) + EPS)
    vln = (vln * lng_ref[...] + lnb_ref[...]).astype(BF16)

    row = lax.broadcasted_iota(jnp.int32, (2 * CHUNK, CHUNK), 0)
    col = lax.broadcasted_iota(jnp.int32, (2 * CHUNK, CHUNK), 1)
    causal = (row % CHUNK) >= col
    first_group = lax.broadcasted_iota(jnp.int32, (CHUNK, LANES), 1) < HEAD_DIM
    wmix = [jnp.where(causal, ws_ref[p], jnp.zeros((), BF16)) for p in range(A_GROUPS // 2)]
    for c in range(tm // CHUNK):
        rows = slice(c * CHUNK, (c + 1) * CHUNK)
        parts = []
        for p in range(A_GROUPS // 2):
            vp = vln[rows, p * LANES:(p + 1) * LANES]
            r = jnp.dot(wmix[p], vp, preferred_element_type=F32)
            parts.append(jnp.where(first_group, r[:CHUNK], r[CHUNK:]))
        s = jnp.concatenate(parts, axis=-1) + bs_ref[...]
        oa = u[rows] * s
        oa = oa * lax.rsqrt(jnp.mean(oa * oa, axis=-1, keepdims=True) + EPS) * ga_ref[...]
        a_ref[rows, :] = oa.astype(BF16)


def _front(x, mod_l, n1g, win, lng, lnb, ws2, bs_exp, ga):
    tm = K1_TM
    tiles_per_seq = SEQ // tm
    row1024 = pl.BlockSpec((1, D_MODEL), lambda i: (0, 0))
    row512 = pl.BlockSpec((1, A_WIDTH), lambda i: (0, 0))
    half = pl.BlockSpec((tm, A_WIDTH), lambda i: (i, 0))
    return pl.pallas_call(
        _front_kernel,
        grid=(TOKENS // tm,),
        in_specs=[
            pl.BlockSpec((tm, D_MODEL), lambda i: (i, 0)),
            pl.BlockSpec((None, N_MOD, D_MODEL), lambda i: (i // tiles_per_seq, 0, 0)),
            row1024,
            pl.BlockSpec((D_MODEL, IN_COLS), lambda i: (0, 0)),
            row512, row512,
            pl.BlockSpec((A_GROUPS // 2, 2 * CHUNK, CHUNK), lambda i: (0, 0, 0)),
            pl.BlockSpec((CHUNK, A_WIDTH), lambda i: (0, 0)),
            row512,
        ],
        out_specs=[half, half, half, half],
        out_shape=[jax.ShapeDtypeStruct((TOKENS, A_WIDTH), BF16)] +
                  [jax.ShapeDtypeStruct((TOKENS, B_WIDTH), F32)] * 3,
        compiler_params=_cparams(("arbitrary",)),
        name="front",
    )(x, mod_l, n1g, win, lng, lnb, ws2, bs_exp, ga)


def _t5_bucket_np(dist):
    dist = np.maximum(dist, 0)
    ratio = np.log(np.maximum(dist, 1) / REL_MAX_EXACT) / np.log(REL_MAX_DISTANCE / REL_MAX_EXACT)
    large = REL_MAX_EXACT + np.floor(ratio * (REL_BUCKETS - REL_MAX_EXACT)).astype(np.int64)
    large = np.minimum(large, REL_BUCKETS - 1)
    return np.where(dist < REL_MAX_EXACT, dist, large).astype(np.int32)


def _bias_tables(rel_bias):
    blk = ATTN_BLOCK
    n_rel = 3 * blk
    rel = 2 * blk - 1 - np.arange(n_rel)
    tables = []
    for window, d in DILATED_CONFIGS:
        span = window // d
        bucket = _t5_bucket_np(np.clip(rel, 0, span) * d)
        valid = jnp.asarray((rel >= 0) & (rel <= span))
        w = jnp.where(valid[None, :], rel_bias.astype(F32)[bucket].T, NEG_INF)
        flat = jnp.tile(w, (1, blk))
        skew = flat[:, blk - 1:blk - 1 + blk * (n_rel - 1)].reshape(-1, blk, n_rel - 1)
        tab = skew[:, :, :2 * blk]
        sub = RESIDUES // d
        ln = blk // sub
        tab = tab.reshape(-1, ln, sub, 2, ln, sub).transpose(0, 2, 1, 3, 5, 4)
        tables.append(tab.reshape(-1, blk, 2 * blk) * LOG2E)
    return jnp.stack(tables, axis=0)


def _attn_kernel(q_ref, k_ref, v_ref, bias_ref, o_ref, qp, kp, vp, m_sc, l_sc, acc_sc):
    blk = ATTN_BLOCK
    res = RESIDUES
    per = SEQ // res
    sq = res * res
    seg = sq // res
    lane = lax.broadcasted_iota(jnp.int32, (blk, LANES), 1)
    head0 = lane < HEAD_DIM
    ones = jnp.ones((2 * blk, LANES), BF16)

    pa = lax.broadcasted_iota(jnp.int32, (sq, sq), 0)
    pb = lax.broadcasted_iota(jnp.int32, (sq, sq), 1)
    regroup = jnp.where(pb == res * (pa % res) + pa // res, 1.0, 0.0).astype(BF16)

    def residue_rows(g):
        return [slice(per * r + seg * g, per * r + seg * (g + 1)) for r in range(res)]

    def load(ref, slices):
        return jnp.concatenate([ref[s, :] for s in slices], axis=0) if len(slices) > 1 \
            else ref[slices[0], :]

    def store(ref, slices, val):
        ln = val.shape[0] // len(slices)
        for i, s in enumerate(slices):
            ref[s, :] = val[i * ln:(i + 1) * ln]

    for g in range(SEQ // sq):
        rows = slice(sq * g, sq * (g + 1))
        qk = jnp.concatenate([q_ref[rows, :], k_ref[rows, :]], axis=1).astype(BF16)
        qk = jnp.dot(regroup, qk, preferred_element_type=F32)
        vv = jnp.dot(regroup, v_ref[rows, :].astype(BF16), preferred_element_type=F32)
        store(qp, residue_rows(g), qk[:, :LANES])
        store(kp, residue_rows(g), qk[:, LANES:])
        store(vp, residue_rows(g), vv)

    def merge_heads(t):
        return jnp.where(head0, t[:blk], t[blk:])

    def scores_block(q, kcat, vcat, bias):
        nk = kcat.shape[0]
        zero = jnp.zeros_like(q)
        qs = jnp.concatenate([jnp.where(head0, q, zero), jnp.where(head0, zero, q)],
                             axis=0).astype(BF16)
        s = lax.dot_general(qs, kcat, (((1,), (1,)), ((), ())), preferred_element_type=F32) + bias
        m = jnp.max(s, axis=-1, keepdims=True)
        p = jnp.exp2(s - m).astype(BF16)
        pv = jnp.dot(p, jnp.concatenate([vcat, ones[:nk]], axis=1), preferred_element_type=F32)
        return (merge_heads(jnp.broadcast_to(m, (2 * blk, LANES))),
                merge_heads(pv[:, LANES:]), merge_heads(pv[:, :LANES]))

    def update(ci, slices, m_c, l_c, o_c):
        if ci > 0:
            m_r = load(m_sc, slices)
            m_n = jnp.maximum(m_r, m_c)
            a = jnp.exp2(m_r - m_n)
            b = jnp.exp2(m_c - m_n)
            l_c = a * load(l_sc, slices) + b * l_c
            o_c = a * load(acc_sc, slices) + b * o_c
            m_c = m_n
        store(m_sc, slices, m_c)
        store(l_sc, slices, l_c)
        store(acc_sc, slices, o_c)

    for ci, (window, d) in enumerate(DILATED_CONFIGS):
        nb = SEQ // d // blk
        sub = res // d
        ln = blk // sub
        for r in range(d):
            k_prev = v_prev = None
            for n in range(nb):
                slices = [slice(per * (r + d * c) + ln * n, per * (r + d * c) + ln * (n + 1))
                          for c in range(sub)]
                k_cur = load(kp, slices).astype(BF16)
                v_cur = load(vp, slices).astype(BF16)
                if k_prev is None:
                    bias = jnp.concatenate([bias_ref[ci, 0, :, blk:], bias_ref[ci, 1, :, blk:]],
                                           axis=0)
                    result = scores_block(load(qp, slices), k_cur, v_cur, bias)
                else:
                    bias = jnp.concatenate([bias_ref[ci, 0], bias_ref[ci, 1]], axis=0)
                    result = scores_block(load(qp, slices),
                                          jnp.concatenate([k_prev, k_cur], axis=0),
                                          jnp.concatenate([v_prev, v_cur], axis=0), bias)
                update(ci, slices, *result)
                k_prev, v_prev = k_cur, v_cur

    for g in range(SEQ // sq):
        o = load(acc_sc, residue_rows(g)) / load(l_sc, residue_rows(g))
        hi = o.astype(BF16)
        lo = (o - hi.astype(F32)).astype(BF16)
        back = jnp.dot(regroup, jnp.concatenate([hi, lo], axis=1), preferred_element_type=F32)
        o_ref[sq * g:sq * (g + 1), :] = back[:, :LANES] + back[:, LANES:]


def _attention(q, k, v, bias_tab):
    n_cfg = len(DILATED_CONFIGS)
    blk = ATTN_BLOCK
    seq_spec = pl.BlockSpec((SEQ, HEAD_PAIR), lambda b, p: (b, p))
    return pl.pallas_call(
        _attn_kernel,
        grid=(BATCH, N_PAIRS),
        in_specs=[seq_spec, seq_spec, seq_spec,
                  pl.BlockSpec((n_cfg, 2, blk, 2 * blk), lambda b, p: (0, p, 0, 0))],
        out_specs=seq_spec,
        out_shape=jax.ShapeDtypeStruct((TOKENS, B_WIDTH), F32),
        scratch_shapes=[pltpu.VMEM((SEQ, HEAD_PAIR), F32)] * 6,
        compiler_params=_cparams(("arbitrary", "arbitrary")),
        name="attention",
    )(q, k, v, bias_tab)


K4_TM = 512


def _top2_sum(a, b, c, d):
    hi1, lo1 = jnp.maximum(a, b), jnp.minimum(a, b)
    hi2, lo2 = jnp.maximum(c, d), jnp.minimum(c, d)
    return jnp.maximum(hi1, hi2) + jnp.maximum(jnp.minimum(hi1, hi2), jnp.maximum(lo1, lo2))


def _route(logits_t, rb_col):
    m = jnp.max(logits_t, axis=0, keepdims=True)
    e = jnp.exp(logits_t - m)
    probs = e / jnp.sum(e, axis=0, keepdims=True)
    sel = probs + rb_col
    sel_rows = [sel[i:i + 1, :] for i in range(N_EXPERTS)]
    prob_rows = [probs[i:i + 1, :] for i in range(N_EXPERTS)]
    gsz = EXPERTS_PER_GROUP
    score = [_top2_sum(*sel_rows[g * gsz:(g + 1) * gsz]) for g in range(N_EXPERT_GROUPS)]
    chosen = []
    for g in range(N_EXPERT_GROUPS):
        best = None
        for g2 in range(N_EXPERT_GROUPS):
            if g2 == g:
                continue
            c = (score[g] > score[g2]) if g2 < g else (score[g] >= score[g2])
            best = c if best is None else jnp.logical_and(best, c)
        for i in range(gsz):
            ei = g * gsz + i
            rank = jnp.zeros_like(sel_rows[ei])
            for j in range(gsz):
                if j == i:
                    continue
                ej = g * gsz + j
                ahead = (sel_rows[ej] >= sel_rows[ei]) if j < i else (sel_rows[ej] > sel_rows[ei])
                rank = rank + jnp.where(ahead, 1.0, 0.0)
            chosen.append(jnp.logical_and(best, rank < float(2)))
    picked = [jnp.where(chosen[i], prob_rows[i], 0.0) for i in range(N_EXPERTS)]
    denom = picked[0]
    for i in range(1, N_EXPERTS):
        denom = denom + picked[i]
    gates = jnp.concatenate([pk / denom for pk in picked], axis=0)
    onehot = jnp.concatenate([jnp.where(ch, 1.0, 0.0) for ch in chosen], axis=0)
    return gates, onehot


def _mid_kernel(a_ref, ob_ref, x_ref, mod_ref, gb_ref, wout_ref, n2g_ref, rwt_ref, rb_ref,
                x1_ref, h2_ref, gates_ref, onehot_ref, count_ref):
    ob = ob_ref[...]
    bn = ob * lax.rsqrt(jnp.mean(ob * ob, axis=-1, keepdims=True) + EPS) * gb_ref[...]
    mixed = jnp.dot(a_ref[...], wout_ref[:A_WIDTH, :], preferred_element_type=F32)
    mixed = mixed + jnp.dot(bn.astype(BF16), wout_ref[A_WIDTH:, :], preferred_element_type=F32)
    x1 = x_ref[...] + mod_ref[2:3, :] * mixed
    x1_ref[...] = x1
    h2 = x1 * lax.rsqrt(jnp.mean(x1 * x1, axis=-1, keepdims=True) + EPS) * n2g_ref[...]
    h2 = (h2 * (1.0 + mod_ref[4:5, :]) + mod_ref[3:4, :]).astype(BF16)
    h2_ref[...] = h2
    logits_t = lax.dot_general(rwt_ref[...], h2, (((1,), (1,)), ((), ())),
                               preferred_element_type=F32)
    gates_t, onehot_t = _route(logits_t, rb_ref[...])
    gates_ref[...] = gates_t
    onehot_ref[...] = onehot_t
    for s in range(count_ref.shape[0]):
        count_ref[s] = jnp.sum(onehot_t[:, s * MOE_TOK:(s + 1) * MOE_TOK], axis=1, keepdims=True)


def _mid(out_a, out_b, x, mod_l, gb, wout, n2g, rwt, rb_col):
    tm = K4_TM
    tiles_per_seq = SEQ // tm
    tok = pl.BlockSpec((tm, D_MODEL), lambda i: (i, 0))
    half = pl.BlockSpec((tm, A_WIDTH), lambda i: (i, 0))
    route = pl.BlockSpec((N_EXPERTS, tm), lambda i: (0, i))
    return pl.pallas_call(
        _mid_kernel,
        grid=(TOKENS // tm,),
        in_specs=[
            half, half, tok,
            pl.BlockSpec((None, N_MOD, D_MODEL), lambda i: (i // tiles_per_seq, 0, 0)),
            pl.BlockSpec((1, B_WIDTH), lambda i: (0, 0)),
            pl.BlockSpec((D_MODEL, D_MODEL), lambda i: (0, 0)),
            pl.BlockSpec((1, D_MODEL), lambda i: (0, 0)),
            pl.BlockSpec((N_EXPERTS, D_MODEL), lambda i: (0, 0)),
            pl.BlockSpec((N_EXPERTS, 1), lambda i: (0, 0)),
        ],
        out_specs=[tok, tok, route, route,
                   pl.BlockSpec((tm // MOE_TOK, N_EXPERTS, 1), lambda i: (i, 0, 0))],
        out_shape=[jax.ShapeDtypeStruct((TOKENS, D_MODEL), F32),
                   jax.ShapeDtypeStruct((TOKENS, D_MODEL), BF16),
                   jax.ShapeDtypeStruct((N_EXPERTS, TOKENS), F32),
                   jax.ShapeDtypeStruct((N_EXPERTS, TOKENS), F32),
                   jax.ShapeDtypeStruct((TOKENS // MOE_TOK, N_EXPERTS, 1), F32)],
        compiler_params=_cparams(("arbitrary",)),
        name="mid",
    )(out_a, out_b, x, mod_l, gb, wout, n2g, rwt, rb_col)


MOE_TOK = 256
ROW_ALIGN = 8
MOE_LB = 2 * MOE_TOK + N_EXPERTS * ROW_ALIGN
FFN_TM = 512
N_TOK_TILES = TOKENS // MOE_TOK
ROWS_MAX = -(-(2 * TOKENS + N_TOK_TILES * N_EXPERTS * (ROW_ALIGN - 1)
               + N_EXPERTS * (FFN_TM - ROW_ALIGN)) // FFN_TM) * FFN_TM


def _moe_plan(count):
    n = count.reshape(N_TOK_TILES, N_EXPERTS).astype(jnp.int32)
    n8 = (n + (ROW_ALIGN - 1)) // ROW_ALIGN * ROW_ALIGN
    local = jnp.cumsum(n8, axis=1) - n8
    seg = (jnp.sum(n8, axis=0) + (FFN_TM - 1)) // FFN_TM * FFN_TM
    seg_end = jnp.cumsum(seg)
    glob = (seg_end - seg)[None, :] + jnp.cumsum(n8, axis=0) - n8
    n_ffn = ROWS_MAX // FFN_TM
    n_active = seg_end[-1] // FFN_TM
    tile_row = jnp.arange(n_ffn, dtype=jnp.int32) * FFN_TM
    texp = jnp.sum((tile_row[:, None] >= seg_end[None, :]).astype(jnp.int32), axis=1)
    texp = jnp.minimum(texp, N_EXPERTS - 1)
    texp = jnp.where(jnp.arange(n_ffn) < n_active, texp, texp[n_active - 1])
    return dict(n8=n8.reshape(-1), local=local.reshape(-1), glob=glob.reshape(-1),
                local_col=local.astype(F32).reshape(N_TOK_TILES, N_EXPERTS, 1),
                texp=texp.astype(jnp.int32), n_active=n_active.reshape(1).astype(jnp.int32))


def _run_copies(tile, n8_ref, local_ref, glob_ref, make_copy, start):
    for e in range(N_EXPERTS):
        idx = tile * N_EXPERTS + e
        n = n8_ref[idx]

        @pl.when(n > 0)
        def _(idx=idx, n=n):
            cp = make_copy(pl.multiple_of(local_ref[idx], ROW_ALIGN),
                           pl.multiple_of(glob_ref[idx], ROW_ALIGN),
                           pl.multiple_of(n, ROW_ALIGN))
            if start:
                cp.start()
            else:
                cp.wait()


def _sorted_rows(onehot, local_col):
    tm = onehot.shape[1]
    src = lax.broadcasted_iota(jnp.int32, (tm, tm), 0)
    dst = lax.broadcasted_iota(jnp.int32, (tm, tm), 1)
    before = jnp.where(src < dst, 1.0, 0.0).astype(BF16)
    rank = jnp.dot(onehot.astype(BF16), before, preferred_element_type=F32)
    pos = local_col + rank
    chosen = onehot > 0.5
    lo = jnp.min(jnp.where(chosen, pos, float(MOE_LB)), axis=0, keepdims=True)
    hi = jnp.max(jnp.where(chosen, pos, -1.0), axis=0, keepdims=True)
    return pos, chosen, lo, hi


def _dispatch_kernel(n8_ref, local_ref, glob_ref, h_ref, oh_ref, lcol_ref, xs_ref, lbuf, sem):
    j = pl.program_id(0)
    last = pl.num_programs(0) - 1
    slot = j % 2
    _, _, lo, hi = _sorted_rows(oh_ref[...], lcol_ref[...])
    row = lax.broadcasted_iota(jnp.int32, (MOE_LB, MOE_TOK), 0).astype(F32)
    perm = (jnp.where(row == lo, 1.0, 0.0) + jnp.where(row == hi, 1.0, 0.0)).astype(BF16)
    lbuf[slot] = jnp.dot(perm, h_ref[...], preferred_element_type=F32)

    def copy_from(slot_):
        def make(lo_, go_, size):
            return pltpu.make_async_copy(lbuf.at[slot_, pl.ds(lo_, size), :],
                                         xs_ref.at[pl.ds(go_, size), :], sem.at[slot_])
        return make

    refs = (n8_ref, local_ref, glob_ref)
    _run_copies(j, *refs, copy_from(slot), True)

    @pl.when(j > 0)
    def _():
        _run_copies(j - 1, *refs, copy_from(1 - slot), False)

    @pl.when(j == last)
    def _():
        _run_copies(j, *refs, copy_from(slot), False)


def _dispatch(plan, h2, onehot):
    grid_spec = pltpu.PrefetchScalarGridSpec(
        num_scalar_prefetch=3,
        grid=(N_TOK_TILES,),
        in_specs=[
            pl.BlockSpec((MOE_TOK, D_MODEL), lambda j, *_: (j, 0)),
            pl.BlockSpec((N_EXPERTS, MOE_TOK), lambda j, *_: (0, j)),
            pl.BlockSpec((None, N_EXPERTS, 1), lambda j, *_: (j, 0, 0)),
        ],
        out_specs=pl.BlockSpec(memory_space=pl.ANY),
        scratch_shapes=[pltpu.VMEM((2, MOE_LB, D_MODEL), F32), pltpu.SemaphoreType.DMA((2,))],
    )
    return pl.pallas_call(
        _dispatch_kernel,
        grid_spec=grid_spec,
        out_shape=jax.ShapeDtypeStruct((ROWS_MAX, D_MODEL), F32),
        compiler_params=_cparams(("arbitrary",)),
        name="dispatch",
    )(plan["n8"], plan["local"], plan["glob"], h2, onehot, plan["local_col"])


def _experts_kernel(texp_ref, nact_ref, xs_ref, wg_ref, wu_ref, wd_ref, ys_ref, wg_b, wu_b, wd_b):
    i = pl.program_id(0)
    active = i < nact_ref[0]
    new_expert = jnp.logical_or(i == 0, texp_ref[i] != texp_ref[jnp.maximum(i - 1, 0)])

    @pl.when(jnp.logical_and(active, new_expert))
    def _():
        wg_b[...] = wg_ref[...].astype(BF16)
        wu_b[...] = wu_ref[...].astype(BF16)
        wd_b[...] = wd_ref[...].astype(BF16)

    @pl.when(active)
    def _():
        x = xs_ref[...].astype(BF16)
        hg = jnp.dot(x, wg_b[...], preferred_element_type=F32)
        hu = jnp.dot(x, wu_b[...], preferred_element_type=F32)
        act = (jax.nn.silu(hg) * hu).astype(BF16)
        ys_ref[...] = jnp.dot(act, wd_b[...], preferred_element_type=F32)


def _experts(plan, xs, layer, wg, wu, wd):
    def rows(i, texp, nact):
        return (jnp.minimum(i, nact[0] - 1), 0)

    def expert(i, texp, nact):
        return (layer, texp[i], 0, 0)

    grid_spec = pltpu.PrefetchScalarGridSpec(
        num_scalar_prefetch=2,
        grid=(ROWS_MAX // FFN_TM,),
        in_specs=[
            pl.BlockSpec((FFN_TM, D_MODEL), rows),
            pl.BlockSpec((None, None, D_MODEL, D_EXPERT), expert),
            pl.BlockSpec((None, None, D_MODEL, D_EXPERT), expert),
            pl.BlockSpec((None, None, D_EXPERT, D_MODEL), expert),
        ],
        out_specs=pl.BlockSpec((FFN_TM, D_MODEL), rows),
        scratch_shapes=[pltpu.VMEM((D_MODEL, D_EXPERT), BF16), pltpu.VMEM((D_MODEL, D_EXPERT), BF16),
                        pltpu.VMEM((D_EXPERT, D_MODEL), BF16)],
    )
    return pl.pallas_call(
        _experts_kernel,
        grid_spec=grid_spec,
        out_shape=jax.ShapeDtypeStruct((ROWS_MAX, D_MODEL), F32),
        compiler_params=_cparams(("arbitrary",)),
        name="experts",
    )(plan["texp"], plan["n_active"], xs, wg, wu, wd)


def _combine_kernel(n8_ref, local_ref, glob_ref, oh_ref, g_ref, lcol_ref, x_ref, mod_ref, fg_ref,
                    ys_ref, o_ref, ybuf, sem, *, final_norm):
    j = pl.program_id(0)
    n_tiles = pl.num_programs(0)
    slot = j % 2
    refs = (n8_ref, local_ref, glob_ref)

    def copy_to(slot_):
        def make(lo_, go_, size):
            return pltpu.make_async_copy(ys_ref.at[pl.ds(go_, size), :],
                                         ybuf.at[slot_, pl.ds(lo_, size), :], sem.at[slot_])
        return make

    @pl.when(j == 0)
    def _():
        ybuf[...] = jnp.zeros(ybuf.shape, F32)
        _run_copies(0, *refs, copy_to(0), True)

    @pl.when(j + 1 < n_tiles)
    def _():
        _run_copies(j + 1, *refs, copy_to(1 - slot), True)

    pos, chosen, lo, hi = _sorted_rows(oh_ref[...], lcol_ref[...])
    gate = jnp.where(chosen, g_ref[...], 0.0)
    g_lo = jnp.sum(jnp.where(pos == lo, gate, 0.0), axis=0, keepdims=True)
    g_hi = jnp.sum(jnp.where(pos == hi, gate, 0.0), axis=0, keepdims=True)
    row = lax.broadcasted_iota(jnp.int32, (MOE_LB, MOE_TOK), 0).astype(F32)
    perm = (jnp.where(row == lo, g_lo, 0.0) + jnp.where(row == hi, g_hi, 0.0)).astype(BF16)

    _run_copies(j, *refs, copy_to(slot), False)
    y = lax.dot_general(perm, ybuf[slot].astype(BF16), (((0,), (0,)), ((), ())),
                        preferred_element_type=F32)
    x = x_ref[...] + mod_ref[5:6, :] * y
    if final_norm:
        x = x * lax.rsqrt(jnp.mean(x * x, axis=-1, keepdims=True) + EPS) * fg_ref[...]
    o_ref[...] = x


def _combine(plan, onehot, gates, ys, x, mod_l, final_g, final_norm):
    tiles_per_seq = SEQ // MOE_TOK
    route = pl.BlockSpec((N_EXPERTS, MOE_TOK), lambda j, *_: (0, j))
    tok = pl.BlockSpec((MOE_TOK, D_MODEL), lambda j, *_: (j, 0))
    grid_spec = pltpu.PrefetchScalarGridSpec(
        num_scalar_prefetch=3,
        grid=(N_TOK_TILES,),
        in_specs=[
            route, route,
            pl.BlockSpec((None, N_EXPERTS, 1), lambda j, *_: (j, 0, 0)),
            tok,
            pl.BlockSpec((None, N_MOD, D_MODEL), lambda j, *_: (j // tiles_per_seq, 0, 0)),
            pl.BlockSpec((1, D_MODEL), lambda j, *_: (0, 0)),
            pl.BlockSpec(memory_space=pl.ANY),
        ],
        out_specs=tok,
        scratch_shapes=[pltpu.VMEM((2, MOE_LB, D_MODEL), F32), pltpu.SemaphoreType.DMA((2,))],
    )
    return pl.pallas_call(
        functools.partial(_combine_kernel, final_norm=final_norm),
        grid_spec=grid_spec,
        out_shape=jax.ShapeDtypeStruct((TOKENS, D_MODEL), F32),
        compiler_params=_cparams(("arbitrary",)),
        name="combine",
    )(plan["n8"], plan["local"], plan["glob"], onehot, gates, plan["local_col"], x, mod_l,
      final_g, ys)


def kernel(x, c, rel_bias, router_w, router_b, mod_w, mod_b, norm1_g, w_in, gmlp_ln_g, gmlp_ln_b,
           gmlp_ws, gmlp_bs, out_norm_a_g, out_norm_b_g, w_out, norm2_g, moe_w_gate, moe_w_up,
           moe_w_down, final_g):
    mod = _modulation(c, mod_w, mod_b).reshape(DEPTH, BATCH, N_MOD, D_MODEL)
    bias_tab = _bias_tables(rel_bias)
    rwt = router_w.T.astype(BF16)
    rb_col = router_b.reshape(N_EXPERTS, 1)
    xt = x.reshape(TOKENS, D_MODEL)
    win_b = w_in.astype(BF16)
    wout_b = w_out.astype(BF16)
    for l in range(DEPTH):
        ws2 = gmlp_ws[l].astype(BF16).reshape(A_GROUPS // 2, 2 * CHUNK, CHUNK)
        bs_exp = jnp.repeat(gmlp_bs[l].T, HEAD_DIM, axis=1)
        out_a, q, k, v = _front(
            xt, mod[l], norm1_g[l].reshape(1, -1), win_b[l], gmlp_ln_g[l].reshape(1, -1),
            gmlp_ln_b[l].reshape(1, -1), ws2, bs_exp, out_norm_a_g[l].reshape(1, -1))
        out_b = _attention(q, k, v, bias_tab)
        xt, h2, gates, onehot, count = _mid(
            out_a, out_b, xt, mod[l], out_norm_b_g[l].reshape(1, -1),
            wout_b[l], norm2_g[l].reshape(1, -1), rwt, rb_col)
        plan = _moe_plan(count)
        xs = _dispatch(plan, h2, onehot)
        ys = _experts(plan, xs, l, moe_w_gate, moe_w_up, moe_w_down)
        xt = _combine(plan, onehot, gates, ys, xt, mod[l], final_g.reshape(1, -1),
                      final_norm=(l == DEPTH - 1))
    return xt.reshape(BATCH, SEQ, D_MODEL)
```

```python
import functools
import math

import numpy as np
import jax
import jax.numpy as jnp
from jax import lax
from jax.experimental import pallas as pl
from jax.experimental.pallas import tpu as pltpu

D_MODEL = 1024
BATCH = 8
SEQ = 2048
DEPTH = 2
TOKENS = BATCH * SEQ
HEAD_DIM = 64
A_WIDTH = 512
B_WIDTH = 512
A_GROUPS = 8
IN_COLS = 2 * A_WIDTH + 3 * B_WIDTH
CHUNK = 128
DILATED_CONFIGS = ((128, 1), (512, 4), (2048, 16))
ATTN_BLOCK = 128
REL_BUCKETS = 32
REL_MAX_EXACT = REL_BUCKETS // 2
REL_MAX_DISTANCE = 2048
N_EXPERTS = 16
N_EXPERT_GROUPS = 4
EXPERTS_PER_GROUP = 4
D_EXPERT = 512
N_MOD = 6
EPS = 1e-6
NEG_INF = -1e30

LANES = 128
HEAD_PAIR = 2 * HEAD_DIM
N_PAIRS = B_WIDTH // HEAD_PAIR
RESIDUES = max(d for _, d in DILATED_CONFIGS)
LOG2E = math.log2(math.e)

F32 = jnp.float32
BF16 = jnp.bfloat16

VMEM_LIMIT = 56 * 1024 * 1024


def _cparams(sem):
    return pltpu.CompilerParams(dimension_semantics=sem, vmem_limit_bytes=VMEM_LIMIT)


def _gelu(x):
    return 0.5 * x * (1.0 + lax.erf(x * math.sqrt(0.5)))


MOD_TN = 1024


def _mod_kernel(c_ref, w_ref, b_ref, o_ref):
    ca = jax.nn.silu(c_ref[...])
    o_ref[...] = jnp.dot(ca.astype(BF16), w_ref[...].astype(BF16),
                         preferred_element_type=F32) + b_ref[...]


def _modulation(c, mod_w, mod_b):
    n_cols = N_MOD * D_MODEL
    return pl.pallas_call(
        _mod_kernel,
        grid=(DEPTH, n_cols // MOD_TN),
        in_specs=[
            pl.BlockSpec((BATCH, D_MODEL), lambda l, j: (0, 0)),
            pl.BlockSpec((None, D_MODEL, MOD_TN), lambda l, j: (l, 0, j)),
            pl.BlockSpec((None, 1, MOD_TN), lambda l, j: (l, 0, j)),
        ],
        out_specs=pl.BlockSpec((None, BATCH, MOD_TN), lambda l, j: (l, 0, j)),
        out_shape=jax.ShapeDtypeStruct((DEPTH, BATCH, n_cols), F32),
        compiler_params=_cparams(("arbitrary", "arbitrary")),
        name="modulation",
    )(c, mod_w, mod_b.reshape(DEPTH, 1, n_cols))


K1_TM = 512


def _front_kernel(x_ref, mod_ref, n1g_ref, win_ref, lng_ref, lnb_ref, ws_ref, bs_ref, ga_ref,
                  a_ref, q_ref, k_ref, v_ref):
    x = x_ref[...]
    tm = x.shape[0]
    h = x * lax.rsqrt(jnp.mean(x * x, axis=-1, keepdims=True) + EPS) * n1g_ref[...]
    h = h * (1.0 + mod_ref[1:2, :]) + mod_ref[0:1, :]
    proj = jnp.dot(h.astype(BF16), win_ref[...], preferred_element_type=F32)

    q_ref[...] = proj[:, 2 * A_WIDTH:2 * A_WIDTH + B_WIDTH] * (HEAD_DIM ** -0.5 * LOG2E)
    k_ref[...] = proj[:, 2 * A_WIDTH + B_WIDTH:2 * A_WIDTH + 2 * B_WIDTH]
    v_ref[...] = proj[:, 2 * A_WIDTH + 2 * B_WIDTH:]

    u = _gelu(proj[:, :A_WIDTH])
    va = _gelu(proj[:, A_WIDTH:2 * A_WIDTH])
    mu = jnp.mean(va, axis=-1, keepdims=True)
    vc = va - mu
    vln = vc * lax.rsqrt(jnp.mean(vc * vc, axis=-1, keepdims=True) + EPS)
    vln = (vln * lng_ref[...] + lnb_ref[...]).astype(BF16)

    row = lax.broadcasted_iota(jnp.int32, (2 * CHUNK, CHUNK), 0)
    col = lax.broadcasted_iota(jnp.int32, (2 * CHUNK, CHUNK), 1)
    causal = (row % CHUNK) >= col
    first_group = lax.broadcasted_iota(jnp.int32, (CHUNK, LANES), 1) < HEAD_DIM
    wmix = [jnp.where(causal, ws_ref[p], jnp.zeros((), BF16)) for p in range(A_GROUPS // 2)]
    for c in range(tm // CHUNK):
        rows = slice(c * CHUNK, (c + 1) * CHUNK)
        parts = []
        for p in range(A_GROUPS // 2):
            vp = vln[rows, p * LANES:(p + 1) * LANES]
            r = jnp.dot(wmix[p], vp, preferred_element_type=F32)
            parts.append(jnp.where(first_group, r[:CHUNK], r[CHUNK:]))
        s = jnp.concatenate(parts, axis=-1) + bs_ref[...]
        oa = u[rows] * s
        oa = oa * lax.rsqrt(jnp.mean(oa * oa, axis=-1, keepdims=True) + EPS) * ga_ref[...]
        a_ref[rows, :] = oa.astype(BF16)


def _front(x, mod_l, n1g, win, lng, lnb, ws2, bs_exp, ga):
    tm = K1_TM
    tiles_per_seq = SEQ // tm
    row1024 = pl.BlockSpec((1, D_MODEL), lambda i: (0, 0))
    row512 = pl.BlockSpec((1, A_WIDTH), lambda i: (0, 0))
    half = pl.BlockSpec((tm, A_WIDTH), lambda i: (i, 0))
    return pl.pallas_call(
        _front_kernel,
        grid=(TOKENS // tm,),
        in_specs=[
            pl.BlockSpec((tm, D_MODEL), lambda i: (i, 0)),
            pl.BlockSpec((None, N_MOD, D_MODEL), lambda i: (i // tiles_per_seq, 0, 0)),
            row1024,
            pl.BlockSpec((D_MODEL, IN_COLS), lambda i: (0, 0)),
            row512, row512,
            pl.BlockSpec((A_GROUPS // 2, 2 * CHUNK, CHUNK), lambda i: (0, 0, 0)),
            pl.BlockSpec((CHUNK, A_WIDTH), lambda i: (0, 0)),
            row512,
        ],
        out_specs=[half, half, half, half],
        out_shape=[jax.ShapeDtypeStruct((TOKENS, A_WIDTH), BF16)] +
                  [jax.ShapeDtypeStruct((TOKENS, B_WIDTH), F32)] * 3,
        compiler_params=_cparams(("arbitrary",)),
        name="front",
    )(x, mod_l, n1g, win, lng, lnb, ws2, bs_exp, ga)


def _t5_bucket_np(dist):
    dist = np.maximum(dist, 0)
    ratio = np.log(np.maximum(dist, 1) / REL_MAX_EXACT) / np.log(REL_MAX_DISTANCE / REL_MAX_EXACT)
    large = REL_MAX_EXACT + np.floor(ratio * (REL_BUCKETS - REL_MAX_EXACT)).astype(np.int64)
    large = np.minimum(large, REL_BUCKETS - 1)
    return np.where(dist < REL_MAX_EXACT, dist, large).astype(np.int32)


def _bias_tables(rel_bias):
    blk = ATTN_BLOCK
    n_rel = 3 * blk
    rel = 2 * blk - 1 - np.arange(n_rel)
    tables = []
    for window, d in DILATED_CONFIGS:
        span = window // d
        bucket = _t5_bucket_np(np.clip(rel, 0, span) * d)
        valid = jnp.asarray((rel >= 0) & (rel <= span))
        w = jnp.where(valid[None, :], rel_bias.astype(F32)[bucket].T, NEG_INF)
        flat = jnp.tile(w, (1, blk))
        skew = flat[:, blk - 1:blk - 1 + blk * (n_rel - 1)].reshape(-1, blk, n_rel - 1)
        tab = skew[:, :, :2 * blk]
        sub = RESIDUES // d
        ln = blk // sub
        tab = tab.reshape(-1, ln, sub, 2, ln, sub).transpose(0, 2, 1, 3, 5, 4)
        tables.append(tab.reshape(-1, blk, 2 * blk) * LOG2E)
    return jnp.stack(tables, axis=0)


def _attn_kernel(q_ref, k_ref, v_ref, bias_ref, o_ref, qp, kp, vp, m_sc, l_sc, acc_sc):
    blk = ATTN_BLOCK
    res = RESIDUES
    per = SEQ // res
    sq = res * res
    seg = sq // res
    lane = lax.broadcasted_iota(jnp.int32, (blk, LANES), 1)
    head0 = lane < HEAD_DIM
    ones = jnp.ones((2 * blk, LANES), BF16)

    pa = lax.broadcasted_iota(jnp.int32, (sq, sq), 0)
    pb = lax.broadcasted_iota(jnp.int32, (sq, sq), 1)
    regroup = jnp.where(pb == res * (pa % res) + pa // res, 1.0, 0.0).astype(BF16)

    def residue_rows(g):
        return [slice(per * r + seg * g, per * r + seg * (g + 1)) for r in range(res)]

    def load(ref, slices):
        return jnp.concatenate([ref[s, :] for s in slices], axis=0) if len(slices) > 1 \
            else ref[slices[0], :]

    def store(ref, slices, val):
        ln = val.shape[0] // len(slices)
        for i, s in enumerate(slices):
            ref[s, :] = val[i * ln:(i + 1) * ln]

    for g in range(SEQ // sq):
        rows = slice(sq * g, sq * (g + 1))
        qk = jnp.concatenate([q_ref[rows, :], k_ref[rows, :]], axis=1).astype(BF16)
        qk = jnp.dot(regroup, qk, preferred_element_type=F32)
        vv = jnp.dot(regroup, v_ref[rows, :].astype(BF16), preferred_element_type=F32)
        store(qp, residue_rows(g), qk[:, :LANES])
        store(kp, residue_rows(g), qk[:, LANES:])
        store(vp, residue_rows(g), vv)

    def merge_heads(t):
        return jnp.where(head0, t[:blk], t[blk:])

    def scores_block(q, kcat, vcat, bias):
        nk = kcat.shape[0]
        zero = jnp.zeros_like(q)
        qs = jnp.concatenate([jnp.where(head0, q, zero), jnp.where(head0, zero, q)],
                             axis=0).astype(BF16)
        s = lax.dot_general(qs, kcat, (((1,), (1,)), ((), ())), preferred_element_type=F32) + bias
        m = jnp.max(s, axis=-1, keepdims=True)
        p = jnp.exp2(s - m).astype(BF16)
        pv = jnp.dot(p, jnp.concatenate([vcat, ones[:nk]], axis=1), preferred_element_type=F32)
        return (merge_heads(jnp.broadcast_to(m, (2 * blk, LANES))),
                merge_heads(pv[:, LANES:]), merge_heads(pv[:, :LANES]))

    def update(ci, slices, m_c, l_c, o_c):
        if ci > 0:
            m_r = load(m_sc, slices)
            m_n = jnp.maximum(m_r, m_c)
            a = jnp.exp2(m_r - m_n)
            b = jnp.exp2(m_c - m_n)
            l_c = a * load(l_sc, slices) + b * l_c
            o_c = a * load(acc_sc, slices) + b * o_c
            m_c = m_n
        store(m_sc, slices, m_c)
        store(l_sc, slices, l_c)
        store(acc_sc, slices, o_c)

    for ci, (window, d) in enumerate(DILATED_CONFIGS):
        nb = SEQ // d // blk
        sub = res // d
        ln = blk // sub
        for r in range(d):
            k_prev = v_prev = None
            for n in range(nb):
                slices = [slice(per * (r + d * c) + ln * n, per * (r + d * c) + ln * (n + 1))
                          for c in range(sub)]
                k_cur = load(kp, slices).astype(BF16)
                v_cur = load(vp, slices).astype(BF16)
                if k_prev is None:
                    bias = jnp.concatenate([bias_ref[ci, 0, :, blk:], bias_ref[ci, 1, :, blk:]],
                                           axis=0)
                    result = scores_block(load(qp, slices), k_cur, v_cur, bias)
                else:
                    bias = jnp.concatenate([bias_ref[ci, 0], bias_ref[ci, 1]], axis=0)
                    result = scores_block(load(qp, slices),
                                          jnp.concatenate([k_prev, k_cur], axis=0),
                                          jnp.concatenate([v_prev, v_cur], axis=0), bias)
                update(ci, slices, *result)
                k_prev, v_prev = k_cur, v_cur

    for g in range(SEQ // sq):
        o = load(acc_sc, residue_rows(g)) / load(l_sc, residue_rows(g))
        hi = o.astype(BF16)
        lo = (o - hi.astype(F32)).astype(BF16)
        back = jnp.dot(regroup, jnp.concatenate([hi, lo], axis=1), preferred_element_type=F32)
        o_ref[sq * g:sq * (g + 1), :] = back[:, :LANES] + back[:, LANES:]


def _attention(q, k, v, bias_tab):
    n_cfg = len(DILATED_CONFIGS)
    blk = ATTN_BLOCK
    seq_spec = pl.BlockSpec((SEQ, HEAD_PAIR), lambda b, p: (b, p))
    return pl.pallas_call(
        _attn_kernel,
        grid=(BATCH, N_PAIRS),
        in_specs=[seq_spec, seq_spec, seq_spec,
                  pl.BlockSpec((n_cfg, 2, blk, 2 * blk), lambda b, p: (0, p, 0, 0))],
        out_specs=seq_spec,
        out_shape=jax.ShapeDtypeStruct((TOKENS, B_WIDTH), F32),
        scratch_shapes=[pltpu.VMEM((SEQ, HEAD_PAIR), F32)] * 6,
        compiler_params=_cparams(("arbitrary", "arbitrary")),
        name="attention",
    )(q, k, v, bias_tab)


K4_TM = 512


def _top2_sum(a, b, c, d):
    hi1, lo1 = jnp.maximum(a, b), jnp.minimum(a, b)
    hi2, lo2 = jnp.maximum(c, d), jnp.minimum(c, d)
    return jnp.maximum(hi1, hi2) + jnp.maximum(jnp.minimum(hi1, hi2), jnp.maximum(lo1, lo2))


def _route(logits_t, rb_col):
    m = jnp.max(logits_t, axis=0, keepdims=True)
    e = jnp.exp(logits_t - m)
    probs = e / jnp.sum(e, axis=0, keepdims=True)
    sel = probs + rb_col
    sel_rows = [sel[i:i + 1, :] for i in range(N_EXPERTS)]
    prob_rows = [probs[i:i + 1, :] for i in range(N_EXPERTS)]
    gsz = EXPERTS_PER_GROUP
    score = [_top2_sum(*sel_rows[g * gsz:(g + 1) * gsz]) for g in range(N_EXPERT_GROUPS)]
    chosen = []
    for g in range(N_EXPERT_GROUPS):
        best = None
        for g2 in range(N_EXPERT_GROUPS):
            if g2 == g:
                continue
            c = (score[g] > score[g2]) if g2 < g else (score[g] >= score[g2])
            best = c if best is None else jnp.logical_and(best, c)
        for i in range(gsz):
            ei = g * gsz + i
            rank = jnp.zeros_like(sel_rows[ei])
            for j in range(gsz):
                if j == i:
                    continue
                ej = g * gsz + j
                ahead = (sel_rows[ej] >= sel_rows[ei]) if j < i else (sel_rows[ej] > sel_rows[ei])
                rank = rank + jnp.where(ahead, 1.0, 0.0)
            chosen.append(jnp.logical_and(best, rank < float(2)))
    picked = [jnp.where(chosen[i], prob_rows[i], 0.0) for i in range(N_EXPERTS)]
    denom = picked[0]
    for i in range(1, N_EXPERTS):
        denom = denom + picked[i]
    gates = jnp.concatenate([pk / denom for pk in picked], axis=0)
    onehot = jnp.concatenate([jnp.where(ch, 1.0, 0.0) for ch in chosen], axis=0)
    return gates, onehot


def _mid_kernel(a_ref, ob_ref, x_ref, mod_ref, gb_ref, wout_ref, n2g_ref, rwt_ref, rb_ref,
                x1_ref, h2_ref, gates_ref, onehot_ref, count_ref):
    ob = ob_ref[...]
    bn = ob * lax.rsqrt(jnp.mean(ob * ob, axis=-1, keepdims=True) + EPS) * gb_ref[...]
    mixed = jnp.dot(a_ref[...], wout_ref[:A_WIDTH, :], preferred_element_type=F32)
    mixed = mixed + jnp.dot(bn.astype(BF16), wout_ref[A_WIDTH:, :], preferred_element_type=F32)
    x1 = x_ref[...] + mod_ref[2:3, :] * mixed
    x1_ref[...] = x1
    h2 = x1 * lax.rsqrt(jnp.mean(x1 * x1, axis=-1, keepdims=True) + EPS) * n2g_ref[...]
    h2 = (h2 * (1.0 + mod_ref[4:5, :]) + mod_ref[3:4, :]).astype(BF16)
    h2_ref[...] = h2
    logits_t = lax.dot_general(rwt_ref[...], h2, (((1,), (1,)), ((), ())),
                               preferred_element_type=F32)
    gates_t, onehot_t = _route(logits_t, rb_ref[...])
    gates_ref[...] = gates_t
    onehot_ref[...] = onehot_t
    for s in range(count_ref.shape[0]):
        count_ref[s] = jnp.sum(onehot_t[:, s * MOE_TOK:(s + 1) * MOE_TOK], axis=1, keepdims=True)


def _mid(out_a, out_b, x, mod_l, gb, wout, n2g, rwt, rb_col):
    tm = K4_TM
    tiles_per_seq = SEQ // tm
    tok = pl.BlockSpec((tm, D_MODEL), lambda i: (i, 0))
    half = pl.BlockSpec((tm, A_WIDTH), lambda i: (i, 0))
    route = pl.BlockSpec((N_EXPERTS, tm), lambda i: (0, i))
    return pl.pallas_call(
        _mid_kernel,
        grid=(TOKENS // tm,),
        in_specs=[
            half, half, tok,
            pl.BlockSpec((None, N_MOD, D_MODEL), lambda i: (i // tiles_per_seq, 0, 0)),
            pl.BlockSpec((1, B_WIDTH), lambda i: (0, 0)),
            pl.BlockSpec((D_MODEL, D_MODEL), lambda i: (0, 0)),
            pl.BlockSpec((1, D_MODEL), lambda i: (0, 0)),
            pl.BlockSpec((N_EXPERTS, D_MODEL), lambda i: (0, 0)),
            pl.BlockSpec((N_EXPERTS, 1), lambda i: (0, 0)),
        ],
        out_specs=[tok, tok, route, route,
                   pl.BlockSpec((tm // MOE_TOK, N_EXPERTS, 1), lambda i: (i, 0, 0))],
        out_shape=[jax.ShapeDtypeStruct((TOKENS, D_MODEL), F32),
                   jax.ShapeDtypeStruct((TOKENS, D_MODEL), BF16),
                   jax.ShapeDtypeStruct((N_EXPERTS, TOKENS), F32),
                   jax.ShapeDtypeStruct((N_EXPERTS, TOKENS), F32),
                   jax.ShapeDtypeStruct((TOKENS // MOE_TOK, N_EXPERTS, 1), F32)],
        compiler_params=_cparams(("arbitrary",)),
        name="mid",
    )(out_a, out_b, x, mod_l, gb, wout, n2g, rwt, rb_col)


MOE_TOK = 256
ROW_ALIGN = 16
MOE_LB = 2 * MOE_TOK + N_EXPERTS * ROW_ALIGN
FFN_TM = 512
N_TOK_TILES = TOKENS // MOE_TOK
ROWS_MAX = -(-(2 * TOKENS + N_TOK_TILES * N_EXPERTS * (ROW_ALIGN - 1)
               + N_EXPERTS * (FFN_TM - ROW_ALIGN)) // FFN_TM) * FFN_TM


def _moe_plan(count):
    n = count.reshape(N_TOK_TILES, N_EXPERTS).astype(jnp.int32)
    n8 = (n + (ROW_ALIGN - 1)) // ROW_ALIGN * ROW_ALIGN
    local = jnp.cumsum(n8, axis=1) - n8
    seg = (jnp.sum(n8, axis=0) + (FFN_TM - 1)) // FFN_TM * FFN_TM
    seg_end = jnp.cumsum(seg)
    glob = (seg_end - seg)[None, :] + jnp.cumsum(n8, axis=0) - n8
    n_ffn = ROWS_MAX // FFN_TM
    n_active = seg_end[-1] // FFN_TM
    tile_row = jnp.arange(n_ffn, dtype=jnp.int32) * FFN_TM
    texp = jnp.sum((tile_row[:, None] >= seg_end[None, :]).astype(jnp.int32), axis=1)
    texp = jnp.minimum(texp, N_EXPERTS - 1)
    texp = jnp.where(jnp.arange(n_ffn) < n_active, texp, texp[n_active - 1])
    return dict(n8=n8.reshape(-1), local=local.reshape(-1), glob=glob.reshape(-1),
                local_col=local.astype(F32).reshape(N_TOK_TILES, N_EXPERTS, 1),
                texp=texp.astype(jnp.int32), n_active=n_active.reshape(1).astype(jnp.int32))


def _run_copies(tile, n8_ref, local_ref, glob_ref, make_copy, start):
    for e in range(N_EXPERTS):
        idx = tile * N_EXPERTS + e
        n = n8_ref[idx]

        @pl.when(n > 0)
        def _(idx=idx, n=n):
            cp = make_copy(pl.multiple_of(local_ref[idx], ROW_ALIGN),
                           pl.multiple_of(glob_ref[idx], ROW_ALIGN),
                           pl.multiple_of(n, ROW_ALIGN))
            if start:
                cp.start()
            else:
                cp.wait()


def _sorted_rows(onehot, local_col):
    tm = onehot.shape[1]
    src = lax.broadcasted_iota(jnp.int32, (tm, tm), 0)
    dst = lax.broadcasted_iota(jnp.int32, (tm, tm), 1)
    before = jnp.where(src < dst, 1.0, 0.0).astype(BF16)
    rank = jnp.dot(onehot.astype(BF16), before, preferred_element_type=F32)
    pos = local_col + rank
    chosen = onehot > 0.5
    lo = jnp.min(jnp.where(chosen, pos, float(MOE_LB)), axis=0, keepdims=True)
    hi = jnp.max(jnp.where(chosen, pos, -1.0), axis=0, keepdims=True)
    return pos, chosen, lo, hi


def _dispatch_kernel(n8_ref, local_ref, glob_ref, h_ref, oh_ref, lcol_ref, xs_ref, lbuf, sem):
    j = pl.program_id(0)
    last = pl.num_programs(0) - 1
    slot = j % 2
    _, _, lo, hi = _sorted_rows(oh_ref[...], lcol_ref[...])
    row = lax.broadcasted_iota(jnp.int32, (MOE_LB, MOE_TOK), 0).astype(F32)
    perm = (jnp.where(row == lo, 1.0, 0.0) + jnp.where(row == hi, 1.0, 0.0)).astype(BF16)
    lbuf[slot] = jnp.dot(perm, h_ref[...], preferred_element_type=F32).astype(BF16)

    def copy_from(slot_):
        def make(lo_, go_, size):
            return pltpu.make_async_copy(lbuf.at[slot_, pl.ds(lo_, size), :],
                                         xs_ref.at[pl.ds(go_, size), :], sem.at[slot_])
        return make

    refs = (n8_ref, local_ref, glob_ref)
    _run_copies(j, *refs, copy_from(slot), True)

    @pl.when(j > 0)
    def _():
        _run_copies(j - 1, *refs, copy_from(1 - slot), False)

    @pl.when(j == last)
    def _():
        _run_copies(j, *refs, copy_from(slot), False)


def _dispatch(plan, h2, onehot):
    grid_spec = pltpu.PrefetchScalarGridSpec(
        num_scalar_prefetch=3,
        grid=(N_TOK_TILES,),
        in_specs=[
            pl.BlockSpec((MOE_TOK, D_MODEL), lambda j, *_: (j, 0)),
            pl.BlockSpec((N_EXPERTS, MOE_TOK), lambda j, *_: (0, j)),
            pl.BlockSpec((None, N_EXPERTS, 1), lambda j, *_: (j, 0, 0)),
        ],
        out_specs=pl.BlockSpec(memory_space=pl.ANY),
        scratch_shapes=[pltpu.VMEM((2, MOE_LB, D_MODEL), BF16), pltpu.SemaphoreType.DMA((2,))],
    )
    return pl.pallas_call(
        _dispatch_kernel,
        grid_spec=grid_spec,
        out_shape=jax.ShapeDtypeStruct((ROWS_MAX, D_MODEL), BF16),
        compiler_params=_cparams(("arbitrary",)),
        name="dispatch",
    )(plan["n8"], plan["local"], plan["glob"], h2, onehot, plan["local_col"])


def _experts_kernel(texp_ref, nact_ref, xs_ref, wg_ref, wu_ref, wd_ref, ys_ref, wg_b, wu_b, wd_b):
    i = pl.program_id(0)
    active = i < nact_ref[0]
    new_expert = jnp.logical_or(i == 0, texp_ref[i] != texp_ref[jnp.maximum(i - 1, 0)])

    @pl.when(jnp.logical_and(active, new_expert))
    def _():
        wg_b[...] = wg_ref[...].astype(BF16)
        wu_b[...] = wu_ref[...].astype(BF16)
        wd_b[...] = wd_ref[...].astype(BF16)

    @pl.when(active)
    def _():
        x = xs_ref[...]
        hg = jnp.dot(x, wg_b[...], preferred_element_type=F32)
        hu = jnp.dot(x, wu_b[...], preferred_element_type=F32)
        act = (jax.nn.silu(hg) * hu).astype(BF16)
        ys_ref[...] = jnp.dot(act, wd_b[...], preferred_element_type=F32).astype(BF16)


def _experts(plan, xs, layer, wg, wu, wd):
    def rows(i, texp, nact):
        return (jnp.minimum(i, nact[0] - 1), 0)

    def expert(i, texp, nact):
        return (layer, texp[i], 0, 0)

    grid_spec = pltpu.PrefetchScalarGridSpec(
        num_scalar_prefetch=2,
        grid=(ROWS_MAX // FFN_TM,),
        in_specs=[
            pl.BlockSpec((FFN_TM, D_MODEL), rows),
            pl.BlockSpec((None, None, D_MODEL, D_EXPERT), expert),
            pl.BlockSpec((None, None, D_MODEL, D_EXPERT), expert),
            pl.BlockSpec((None, None, D_EXPERT, D_MODEL), expert),
        ],
        out_specs=pl.BlockSpec((FFN_TM, D_MODEL), rows),
        scratch_shapes=[pltpu.VMEM((D_MODEL, D_EXPERT), BF16), pltpu.VMEM((D_MODEL, D_EXPERT), BF16),
                        pltpu.VMEM((D_EXPERT, D_MODEL), BF16)],
    )
    return pl.pallas_call(
        _experts_kernel,
        grid_spec=grid_spec,
        out_shape=jax.ShapeDtypeStruct((ROWS_MAX, D_MODEL), BF16),
        compiler_params=_cparams(("arbitrary",)),
        name="experts",
    )(plan["texp"], plan["n_active"], xs, wg, wu, wd)


def _combine_kernel(n8_ref, local_ref, glob_ref, oh_ref, g_ref, lcol_ref, x_ref, mod_ref, fg_ref,
                    ys_ref, o_ref, ybuf, sem, *, final_norm):
    j = pl.program_id(0)
    n_tiles = pl.num_programs(0)
    slot = j % 2
    refs = (n8_ref, local_ref, glob_ref)

    def copy_to(slot_):
        def make(lo_, go_, size):
            return pltpu.make_async_copy(ys_ref.at[pl.ds(go_, size), :],
                                         ybuf.at[slot_, pl.ds(lo_, size), :], sem.at[slot_])
        return make

    @pl.when(j == 0)
    def _():
        ybuf[...] = jnp.zeros(ybuf.shape, BF16)
        _run_copies(0, *refs, copy_to(0), True)

    @pl.when(j + 1 < n_tiles)
    def _():
        _run_copies(j + 1, *refs, copy_to(1 - slot), True)

    pos, chosen, lo, hi = _sorted_rows(oh_ref[...], lcol_ref[...])
    gate = jnp.where(chosen, g_ref[...], 0.0)
    g_lo = jnp.sum(jnp.where(pos == lo, gate, 0.0), axis=0, keepdims=True)
    g_hi = jnp.sum(jnp.where(pos == hi, gate, 0.0), axis=0, keepdims=True)
    row = lax.broadcasted_iota(jnp.int32, (MOE_LB, MOE_TOK), 0).astype(F32)
    perm = (jnp.where(row == lo, g_lo, 0.0) + jnp.where(row == hi, g_hi, 0.0)).astype(BF16)

    _run_copies(j, *refs, copy_to(slot), False)
    y = lax.dot_general(perm, ybuf[slot], (((0,), (0,)), ((), ())),
                        preferred_element_type=F32)
    x = x_ref[...] + mod_ref[5:6, :] * y
    if final_norm:
        x = x * lax.rsqrt(jnp.mean(x * x, axis=-1, keepdims=True) + EPS) * fg_ref[...]
    o_ref[...] = x


def _combine(plan, onehot, gates, ys, x, mod_l, final_g, final_norm):
    tiles_per_seq = SEQ // MOE_TOK
    route = pl.BlockSpec((N_EXPERTS, MOE_TOK), lambda j, *_: (0, j))
    tok = pl.BlockSpec((MOE_TOK, D_MODEL), lambda j, *_: (j, 0))
    grid_spec = pltpu.PrefetchScalarGridSpec(
        num_scalar_prefetch=3,
        grid=(N_TOK_TILES,),
        in_specs=[
            route, route,
            pl.BlockSpec((None, N_EXPERTS, 1), lambda j, *_: (j, 0, 0)),
            tok,
            pl.BlockSpec((None, N_MOD, D_MODEL), lambda j, *_: (j // tiles_per_seq, 0, 0)),
            pl.BlockSpec((1, D_MODEL), lambda j, *_: (0, 0)),
            pl.BlockSpec(memory_space=pl.ANY),
        ],
        out_specs=tok,
        scratch_shapes=[pltpu.VMEM((2, MOE_LB, D_MODEL), BF16), pltpu.SemaphoreType.DMA((2,))],
    )
    return pl.pallas_call(
        functools.partial(_combine_kernel, final_norm=final_norm),
        grid_spec=grid_spec,
        out_shape=jax.ShapeDtypeStruct((TOKENS, D_MODEL), F32),
        compiler_params=_cparams(("arbitrary",)),
        name="combine",
    )(plan["n8"], plan["local"], plan["glob"], onehot, gates, plan["local_col"], x, mod_l,
      final_g, ys)


def kernel(x, c, rel_bias, router_w, router_b, mod_w, mod_b, norm1_g, w_in, gmlp_ln_g, gmlp_ln_b,
           gmlp_ws, gmlp_bs, out_norm_a_g, out_norm_b_g, w_out, norm2_g, moe_w_gate, moe_w_up,
           moe_w_down, final_g):
    mod = _modulation(c, mod_w, mod_b).reshape(DEPTH, BATCH, N_MOD, D_MODEL)
    bias_tab = _bias_tables(rel_bias)
    rwt = router_w.T.astype(BF16)
    rb_col = router_b.reshape(N_EXPERTS, 1)
    xt = x.reshape(TOKENS, D_MODEL)
    win_b = w_in.astype(BF16)
    wout_b = w_out.astype(BF16)
    for l in range(DEPTH):
        ws2 = gmlp_ws[l].astype(BF16).reshape(A_GROUPS // 2, 2 * CHUNK, CHUNK)
        bs_exp = jnp.repeat(gmlp_bs[l].T, HEAD_DIM, axis=1)
        out_a, q, k, v = _front(
            xt, mod[l], norm1_g[l].reshape(1, -1), win_b[l], gmlp_ln_g[l].reshape(1, -1),
            gmlp_ln_b[l].reshape(1, -1), ws2, bs_exp, out_norm_a_g[l].reshape(1, -1))
        out_b = _attention(q, k, v, bias_tab)
        xt, h2, gates, onehot, count = _mid(
            out_a, out_b, xt, mod[l], out_norm_b_g[l].reshape(1, -1),
            wout_b[l], norm2_g[l].reshape(1, -1), rwt, rb_col)
        plan = _moe_plan(count)
        xs = _dispatch(plan, h2, onehot)
        ys = _experts(plan, xs, l, moe_w_gate, moe_w_up, moe_w_down)
        xt = _combine(plan, onehot, gates, ys, xt, mod[l], final_g.reshape(1, -1),
                      final_norm=(l == DEPTH - 1))
    return xt.reshape(BATCH, SEQ, D_MODEL)
```

```python
import functools
import math

import numpy as np
import jax
import jax.numpy as jnp
from jax import lax
from jax.experimental import pallas as pl
from jax.experimental.pallas import tpu as pltpu

D_MODEL = 1024
BATCH = 8
SEQ = 2048
DEPTH = 2
TOKENS = BATCH * SEQ
HEAD_DIM = 64
A_WIDTH = 512
B_WIDTH = 512
A_GROUPS = 8
IN_COLS = 2 * A_WIDTH + 3 * B_WIDTH
CHUNK = 128
DILATED_CONFIGS = ((128, 1), (512, 4), (2048, 16))
ATTN_BLOCK = 128
REL_BUCKETS = 32
REL_MAX_EXACT = REL_BUCKETS // 2
REL_MAX_DISTANCE = 2048
N_EXPERTS = 16
N_EXPERT_GROUPS = 4
EXPERTS_PER_GROUP = 4
D_EXPERT = 512
N_MOD = 6
EPS = 1e-6
NEG_INF = -1e30

LANES = 128
HEAD_PAIR = 2 * HEAD_DIM
N_PAIRS = B_WIDTH // HEAD_PAIR
RESIDUES = max(d for _, d in DILATED_CONFIGS)
LOG2E = math.log2(math.e)

F32 = jnp.float32
BF16 = jnp.bfloat16

VMEM_LIMIT = 56 * 1024 * 1024


def _cparams(sem):
    return pltpu.CompilerParams(dimension_semantics=sem, vmem_limit_bytes=VMEM_LIMIT)


def _gelu(x):
    return 0.5 * x * (1.0 + lax.erf(x * math.sqrt(0.5)))


MOD_TN = 1024


def _mod_kernel(c_ref, w_ref, b_ref, o_ref):
    ca = jax.nn.silu(c_ref[...])
    o_ref[...] = jnp.dot(ca.astype(BF16), w_ref[...].astype(BF16),
                         preferred_element_type=F32) + b_ref[...]


def _modulation(c, mod_w, mod_b):
    n_cols = N_MOD * D_MODEL
    return pl.pallas_call(
        _mod_kernel,
        grid=(DEPTH, n_cols // MOD_TN),
        in_specs=[
            pl.BlockSpec((BATCH, D_MODEL), lambda l, j: (0, 0)),
            pl.BlockSpec((None, D_MODEL, MOD_TN), lambda l, j: (l, 0, j)),
            pl.BlockSpec((None, 1, MOD_TN), lambda l, j: (l, 0, j)),
        ],
        out_specs=pl.BlockSpec((None, BATCH, MOD_TN), lambda l, j: (l, 0, j)),
        out_shape=jax.ShapeDtypeStruct((DEPTH, BATCH, n_cols), F32),
        compiler_params=_cparams(("arbitrary", "arbitrary")),
        name="modulation",
    )(c, mod_w, mod_b.reshape(DEPTH, 1, n_cols))


K1_TM = 512


def _front_kernel(x_ref, mod_ref, n1g_ref, win_ref, lng_ref, lnb_ref, ws_ref, bs_ref, ga_ref,
                  a_ref, q_ref, k_ref, v_ref):
    x = x_ref[...]
    tm = x.shape[0]
    h = x * lax.rsqrt(jnp.mean(x * x, axis=-1, keepdims=True) + EPS) * n1g_ref[...]
    h = h * (1.0 + mod_ref[1:2, :]) + mod_ref[0:1, :]
    proj = jnp.dot(h.astype(BF16), win_ref[...], preferred_element_type=F32)

    q_ref[...] = proj[:, 2 * A_WIDTH:2 * A_WIDTH + B_WIDTH] * (HEAD_DIM ** -0.5 * LOG2E)
    k_ref[...] = proj[:, 2 * A_WIDTH + B_WIDTH:2 * A_WIDTH + 2 * B_WIDTH]
    v_ref[...] = proj[:, 2 * A_WIDTH + 2 * B_WIDTH:]

    u = _gelu(proj[:, :A_WIDTH])
    va = _gelu(proj[:, A_WIDTH:2 * A_WIDTH])
    mu = jnp.mean(va, axis=-1, keepdims=True)
    vc = va - mu
    vln = vc * lax.rsqrt(jnp.mean(vc * vc, axis=-1, keepdims=True) + EPS)
    vln = (vln * lng_ref[...] + lnb_ref[...]).astype(BF16)

    row = lax.broadcasted_iota(jnp.int32, (2 * CHUNK, CHUNK), 0)
    col = lax.broadcasted_iota(jnp.int32, (2 * CHUNK, CHUNK), 1)
    causal = (row % CHUNK) >= col
    first_group = lax.broadcasted_iota(jnp.int32, (CHUNK, LANES), 1) < HEAD_DIM
    wmix = [jnp.where(causal, ws_ref[p], jnp.zeros((), BF16)) for p in range(A_GROUPS // 2)]
    for c in range(tm // CHUNK):
        rows = slice(c * CHUNK, (c + 1) * CHUNK)
        parts = []
        for p in range(A_GROUPS // 2):
            vp = vln[rows, p * LANES:(p + 1) * LANES]
            r = jnp.dot(wmix[p], vp, preferred_element_type=F32)
            parts.append(jnp.where(first_group, r[:CHUNK], r[CHUNK:]))
        s = jnp.concatenate(parts, axis=-1) + bs_ref[...]
        oa = u[rows] * s
        oa = oa * lax.rsqrt(jnp.mean(oa * oa, axis=-1, keepdims=True) + EPS) * ga_ref[...]
        a_ref[rows, :] = oa.astype(BF16)


def _front(x, mod_l, n1g, win, lng, lnb, ws2, bs_exp, ga):
    tm = K1_TM
    tiles_per_seq = SEQ // tm
    row1024 = pl.BlockSpec((1, D_MODEL), lambda i: (0, 0))
    row512 = pl.BlockSpec((1, A_WIDTH), lambda i: (0, 0))
    half = pl.BlockSpec((tm, A_WIDTH), lambda i: (i, 0))
    return pl.pallas_call(
        _front_kernel,
        grid=(TOKENS // tm,),
        in_specs=[
            pl.BlockSpec((tm, D_MODEL), lambda i: (i, 0)),
            pl.BlockSpec((None, N_MOD, D_MODEL), lambda i: (i // tiles_per_seq, 0, 0)),
            row1024,
            pl.BlockSpec((D_MODEL, IN_COLS), lambda i: (0, 0)),
            row512, row512,
            pl.BlockSpec((A_GROUPS // 2, 2 * CHUNK, CHUNK), lambda i: (0, 0, 0)),
            pl.BlockSpec((CHUNK, A_WIDTH), lambda i: (0, 0)),
            row512,
        ],
        out_specs=[half, half, half, half],
        out_shape=[jax.ShapeDtypeStruct((TOKENS, A_WIDTH), BF16)] +
                  [jax.ShapeDtypeStruct((TOKENS, B_WIDTH), F32)] * 3,
        compiler_params=_cparams(("arbitrary",)),
        name="front",
    )(x, mod_l, n1g, win, lng, lnb, ws2, bs_exp, ga)


def _t5_bucket_np(dist):
    dist = np.maximum(dist, 0)
    ratio = np.log(np.maximum(dist, 1) / REL_MAX_EXACT) / np.log(REL_MAX_DISTANCE / REL_MAX_EXACT)
    large = REL_MAX_EXACT + np.floor(ratio * (REL_BUCKETS - REL_MAX_EXACT)).astype(np.int64)
    large = np.minimum(large, REL_BUCKETS - 1)
    return np.where(dist < REL_MAX_EXACT, dist, large).astype(np.int32)


def _bias_tables(rel_bias):
    blk = ATTN_BLOCK
    n_rel = 3 * blk
    rel = 2 * blk - 1 - np.arange(n_rel)
    tables = []
    for window, d in DILATED_CONFIGS:
        span = window // d
        bucket = _t5_bucket_np(np.clip(rel, 0, span) * d)
        valid = jnp.asarray((rel >= 0) & (rel <= span))
        w = jnp.where(valid[None, :], rel_bias.astype(F32)[bucket].T, NEG_INF)
        flat = jnp.tile(w, (1, blk))
        skew = flat[:, blk - 1:blk - 1 + blk * (n_rel - 1)].reshape(-1, blk, n_rel - 1)
        tab = skew[:, :, :2 * blk]
        sub = RESIDUES // d
        ln = blk // sub
        pos = np.arange(blk)
        rows = np.eye(blk, dtype=np.float32)[(pos % ln) * sub + pos // ln]
        cols = np.kron(np.eye(2, dtype=np.float32), rows)
        tab = jnp.einsum('pi,hik,qk->hpq', rows, tab * LOG2E, cols,
                         precision=lax.Precision.HIGHEST)
        tables.append(tab)
    return jnp.stack(tables, axis=0)


def _attn_kernel(q_ref, k_ref, v_ref, bias_ref, o_ref, qp, kp, vp, m_sc, l_sc, acc_sc):
    blk = ATTN_BLOCK
    res = RESIDUES
    per = SEQ // res
    sq = res * res
    seg = sq // res
    lane = lax.broadcasted_iota(jnp.int32, (blk, LANES), 1)
    head0 = lane < HEAD_DIM
    ones = jnp.ones((2 * blk, LANES), BF16)

    pa = lax.broadcasted_iota(jnp.int32, (sq, sq), 0)
    pb = lax.broadcasted_iota(jnp.int32, (sq, sq), 1)
    regroup = jnp.where(pb == res * (pa % res) + pa // res, 1.0, 0.0).astype(BF16)

    def residue_rows(g):
        return [slice(per * r + seg * g, per * r + seg * (g + 1)) for r in range(res)]

    def load(ref, slices):
        return jnp.concatenate([ref[s, :] for s in slices], axis=0) if len(slices) > 1 \
            else ref[slices[0], :]

    def store(ref, slices, val):
        ln = val.shape[0] // len(slices)
        for i, s in enumerate(slices):
            ref[s, :] = val[i * ln:(i + 1) * ln]

    for g in range(SEQ // sq):
        rows = slice(sq * g, sq * (g + 1))
        qk = jnp.concatenate([q_ref[rows, :], k_ref[rows, :]], axis=1).astype(BF16)
        qk = jnp.dot(regroup, qk, preferred_element_type=F32)
        vv = jnp.dot(regroup, v_ref[rows, :].astype(BF16), preferred_element_type=F32)
        store(qp, residue_rows(g), qk[:, :LANES])
        store(kp, residue_rows(g), qk[:, LANES:])
        store(vp, residue_rows(g), vv)

    def merge_heads(t):
        return jnp.where(head0, t[:blk], t[blk:])

    def scores_block(q, kcat, vcat, bias):
        nk = kcat.shape[0]
        zero = jnp.zeros_like(q)
        qs = jnp.concatenate([jnp.where(head0, q, zero), jnp.where(head0, zero, q)],
                             axis=0).astype(BF16)
        s = lax.dot_general(qs, kcat, (((1,), (1,)), ((), ())), preferred_element_type=F32) + bias
        m = jnp.max(s, axis=-1, keepdims=True)
        p = jnp.exp2(s - m).astype(BF16)
        pv = jnp.dot(p, jnp.concatenate([vcat, ones[:nk]], axis=1), preferred_element_type=F32)
        return (merge_heads(jnp.broadcast_to(m, (2 * blk, LANES))),
                merge_heads(pv[:, LANES:]), merge_heads(pv[:, :LANES]))

    def update(ci, slices, m_c, l_c, o_c):
        if ci > 0:
            m_r = load(m_sc, slices)
            m_n = jnp.maximum(m_r, m_c)
            a = jnp.exp2(m_r - m_n)
            b = jnp.exp2(m_c - m_n)
            l_c = a * load(l_sc, slices) + b * l_c
            o_c = a * load(acc_sc, slices) + b * o_c
            m_c = m_n
        store(m_sc, slices, m_c)
        store(l_sc, slices, l_c)
        store(acc_sc, slices, o_c)

    for ci, (window, d) in enumerate(DILATED_CONFIGS):
        nb = SEQ // d // blk
        sub = res // d
        ln = blk // sub
        for r in range(d):
            k_prev = v_prev = None
            for n in range(nb):
                slices = [slice(per * (r + d * c) + ln * n, per * (r + d * c) + ln * (n + 1))
                          for c in range(sub)]
                k_cur = load(kp, slices).astype(BF16)
                v_cur = load(vp, slices).astype(BF16)
                if k_prev is None:
                    bias = jnp.concatenate([bias_ref[ci, 0, :, blk:], bias_ref[ci, 1, :, blk:]],
                                           axis=0)
                    result = scores_block(load(qp, slices), k_cur, v_cur, bias)
                else:
                    bias = jnp.concatenate([bias_ref[ci, 0], bias_ref[ci, 1]], axis=0)
                    result = scores_block(load(qp, slices),
                                          jnp.concatenate([k_prev, k_cur], axis=0),
                                          jnp.concatenate([v_prev, v_cur], axis=0), bias)
                update(ci, slices, *result)
                k_prev, v_prev = k_cur, v_cur

    for g in range(SEQ // sq):
        o = load(acc_sc, residue_rows(g)) / load(l_sc, residue_rows(g))
        hi = o.astype(BF16)
        lo = (o - hi.astype(F32)).astype(BF16)
        back = jnp.dot(regroup, jnp.concatenate([hi, lo], axis=1), preferred_element_type=F32)
        o_ref[sq * g:sq * (g + 1), :] = back[:, :LANES] + back[:, LANES:]


def _attention(q, k, v, bias_tab):
    n_cfg = len(DILATED_CONFIGS)
    blk = ATTN_BLOCK
    seq_spec = pl.BlockSpec((SEQ, HEAD_PAIR), lambda b, p: (b, p))
    return pl.pallas_call(
        _attn_kernel,
        grid=(BATCH, N_PAIRS),
        in_specs=[seq_spec, seq_spec, seq_spec,
                  pl.BlockSpec((n_cfg, 2, blk, 2 * blk), lambda b, p: (0, p, 0, 0))],
        out_specs=seq_spec,
        out_shape=jax.ShapeDtypeStruct((TOKENS, B_WIDTH), F32),
        scratch_shapes=[pltpu.VMEM((SEQ, HEAD_PAIR), F32)] * 6,
        compiler_params=_cparams(("arbitrary", "arbitrary")),
        name="attention",
    )(q, k, v, bias_tab)


K4_TM = 512


def _top2_sum(a, b, c, d):
    hi1, lo1 = jnp.maximum(a, b), jnp.minimum(a, b)
    hi2, lo2 = jnp.maximum(c, d), jnp.minimum(c, d)
    return jnp.maximum(hi1, hi2) + jnp.maximum(jnp.minimum(hi1, hi2), jnp.maximum(lo1, lo2))


def _route(logits_t, rb_col):
    m = jnp.max(logits_t, axis=0, keepdims=True)
    e = jnp.exp(logits_t - m)
    probs = e / jnp.sum(e, axis=0, keepdims=True)
    sel = probs + rb_col
    sel_rows = [sel[i:i + 1, :] for i in range(N_EXPERTS)]
    prob_rows = [probs[i:i + 1, :] for i in range(N_EXPERTS)]
    gsz = EXPERTS_PER_GROUP
    score = [_top2_sum(*sel_rows[g * gsz:(g + 1) * gsz]) for g in range(N_EXPERT_GROUPS)]
    chosen = []
    for g in range(N_EXPERT_GROUPS):
        best = None
        for g2 in range(N_EXPERT_GROUPS):
            if g2 == g:
                continue
            c = (score[g] > score[g2]) if g2 < g else (score[g] >= score[g2])
            best = c if best is None else jnp.logical_and(best, c)
        for i in range(gsz):
            ei = g * gsz + i
            rank = jnp.zeros_like(sel_rows[ei])
            for j in range(gsz):
                if j == i:
                    continue
                ej = g * gsz + j
                ahead = (sel_rows[ej] >= sel_rows[ei]) if j < i else (sel_rows[ej] > sel_rows[ei])
                rank = rank + jnp.where(ahead, 1.0, 0.0)
            chosen.append(jnp.logical_and(best, rank < float(2)))
    picked = [jnp.where(chosen[i], prob_rows[i], 0.0) for i in range(N_EXPERTS)]
    denom = picked[0]
    for i in range(1, N_EXPERTS):
        denom = denom + picked[i]
    gates = jnp.concatenate([pk / denom for pk in picked], axis=0)
    onehot = jnp.concatenate([jnp.where(ch, 1.0, 0.0) for ch in chosen], axis=0)
    return gates, onehot


def _mid_kernel(a_ref, ob_ref, x_ref, mod_ref, gb_ref, wout_ref, n2g_ref, rwt_ref, rb_ref,
                x1_ref, h2_ref, gates_ref, onehot_ref, count_ref):
    ob = ob_ref[...]
    bn = ob * lax.rsqrt(jnp.mean(ob * ob, axis=-1, keepdims=True) + EPS) * gb_ref[...]
    mixed = jnp.dot(a_ref[...], wout_ref[:A_WIDTH, :], preferred_element_type=F32)
    mixed = mixed + jnp.dot(bn.astype(BF16), wout_ref[A_WIDTH:, :], preferred_element_type=F32)
    x1 = x_ref[...] + mod_ref[2:3, :] * mixed
    x1_ref[...] = x1
    h2 = x1 * lax.rsqrt(jnp.mean(x1 * x1, axis=-1, keepdims=True) + EPS) * n2g_ref[...]
    h2 = (h2 * (1.0 + mod_ref[4:5, :]) + mod_ref[3:4, :]).astype(BF16)
    h2_ref[...] = h2
    logits_t = lax.dot_general(rwt_ref[...], h2, (((1,), (1,)), ((), ())),
                               preferred_element_type=F32)
    gates_t, onehot_t = _route(logits_t, rb_ref[...])
    gates_ref[...] = gates_t
    onehot_ref[...] = onehot_t
    for s in range(count_ref.shape[0]):
        count_ref[s] = jnp.sum(onehot_t[:, s * MOE_TOK:(s + 1) * MOE_TOK], axis=1, keepdims=True)


def _mid(out_a, out_b, x, mod_l, gb, wout, n2g, rwt, rb_col):
    tm = K4_TM
    tiles_per_seq = SEQ // tm
    tok = pl.BlockSpec((tm, D_MODEL), lambda i: (i, 0))
    half = pl.BlockSpec((tm, A_WIDTH), lambda i: (i, 0))
    route = pl.BlockSpec((N_EXPERTS, tm), lambda i: (0, i))
    return pl.pallas_call(
        _mid_kernel,
        grid=(TOKENS // tm,),
        in_specs=[
            half, half, tok,
            pl.BlockSpec((None, N_MOD, D_MODEL), lambda i: (i // tiles_per_seq, 0, 0)),
            pl.BlockSpec((1, B_WIDTH), lambda i: (0, 0)),
            pl.BlockSpec((D_MODEL, D_MODEL), lambda i: (0, 0)),
            pl.BlockSpec((1, D_MODEL), lambda i: (0, 0)),
            pl.BlockSpec((N_EXPERTS, D_MODEL), lambda i: (0, 0)),
            pl.BlockSpec((N_EXPERTS, 1), lambda i: (0, 0)),
        ],
        out_specs=[tok, tok, route, route,
                   pl.BlockSpec((tm // MOE_TOK, N_EXPERTS, 1), lambda i: (i, 0, 0))],
        out_shape=[jax.ShapeDtypeStruct((TOKENS, D_MODEL), F32),
                   jax.ShapeDtypeStruct((TOKENS, D_MODEL), BF16),
                   jax.ShapeDtypeStruct((N_EXPERTS, TOKENS), F32),
                   jax.ShapeDtypeStruct((N_EXPERTS, TOKENS), F32),
                   jax.ShapeDtypeStruct((TOKENS // MOE_TOK, N_EXPERTS, 1), F32)],
        compiler_params=_cparams(("arbitrary",)),
        name="mid",
    )(out_a, out_b, x, mod_l, gb, wout, n2g, rwt, rb_col)


MOE_TOK = 256
ROW_ALIGN = 16
MOE_SUB = 2
MOE_LB = 2 * MOE_TOK + N_EXPERTS * ROW_ALIGN
FFN_TM = 1024
N_TOK_TILES = TOKENS // MOE_TOK
ROWS_MAX = -(-(2 * TOKENS + N_TOK_TILES * N_EXPERTS * (ROW_ALIGN - 1)
               + N_EXPERTS * (FFN_TM - ROW_ALIGN)) // FFN_TM) * FFN_TM


def _moe_plan(count):
    n = count.reshape(N_TOK_TILES, N_EXPERTS).astype(jnp.int32)
    n8 = (n + (ROW_ALIGN - 1)) // ROW_ALIGN * ROW_ALIGN
    local = jnp.cumsum(n8, axis=1) - n8
    seg = (jnp.sum(n8, axis=0) + (FFN_TM - 1)) // FFN_TM * FFN_TM
    seg_end = jnp.cumsum(seg)
    glob = (seg_end - seg)[None, :] + jnp.cumsum(n8, axis=0) - n8
    n_ffn = ROWS_MAX // FFN_TM
    n_active = seg_end[-1] // FFN_TM
    tile_row = jnp.arange(n_ffn, dtype=jnp.int32) * FFN_TM
    texp = jnp.sum((tile_row[:, None] >= seg_end[None, :]).astype(jnp.int32), axis=1)
    texp = jnp.minimum(texp, N_EXPERTS - 1)
    texp = jnp.where(jnp.arange(n_ffn) < n_active, texp, texp[n_active - 1])
    return dict(n8=n8.reshape(-1), local=local.reshape(-1), glob=glob.reshape(-1),
                local_col=local.astype(F32).reshape(N_TOK_TILES, N_EXPERTS, 1),
                texp=texp.astype(jnp.int32), n_active=n_active.reshape(1).astype(jnp.int32))


def _run_copies(tile, n8_ref, local_ref, glob_ref, make_copy, start):
    for e in range(N_EXPERTS):
        idx = tile * N_EXPERTS + e
        n = n8_ref[idx]

        @pl.when(n > 0)
        def _(idx=idx, n=n):
            cp = make_copy(pl.multiple_of(local_ref[idx], ROW_ALIGN),
                           pl.multiple_of(glob_ref[idx], ROW_ALIGN),
                           pl.multiple_of(n, ROW_ALIGN))
            if start:
                cp.start()
            else:
                cp.wait()


def _fill_before(before_sc):
    @pl.when(pl.program_id(0) == 0)
    def _():
        src = lax.broadcasted_iota(jnp.int32, before_sc.shape, 0)
        dst = lax.broadcasted_iota(jnp.int32, before_sc.shape, 1)
        before_sc[...] = jnp.where(src < dst, 1.0, 0.0).astype(BF16)


def _sorted_rows(onehot, local_col, before):
    rank = jnp.dot(onehot.astype(BF16), before, preferred_element_type=F32)
    pos = local_col + rank
    chosen = onehot > 0.5
    lo = jnp.min(jnp.where(chosen, pos, float(MOE_LB)), axis=0, keepdims=True)
    hi = jnp.max(jnp.where(chosen, pos, -1.0), axis=0, keepdims=True)
    return pos, chosen, lo, hi


def _dispatch_kernel(n8_ref, local_ref, glob_ref, h_ref, oh_ref, lcol_ref, xs_ref,
                     lbuf, before_sc, sem):
    j = pl.program_id(0)
    last = pl.num_programs(0) - 1
    par = j % 2
    _fill_before(before_sc)
    row = lax.broadcasted_iota(jnp.int32, (MOE_LB, MOE_TOK), 0).astype(F32)
    for u in range(MOE_SUB):
        tok = slice(u * MOE_TOK, (u + 1) * MOE_TOK)
        _, _, lo, hi = _sorted_rows(oh_ref[:, tok], lcol_ref[u], before_sc[...])
        perm = (jnp.where(row == lo, 1.0, 0.0) + jnp.where(row == hi, 1.0, 0.0)).astype(BF16)
        lbuf[par * MOE_SUB + u] = jnp.dot(perm, h_ref[tok, :],
                                          preferred_element_type=F32).astype(BF16)

    def copy_from(slot):
        def make(lo_, go_, size):
            return pltpu.make_async_copy(lbuf.at[slot, pl.ds(lo_, size), :],
                                         xs_ref.at[pl.ds(go_, size), :], sem.at[slot])
        return make

    refs = (n8_ref, local_ref, glob_ref)
    for u in range(MOE_SUB):
        _run_copies(j * MOE_SUB + u, *refs, copy_from(par * MOE_SUB + u), True)

    @pl.when(j > 0)
    def _():
        for u in range(MOE_SUB):
            _run_copies((j - 1) * MOE_SUB + u, *refs, copy_from((1 - par) * MOE_SUB + u), False)

    @pl.when(j == last)
    def _():
        for u in range(MOE_SUB):
            _run_copies(j * MOE_SUB + u, *refs, copy_from(par * MOE_SUB + u), False)


def _dispatch(plan, h2, onehot):
    step_tok = MOE_SUB * MOE_TOK
    grid_spec = pltpu.PrefetchScalarGridSpec(
        num_scalar_prefetch=3,
        grid=(N_TOK_TILES // MOE_SUB,),
        in_specs=[
            pl.BlockSpec((step_tok, D_MODEL), lambda j, *_: (j, 0)),
            pl.BlockSpec((N_EXPERTS, step_tok), lambda j, *_: (0, j)),
            pl.BlockSpec((MOE_SUB, N_EXPERTS, 1), lambda j, *_: (j, 0, 0)),
        ],
        out_specs=pl.BlockSpec(memory_space=pl.ANY),
        scratch_shapes=[pltpu.VMEM((2 * MOE_SUB, MOE_LB, D_MODEL), BF16),
                        pltpu.VMEM((MOE_TOK, MOE_TOK), BF16),
                        pltpu.SemaphoreType.DMA((2 * MOE_SUB,))],
    )
    return pl.pallas_call(
        _dispatch_kernel,
        grid_spec=grid_spec,
        out_shape=jax.ShapeDtypeStruct((ROWS_MAX, D_MODEL), BF16),
        compiler_params=_cparams(("arbitrary",)),
        name="dispatch",
    )(plan["n8"], plan["local"], plan["glob"], h2, onehot, plan["local_col"])


def _experts_kernel(texp_ref, nact_ref, xs_ref, wg_ref, wu_ref, wd_ref, ys_ref, wg_b, wu_b, wd_b):
    i = pl.program_id(0)
    active = i < nact_ref[0]
    new_expert = jnp.logical_or(i == 0, texp_ref[i] != texp_ref[jnp.maximum(i - 1, 0)])

    @pl.when(jnp.logical_and(active, new_expert))
    def _():
        wg_b[...] = wg_ref[...].astype(BF16)
        wu_b[...] = wu_ref[...].astype(BF16)
        wd_b[...] = wd_ref[...].astype(BF16)

    @pl.when(active)
    def _():
        x = xs_ref[...]
        hg = jnp.dot(x, wg_b[...], preferred_element_type=F32)
        hu = jnp.dot(x, wu_b[...], preferred_element_type=F32)
        act = (jax.nn.silu(hg) * hu).astype(BF16)
        ys_ref[...] = jnp.dot(act, wd_b[...], preferred_element_type=F32).astype(BF16)


def _experts(plan, xs, layer, wg, wu, wd):
    def rows(i, texp, nact):
        return (jnp.minimum(i, nact[0] - 1), 0)

    def expert(i, texp, nact):
        return (layer, texp[i], 0, 0)

    grid_spec = pltpu.PrefetchScalarGridSpec(
        num_scalar_prefetch=2,
        grid=(ROWS_MAX // FFN_TM,),
        in_specs=[
            pl.BlockSpec((FFN_TM, D_MODEL), rows),
            pl.BlockSpec((None, None, D_MODEL, D_EXPERT), expert),
            pl.BlockSpec((None, None, D_MODEL, D_EXPERT), expert),
            pl.BlockSpec((None, None, D_EXPERT, D_MODEL), expert),
        ],
        out_specs=pl.BlockSpec((FFN_TM, D_MODEL), rows),
        scratch_shapes=[pltpu.VMEM((D_MODEL, D_EXPERT), BF16), pltpu.VMEM((D_MODEL, D_EXPERT), BF16),
                        pltpu.VMEM((D_EXPERT, D_MODEL), BF16)],
    )
    return pl.pallas_call(
        _experts_kernel,
        grid_spec=grid_spec,
        out_shape=jax.ShapeDtypeStruct((ROWS_MAX, D_MODEL), BF16),
        compiler_params=_cparams(("arbitrary",)),
        name="experts",
    )(plan["texp"], plan["n_active"], xs, wg, wu, wd)


def _combine_kernel(n8_ref, local_ref, glob_ref, oh_ref, g_ref, lcol_ref, x_ref, mod_ref, fg_ref,
                    ys_ref, o_ref, ybuf, before_sc, sem, *, final_norm):
    j = pl.program_id(0)
    n_steps = pl.num_programs(0)
    par = j % 2
    refs = (n8_ref, local_ref, glob_ref)
    _fill_before(before_sc)

    def copy_to(slot):
        def make(lo_, go_, size):
            return pltpu.make_async_copy(ys_ref.at[pl.ds(go_, size), :],
                                         ybuf.at[slot, pl.ds(lo_, size), :], sem.at[slot])
        return make

    @pl.when(j == 0)
    def _():
        ybuf[...] = jnp.zeros(ybuf.shape, BF16)
        for u in range(MOE_SUB):
            _run_copies(u, *refs, copy_to(u), True)

    @pl.when(j + 1 < n_steps)
    def _():
        for u in range(MOE_SUB):
            _run_copies((j + 1) * MOE_SUB + u, *refs, copy_to((1 - par) * MOE_SUB + u), True)

    row = lax.broadcasted_iota(jnp.int32, (MOE_LB, MOE_TOK), 0).astype(F32)
    for u in range(MOE_SUB):
        tok = slice(u * MOE_TOK, (u + 1) * MOE_TOK)
        pos, chosen, lo, hi = _sorted_rows(oh_ref[:, tok], lcol_ref[u], before_sc[...])
        gate = jnp.where(chosen, g_ref[:, tok], 0.0)
        g_lo = jnp.sum(jnp.where(pos == lo, gate, 0.0), axis=0, keepdims=True)
        g_hi = jnp.sum(jnp.where(pos == hi, gate, 0.0), axis=0, keepdims=True)
        perm = (jnp.where(row == lo, g_lo, 0.0) + jnp.where(row == hi, g_hi, 0.0)).astype(BF16)
        slot = par * MOE_SUB + u
        _run_copies(j * MOE_SUB + u, *refs, copy_to(slot), False)
        y = lax.dot_general(perm, ybuf[slot], (((0,), (0,)), ((), ())),
                            preferred_element_type=F32)
        x = x_ref[tok, :] + mod_ref[5:6, :] * y
        if final_norm:
            x = x * lax.rsqrt(jnp.mean(x * x, axis=-1, keepdims=True) + EPS) * fg_ref[...]
        o_ref[tok, :] = x


def _combine(plan, onehot, gates, ys, x, mod_l, final_g, final_norm):
    step_tok = MOE_SUB * MOE_TOK
    steps_per_seq = SEQ // step_tok
    route = pl.BlockSpec((N_EXPERTS, step_tok), lambda j, *_: (0, j))
    tok = pl.BlockSpec((step_tok, D_MODEL), lambda j, *_: (j, 0))
    grid_spec = pltpu.PrefetchScalarGridSpec(
        num_scalar_prefetch=3,
        grid=(N_TOK_TILES // MOE_SUB,),
        in_specs=[
            route, route,
            pl.BlockSpec((MOE_SUB, N_EXPERTS, 1), lambda j, *_: (j, 0, 0)),
            tok,
            pl.BlockSpec((None, N_MOD, D_MODEL), lambda j, *_: (j // steps_per_seq, 0, 0)),
            pl.BlockSpec((1, D_MODEL), lambda j, *_: (0, 0)),
            pl.BlockSpec(memory_space=pl.ANY),
        ],
        out_specs=tok,
        scratch_shapes=[pltpu.VMEM((2 * MOE_SUB, MOE_LB, D_MODEL), BF16),
                        pltpu.VMEM((MOE_TOK, MOE_TOK), BF16),
                        pltpu.SemaphoreType.DMA((2 * MOE_SUB,))],
    )
    return pl.pallas_call(
        functools.partial(_combine_kernel, final_norm=final_norm),
        grid_spec=grid_spec,
        out_shape=jax.ShapeDtypeStruct((TOKENS, D_MODEL), F32),
        compiler_params=_cparams(("arbitrary",)),
        name="combine",
    )(plan["n8"], plan["local"], plan["glob"], onehot, gates, plan["local_col"], x, mod_l,
      final_g, ys)


def kernel(x, c, rel_bias, router_w, router_b, mod_w, mod_b, norm1_g, w_in, gmlp_ln_g, gmlp_ln_b,
           gmlp_ws, gmlp_bs, out_norm_a_g, out_norm_b_g, w_out, norm2_g, moe_w_gate, moe_w_up,
           moe_w_down, final_g):
    mod = _modulation(c, mod_w, mod_b).reshape(DEPTH, BATCH, N_MOD, D_MODEL)
    bias_tab = _bias_tables(rel_bias)
    rwt = router_w.T.astype(BF16)
    rb_col = router_b.reshape(N_EXPERTS, 1)
    xt = x.reshape(TOKENS, D_MODEL)
    win_b = w_in.astype(BF16)
    wout_b = w_out.astype(BF16)
    for l in range(DEPTH):
        ws2 = gmlp_ws[l].astype(BF16).reshape(A_GROUPS // 2, 2 * CHUNK, CHUNK)
        bs_exp = jnp.repeat(gmlp_bs[l].T, HEAD_DIM, axis=1)
        out_a, q, k, v = _front(
            xt, mod[l], norm1_g[l].reshape(1, -1), win_b[l], gmlp_ln_g[l].reshape(1, -1),
            gmlp_ln_b[l].reshape(1, -1), ws2, bs_exp, out_norm_a_g[l].reshape(1, -1))
        out_b = _attention(q, k, v, bias_tab)
        xt, h2, gates, onehot, count = _mid(
            out_a, out_b, xt, mod[l], out_norm_b_g[l].reshape(1, -1),
            wout_b[l], norm2_g[l].reshape(1, -1), rwt, rb_col)
        plan = _moe_plan(count)
        xs = _dispatch(plan, h2, onehot)
        ys = _experts(plan, xs, l, moe_w_gate, moe_w_up, moe_w_down)
        xt = _combine(plan, onehot, gates, ys, xt, mod[l], final_g.reshape(1, -1),
                      final_norm=(l == DEPTH - 1))
    return xt.reshape(BATCH, SEQ, D_MODEL)
```

```python
import functools
import math

import numpy as np
import jax
import jax.numpy as jnp
from jax import lax
from jax.experimental import pallas as pl
from jax.experimental.pallas import tpu as pltpu

D_MODEL = 1024
BATCH = 8
SEQ = 2048
DEPTH = 2
TOKENS = BATCH * SEQ
HEAD_DIM = 64
A_WIDTH = 512
B_WIDTH = 512
A_GROUPS = 8
IN_COLS = 2 * A_WIDTH + 3 * B_WIDTH
CHUNK = 128
DILATED_CONFIGS = ((128, 1), (512, 4), (2048, 16))
ATTN_BLOCK = 128
REL_BUCKETS = 32
REL_MAX_EXACT = REL_BUCKETS // 2
REL_MAX_DISTANCE = 2048
N_EXPERTS = 16
N_EXPERT_GROUPS = 4
EXPERTS_PER_GROUP = 4
D_EXPERT = 512
N_MOD = 6
EPS = 1e-6
NEG_INF = -1e30

LANES = 128
HEAD_PAIR = 2 * HEAD_DIM
N_PAIRS = B_WIDTH // HEAD_PAIR
RESIDUES = max(d for _, d in DILATED_CONFIGS)
LOG2E = math.log2(math.e)

F32 = jnp.float32
BF16 = jnp.bfloat16

VMEM_LIMIT = 56 * 1024 * 1024


def _cparams(sem):
    return pltpu.CompilerParams(dimension_semantics=sem, vmem_limit_bytes=VMEM_LIMIT)


def _gelu(x):
    return 0.5 * x * (1.0 + lax.erf(x * math.sqrt(0.5)))


MOD_TN = 1024


def _mod_kernel(c_ref, w_ref, b_ref, o_ref):
    ca = jax.nn.silu(c_ref[...])
    o_ref[...] = jnp.dot(ca.astype(BF16), w_ref[...].astype(BF16),
                         preferred_element_type=F32) + b_ref[...]


def _modulation(c, mod_w, mod_b):
    n_cols = N_MOD * D_MODEL
    return pl.pallas_call(
        _mod_kernel,
        grid=(DEPTH, n_cols // MOD_TN),
        in_specs=[
            pl.BlockSpec((BATCH, D_MODEL), lambda l, j: (0, 0)),
            pl.BlockSpec((None, D_MODEL, MOD_TN), lambda l, j: (l, 0, j)),
            pl.BlockSpec((None, 1, MOD_TN), lambda l, j: (l, 0, j)),
        ],
        out_specs=pl.BlockSpec((None, BATCH, MOD_TN), lambda l, j: (l, 0, j)),
        out_shape=jax.ShapeDtypeStruct((DEPTH, BATCH, n_cols), F32),
        compiler_params=_cparams(("arbitrary", "arbitrary")),
        name="modulation",
    )(c, mod_w, mod_b.reshape(DEPTH, 1, n_cols))


K1_TM = 512


def _front_kernel(x_ref, mod_ref, n1g_ref, win_ref, lng_ref, lnb_ref, ws_ref, bs_ref, ga_ref,
                  a_ref, q_ref, k_ref, v_ref):
    x = x_ref[...]
    tm = x.shape[0]
    h = x * lax.rsqrt(jnp.mean(x * x, axis=-1, keepdims=True) + EPS) * n1g_ref[...]
    h = h * (1.0 + mod_ref[1:2, :]) + mod_ref[0:1, :]
    proj = jnp.dot(h.astype(BF16), win_ref[...], preferred_element_type=F32)

    q_ref[...] = proj[:, 2 * A_WIDTH:2 * A_WIDTH + B_WIDTH] * (HEAD_DIM ** -0.5 * LOG2E)
    k_ref[...] = proj[:, 2 * A_WIDTH + B_WIDTH:2 * A_WIDTH + 2 * B_WIDTH]
    v_ref[...] = proj[:, 2 * A_WIDTH + 2 * B_WIDTH:]

    u = _gelu(proj[:, :A_WIDTH])
    va = _gelu(proj[:, A_WIDTH:2 * A_WIDTH])
    mu = jnp.mean(va, axis=-1, keepdims=True)
    vc = va - mu
    vln = vc * lax.rsqrt(jnp.mean(vc * vc, axis=-1, keepdims=True) + EPS)
    vln = (vln * lng_ref[...] + lnb_ref[...]).astype(BF16)

    row = lax.broadcasted_iota(jnp.int32, (2 * CHUNK, CHUNK), 0)
    col = lax.broadcasted_iota(jnp.int32, (2 * CHUNK, CHUNK), 1)
    causal = (row % CHUNK) >= col
    first_group = lax.broadcasted_iota(jnp.int32, (CHUNK, LANES), 1) < HEAD_DIM
    wmix = [jnp.where(causal, ws_ref[p], jnp.zeros((), BF16)) for p in range(A_GROUPS // 2)]
    for c in range(tm // CHUNK):
        rows = slice(c * CHUNK, (c + 1) * CHUNK)
        parts = []
        for p in range(A_GROUPS // 2):
            vp = vln[rows, p * LANES:(p + 1) * LANES]
            r = jnp.dot(wmix[p], vp, preferred_element_type=F32)
            parts.append(jnp.where(first_group, r[:CHUNK], r[CHUNK:]))
        s = jnp.concatenate(parts, axis=-1) + bs_ref[...]
        oa = u[rows] * s
        oa = oa * lax.rsqrt(jnp.mean(oa * oa, axis=-1, keepdims=True) + EPS) * ga_ref[...]
        a_ref[rows, :] = oa.astype(BF16)


def _front(x, mod_l, n1g, win, lng, lnb, ws2, bs_exp, ga):
    tm = K1_TM
    tiles_per_seq = SEQ // tm
    row1024 = pl.BlockSpec((1, D_MODEL), lambda i: (0, 0))
    row512 = pl.BlockSpec((1, A_WIDTH), lambda i: (0, 0))
    half = pl.BlockSpec((tm, A_WIDTH), lambda i: (i, 0))
    return pl.pallas_call(
        _front_kernel,
        grid=(TOKENS // tm,),
        in_specs=[
            pl.BlockSpec((tm, D_MODEL), lambda i: (i, 0)),
            pl.BlockSpec((None, N_MOD, D_MODEL), lambda i: (i // tiles_per_seq, 0, 0)),
            row1024,
            pl.BlockSpec((D_MODEL, IN_COLS), lambda i: (0, 0)),
            row512, row512,
            pl.BlockSpec((A_GROUPS // 2, 2 * CHUNK, CHUNK), lambda i: (0, 0, 0)),
            pl.BlockSpec((CHUNK, A_WIDTH), lambda i: (0, 0)),
            row512,
        ],
        out_specs=[half, half, half, half],
        out_shape=[jax.ShapeDtypeStruct((TOKENS, A_WIDTH), BF16)] +
                  [jax.ShapeDtypeStruct((TOKENS, B_WIDTH), F32)] * 3,
        compiler_params=_cparams(("arbitrary",)),
        name="front",
    )(x, mod_l, n1g, win, lng, lnb, ws2, bs_exp, ga)


def _t5_bucket_np(dist):
    dist = np.maximum(dist, 0)
    ratio = np.log(np.maximum(dist, 1) / REL_MAX_EXACT) / np.log(REL_MAX_DISTANCE / REL_MAX_EXACT)
    large = REL_MAX_EXACT + np.floor(ratio * (REL_BUCKETS - REL_MAX_EXACT)).astype(np.int64)
    large = np.minimum(large, REL_BUCKETS - 1)
    return np.where(dist < REL_MAX_EXACT, dist, large).astype(np.int32)


def _bias_tables(rel_bias):
    blk = ATTN_BLOCK
    n_rel = 3 * blk
    rel = 2 * blk - 1 - np.arange(n_rel)
    tables = []
    for window, d in DILATED_CONFIGS:
        span = window // d
        bucket = _t5_bucket_np(np.clip(rel, 0, span) * d)
        valid = jnp.asarray((rel >= 0) & (rel <= span))
        w = jnp.where(valid[None, :], rel_bias.astype(F32)[bucket].T, NEG_INF)
        flat = jnp.tile(w, (1, blk))
        skew = flat[:, blk - 1:blk - 1 + blk * (n_rel - 1)].reshape(-1, blk, n_rel - 1)
        tab = skew[:, :, :2 * blk]
        sub = RESIDUES // d
        ln = blk // sub
        pos = np.arange(blk)
        rows = np.eye(blk, dtype=np.float32)[(pos % ln) * sub + pos // ln]
        cols = np.kron(np.eye(2, dtype=np.float32), rows)
        tab = jnp.einsum('pi,hik,qk->hpq', rows, tab * LOG2E, cols,
                         precision=lax.Precision.HIGHEST)
        tables.append(tab)
    return jnp.stack(tables, axis=0)


def _attn_kernel(q_ref, k_ref, v_ref, bias_ref, o_ref, qp, kp, vp, m_sc, l_sc, acc_sc):
    blk = ATTN_BLOCK
    res = RESIDUES
    per = SEQ // res
    sq = res * res
    seg = sq // res
    lane = lax.broadcasted_iota(jnp.int32, (blk, LANES), 1)
    head0 = lane < HEAD_DIM
    ones = jnp.ones((2 * blk, LANES), BF16)

    pa = lax.broadcasted_iota(jnp.int32, (sq, sq), 0)
    pb = lax.broadcasted_iota(jnp.int32, (sq, sq), 1)
    regroup = jnp.where(pb == res * (pa % res) + pa // res, 1.0, 0.0).astype(BF16)

    def residue_rows(g):
        return [slice(per * r + seg * g, per * r + seg * (g + 1)) for r in range(res)]

    def load(ref, slices):
        return jnp.concatenate([ref[s, :] for s in slices], axis=0) if len(slices) > 1 \
            else ref[slices[0], :]

    def store(ref, slices, val):
        ln = val.shape[0] // len(slices)
        for i, s in enumerate(slices):
            ref[s, :] = val[i * ln:(i + 1) * ln]

    for g in range(SEQ // sq):
        rows = slice(sq * g, sq * (g + 1))
        qkv = jnp.concatenate([q_ref[rows, :], k_ref[rows, :], v_ref[rows, :]], axis=1)
        qkv = jnp.dot(regroup, qkv.astype(BF16), preferred_element_type=F32)
        store(qp, residue_rows(g), qkv[:, :LANES])
        store(kp, residue_rows(g), qkv[:, LANES:2 * LANES])
        store(vp, residue_rows(g), qkv[:, 2 * LANES:])

    def merge_heads(t):
        return jnp.where(head0, t[:blk], t[blk:])

    def scores(q, kcat, bias):
        zero = jnp.zeros_like(q)
        qs = jnp.concatenate([jnp.where(head0, q, zero), jnp.where(head0, zero, q)],
                             axis=0).astype(BF16)
        return lax.dot_general(qs, kcat, (((1,), (1,)), ((), ())),
                               preferred_element_type=F32) + bias

    def weighted_values(s, vcat):
        nk = vcat.shape[0]
        m = jnp.max(s, axis=-1, keepdims=True)
        p = jnp.exp2(s - m).astype(BF16)
        pv = jnp.dot(p, jnp.concatenate([vcat, ones[:nk]], axis=1), preferred_element_type=F32)
        return (merge_heads(jnp.broadcast_to(m, (2 * blk, LANES))),
                merge_heads(pv[:, LANES:]), merge_heads(pv[:, :LANES]))

    def update(ci, slices, m_c, l_c, o_c):
        if ci > 0:
            m_r = load(m_sc, slices)
            m_n = jnp.maximum(m_r, m_c)
            a = jnp.exp2(m_r - m_n)
            b = jnp.exp2(m_c - m_n)
            l_c = a * load(l_sc, slices) + b * l_c
            o_c = a * load(acc_sc, slices) + b * o_c
            m_c = m_n
        store(m_sc, slices, m_c)
        store(l_sc, slices, l_c)
        store(acc_sc, slices, o_c)

    blocks = []
    for ci, (window, d) in enumerate(DILATED_CONFIGS):
        sub = res // d
        ln = blk // sub
        for r in range(d):
            for n in range(SEQ // d // blk):
                slices = [slice(per * (r + d * c) + ln * n, per * (r + d * c) + ln * (n + 1))
                          for c in range(sub)]
                blocks.append((ci, slices, n == 0))

    kv_prev = [None, None]

    def score_stage(ci, slices, first):
        k_cur = load(kp, slices).astype(BF16)
        v_cur = load(vp, slices).astype(BF16)
        if first:
            bias = jnp.concatenate([bias_ref[ci, 0, :, blk:], bias_ref[ci, 1, :, blk:]], axis=0)
            out = scores(load(qp, slices), k_cur, bias), v_cur
        else:
            bias = jnp.concatenate([bias_ref[ci, 0], bias_ref[ci, 1]], axis=0)
            out = (scores(load(qp, slices), jnp.concatenate([kv_prev[0], k_cur], axis=0), bias),
                   jnp.concatenate([kv_prev[1], v_cur], axis=0))
        kv_prev[0], kv_prev[1] = k_cur, v_cur
        return out

    ahead = score_stage(*blocks[0])
    for i, (ci, slices, _) in enumerate(blocks):
        s, vcat = ahead
        if i + 1 < len(blocks):
            ahead = score_stage(*blocks[i + 1])
        update(ci, slices, *weighted_values(s, vcat))

    for g in range(SEQ // sq):
        o = load(acc_sc, residue_rows(g)) / load(l_sc, residue_rows(g))
        hi = o.astype(BF16)
        lo = (o - hi.astype(F32)).astype(BF16)
        back = jnp.dot(regroup, jnp.concatenate([hi, lo], axis=1), preferred_element_type=F32)
        o_ref[sq * g:sq * (g + 1), :] = back[:, :LANES] + back[:, LANES:]


def _attention(q, k, v, bias_tab):
    n_cfg = len(DILATED_CONFIGS)
    blk = ATTN_BLOCK
    seq_spec = pl.BlockSpec((SEQ, HEAD_PAIR), lambda b, p: (b, p))
    return pl.pallas_call(
        _attn_kernel,
        grid=(BATCH, N_PAIRS),
        in_specs=[seq_spec, seq_spec, seq_spec,
                  pl.BlockSpec((n_cfg, 2, blk, 2 * blk), lambda b, p: (0, p, 0, 0))],
        out_specs=seq_spec,
        out_shape=jax.ShapeDtypeStruct((TOKENS, B_WIDTH), F32),
        scratch_shapes=[pltpu.VMEM((SEQ, HEAD_PAIR), F32)] * 6,
        compiler_params=_cparams(("arbitrary", "arbitrary")),
        name="attention",
    )(q, k, v, bias_tab)


K4_TM = 512


def _top2_sum(a, b, c, d):
    hi1, lo1 = jnp.maximum(a, b), jnp.minimum(a, b)
    hi2, lo2 = jnp.maximum(c, d), jnp.minimum(c, d)
    return jnp.maximum(hi1, hi2) + jnp.maximum(jnp.minimum(hi1, hi2), jnp.maximum(lo1, lo2))


def _route(logits_t, rb_col):
    m = jnp.max(logits_t, axis=0, keepdims=True)
    e = jnp.exp(logits_t - m)
    probs = e / jnp.sum(e, axis=0, keepdims=True)
    sel = probs + rb_col
    sel_rows = [sel[i:i + 1, :] for i in range(N_EXPERTS)]
    prob_rows = [probs[i:i + 1, :] for i in range(N_EXPERTS)]
    gsz = EXPERTS_PER_GROUP
    score = [_top2_sum(*sel_rows[g * gsz:(g + 1) * gsz]) for g in range(N_EXPERT_GROUPS)]
    chosen = []
    for g in range(N_EXPERT_GROUPS):
        best = None
        for g2 in range(N_EXPERT_GROUPS):
            if g2 == g:
                continue
            c = (score[g] > score[g2]) if g2 < g else (score[g] >= score[g2])
            best = c if best is None else jnp.logical_and(best, c)
        for i in range(gsz):
            ei = g * gsz + i
            rank = jnp.zeros_like(sel_rows[ei])
            for j in range(gsz):
                if j == i:
                    continue
                ej = g * gsz + j
                ahead = (sel_rows[ej] >= sel_rows[ei]) if j < i else (sel_rows[ej] > sel_rows[ei])
                rank = rank + jnp.where(ahead, 1.0, 0.0)
            chosen.append(jnp.logical_and(best, rank < float(2)))
    picked = [jnp.where(chosen[i], prob_rows[i], 0.0) for i in range(N_EXPERTS)]
    denom = picked[0]
    for i in range(1, N_EXPERTS):
        denom = denom + picked[i]
    gates = jnp.concatenate([pk / denom for pk in picked], axis=0)
    onehot = jnp.concatenate([jnp.where(ch, 1.0, 0.0) for ch in chosen], axis=0)
    return gates, onehot


def _mid_kernel(a_ref, ob_ref, x_ref, mod_ref, gb_ref, wout_ref, n2g_ref, rwt_ref, rb_ref,
                x1_ref, h2_ref, gates_ref, onehot_ref, count_ref):
    ob = ob_ref[...]
    bn = ob * lax.rsqrt(jnp.mean(ob * ob, axis=-1, keepdims=True) + EPS) * gb_ref[...]
    mixed = jnp.dot(a_ref[...], wout_ref[:A_WIDTH, :], preferred_element_type=F32)
    mixed = mixed + jnp.dot(bn.astype(BF16), wout_ref[A_WIDTH:, :], preferred_element_type=F32)
    x1 = x_ref[...] + mod_ref[2:3, :] * mixed
    x1_ref[...] = x1
    h2 = x1 * lax.rsqrt(jnp.mean(x1 * x1, axis=-1, keepdims=True) + EPS) * n2g_ref[...]
    h2 = (h2 * (1.0 + mod_ref[4:5, :]) + mod_ref[3:4, :]).astype(BF16)
    h2_ref[...] = h2
    logits_t = lax.dot_general(rwt_ref[...], h2, (((1,), (1,)), ((), ())),
                               preferred_element_type=F32)
    gates_t, onehot_t = _route(logits_t, rb_ref[...])
    gates_ref[...] = gates_t
    onehot_ref[...] = onehot_t
    for s in range(count_ref.shape[0]):
        count_ref[s] = jnp.sum(onehot_t[:, s * MOE_TOK:(s + 1) * MOE_TOK], axis=1, keepdims=True)


def _mid(out_a, out_b, x, mod_l, gb, wout, n2g, rwt, rb_col):
    tm = K4_TM
    tiles_per_seq = SEQ // tm
    tok = pl.BlockSpec((tm, D_MODEL), lambda i: (i, 0))
    half = pl.BlockSpec((tm, A_WIDTH), lambda i: (i, 0))
    route = pl.BlockSpec((N_EXPERTS, tm), lambda i: (0, i))
    return pl.pallas_call(
        _mid_kernel,
        grid=(TOKENS // tm,),
        in_specs=[
            half, half, tok,
            pl.BlockSpec((None, N_MOD, D_MODEL), lambda i: (i // tiles_per_seq, 0, 0)),
            pl.BlockSpec((1, B_WIDTH), lambda i: (0, 0)),
            pl.BlockSpec((D_MODEL, D_MODEL), lambda i: (0, 0)),
            pl.BlockSpec((1, D_MODEL), lambda i: (0, 0)),
            pl.BlockSpec((N_EXPERTS, D_MODEL), lambda i: (0, 0)),
            pl.BlockSpec((N_EXPERTS, 1), lambda i: (0, 0)),
        ],
        out_specs=[tok, tok, route, route,
                   pl.BlockSpec((tm // MOE_TOK, N_EXPERTS, 1), lambda i: (i, 0, 0))],
        out_shape=[jax.ShapeDtypeStruct((TOKENS, D_MODEL), F32),
                   jax.ShapeDtypeStruct((TOKENS, D_MODEL), BF16),
                   jax.ShapeDtypeStruct((N_EXPERTS, TOKENS), F32),
                   jax.ShapeDtypeStruct((N_EXPERTS, TOKENS), F32),
                   jax.ShapeDtypeStruct((TOKENS // MOE_TOK, N_EXPERTS, 1), F32)],
        compiler_params=_cparams(("arbitrary",)),
        name="mid",
    )(out_a, out_b, x, mod_l, gb, wout, n2g, rwt, rb_col)


MOE_TOK = 256
ROW_ALIGN = 16
MOE_SUB = 2
MOE_LB = 2 * MOE_TOK + N_EXPERTS * ROW_ALIGN
FFN_TM = 1024
N_TOK_TILES = TOKENS // MOE_TOK
ROWS_MAX = -(-(2 * TOKENS + N_TOK_TILES * N_EXPERTS * (ROW_ALIGN - 1)
               + N_EXPERTS * (FFN_TM - ROW_ALIGN)) // FFN_TM) * FFN_TM


def _moe_plan(count):
    n = count.reshape(N_TOK_TILES, N_EXPERTS).astype(jnp.int32)
    n8 = (n + (ROW_ALIGN - 1)) // ROW_ALIGN * ROW_ALIGN
    local = jnp.cumsum(n8, axis=1) - n8
    seg = (jnp.sum(n8, axis=0) + (FFN_TM - 1)) // FFN_TM * FFN_TM
    seg_end = jnp.cumsum(seg)
    glob = (seg_end - seg)[None, :] + jnp.cumsum(n8, axis=0) - n8
    n_ffn = ROWS_MAX // FFN_TM
    n_active = seg_end[-1] // FFN_TM
    tile_row = jnp.arange(n_ffn, dtype=jnp.int32) * FFN_TM
    texp = jnp.sum((tile_row[:, None] >= seg_end[None, :]).astype(jnp.int32), axis=1)
    texp = jnp.minimum(texp, N_EXPERTS - 1)
    texp = jnp.where(jnp.arange(n_ffn) < n_active, texp, texp[n_active - 1])
    return dict(n8=n8.reshape(-1), local=local.reshape(-1), glob=glob.reshape(-1),
                local_col=local.astype(F32).reshape(N_TOK_TILES, N_EXPERTS, 1),
                texp=texp.astype(jnp.int32), n_active=n_active.reshape(1).astype(jnp.int32))


def _run_copies(tile, n8_ref, local_ref, glob_ref, make_copy, start):
    for e in range(N_EXPERTS):
        idx = tile * N_EXPERTS + e
        n = n8_ref[idx]

        @pl.when(n > 0)
        def _(idx=idx, n=n):
            cp = make_copy(pl.multiple_of(local_ref[idx], ROW_ALIGN),
                           pl.multiple_of(glob_ref[idx], ROW_ALIGN),
                           pl.multiple_of(n, ROW_ALIGN))
            if start:
                cp.start()
            else:
                cp.wait()


def _fill_before(before_sc):
    @pl.when(pl.program_id(0) == 0)
    def _():
        src = lax.broadcasted_iota(jnp.int32, before_sc.shape, 0)
        dst = lax.broadcasted_iota(jnp.int32, before_sc.shape, 1)
        before_sc[...] = jnp.where(src < dst, 1.0, 0.0).astype(BF16)


def _sorted_rows(onehot, local_col, before):
    rank = jnp.dot(onehot.astype(BF16), before, preferred_element_type=F32)
    pos = local_col + rank
    chosen = onehot > 0.5
    lo = jnp.min(jnp.where(chosen, pos, float(MOE_LB)), axis=0, keepdims=True)
    hi = jnp.max(jnp.where(chosen, pos, -1.0), axis=0, keepdims=True)
    return pos, chosen, lo, hi


def _dispatch_kernel(n8_ref, local_ref, glob_ref, h_ref, oh_ref, lcol_ref, xs_ref,
                     lbuf, before_sc, sem):
    j = pl.program_id(0)
    last = pl.num_programs(0) - 1
    par = j % 2
    _fill_before(before_sc)
    row = lax.broadcasted_iota(jnp.int32, (MOE_LB, MOE_TOK), 0).astype(F32)
    for u in range(MOE_SUB):
        tok = slice(u * MOE_TOK, (u + 1) * MOE_TOK)
        _, _, lo, hi = _sorted_rows(oh_ref[:, tok], lcol_ref[u], before_sc[...])
        perm = (jnp.where(row == lo, 1.0, 0.0) + jnp.where(row == hi, 1.0, 0.0)).astype(BF16)
        lbuf[par * MOE_SUB + u] = jnp.dot(perm, h_ref[tok, :],
                                          preferred_element_type=F32).astype(BF16)

    def copy_from(slot):
        def make(lo_, go_, size):
            return pltpu.make_async_copy(lbuf.at[slot, pl.ds(lo_, size), :],
                                         xs_ref.at[pl.ds(go_, size), :], sem.at[slot])
        return make

    refs = (n8_ref, local_ref, glob_ref)
    for u in range(MOE_SUB):
        _run_copies(j * MOE_SUB + u, *refs, copy_from(par * MOE_SUB + u), True)

    @pl.when(j > 0)
    def _():
        for u in range(MOE_SUB):
            _run_copies((j - 1) * MOE_SUB + u, *refs, copy_from((1 - par) * MOE_SUB + u), False)

    @pl.when(j == last)
    def _():
        for u in range(MOE_SUB):
            _run_copies(j * MOE_SUB + u, *refs, copy_from(par * MOE_SUB + u), False)


def _dispatch(plan, h2, onehot):
    step_tok = MOE_SUB * MOE_TOK
    grid_spec = pltpu.PrefetchScalarGridSpec(
        num_scalar_prefetch=3,
        grid=(N_TOK_TILES // MOE_SUB,),
        in_specs=[
            pl.BlockSpec((step_tok, D_MODEL), lambda j, *_: (j, 0)),
            pl.BlockSpec((N_EXPERTS, step_tok), lambda j, *_: (0, j)),
            pl.BlockSpec((MOE_SUB, N_EXPERTS, 1), lambda j, *_: (j, 0, 0)),
        ],
        out_specs=pl.BlockSpec(memory_space=pl.ANY),
        scratch_shapes=[pltpu.VMEM((2 * MOE_SUB, MOE_LB, D_MODEL), BF16),
                        pltpu.VMEM((MOE_TOK, MOE_TOK), BF16),
                        pltpu.SemaphoreType.DMA((2 * MOE_SUB,))],
    )
    return pl.pallas_call(
        _dispatch_kernel,
        grid_spec=grid_spec,
        out_shape=jax.ShapeDtypeStruct((ROWS_MAX, D_MODEL), BF16),
        compiler_params=_cparams(("arbitrary",)),
        name="dispatch",
    )(plan["n8"], plan["local"], plan["glob"], h2, onehot, plan["local_col"])


def _experts_kernel(texp_ref, nact_ref, xs_ref, wg_ref, wu_ref, wd_ref, ys_ref, wg_b, wu_b, wd_b):
    i = pl.program_id(0)
    active = i < nact_ref[0]
    new_expert = jnp.logical_or(i == 0, texp_ref[i] != texp_ref[jnp.maximum(i - 1, 0)])

    @pl.when(jnp.logical_and(active, new_expert))
    def _():
        wg_b[...] = wg_ref[...].astype(BF16)
        wu_b[...] = wu_ref[...].astype(BF16)
        wd_b[...] = wd_ref[...].astype(BF16)

    @pl.when(active)
    def _():
        x = xs_ref[...]
        hg = jnp.dot(x, wg_b[...], preferred_element_type=F32)
        hu = jnp.dot(x, wu_b[...], preferred_element_type=F32)
        act = (jax.nn.silu(hg) * hu).astype(BF16)
        ys_ref[...] = jnp.dot(act, wd_b[...], preferred_element_type=F32).astype(BF16)


def _experts(plan, xs, layer, wg, wu, wd):
    def rows(i, texp, nact):
        return (jnp.minimum(i, nact[0] - 1), 0)

    def expert(i, texp, nact):
        return (layer, texp[i], 0, 0)

    grid_spec = pltpu.PrefetchScalarGridSpec(
        num_scalar_prefetch=2,
        grid=(ROWS_MAX // FFN_TM,),
        in_specs=[
            pl.BlockSpec((FFN_TM, D_MODEL), rows),
            pl.BlockSpec((None, None, D_MODEL, D_EXPERT), expert),
            pl.BlockSpec((None, None, D_MODEL, D_EXPERT), expert),
            pl.BlockSpec((None, None, D_EXPERT, D_MODEL), expert),
        ],
        out_specs=pl.BlockSpec((FFN_TM, D_MODEL), rows),
        scratch_shapes=[pltpu.VMEM((D_MODEL, D_EXPERT), BF16), pltpu.VMEM((D_MODEL, D_EXPERT), BF16),
                        pltpu.VMEM((D_EXPERT, D_MODEL), BF16)],
    )
    return pl.pallas_call(
        _experts_kernel,
        grid_spec=grid_spec,
        out_shape=jax.ShapeDtypeStruct((ROWS_MAX, D_MODEL), BF16),
        compiler_params=_cparams(("arbitrary",)),
        name="experts",
    )(plan["texp"], plan["n_active"], xs, wg, wu, wd)


def _combine_kernel(n8_ref, local_ref, glob_ref, oh_ref, g_ref, lcol_ref, x_ref, mod_ref, fg_ref,
                    ys_ref, o_ref, ybuf, before_sc, sem, *, final_norm):
    j = pl.program_id(0)
    n_steps = pl.num_programs(0)
    par = j % 2
    refs = (n8_ref, local_ref, glob_ref)
    _fill_before(before_sc)

    def copy_to(slot):
        def make(lo_, go_, size):
            return pltpu.make_async_copy(ys_ref.at[pl.ds(go_, size), :],
                                         ybuf.at[slot, pl.ds(lo_, size), :], sem.at[slot])
        return make

    @pl.when(j == 0)
    def _():
        ybuf[...] = jnp.zeros(ybuf.shape, BF16)
        for u in range(MOE_SUB):
            _run_copies(u, *refs, copy_to(u), True)

    @pl.when(j + 1 < n_steps)
    def _():
        for u in range(MOE_SUB):
            _run_copies((j + 1) * MOE_SUB + u, *refs, copy_to((1 - par) * MOE_SUB + u), True)

    row = lax.broadcasted_iota(jnp.int32, (MOE_LB, MOE_TOK), 0).astype(F32)
    for u in range(MOE_SUB):
        tok = slice(u * MOE_TOK, (u + 1) * MOE_TOK)
        pos, chosen, lo, hi = _sorted_rows(oh_ref[:, tok], lcol_ref[u], before_sc[...])
        gate = jnp.where(chosen, g_ref[:, tok], 0.0)
        g_lo = jnp.sum(jnp.where(pos == lo, gate, 0.0), axis=0, keepdims=True)
        g_hi = jnp.sum(jnp.where(pos == hi, gate, 0.0), axis=0, keepdims=True)
        perm = (jnp.where(row == lo, g_lo, 0.0) + jnp.where(row == hi, g_hi, 0.0)).astype(BF16)
        slot = par * MOE_SUB + u
        _run_copies(j * MOE_SUB + u, *refs, copy_to(slot), False)
        y = lax.dot_general(perm, ybuf[slot], (((0,), (0,)), ((), ())),
                            preferred_element_type=F32)
        x = x_ref[tok, :] + mod_ref[5:6, :] * y
        if final_norm:
            x = x * lax.rsqrt(jnp.mean(x * x, axis=-1, keepdims=True) + EPS) * fg_ref[...]
        o_ref[tok, :] = x


def _combine(plan, onehot, gates, ys, x, mod_l, final_g, final_norm):
    step_tok = MOE_SUB * MOE_TOK
    steps_per_seq = SEQ // step_tok
    route = pl.BlockSpec((N_EXPERTS, step_tok), lambda j, *_: (0, j))
    tok = pl.BlockSpec((step_tok, D_MODEL), lambda j, *_: (j, 0))
    grid_spec = pltpu.PrefetchScalarGridSpec(
        num_scalar_prefetch=3,
        grid=(N_TOK_TILES // MOE_SUB,),
        in_specs=[
            route, route,
            pl.BlockSpec((MOE_SUB, N_EXPERTS, 1), lambda j, *_: (j, 0, 0)),
            tok,
            pl.BlockSpec((None, N_MOD, D_MODEL), lambda j, *_: (j // steps_per_seq, 0, 0)),
            pl.BlockSpec((1, D_MODEL), lambda j, *_: (0, 0)),
            pl.BlockSpec(memory_space=pl.ANY),
        ],
        out_specs=tok,
        scratch_shapes=[pltpu.VMEM((2 * MOE_SUB, MOE_LB, D_MODEL), BF16),
                        pltpu.VMEM((MOE_TOK, MOE_TOK), BF16),
                        pltpu.SemaphoreType.DMA((2 * MOE_SUB,))],
    )
    return pl.pallas_call(
        functools.partial(_combine_kernel, final_norm=final_norm),
        grid_spec=grid_spec,
        out_shape=jax.ShapeDtypeStruct((TOKENS, D_MODEL), F32),
        compiler_params=_cparams(("arbitrary",)),
        name="combine",
    )(plan["n8"], plan["local"], plan["glob"], onehot, gates, plan["local_col"], x, mod_l,
      final_g, ys)


def kernel(x, c, rel_bias, router_w, router_b, mod_w, mod_b, norm1_g, w_in, gmlp_ln_g, gmlp_ln_b,
           gmlp_ws, gmlp_bs, out_norm_a_g, out_norm_b_g, w_out, norm2_g, moe_w_gate, moe_w_up,
           moe_w_down, final_g):
    mod = _modulation(c, mod_w, mod_b).reshape(DEPTH, BATCH, N_MOD, D_MODEL)
    bias_tab = _bias_tables(rel_bias)
    rwt = router_w.T.astype(BF16)
    rb_col = router_b.reshape(N_EXPERTS, 1)
    xt = x.reshape(TOKENS, D_MODEL)
    win_b = w_in.astype(BF16)
    wout_b = w_out.astype(BF16)
    for l in range(DEPTH):
        ws2 = gmlp_ws[l].astype(BF16).reshape(A_GROUPS // 2, 2 * CHUNK, CHUNK)
        bs_exp = jnp.repeat(gmlp_bs[l].T, HEAD_DIM, axis=1)
        out_a, q, k, v = _front(
            xt, mod[l], norm1_g[l].reshape(1, -1), win_b[l], gmlp_ln_g[l].reshape(1, -1),
            gmlp_ln_b[l].reshape(1, -1), ws2, bs_exp, out_norm_a_g[l].reshape(1, -1))
        out_b = _attention(q, k, v, bias_tab)
        xt, h2, gates, onehot, count = _mid(
            out_a, out_b, xt, mod[l], out_norm_b_g[l].reshape(1, -1),
            wout_b[l], norm2_g[l].reshape(1, -1), rwt, rb_col)
        plan = _moe_plan(count)
        xs = _dispatch(plan, h2, onehot)
        ys = _experts(plan, xs, l, moe_w_gate, moe_w_up, moe_w_down)
        xt = _combine(plan, onehot, gates, ys, xt, mod[l], final_g.reshape(1, -1),
                      final_norm=(l == DEPTH - 1))
    return xt.reshape(BATCH, SEQ, D_MODEL)
```

```python
import functools
import math

import numpy as np
import jax
import jax.numpy as jnp
from jax import lax
from jax.experimental import pallas as pl
from jax.experimental.pallas import tpu as pltpu

D_MODEL = 1024
BATCH = 8
SEQ = 2048
DEPTH = 2
TOKENS = BATCH * SEQ
HEAD_DIM = 64
A_WIDTH = 512
B_WIDTH = 512
A_GROUPS = 8
IN_COLS = 2 * A_WIDTH + 3 * B_WIDTH
CHUNK = 128
DILATED_CONFIGS = ((128, 1), (512, 4), (2048, 16))
ATTN_BLOCK = 128
REL_BUCKETS = 32
REL_MAX_EXACT = REL_BUCKETS // 2
REL_MAX_DISTANCE = 2048
N_EXPERTS = 16
N_EXPERT_GROUPS = 4
EXPERTS_PER_GROUP = 4
D_EXPERT = 512
N_MOD = 6
EPS = 1e-6
NEG_INF = -1e30

LANES = 128
HEAD_PAIR = 2 * HEAD_DIM
N_PAIRS = B_WIDTH // HEAD_PAIR
RESIDUES = max(d for _, d in DILATED_CONFIGS)
LOG2E = math.log2(math.e)

F32 = jnp.float32
BF16 = jnp.bfloat16

VMEM_LIMIT = 56 * 1024 * 1024


def _cparams(sem):
    return pltpu.CompilerParams(dimension_semantics=sem, vmem_limit_bytes=VMEM_LIMIT)


def _gelu(x):
    return 0.5 * x * (1.0 + lax.erf(x * math.sqrt(0.5)))


MOD_TN = 1024


def _mod_kernel(c_ref, w_ref, b_ref, o_ref):
    ca = jax.nn.silu(c_ref[...])
    o_ref[...] = jnp.dot(ca.astype(BF16), w_ref[...].astype(BF16),
                         preferred_element_type=F32) + b_ref[...]


def _modulation(c, mod_w, mod_b):
    n_cols = N_MOD * D_MODEL
    return pl.pallas_call(
        _mod_kernel,
        grid=(DEPTH, n_cols // MOD_TN),
        in_specs=[
            pl.BlockSpec((BATCH, D_MODEL), lambda l, j: (0, 0)),
            pl.BlockSpec((None, D_MODEL, MOD_TN), lambda l, j: (l, 0, j)),
            pl.BlockSpec((None, 1, MOD_TN), lambda l, j: (l, 0, j)),
        ],
        out_specs=pl.BlockSpec((None, BATCH, MOD_TN), lambda l, j: (l, 0, j)),
        out_shape=jax.ShapeDtypeStruct((DEPTH, BATCH, n_cols), F32),
        compiler_params=_cparams(("arbitrary", "arbitrary")),
        name="modulation",
    )(c, mod_w, mod_b.reshape(DEPTH, 1, n_cols))


K1_TM = 512


def _front_kernel(x_ref, mod_ref, n1g_ref, win_ref, lng_ref, lnb_ref, ws_ref, bs_ref, ga_ref,
                  a_ref, q_ref, k_ref, v_ref):
    x = x_ref[...]
    tm = x.shape[0]
    h = x * lax.rsqrt(jnp.mean(x * x, axis=-1, keepdims=True) + EPS) * n1g_ref[...]
    h = h * (1.0 + mod_ref[1:2, :]) + mod_ref[0:1, :]
    proj = jnp.dot(h.astype(BF16), win_ref[...], preferred_element_type=F32)

    q_ref[...] = proj[:, 2 * A_WIDTH:2 * A_WIDTH + B_WIDTH] * (HEAD_DIM ** -0.5 * LOG2E)
    k_ref[...] = proj[:, 2 * A_WIDTH + B_WIDTH:2 * A_WIDTH + 2 * B_WIDTH]
    v_ref[...] = proj[:, 2 * A_WIDTH + 2 * B_WIDTH:]

    u = _gelu(proj[:, :A_WIDTH])
    va = _gelu(proj[:, A_WIDTH:2 * A_WIDTH])
    mu = jnp.mean(va, axis=-1, keepdims=True)
    vc = va - mu
    vln = vc * lax.rsqrt(jnp.mean(vc * vc, axis=-1, keepdims=True) + EPS)
    vln = (vln * lng_ref[...] + lnb_ref[...]).astype(BF16)

    row = lax.broadcasted_iota(jnp.int32, (2 * CHUNK, CHUNK), 0)
    col = lax.broadcasted_iota(jnp.int32, (2 * CHUNK, CHUNK), 1)
    causal = (row % CHUNK) >= col
    first_group = lax.broadcasted_iota(jnp.int32, (CHUNK, LANES), 1) < HEAD_DIM
    wmix = [jnp.where(causal, ws_ref[p], jnp.zeros((), BF16)) for p in range(A_GROUPS // 2)]
    for c in range(tm // CHUNK):
        rows = slice(c * CHUNK, (c + 1) * CHUNK)
        parts = []
        for p in range(A_GROUPS // 2):
            vp = vln[rows, p * LANES:(p + 1) * LANES]
            r = jnp.dot(wmix[p], vp, preferred_element_type=F32)
            parts.append(jnp.where(first_group, r[:CHUNK], r[CHUNK:]))
        s = jnp.concatenate(parts, axis=-1) + bs_ref[...]
        oa = u[rows] * s
        oa = oa * lax.rsqrt(jnp.mean(oa * oa, axis=-1, keepdims=True) + EPS) * ga_ref[...]
        a_ref[rows, :] = oa.astype(BF16)


def _front(x, mod_l, n1g, win, lng, lnb, ws2, bs_exp, ga):
    tm = K1_TM
    tiles_per_seq = SEQ // tm
    row1024 = pl.BlockSpec((1, D_MODEL), lambda i: (0, 0))
    row512 = pl.BlockSpec((1, A_WIDTH), lambda i: (0, 0))
    half = pl.BlockSpec((tm, A_WIDTH), lambda i: (i, 0))
    return pl.pallas_call(
        _front_kernel,
        grid=(TOKENS // tm,),
        in_specs=[
            pl.BlockSpec((tm, D_MODEL), lambda i: (i, 0)),
            pl.BlockSpec((None, N_MOD, D_MODEL), lambda i: (i // tiles_per_seq, 0, 0)),
            row1024,
            pl.BlockSpec((D_MODEL, IN_COLS), lambda i: (0, 0)),
            row512, row512,
            pl.BlockSpec((A_GROUPS // 2, 2 * CHUNK, CHUNK), lambda i: (0, 0, 0)),
            pl.BlockSpec((CHUNK, A_WIDTH), lambda i: (0, 0)),
            row512,
        ],
        out_specs=[half, half, half, half],
        out_shape=[jax.ShapeDtypeStruct((TOKENS, A_WIDTH), BF16)] +
                  [jax.ShapeDtypeStruct((TOKENS, B_WIDTH), F32)] * 3,
        compiler_params=_cparams(("arbitrary",)),
        name="front",
    )(x, mod_l, n1g, win, lng, lnb, ws2, bs_exp, ga)


def _t5_bucket_np(dist):
    dist = np.maximum(dist, 0)
    ratio = np.log(np.maximum(dist, 1) / REL_MAX_EXACT) / np.log(REL_MAX_DISTANCE / REL_MAX_EXACT)
    large = REL_MAX_EXACT + np.floor(ratio * (REL_BUCKETS - REL_MAX_EXACT)).astype(np.int64)
    large = np.minimum(large, REL_BUCKETS - 1)
    return np.where(dist < REL_MAX_EXACT, dist, large).astype(np.int32)


def _bias_tables(rel_bias):
    blk = ATTN_BLOCK
    n_rel = 3 * blk
    rel = 2 * blk - 1 - np.arange(n_rel)
    tables = []
    for window, d in DILATED_CONFIGS:
        span = window // d
        bucket = _t5_bucket_np(np.clip(rel, 0, span) * d)
        valid = jnp.asarray((rel >= 0) & (rel <= span))
        w = jnp.where(valid[None, :], rel_bias.astype(F32)[bucket].T, NEG_INF)
        flat = jnp.tile(w, (1, blk))
        skew = flat[:, blk - 1:blk - 1 + blk * (n_rel - 1)].reshape(-1, blk, n_rel - 1)
        tab = skew[:, :, :2 * blk]
        sub = RESIDUES // d
        ln = blk // sub
        pos = np.arange(blk)
        rows = np.eye(blk, dtype=np.float32)[(pos % ln) * sub + pos // ln]
        cols = np.kron(np.eye(2, dtype=np.float32), rows)
        tab = jnp.einsum('pi,hik,qk->hpq', rows, tab * LOG2E, cols,
                         precision=lax.Precision.HIGHEST)
        tables.append(tab)
    return jnp.stack(tables, axis=0)


def _attn_kernel(q_ref, k_ref, v_ref, bias_ref, o_ref, qp, kp, vp, m_sc, l_sc, acc_sc):
    blk = ATTN_BLOCK
    res = RESIDUES
    per = SEQ // res
    sq = res * res
    seg = sq // res
    lane = lax.broadcasted_iota(jnp.int32, (blk, LANES), 1)
    head0 = lane < HEAD_DIM
    ones = jnp.ones((2 * blk, LANES), BF16)

    pa = lax.broadcasted_iota(jnp.int32, (sq, sq), 0)
    pb = lax.broadcasted_iota(jnp.int32, (sq, sq), 1)
    regroup = jnp.where(pb == res * (pa % res) + pa // res, 1.0, 0.0).astype(BF16)

    def residue_rows(g):
        return [slice(per * r + seg * g, per * r + seg * (g + 1)) for r in range(res)]

    def load(ref, slices):
        return jnp.concatenate([ref[s, :] for s in slices], axis=0) if len(slices) > 1 \
            else ref[slices[0], :]

    def store(ref, slices, val):
        ln = val.shape[0] // len(slices)
        for i, s in enumerate(slices):
            ref[s, :] = val[i * ln:(i + 1) * ln]

    for g in range(SEQ // sq):
        rows = slice(sq * g, sq * (g + 1))
        qkv = jnp.concatenate([q_ref[rows, :], k_ref[rows, :], v_ref[rows, :]], axis=1)
        qkv = jnp.dot(regroup, qkv.astype(BF16), preferred_element_type=F32)
        store(qp, residue_rows(g), qkv[:, :LANES])
        store(kp, residue_rows(g), qkv[:, LANES:2 * LANES])
        store(vp, residue_rows(g), qkv[:, 2 * LANES:])

    def merge_heads(t):
        return jnp.where(head0, t[:blk], t[blk:])

    def scores(q, kcat, bias):
        zero = jnp.zeros_like(q)
        qs = jnp.concatenate([jnp.where(head0, q, zero), jnp.where(head0, zero, q)],
                             axis=0).astype(BF16)
        return lax.dot_general(qs, kcat, (((1,), (1,)), ((), ())),
                               preferred_element_type=F32) + bias

    def weighted_values(s, vcat):
        nk = vcat.shape[0]
        m = jnp.max(s, axis=-1, keepdims=True)
        p = jnp.exp2(s - m).astype(BF16)
        pv = jnp.dot(p, jnp.concatenate([vcat, ones[:nk]], axis=1), preferred_element_type=F32)
        return (merge_heads(jnp.broadcast_to(m, (2 * blk, LANES))),
                merge_heads(pv[:, LANES:]), merge_heads(pv[:, :LANES]))

    def update(ci, slices, m_c, l_c, o_c):
        if ci > 0:
            m_r = load(m_sc, slices)
            m_n = jnp.maximum(m_r, m_c)
            a = jnp.exp2(m_r - m_n)
            b = jnp.exp2(m_c - m_n)
            l_c = a * load(l_sc, slices) + b * l_c
            o_c = a * load(acc_sc, slices) + b * o_c
            m_c = m_n
        store(m_sc, slices, m_c)
        store(l_sc, slices, l_c)
        store(acc_sc, slices, o_c)

    blocks = []
    for ci, (window, d) in enumerate(DILATED_CONFIGS):
        sub = res // d
        ln = blk // sub
        for r in range(d):
            for n in range(SEQ // d // blk):
                slices = [slice(per * (r + d * c) + ln * n, per * (r + d * c) + ln * (n + 1))
                          for c in range(sub)]
                blocks.append((ci, slices, n == 0))

    kv_prev = [None, None]

    def score_stage(ci, slices, first):
        k_cur = load(kp, slices).astype(BF16)
        v_cur = load(vp, slices).astype(BF16)
        if first:
            bias = jnp.concatenate([bias_ref[ci, 0, :, blk:], bias_ref[ci, 1, :, blk:]], axis=0)
            out = scores(load(qp, slices), k_cur, bias), v_cur
        else:
            bias = jnp.concatenate([bias_ref[ci, 0], bias_ref[ci, 1]], axis=0)
            out = (scores(load(qp, slices), jnp.concatenate([kv_prev[0], k_cur], axis=0), bias),
                   jnp.concatenate([kv_prev[1], v_cur], axis=0))
        kv_prev[0], kv_prev[1] = k_cur, v_cur
        return out

    ahead = score_stage(*blocks[0])
    for i, (ci, slices, _) in enumerate(blocks):
        s, vcat = ahead
        if i + 1 < len(blocks):
            ahead = score_stage(*blocks[i + 1])
        update(ci, slices, *weighted_values(s, vcat))

    for g in range(SEQ // sq):
        o = load(acc_sc, residue_rows(g)) / load(l_sc, residue_rows(g))
        hi = o.astype(BF16)
        lo = (o - hi.astype(F32)).astype(BF16)
        back = jnp.dot(regroup, jnp.concatenate([hi, lo], axis=1), preferred_element_type=F32)
        o_ref[sq * g:sq * (g + 1), :] = back[:, :LANES] + back[:, LANES:]


def _attention(q, k, v, bias_tab):
    n_cfg = len(DILATED_CONFIGS)
    blk = ATTN_BLOCK
    seq_spec = pl.BlockSpec((SEQ, HEAD_PAIR), lambda b, p: (b, p))
    return pl.pallas_call(
        _attn_kernel,
        grid=(BATCH, N_PAIRS),
        in_specs=[seq_spec, seq_spec, seq_spec,
                  pl.BlockSpec((n_cfg, 2, blk, 2 * blk), lambda b, p: (0, p, 0, 0))],
        out_specs=seq_spec,
        out_shape=jax.ShapeDtypeStruct((TOKENS, B_WIDTH), F32),
        scratch_shapes=[pltpu.VMEM((SEQ, HEAD_PAIR), F32)] * 6,
        compiler_params=_cparams(("arbitrary", "arbitrary")),
        name="attention",
    )(q, k, v, bias_tab)


K4_TM = 512


def _top2_sum(a, b, c, d):
    hi1, lo1 = jnp.maximum(a, b), jnp.minimum(a, b)
    hi2, lo2 = jnp.maximum(c, d), jnp.minimum(c, d)
    return jnp.maximum(hi1, hi2) + jnp.maximum(jnp.minimum(hi1, hi2), jnp.maximum(lo1, lo2))


def _route(logits_t, rb_col):
    m = jnp.max(logits_t, axis=0, keepdims=True)
    e = jnp.exp(logits_t - m)
    probs = e / jnp.sum(e, axis=0, keepdims=True)
    sel = probs + rb_col
    sel_rows = [sel[i:i + 1, :] for i in range(N_EXPERTS)]
    prob_rows = [probs[i:i + 1, :] for i in range(N_EXPERTS)]
    gsz = EXPERTS_PER_GROUP
    score = [_top2_sum(*sel_rows[g * gsz:(g + 1) * gsz]) for g in range(N_EXPERT_GROUPS)]
    chosen = []
    for g in range(N_EXPERT_GROUPS):
        best = None
        for g2 in range(N_EXPERT_GROUPS):
            if g2 == g:
                continue
            c = (score[g] > score[g2]) if g2 < g else (score[g] >= score[g2])
            best = c if best is None else jnp.logical_and(best, c)
        for i in range(gsz):
            ei = g * gsz + i
            rank = jnp.zeros_like(sel_rows[ei])
            for j in range(gsz):
                if j == i:
                    continue
                ej = g * gsz + j
                ahead = (sel_rows[ej] >= sel_rows[ei]) if j < i else (sel_rows[ej] > sel_rows[ei])
                rank = rank + jnp.where(ahead, 1.0, 0.0)
            chosen.append(jnp.logical_and(best, rank < float(2)))
    picked = [jnp.where(chosen[i], prob_rows[i], 0.0) for i in range(N_EXPERTS)]
    denom = picked[0]
    for i in range(1, N_EXPERTS):
        denom = denom + picked[i]
    gates = jnp.concatenate([pk / denom for pk in picked], axis=0)
    onehot = jnp.concatenate([jnp.where(ch, 1.0, 0.0) for ch in chosen], axis=0)
    return gates, onehot


def _mid_kernel(a_ref, ob_ref, x_ref, mod_ref, gb_ref, wout_ref, n2g_ref, rwt_ref, rb_ref,
                x1_ref, h2_ref, gates_ref, onehot_ref, count_ref):
    ob = ob_ref[...]
    bn = ob * lax.rsqrt(jnp.mean(ob * ob, axis=-1, keepdims=True) + EPS) * gb_ref[...]
    mixed = jnp.dot(a_ref[...], wout_ref[:A_WIDTH, :], preferred_element_type=F32)
    mixed = mixed + jnp.dot(bn.astype(BF16), wout_ref[A_WIDTH:, :], preferred_element_type=F32)
    x1 = x_ref[...] + mod_ref[2:3, :] * mixed
    x1_ref[...] = x1
    h2 = x1 * lax.rsqrt(jnp.mean(x1 * x1, axis=-1, keepdims=True) + EPS) * n2g_ref[...]
    h2 = (h2 * (1.0 + mod_ref[4:5, :]) + mod_ref[3:4, :]).astype(BF16)
    h2_ref[...] = h2
    logits_t = lax.dot_general(rwt_ref[...], h2, (((1,), (1,)), ((), ())),
                               preferred_element_type=F32)
    gates_t, onehot_t = _route(logits_t, rb_ref[...])
    gates_ref[...] = gates_t
    onehot_ref[...] = onehot_t
    for s in range(count_ref.shape[0]):
        count_ref[s] = jnp.sum(onehot_t[:, s * MOE_TOK:(s + 1) * MOE_TOK], axis=1, keepdims=True)


def _mid(out_a, out_b, x, mod_l, gb, wout, n2g, rwt, rb_col):
    tm = K4_TM
    tiles_per_seq = SEQ // tm
    tok = pl.BlockSpec((tm, D_MODEL), lambda i: (i, 0))
    half = pl.BlockSpec((tm, A_WIDTH), lambda i: (i, 0))
    route = pl.BlockSpec((N_EXPERTS, tm), lambda i: (0, i))
    return pl.pallas_call(
        _mid_kernel,
        grid=(TOKENS // tm,),
        in_specs=[
            half, half, tok,
            pl.BlockSpec((None, N_MOD, D_MODEL), lambda i: (i // tiles_per_seq, 0, 0)),
            pl.BlockSpec((1, B_WIDTH), lambda i: (0, 0)),
            pl.BlockSpec((D_MODEL, D_MODEL), lambda i: (0, 0)),
            pl.BlockSpec((1, D_MODEL), lambda i: (0, 0)),
            pl.BlockSpec((N_EXPERTS, D_MODEL), lambda i: (0, 0)),
            pl.BlockSpec((N_EXPERTS, 1), lambda i: (0, 0)),
        ],
        out_specs=[tok, tok, route, route,
                   pl.BlockSpec((tm // MOE_TOK, N_EXPERTS, 1), lambda i: (i, 0, 0))],
        out_shape=[jax.ShapeDtypeStruct((TOKENS, D_MODEL), F32),
                   jax.ShapeDtypeStruct((TOKENS, D_MODEL), BF16),
                   jax.ShapeDtypeStruct((N_EXPERTS, TOKENS), F32),
                   jax.ShapeDtypeStruct((N_EXPERTS, TOKENS), F32),
                   jax.ShapeDtypeStruct((TOKENS // MOE_TOK, N_EXPERTS, 1), F32)],
        compiler_params=_cparams(("arbitrary",)),
        name="mid",
    )(out_a, out_b, x, mod_l, gb, wout, n2g, rwt, rb_col)


MOE_TOK = 256
ROW_ALIGN = 16
MOE_SUB = 2
MOE_LB = 2 * MOE_TOK + N_EXPERTS * ROW_ALIGN
FFN_TM = 1024
N_TOK_TILES = TOKENS // MOE_TOK
ROWS_MAX = -(-(2 * TOKENS + N_TOK_TILES * N_EXPERTS * ROW_ALIGN
               + N_EXPERTS * (FFN_TM - ROW_ALIGN)) // FFN_TM) * FFN_TM


def _moe_plan(count):
    n = count.reshape(N_TOK_TILES, N_EXPERTS).astype(jnp.int32)
    n8 = jnp.maximum((n + (ROW_ALIGN - 1)) // ROW_ALIGN, 1) * ROW_ALIGN
    local = jnp.cumsum(n8, axis=1) - n8
    total = jnp.sum(n8, axis=0)
    seg = (total + (FFN_TM - 1)) // FFN_TM * FFN_TM
    seg_end = jnp.cumsum(seg)
    glob = (seg_end - seg)[None, :] + jnp.cumsum(n8, axis=0) - n8
    n_ffn = ROWS_MAX // FFN_TM
    n_active = seg_end[-1] // FFN_TM
    tile_row = jnp.arange(n_ffn, dtype=jnp.int32) * FFN_TM
    texp = jnp.sum((tile_row[:, None] >= seg_end[None, :]).astype(jnp.int32), axis=1)
    texp = jnp.minimum(texp, N_EXPERTS - 1)
    texp = jnp.where(jnp.arange(n_ffn) < n_active, texp, texp[n_active - 1])
    used = jnp.clip((seg_end - seg + total)[texp] - tile_row, 0, FFN_TM)
    return dict(n8=n8.reshape(-1), local=local.reshape(-1), glob=glob.reshape(-1),
                local_col=local.astype(F32).reshape(N_TOK_TILES, N_EXPERTS, 1),
                texp=texp.astype(jnp.int32), used=used.astype(jnp.int32),
                n_active=n_active.reshape(1).astype(jnp.int32))


def _run_copies(tile, n8_ref, local_ref, glob_ref, make_copy, start):
    for e in range(N_EXPERTS):
        idx = tile * N_EXPERTS + e
        cp = make_copy(pl.multiple_of(local_ref[idx], ROW_ALIGN),
                       pl.multiple_of(glob_ref[idx], ROW_ALIGN),
                       pl.multiple_of(n8_ref[idx], ROW_ALIGN))
        if start:
            cp.start()
        else:
            cp.wait()


def _fill_before(before_sc):
    @pl.when(pl.program_id(0) == 0)
    def _():
        src = lax.broadcasted_iota(jnp.int32, before_sc.shape, 0)
        dst = lax.broadcasted_iota(jnp.int32, before_sc.shape, 1)
        before_sc[...] = jnp.where(src < dst, 1.0, 0.0).astype(BF16)


def _sorted_rows(onehot, local_col, before):
    rank = jnp.dot(onehot.astype(BF16), before, preferred_element_type=F32)
    pos = local_col + rank
    chosen = onehot > 0.5
    lo = jnp.min(jnp.where(chosen, pos, float(MOE_LB)), axis=0, keepdims=True)
    hi = jnp.max(jnp.where(chosen, pos, -1.0), axis=0, keepdims=True)
    return pos, chosen, lo, hi


def _dispatch_kernel(n8_ref, local_ref, glob_ref, h_ref, oh_ref, lcol_ref, xs_ref,
                     lbuf, before_sc, sem):
    j = pl.program_id(0)
    last = pl.num_programs(0) - 1
    par = j % 2
    _fill_before(before_sc)
    row = lax.broadcasted_iota(jnp.int32, (MOE_LB, MOE_TOK), 0).astype(F32)
    for u in range(MOE_SUB):
        tok = slice(u * MOE_TOK, (u + 1) * MOE_TOK)
        _, _, lo, hi = _sorted_rows(oh_ref[:, tok], lcol_ref[u], before_sc[...])
        perm = (jnp.where(row == lo, 1.0, 0.0) + jnp.where(row == hi, 1.0, 0.0)).astype(BF16)
        lbuf[par * MOE_SUB + u] = jnp.dot(perm, h_ref[tok, :],
                                          preferred_element_type=F32).astype(BF16)

    def copy_from(slot):
        def make(lo_, go_, size):
            return pltpu.make_async_copy(lbuf.at[slot, pl.ds(lo_, size), :],
                                         xs_ref.at[pl.ds(go_, size), :], sem.at[slot])
        return make

    refs = (n8_ref, local_ref, glob_ref)
    for u in range(MOE_SUB):
        _run_copies(j * MOE_SUB + u, *refs, copy_from(par * MOE_SUB + u), True)

    @pl.when(j > 0)
    def _():
        for u in range(MOE_SUB):
            _run_copies((j - 1) * MOE_SUB + u, *refs, copy_from((1 - par) * MOE_SUB + u), False)

    @pl.when(j == last)
    def _():
        for u in range(MOE_SUB):
            _run_copies(j * MOE_SUB + u, *refs, copy_from(par * MOE_SUB + u), False)


def _dispatch(plan, h2, onehot):
    step_tok = MOE_SUB * MOE_TOK
    grid_spec = pltpu.PrefetchScalarGridSpec(
        num_scalar_prefetch=3,
        grid=(N_TOK_TILES // MOE_SUB,),
        in_specs=[
            pl.BlockSpec((step_tok, D_MODEL), lambda j, *_: (j, 0)),
            pl.BlockSpec((N_EXPERTS, step_tok), lambda j, *_: (0, j)),
            pl.BlockSpec((MOE_SUB, N_EXPERTS, 1), lambda j, *_: (j, 0, 0)),
        ],
        out_specs=pl.BlockSpec(memory_space=pl.ANY),
        scratch_shapes=[pltpu.VMEM((2 * MOE_SUB, MOE_LB, D_MODEL), BF16),
                        pltpu.VMEM((MOE_TOK, MOE_TOK), BF16),
                        pltpu.SemaphoreType.DMA((2 * MOE_SUB,))],
    )
    return pl.pallas_call(
        _dispatch_kernel,
        grid_spec=grid_spec,
        out_shape=jax.ShapeDtypeStruct((ROWS_MAX, D_MODEL), BF16),
        compiler_params=_cparams(("arbitrary",)),
        name="dispatch",
    )(plan["n8"], plan["local"], plan["glob"], h2, onehot, plan["local_col"])


def _experts_kernel(texp_ref, used_ref, nact_ref, xs_ref, wg_ref, wu_ref, wd_ref, ys_ref,
                    wg_b, wu_b, wd_b):
    i = pl.program_id(0)
    active = i < nact_ref[0]
    new_expert = jnp.logical_or(i == 0, texp_ref[i] != texp_ref[jnp.maximum(i - 1, 0)])

    @pl.when(jnp.logical_and(active, new_expert))
    def _():
        wg_b[...] = wg_ref[...].astype(BF16)
        wu_b[...] = wu_ref[...].astype(BF16)
        wd_b[...] = wd_ref[...].astype(BF16)

    half = FFN_TM // 2
    for part in range(2):
        rows = slice(part * half, (part + 1) * half)

        @pl.when(jnp.logical_and(active, used_ref[i] > part * half))
        def _(rows=rows):
            x = xs_ref[rows, :]
            hg = jnp.dot(x, wg_b[...], preferred_element_type=F32)
            hu = jnp.dot(x, wu_b[...], preferred_element_type=F32)
            act = (jax.nn.silu(hg) * hu).astype(BF16)
            ys_ref[rows, :] = jnp.dot(act, wd_b[...], preferred_element_type=F32).astype(BF16)


def _experts(plan, xs, layer, wg, wu, wd):
    def rows(i, texp, used, nact):
        return (jnp.minimum(i, nact[0] - 1), 0)

    def expert(i, texp, used, nact):
        return (layer, texp[i], 0, 0)

    grid_spec = pltpu.PrefetchScalarGridSpec(
        num_scalar_prefetch=3,
        grid=(ROWS_MAX // FFN_TM,),
        in_specs=[
            pl.BlockSpec((FFN_TM, D_MODEL), rows),
            pl.BlockSpec((None, None, D_MODEL, D_EXPERT), expert),
            pl.BlockSpec((None, None, D_MODEL, D_EXPERT), expert),
            pl.BlockSpec((None, None, D_EXPERT, D_MODEL), expert),
        ],
        out_specs=pl.BlockSpec((FFN_TM, D_MODEL), rows),
        scratch_shapes=[pltpu.VMEM((D_MODEL, D_EXPERT), BF16), pltpu.VMEM((D_MODEL, D_EXPERT), BF16),
                        pltpu.VMEM((D_EXPERT, D_MODEL), BF16)],
    )
    return pl.pallas_call(
        _experts_kernel,
        grid_spec=grid_spec,
        out_shape=jax.ShapeDtypeStruct((ROWS_MAX, D_MODEL), BF16),
        compiler_params=_cparams(("arbitrary",)),
        name="experts",
    )(plan["texp"], plan["used"], plan["n_active"], xs, wg, wu, wd)


def _combine_kernel(n8_ref, local_ref, glob_ref, oh_ref, g_ref, lcol_ref, x_ref, mod_ref, fg_ref,
                    ys_ref, o_ref, ybuf, before_sc, sem, *, final_norm):
    j = pl.program_id(0)
    n_steps = pl.num_programs(0)
    par = j % 2
    refs = (n8_ref, local_ref, glob_ref)
    _fill_before(before_sc)

    def copy_to(slot):
        def make(lo_, go_, size):
            return pltpu.make_async_copy(ys_ref.at[pl.ds(go_, size), :],
                                         ybuf.at[slot, pl.ds(lo_, size), :], sem.at[slot])
        return make

    @pl.when(j == 0)
    def _():
        ybuf[...] = jnp.zeros(ybuf.shape, BF16)
        for u in range(MOE_SUB):
            _run_copies(u, *refs, copy_to(u), True)

    @pl.when(j + 1 < n_steps)
    def _():
        for u in range(MOE_SUB):
            _run_copies((j + 1) * MOE_SUB + u, *refs, copy_to((1 - par) * MOE_SUB + u), True)

    row = lax.broadcasted_iota(jnp.int32, (MOE_LB, MOE_TOK), 0).astype(F32)
    for u in range(MOE_SUB):
        tok = slice(u * MOE_TOK, (u + 1) * MOE_TOK)
        pos, chosen, lo, hi = _sorted_rows(oh_ref[:, tok], lcol_ref[u], before_sc[...])
        gate = jnp.where(chosen, g_ref[:, tok], 0.0)
        g_lo = jnp.sum(jnp.where(pos == lo, gate, 0.0), axis=0, keepdims=True)
        g_hi = jnp.sum(jnp.where(pos == hi, gate, 0.0), axis=0, keepdims=True)
        perm = (jnp.where(row == lo, g_lo, 0.0) + jnp.where(row == hi, g_hi, 0.0)).astype(BF16)
        slot = par * MOE_SUB + u
        _run_copies(j * MOE_SUB + u, *refs, copy_to(slot), False)
        y = lax.dot_general(perm, ybuf[slot], (((0,), (0,)), ((), ())),
                            preferred_element_type=F32)
        x = x_ref[tok, :] + mod_ref[5:6, :] * y
        if final_norm:
            x = x * lax.rsqrt(jnp.mean(x * x, axis=-1, keepdims=True) + EPS) * fg_ref[...]
        o_ref[tok, :] = x


def _combine(plan, onehot, gates, ys, x, mod_l, final_g, final_norm):
    step_tok = MOE_SUB * MOE_TOK
    steps_per_seq = SEQ // step_tok
    route = pl.BlockSpec((N_EXPERTS, step_tok), lambda j, *_: (0, j))
    tok = pl.BlockSpec((step_tok, D_MODEL), lambda j, *_: (j, 0))
    grid_spec = pltpu.PrefetchScalarGridSpec(
        num_scalar_prefetch=3,
        grid=(N_TOK_TILES // MOE_SUB,),
        in_specs=[
            route, route,
            pl.BlockSpec((MOE_SUB, N_EXPERTS, 1), lambda j, *_: (j, 0, 0)),
            tok,
            pl.BlockSpec((None, N_MOD, D_MODEL), lambda j, *_: (j // steps_per_seq, 0, 0)),
            pl.BlockSpec((1, D_MODEL), lambda j, *_: (0, 0)),
            pl.BlockSpec(memory_space=pl.ANY),
        ],
        out_specs=tok,
        scratch_shapes=[pltpu.VMEM((2 * MOE_SUB, MOE_LB, D_MODEL), BF16),
                        pltpu.VMEM((MOE_TOK, MOE_TOK), BF16),
                        pltpu.SemaphoreType.DMA((2 * MOE_SUB,))],
    )
    return pl.pallas_call(
        functools.partial(_combine_kernel, final_norm=final_norm),
        grid_spec=grid_spec,
        out_shape=jax.ShapeDtypeStruct((TOKENS, D_MODEL), F32),
        compiler_params=_cparams(("arbitrary",)),
        name="combine",
    )(plan["n8"], plan["local"], plan["glob"], onehot, gates, plan["local_col"], x, mod_l,
      final_g, ys)


def kernel(x, c, rel_bias, router_w, router_b, mod_w, mod_b, norm1_g, w_in, gmlp_ln_g, gmlp_ln_b,
           gmlp_ws, gmlp_bs, out_norm_a_g, out_norm_b_g, w_out, norm2_g, moe_w_gate, moe_w_up,
           moe_w_down, final_g):
    mod = _modulation(c, mod_w, mod_b).reshape(DEPTH, BATCH, N_MOD, D_MODEL)
    bias_tab = _bias_tables(rel_bias)
    rwt = router_w.T.astype(BF16)
    rb_col = router_b.reshape(N_EXPERTS, 1)
    xt = x.reshape(TOKENS, D_MODEL)
    win_b = w_in.astype(BF16)
    wout_b = w_out.astype(BF16)
    for l in range(DEPTH):
        ws2 = gmlp_ws[l].astype(BF16).reshape(A_GROUPS // 2, 2 * CHUNK, CHUNK)
        bs_exp = jnp.repeat(gmlp_bs[l].T, HEAD_DIM, axis=1)
        out_a, q, k, v = _front(
            xt, mod[l], norm1_g[l].reshape(1, -1), win_b[l], gmlp_ln_g[l].reshape(1, -1),
            gmlp_ln_b[l].reshape(1, -1), ws2, bs_exp, out_norm_a_g[l].reshape(1, -1))
        out_b = _attention(q, k, v, bias_tab)
        xt, h2, gates, onehot, count = _mid(
            out_a, out_b, xt, mod[l], out_norm_b_g[l].reshape(1, -1),
            wout_b[l], norm2_g[l].reshape(1, -1), rwt, rb_col)
        plan = _moe_plan(count)
        xs = _dispatch(plan, h2, onehot)
        ys = _experts(plan, xs, l, moe_w_gate, moe_w_up, moe_w_down)
        xt = _combine(plan, onehot, gates, ys, xt, mod[l], final_g.reshape(1, -1),
                      final_norm=(l == DEPTH - 1))
    return xt.reshape(BATCH, SEQ, D_MODEL)
```

```python
import functools
import math

import numpy as np
import jax
import jax.numpy as jnp
from jax import lax
from jax.experimental import pallas as pl
from jax.experimental.pallas import tpu as pltpu

D_MODEL = 1024
BATCH = 8
SEQ = 2048
DEPTH = 2
TOKENS = BATCH * SEQ
HEAD_DIM = 64
A_WIDTH = 512
B_WIDTH = 512
A_GROUPS = 8
IN_COLS = 2 * A_WIDTH + 3 * B_WIDTH
CHUNK = 128
DILATED_CONFIGS = ((128, 1), (512, 4), (2048, 16))
ATTN_BLOCK = 128
REL_BUCKETS = 32
REL_MAX_EXACT = REL_BUCKETS // 2
REL_MAX_DISTANCE = 2048
N_EXPERTS = 16
N_EXPERT_GROUPS = 4
EXPERTS_PER_GROUP = 4
D_EXPERT = 512
N_MOD = 6
EPS = 1e-6
NEG_INF = -1e30

LANES = 128
HEAD_PAIR = 2 * HEAD_DIM
N_PAIRS = B_WIDTH // HEAD_PAIR
RESIDUES = max(d for _, d in DILATED_CONFIGS)
LOG2E = math.log2(math.e)

F32 = jnp.float32
BF16 = jnp.bfloat16

VMEM_LIMIT = 56 * 1024 * 1024


def _cparams(sem):
    return pltpu.CompilerParams(dimension_semantics=sem, vmem_limit_bytes=VMEM_LIMIT)


def _gelu(x):
    return 0.5 * x * (1.0 + lax.erf(x * math.sqrt(0.5)))


MOD_TN = 1024


def _mod_kernel(c_ref, w_ref, b_ref, o_ref):
    ca = jax.nn.silu(c_ref[...])
    o_ref[...] = jnp.dot(ca.astype(BF16), w_ref[...].astype(BF16),
                         preferred_element_type=F32) + b_ref[...]


def _modulation(c, mod_w, mod_b):
    n_cols = N_MOD * D_MODEL
    return pl.pallas_call(
        _mod_kernel,
        grid=(DEPTH, n_cols // MOD_TN),
        in_specs=[
            pl.BlockSpec((BATCH, D_MODEL), lambda l, j: (0, 0)),
            pl.BlockSpec((None, D_MODEL, MOD_TN), lambda l, j: (l, 0, j)),
            pl.BlockSpec((None, 1, MOD_TN), lambda l, j: (l, 0, j)),
        ],
        out_specs=pl.BlockSpec((None, BATCH, MOD_TN), lambda l, j: (l, 0, j)),
        out_shape=jax.ShapeDtypeStruct((DEPTH, BATCH, n_cols), F32),
        compiler_params=_cparams(("arbitrary", "arbitrary")),
        name="modulation",
    )(c, mod_w, mod_b.reshape(DEPTH, 1, n_cols))


K1_TM = 512


def _front_kernel(x_ref, mod_ref, n1g_ref, win_ref, lng_ref, lnb_ref, ws_ref, bs_ref, ga_ref,
                  a_ref, q_ref, k_ref, v_ref):
    x = x_ref[...]
    tm = x.shape[0]
    h = x * lax.rsqrt(jnp.mean(x * x, axis=-1, keepdims=True) + EPS) * n1g_ref[...]
    h = h * (1.0 + mod_ref[1:2, :]) + mod_ref[0:1, :]
    proj = jnp.dot(h.astype(BF16), win_ref[...], preferred_element_type=F32)

    q_ref[...] = proj[:, 2 * A_WIDTH:2 * A_WIDTH + B_WIDTH] * (HEAD_DIM ** -0.5 * LOG2E)
    k_ref[...] = proj[:, 2 * A_WIDTH + B_WIDTH:2 * A_WIDTH + 2 * B_WIDTH]
    v_ref[...] = proj[:, 2 * A_WIDTH + 2 * B_WIDTH:]

    u = _gelu(proj[:, :A_WIDTH])
    va = _gelu(proj[:, A_WIDTH:2 * A_WIDTH])
    mu = jnp.mean(va, axis=-1, keepdims=True)
    vc = va - mu
    vln = vc * lax.rsqrt(jnp.mean(vc * vc, axis=-1, keepdims=True) + EPS)
    vln = (vln * lng_ref[...] + lnb_ref[...]).astype(BF16)

    row = lax.broadcasted_iota(jnp.int32, (2 * CHUNK, CHUNK), 0)
    col = lax.broadcasted_iota(jnp.int32, (2 * CHUNK, CHUNK), 1)
    causal = (row % CHUNK) >= col
    first_group = lax.broadcasted_iota(jnp.int32, (CHUNK, LANES), 1) < HEAD_DIM
    wmix = [jnp.where(causal, ws_ref[p], jnp.zeros((), BF16)) for p in range(A_GROUPS // 2)]
    for c in range(tm // CHUNK):
        rows = slice(c * CHUNK, (c + 1) * CHUNK)
        parts = []
        for p in range(A_GROUPS // 2):
            vp = vln[rows, p * LANES:(p + 1) * LANES]
            r = jnp.dot(wmix[p], vp, preferred_element_type=F32)
            parts.append(jnp.where(first_group, r[:CHUNK], r[CHUNK:]))
        s = jnp.concatenate(parts, axis=-1) + bs_ref[...]
        oa = u[rows] * s
        oa = oa * lax.rsqrt(jnp.mean(oa * oa, axis=-1, keepdims=True) + EPS) * ga_ref[...]
        a_ref[rows, :] = oa.astype(BF16)


def _front(x, mod_l, n1g, win, lng, lnb, ws2, bs_exp, ga):
    tm = K1_TM
    tiles_per_seq = SEQ // tm
    row1024 = pl.BlockSpec((1, D_MODEL), lambda i: (0, 0))
    row512 = pl.BlockSpec((1, A_WIDTH), lambda i: (0, 0))
    half = pl.BlockSpec((tm, A_WIDTH), lambda i: (i, 0))
    return pl.pallas_call(
        _front_kernel,
        grid=(TOKENS // tm,),
        in_specs=[
            pl.BlockSpec((tm, D_MODEL), lambda i: (i, 0)),
            pl.BlockSpec((None, N_MOD, D_MODEL), lambda i: (i // tiles_per_seq, 0, 0)),
            row1024,
            pl.BlockSpec((D_MODEL, IN_COLS), lambda i: (0, 0)),
            row512, row512,
            pl.BlockSpec((A_GROUPS // 2, 2 * CHUNK, CHUNK), lambda i: (0, 0, 0)),
            pl.BlockSpec((CHUNK, A_WIDTH), lambda i: (0, 0)),
            row512,
        ],
        out_specs=[half, half, half, half],
        out_shape=[jax.ShapeDtypeStruct((TOKENS, A_WIDTH), BF16)] +
                  [jax.ShapeDtypeStruct((TOKENS, B_WIDTH), F32)] * 3,
        compiler_params=_cparams(("arbitrary",)),
        name="front",
    )(x, mod_l, n1g, win, lng, lnb, ws2, bs_exp, ga)


def _t5_bucket_np(dist):
    dist = np.maximum(dist, 0)
    ratio = np.log(np.maximum(dist, 1) / REL_MAX_EXACT) / np.log(REL_MAX_DISTANCE / REL_MAX_EXACT)
    large = REL_MAX_EXACT + np.floor(ratio * (REL_BUCKETS - REL_MAX_EXACT)).astype(np.int64)
    large = np.minimum(large, REL_BUCKETS - 1)
    return np.where(dist < REL_MAX_EXACT, dist, large).astype(np.int32)


def _bias_tables(rel_bias):
    blk = ATTN_BLOCK
    n_rel = 3 * blk
    rel = 2 * blk - 1 - np.arange(n_rel)
    tables = []
    for window, d in DILATED_CONFIGS:
        span = window // d
        bucket = _t5_bucket_np(np.clip(rel, 0, span) * d)
        valid = jnp.asarray((rel >= 0) & (rel <= span))
        w = jnp.where(valid[None, :], rel_bias.astype(F32)[bucket].T, NEG_INF)
        flat = jnp.tile(w, (1, blk))
        skew = flat[:, blk - 1:blk - 1 + blk * (n_rel - 1)].reshape(-1, blk, n_rel - 1)
        tab = skew[:, :, :2 * blk]
        sub = RESIDUES // d
        ln = blk // sub
        pos = np.arange(blk)
        rows = np.eye(blk, dtype=np.float32)[(pos % ln) * sub + pos // ln]
        cols = np.kron(np.eye(2, dtype=np.float32), rows)
        tab = jnp.einsum('pi,hik,qk->hpq', rows, tab * LOG2E, cols,
                         precision=lax.Precision.HIGHEST)
        tables.append(tab)
    return jnp.stack(tables, axis=0)


def _attn_kernel(q_ref, k_ref, v_ref, bias_ref, o_ref, qp, kp, vp, m_sc, l_sc, acc_sc):
    blk = ATTN_BLOCK
    res = RESIDUES
    per = SEQ // res
    sq = res * res
    seg = sq // res
    lane = lax.broadcasted_iota(jnp.int32, (blk, LANES), 1)
    head0 = lane < HEAD_DIM
    ones = jnp.ones((2 * blk, LANES), BF16)

    pa = lax.broadcasted_iota(jnp.int32, (sq, sq), 0)
    pb = lax.broadcasted_iota(jnp.int32, (sq, sq), 1)
    regroup = jnp.where(pb == res * (pa % res) + pa // res, 1.0, 0.0).astype(BF16)

    def residue_rows(g):
        return [slice(per * r + seg * g, per * r + seg * (g + 1)) for r in range(res)]

    def load(ref, slices):
        return jnp.concatenate([ref[s, :] for s in slices], axis=0) if len(slices) > 1 \
            else ref[slices[0], :]

    def store(ref, slices, val):
        ln = val.shape[0] // len(slices)
        for i, s in enumerate(slices):
            ref[s, :] = val[i * ln:(i + 1) * ln]

    for g in range(SEQ // sq):
        rows = slice(sq * g, sq * (g + 1))
        qkv = jnp.concatenate([q_ref[rows, :], k_ref[rows, :], v_ref[rows, :]], axis=1)
        qkv = jnp.dot(regroup, qkv.astype(BF16), preferred_element_type=F32)
        store(qp, residue_rows(g), qkv[:, :LANES])
        store(kp, residue_rows(g), qkv[:, LANES:2 * LANES])
        store(vp, residue_rows(g), qkv[:, 2 * LANES:])

    def merge_heads(t):
        return jnp.where(head0, t[:blk], t[blk:])

    def scores(q, kcat, bias):
        zero = jnp.zeros_like(q)
        qs = jnp.concatenate([jnp.where(head0, q, zero), jnp.where(head0, zero, q)],
                             axis=0).astype(BF16)
        return lax.dot_general(qs, kcat, (((1,), (1,)), ((), ())),
                               preferred_element_type=F32) + bias

    def weighted_values(s, vcat):
        nk = vcat.shape[0]
        m = jnp.max(s, axis=-1, keepdims=True)
        p = jnp.exp2(s - m).astype(BF16)
        pv = jnp.dot(p, jnp.concatenate([vcat, ones[:nk]], axis=1), preferred_element_type=F32)
        return (merge_heads(jnp.broadcast_to(m, (2 * blk, LANES))),
                merge_heads(pv[:, LANES:]), merge_heads(pv[:, :LANES]))

    def update(ci, slices, m_c, l_c, o_c):
        if ci > 0:
            m_r = load(m_sc, slices)
            m_n = jnp.maximum(m_r, m_c)
            a = jnp.exp2(m_r - m_n)
            b = jnp.exp2(m_c - m_n)
            l_c = a * load(l_sc, slices) + b * l_c
            o_c = a * load(acc_sc, slices) + b * o_c
            m_c = m_n
        store(m_sc, slices, m_c)
        store(l_sc, slices, l_c)
        store(acc_sc, slices, o_c)

    blocks = []
    for ci, (window, d) in enumerate(DILATED_CONFIGS):
        sub = res // d
        ln = blk // sub
        for r in range(d):
            for n in range(SEQ // d // blk):
                slices = [slice(per * (r + d * c) + ln * n, per * (r + d * c) + ln * (n + 1))
                          for c in range(sub)]
                blocks.append((ci, slices, n == 0))

    kv_prev = [None, None]

    def score_stage(ci, slices, first):
        k_cur = load(kp, slices).astype(BF16)
        v_cur = load(vp, slices).astype(BF16)
        if first:
            bias = jnp.concatenate([bias_ref[ci, 0, :, blk:], bias_ref[ci, 1, :, blk:]], axis=0)
            out = scores(load(qp, slices), k_cur, bias), v_cur
        else:
            bias = jnp.concatenate([bias_ref[ci, 0], bias_ref[ci, 1]], axis=0)
            out = (scores(load(qp, slices), jnp.concatenate([kv_prev[0], k_cur], axis=0), bias),
                   jnp.concatenate([kv_prev[1], v_cur], axis=0))
        kv_prev[0], kv_prev[1] = k_cur, v_cur
        return out

    ahead = score_stage(*blocks[0])
    for i, (ci, slices, _) in enumerate(blocks):
        s, vcat = ahead
        if i + 1 < len(blocks):
            ahead = score_stage(*blocks[i + 1])
        update(ci, slices, *weighted_values(s, vcat))

    for g in range(SEQ // sq):
        o = load(acc_sc, residue_rows(g)) / load(l_sc, residue_rows(g))
        hi = o.astype(BF16)
        lo = (o - hi.astype(F32)).astype(BF16)
        back = jnp.dot(regroup, jnp.concatenate([hi, lo], axis=1), preferred_element_type=F32)
        o_ref[sq * g:sq * (g + 1), :] = back[:, :LANES] + back[:, LANES:]


def _attention(q, k, v, bias_tab):
    n_cfg = len(DILATED_CONFIGS)
    blk = ATTN_BLOCK
    seq_spec = pl.BlockSpec((SEQ, HEAD_PAIR), lambda b, p: (b, p))
    return pl.pallas_call(
        _attn_kernel,
        grid=(BATCH, N_PAIRS),
        in_specs=[seq_spec, seq_spec, seq_spec,
                  pl.BlockSpec((n_cfg, 2, blk, 2 * blk), lambda b, p: (0, p, 0, 0))],
        out_specs=seq_spec,
        out_shape=jax.ShapeDtypeStruct((TOKENS, B_WIDTH), F32),
        scratch_shapes=[pltpu.VMEM((SEQ, HEAD_PAIR), F32)] * 6,
        compiler_params=_cparams(("arbitrary", "arbitrary")),
        name="attention",
    )(q, k, v, bias_tab)


K4_TM = 512


def _top2_sum(a, b, c, d):
    hi1, lo1 = jnp.maximum(a, b), jnp.minimum(a, b)
    hi2, lo2 = jnp.maximum(c, d), jnp.minimum(c, d)
    return jnp.maximum(hi1, hi2) + jnp.maximum(jnp.minimum(hi1, hi2), jnp.maximum(lo1, lo2))


def _route(logits_t, rb_col):
    m = jnp.max(logits_t, axis=0, keepdims=True)
    e = jnp.exp(logits_t - m)
    probs = e / jnp.sum(e, axis=0, keepdims=True)
    sel = probs + rb_col
    sel_rows = [sel[i:i + 1, :] for i in range(N_EXPERTS)]
    prob_rows = [probs[i:i + 1, :] for i in range(N_EXPERTS)]
    gsz = EXPERTS_PER_GROUP
    score = [_top2_sum(*sel_rows[g * gsz:(g + 1) * gsz]) for g in range(N_EXPERT_GROUPS)]
    chosen = []
    for g in range(N_EXPERT_GROUPS):
        best = None
        for g2 in range(N_EXPERT_GROUPS):
            if g2 == g:
                continue
            c = (score[g] > score[g2]) if g2 < g else (score[g] >= score[g2])
            best = c if best is None else jnp.logical_and(best, c)
        for i in range(gsz):
            ei = g * gsz + i
            rank = jnp.zeros_like(sel_rows[ei])
            for j in range(gsz):
                if j == i:
                    continue
                ej = g * gsz + j
                ahead = (sel_rows[ej] >= sel_rows[ei]) if j < i else (sel_rows[ej] > sel_rows[ei])
                rank = rank + jnp.where(ahead, 1.0, 0.0)
            chosen.append(jnp.logical_and(best, rank < float(2)))
    picked = [jnp.where(chosen[i], prob_rows[i], 0.0) for i in range(N_EXPERTS)]
    denom = picked[0]
    for i in range(1, N_EXPERTS):
        denom = denom + picked[i]
    gates = jnp.concatenate([pk / denom for pk in picked], axis=0)
    onehot = jnp.concatenate([jnp.where(ch, 1.0, 0.0) for ch in chosen], axis=0)
    return gates, onehot


def _mid_kernel(a_ref, ob_ref, x_ref, mod_ref, gb_ref, wout_ref, n2g_ref, rwt_ref, rb_ref,
                x1_ref, h2_ref, gates_ref, onehot_ref, count_ref):
    ob = ob_ref[...]
    bn = ob * lax.rsqrt(jnp.mean(ob * ob, axis=-1, keepdims=True) + EPS) * gb_ref[...]
    mixed = jnp.dot(a_ref[...], wout_ref[:A_WIDTH, :], preferred_element_type=F32)
    mixed = mixed + jnp.dot(bn.astype(BF16), wout_ref[A_WIDTH:, :], preferred_element_type=F32)
    x1 = x_ref[...] + mod_ref[2:3, :] * mixed
    x1_ref[...] = x1
    h2 = x1 * lax.rsqrt(jnp.mean(x1 * x1, axis=-1, keepdims=True) + EPS) * n2g_ref[...]
    h2 = (h2 * (1.0 + mod_ref[4:5, :]) + mod_ref[3:4, :]).astype(BF16)
    h2_ref[...] = h2
    logits_t = lax.dot_general(rwt_ref[...], h2, (((1,), (1,)), ((), ())),
                               preferred_element_type=F32)
    gates_t, onehot_t = _route(logits_t, rb_ref[...])
    gates_ref[...] = gates_t
    onehot_ref[...] = onehot_t
    for s in range(count_ref.shape[0]):
        count_ref[s] = jnp.sum(onehot_t[:, s * MOE_TOK:(s + 1) * MOE_TOK], axis=1, keepdims=True)


def _mid(out_a, out_b, x, mod_l, gb, wout, n2g, rwt, rb_col):
    tm = K4_TM
    tiles_per_seq = SEQ // tm
    tok = pl.BlockSpec((tm, D_MODEL), lambda i: (i, 0))
    half = pl.BlockSpec((tm, A_WIDTH), lambda i: (i, 0))
    route = pl.BlockSpec((N_EXPERTS, tm), lambda i: (0, i))
    return pl.pallas_call(
        _mid_kernel,
        grid=(TOKENS // tm,),
        in_specs=[
            half, half, tok,
            pl.BlockSpec((None, N_MOD, D_MODEL), lambda i: (i // tiles_per_seq, 0, 0)),
            pl.BlockSpec((1, B_WIDTH), lambda i: (0, 0)),
            pl.BlockSpec((D_MODEL, D_MODEL), lambda i: (0, 0)),
            pl.BlockSpec((1, D_MODEL), lambda i: (0, 0)),
            pl.BlockSpec((N_EXPERTS, D_MODEL), lambda i: (0, 0)),
            pl.BlockSpec((N_EXPERTS, 1), lambda i: (0, 0)),
        ],
        out_specs=[tok, tok, route, route,
                   pl.BlockSpec((tm // MOE_TOK, N_EXPERTS, 1), lambda i: (i, 0, 0))],
        out_shape=[jax.ShapeDtypeStruct((TOKENS, D_MODEL), F32),
                   jax.ShapeDtypeStruct((TOKENS, D_MODEL), BF16),
                   jax.ShapeDtypeStruct((N_EXPERTS, TOKENS), F32),
                   jax.ShapeDtypeStruct((N_EXPERTS, TOKENS), F32),
                   jax.ShapeDtypeStruct((TOKENS // MOE_TOK, N_EXPERTS, 1), F32)],
        compiler_params=_cparams(("arbitrary",)),
        name="mid",
    )(out_a, out_b, x, mod_l, gb, wout, n2g, rwt, rb_col)


MOE_TOK = 256
ROW_ALIGN = 16
MOE_SUB = 2
MOE_LB = 2 * MOE_TOK + N_EXPERTS * ROW_ALIGN
FFN_TM = 1024
N_TOK_TILES = TOKENS // MOE_TOK
ROWS_MAX = -(-(2 * TOKENS + N_TOK_TILES * N_EXPERTS * ROW_ALIGN
               + N_EXPERTS * (FFN_TM - ROW_ALIGN)) // FFN_TM) * FFN_TM


def _moe_plan(count):
    n = count.reshape(N_TOK_TILES, N_EXPERTS).astype(jnp.int32)
    n8 = jnp.maximum((n + (ROW_ALIGN - 1)) // ROW_ALIGN, 1) * ROW_ALIGN
    local = jnp.cumsum(n8, axis=1) - n8
    total = jnp.sum(n8, axis=0)
    seg = (total + (FFN_TM - 1)) // FFN_TM * FFN_TM
    seg_end = jnp.cumsum(seg)
    glob = (seg_end - seg)[None, :] + jnp.cumsum(n8, axis=0) - n8
    n_ffn = ROWS_MAX // FFN_TM
    n_active = seg_end[-1] // FFN_TM
    tile_row = jnp.arange(n_ffn, dtype=jnp.int32) * FFN_TM
    texp = jnp.sum((tile_row[:, None] >= seg_end[None, :]).astype(jnp.int32), axis=1)
    texp = jnp.minimum(texp, N_EXPERTS - 1)
    texp = jnp.where(jnp.arange(n_ffn) < n_active, texp, texp[n_active - 1])
    return dict(n8=n8.reshape(-1), local=local.reshape(-1), glob=glob.reshape(-1),
                local_col=local.astype(F32).reshape(N_TOK_TILES, N_EXPERTS, 1),
                texp=texp.astype(jnp.int32), n_active=n_active.reshape(1).astype(jnp.int32))


def _run_copies(tile, n8_ref, local_ref, glob_ref, make_copy, start):
    for e in range(N_EXPERTS):
        idx = tile * N_EXPERTS + e
        cp = make_copy(pl.multiple_of(local_ref[idx], ROW_ALIGN),
                       pl.multiple_of(glob_ref[idx], ROW_ALIGN),
                       pl.multiple_of(n8_ref[idx], ROW_ALIGN))
        if start:
            cp.start()
        else:
            cp.wait()


def _fill_before(before_sc):
    @pl.when(pl.program_id(0) == 0)
    def _():
        src = lax.broadcasted_iota(jnp.int32, before_sc.shape, 0)
        dst = lax.broadcasted_iota(jnp.int32, before_sc.shape, 1)
        before_sc[...] = jnp.where(src < dst, 1.0, 0.0).astype(BF16)


def _sorted_rows(onehot, local_col, before):
    rank = jnp.dot(onehot.astype(BF16), before, preferred_element_type=F32)
    pos = local_col + rank
    chosen = onehot > 0.5
    lo = jnp.min(jnp.where(chosen, pos, float(MOE_LB)), axis=0, keepdims=True)
    hi = jnp.max(jnp.where(chosen, pos, -1.0), axis=0, keepdims=True)
    return pos, chosen, lo, hi


def _dispatch_kernel(n8_ref, local_ref, glob_ref, h_ref, oh_ref, lcol_ref, xs_ref,
                     lbuf, before_sc, sem):
    j = pl.program_id(0)
    last = pl.num_programs(0) - 1
    par = j % 2
    _fill_before(before_sc)
    row = lax.broadcasted_iota(jnp.int32, (MOE_LB, MOE_TOK), 0).astype(F32)
    for u in range(MOE_SUB):
        tok = slice(u * MOE_TOK, (u + 1) * MOE_TOK)
        _, _, lo, hi = _sorted_rows(oh_ref[:, tok], lcol_ref[u], before_sc[...])
        perm = (jnp.where(row == lo, 1.0, 0.0) + jnp.where(row == hi, 1.0, 0.0)).astype(BF16)
        lbuf[par * MOE_SUB + u] = jnp.dot(perm, h_ref[tok, :],
                                          preferred_element_type=F32).astype(BF16)

    def copy_from(slot):
        def make(lo_, go_, size):
            return pltpu.make_async_copy(lbuf.at[slot, pl.ds(lo_, size), :],
                                         xs_ref.at[pl.ds(go_, size), :], sem.at[slot])
        return make

    refs = (n8_ref, local_ref, glob_ref)
    for u in range(MOE_SUB):
        _run_copies(j * MOE_SUB + u, *refs, copy_from(par * MOE_SUB + u), True)

    @pl.when(j > 0)
    def _():
        for u in range(MOE_SUB):
            _run_copies((j - 1) * MOE_SUB + u, *refs, copy_from((1 - par) * MOE_SUB + u), False)

    @pl.when(j == last)
    def _():
        for u in range(MOE_SUB):
            _run_copies(j * MOE_SUB + u, *refs, copy_from(par * MOE_SUB + u), False)


def _dispatch(plan, h2, onehot):
    step_tok = MOE_SUB * MOE_TOK
    grid_spec = pltpu.PrefetchScalarGridSpec(
        num_scalar_prefetch=3,
        grid=(N_TOK_TILES // MOE_SUB,),
        in_specs=[
            pl.BlockSpec((step_tok, D_MODEL), lambda j, *_: (j, 0)),
            pl.BlockSpec((N_EXPERTS, step_tok), lambda j, *_: (0, j)),
            pl.BlockSpec((MOE_SUB, N_EXPERTS, 1), lambda j, *_: (j, 0, 0)),
        ],
        out_specs=pl.BlockSpec(memory_space=pl.ANY),
        scratch_shapes=[pltpu.VMEM((2 * MOE_SUB, MOE_LB, D_MODEL), BF16),
                        pltpu.VMEM((MOE_TOK, MOE_TOK), BF16),
                        pltpu.SemaphoreType.DMA((2 * MOE_SUB,))],
    )
    return pl.pallas_call(
        _dispatch_kernel,
        grid_spec=grid_spec,
        out_shape=jax.ShapeDtypeStruct((ROWS_MAX, D_MODEL), BF16),
        compiler_params=_cparams(("arbitrary",)),
        name="dispatch",
    )(plan["n8"], plan["local"], plan["glob"], h2, onehot, plan["local_col"])


def _experts_kernel(texp_ref, nact_ref, xs_ref, wg_ref, wu_ref, wd_ref, ys_ref, wg_b, wu_b, wd_b):
    i = pl.program_id(0)
    active = i < nact_ref[0]
    new_expert = jnp.logical_or(i == 0, texp_ref[i] != texp_ref[jnp.maximum(i - 1, 0)])

    @pl.when(jnp.logical_and(active, new_expert))
    def _():
        wg_b[...] = wg_ref[...].astype(BF16)
        wu_b[...] = wu_ref[...].astype(BF16)
        wd_b[...] = wd_ref[...].astype(BF16)

    @pl.when(active)
    def _():
        x = xs_ref[...]
        hg = jnp.dot(x, wg_b[...], preferred_element_type=F32)
        hu = jnp.dot(x, wu_b[...], preferred_element_type=F32)
        act = (jax.nn.silu(hg) * hu).astype(BF16)
        ys_ref[...] = jnp.dot(act, wd_b[...], preferred_element_type=F32).astype(BF16)


def _experts(plan, xs, layer, wg, wu, wd):
    def rows(i, texp, nact):
        return (jnp.minimum(i, nact[0] - 1), 0)

    def expert(i, texp, nact):
        return (layer, texp[i], 0, 0)

    grid_spec = pltpu.PrefetchScalarGridSpec(
        num_scalar_prefetch=2,
        grid=(ROWS_MAX // FFN_TM,),
        in_specs=[
            pl.BlockSpec((FFN_TM, D_MODEL), rows),
            pl.BlockSpec((None, None, D_MODEL, D_EXPERT), expert),
            pl.BlockSpec((None, None, D_MODEL, D_EXPERT), expert),
            pl.BlockSpec((None, None, D_EXPERT, D_MODEL), expert),
        ],
        out_specs=pl.BlockSpec((FFN_TM, D_MODEL), rows),
        scratch_shapes=[pltpu.VMEM((D_MODEL, D_EXPERT), BF16), pltpu.VMEM((D_MODEL, D_EXPERT), BF16),
                        pltpu.VMEM((D_EXPERT, D_MODEL), BF16)],
    )
    return pl.pallas_call(
        _experts_kernel,
        grid_spec=grid_spec,
        out_shape=jax.ShapeDtypeStruct((ROWS_MAX, D_MODEL), BF16),
        compiler_params=_cparams(("arbitrary",)),
        name="experts",
    )(plan["texp"], plan["n_active"], xs, wg, wu, wd)


def _combine_kernel(n8_ref, local_ref, glob_ref, oh_ref, g_ref, lcol_ref, x_ref, mod_ref, fg_ref,
                    ys_ref, o_ref, ybuf, before_sc, sem, *, final_norm):
    j = pl.program_id(0)
    n_steps = pl.num_programs(0)
    par = j % 2
    refs = (n8_ref, local_ref, glob_ref)
    _fill_before(before_sc)

    def copy_to(slot):
        def make(lo_, go_, size):
            return pltpu.make_async_copy(ys_ref.at[pl.ds(go_, size), :],
                                         ybuf.at[slot, pl.ds(lo_, size), :], sem.at[slot])
        return make

    @pl.when(j == 0)
    def _():
        ybuf[...] = jnp.zeros(ybuf.shape, BF16)
        for u in range(MOE_SUB):
            _run_copies(u, *refs, copy_to(u), True)

    @pl.when(j + 1 < n_steps)
    def _():
        for u in range(MOE_SUB):
            _run_copies((j + 1) * MOE_SUB + u, *refs, copy_to((1 - par) * MOE_SUB + u), True)

    row = lax.broadcasted_iota(jnp.int32, (MOE_LB, MOE_TOK), 0).astype(F32)
    for u in range(MOE_SUB):
        tok = slice(u * MOE_TOK, (u + 1) * MOE_TOK)
        pos, chosen, lo, hi = _sorted_rows(oh_ref[:, tok], lcol_ref[u], before_sc[...])
        gate = jnp.where(chosen, g_ref[:, tok], 0.0)
        g_lo = jnp.sum(jnp.where(pos == lo, gate, 0.0), axis=0, keepdims=True)
        g_hi = jnp.sum(jnp.where(pos == hi, gate, 0.0), axis=0, keepdims=True)
        perm = (jnp.where(row == lo, g_lo, 0.0) + jnp.where(row == hi, g_hi, 0.0)).astype(BF16)
        slot = par * MOE_SUB + u
        _run_copies(j * MOE_SUB + u, *refs, copy_to(slot), False)
        y = lax.dot_general(perm, ybuf[slot], (((0,), (0,)), ((), ())),
                            preferred_element_type=F32)
        x = x_ref[tok, :] + mod_ref[5:6, :] * y
        if final_norm:
            x = x * lax.rsqrt(jnp.mean(x * x, axis=-1, keepdims=True) + EPS) * fg_ref[...]
        o_ref[tok, :] = x


N_FRONT_IN = 8


def _combine_front_kernel(*refs):
    n_in = 3 + 7
    comb_in, front_in = refs[:n_in], refs[n_in:n_in + N_FRONT_IN]
    o_ref, *front_out = refs[n_in + N_FRONT_IN:n_in + N_FRONT_IN + 5]
    scratch = refs[n_in + N_FRONT_IN + 5:]
    _combine_kernel(*comb_in, o_ref, *scratch, final_norm=False)
    _front_kernel(o_ref, *front_in, *front_out)


def _combine(plan, onehot, gates, ys, x, mod_l, final_g, front_args=None):
    step_tok = MOE_SUB * MOE_TOK
    steps_per_seq = SEQ // step_tok
    route = pl.BlockSpec((N_EXPERTS, step_tok), lambda j, *_: (0, j))
    tok = pl.BlockSpec((step_tok, D_MODEL), lambda j, *_: (j, 0))
    modspec = pl.BlockSpec((None, N_MOD, D_MODEL), lambda j, *_: (j // steps_per_seq, 0, 0))
    in_specs = [
        route, route,
        pl.BlockSpec((MOE_SUB, N_EXPERTS, 1), lambda j, *_: (j, 0, 0)),
        tok, modspec,
        pl.BlockSpec((1, D_MODEL), lambda j, *_: (0, 0)),
        pl.BlockSpec(memory_space=pl.ANY),
    ]
    args = [onehot, gates, plan["local_col"], x, mod_l, final_g, ys]
    out_specs = tok
    out_shape = jax.ShapeDtypeStruct((TOKENS, D_MODEL), F32)
    body = functools.partial(_combine_kernel, final_norm=True)
    if front_args is not None:
        assert step_tok == K1_TM and len(front_args) == N_FRONT_IN
        row1024 = pl.BlockSpec((1, D_MODEL), lambda j, *_: (0, 0))
        row512 = pl.BlockSpec((1, A_WIDTH), lambda j, *_: (0, 0))
        half = pl.BlockSpec((step_tok, A_WIDTH), lambda j, *_: (j, 0))
        in_specs += [
            modspec, row1024,
            pl.BlockSpec((D_MODEL, IN_COLS), lambda j, *_: (0, 0)),
            row512, row512,
            pl.BlockSpec((A_GROUPS // 2, 2 * CHUNK, CHUNK), lambda j, *_: (0, 0, 0)),
            pl.BlockSpec((CHUNK, A_WIDTH), lambda j, *_: (0, 0)),
            row512,
        ]
        args += list(front_args)
        out_specs = [tok, half, half, half, half]
        out_shape = [out_shape, jax.ShapeDtypeStruct((TOKENS, A_WIDTH), BF16)] + \
                    [jax.ShapeDtypeStruct((TOKENS, B_WIDTH), F32)] * 3
        body = _combine_front_kernel
    grid_spec = pltpu.PrefetchScalarGridSpec(
        num_scalar_prefetch=3,
        grid=(N_TOK_TILES // MOE_SUB,),
        in_specs=in_specs,
        out_specs=out_specs,
        scratch_shapes=[pltpu.VMEM((2 * MOE_SUB, MOE_LB, D_MODEL), BF16),
                        pltpu.VMEM((MOE_TOK, MOE_TOK), BF16),
                        pltpu.SemaphoreType.DMA((2 * MOE_SUB,))],
    )
    return pl.pallas_call(
        body,
        grid_spec=grid_spec,
        out_shape=out_shape,
        compiler_params=_cparams(("arbitrary",)),
        name="combine",
    )(plan["n8"], plan["local"], plan["glob"], *args)


def kernel(x, c, rel_bias, router_w, router_b, mod_w, mod_b, norm1_g, w_in, gmlp_ln_g, gmlp_ln_b,
           gmlp_ws, gmlp_bs, out_norm_a_g, out_norm_b_g, w_out, norm2_g, moe_w_gate, moe_w_up,
           moe_w_down, final_g):
    mod = _modulation(c, mod_w, mod_b).reshape(DEPTH, BATCH, N_MOD, D_MODEL)
    bias_tab = _bias_tables(rel_bias)
    rwt = router_w.T.astype(BF16)
    rb_col = router_b.reshape(N_EXPERTS, 1)
    xt = x.reshape(TOKENS, D_MODEL)
    win_b = w_in.astype(BF16)
    wout_b = w_out.astype(BF16)

    def front_args(l):
        ws2 = gmlp_ws[l].astype(BF16).reshape(A_GROUPS // 2, 2 * CHUNK, CHUNK)
        bs_exp = jnp.repeat(gmlp_bs[l].T, HEAD_DIM, axis=1)
        return (mod[l], norm1_g[l].reshape(1, -1), win_b[l], gmlp_ln_g[l].reshape(1, -1),
                gmlp_ln_b[l].reshape(1, -1), ws2, bs_exp, out_norm_a_g[l].reshape(1, -1))

    out_a, q, k, v = _front(xt, *front_args(0))
    for l in range(DEPTH):
        out_b = _attention(q, k, v, bias_tab)
        xt, h2, gates, onehot, count = _mid(
            out_a, out_b, xt, mod[l], out_norm_b_g[l].reshape(1, -1),
            wout_b[l], norm2_g[l].reshape(1, -1), rwt, rb_col)
        plan = _moe_plan(count)
        xs = _dispatch(plan, h2, onehot)
        ys = _experts(plan, xs, l, moe_w_gate, moe_w_up, moe_w_down)
        if l + 1 < DEPTH:
            xt, out_a, q, k, v = _combine(plan, onehot, gates, ys, xt, mod[l],
                                          final_g.reshape(1, -1), front_args(l + 1))
        else:
            xt = _combine(plan, onehot, gates, ys, xt, mod[l], final_g.reshape(1, -1))
    return xt.reshape(BATCH, SEQ, D_MODEL)
```

```python
import functools
import math

import numpy as np
import jax
import jax.numpy as jnp
from jax import lax
from jax.experimental import pallas as pl
from jax.experimental.pallas import tpu as pltpu

D_MODEL = 1024
BATCH = 8
SEQ = 2048
DEPTH = 2
TOKENS = BATCH * SEQ
HEAD_DIM = 64
A_WIDTH = 512
B_WIDTH = 512
A_GROUPS = 8
IN_COLS = 2 * A_WIDTH + 3 * B_WIDTH
CHUNK = 128
DILATED_CONFIGS = ((128, 1), (512, 4), (2048, 16))
ATTN_BLOCK = 128
REL_BUCKETS = 32
REL_MAX_EXACT = REL_BUCKETS // 2
REL_MAX_DISTANCE = 2048
N_EXPERTS = 16
N_EXPERT_GROUPS = 4
EXPERTS_PER_GROUP = 4
D_EXPERT = 512
N_MOD = 6
EPS = 1e-6
NEG_INF = -1e30

LANES = 128
HEAD_PAIR = 2 * HEAD_DIM
N_PAIRS = B_WIDTH // HEAD_PAIR
RESIDUES = max(d for _, d in DILATED_CONFIGS)
LOG2E = math.log2(math.e)

F32 = jnp.float32
BF16 = jnp.bfloat16

VMEM_LIMIT = 56 * 1024 * 1024


def _cparams(sem):
    return pltpu.CompilerParams(dimension_semantics=sem, vmem_limit_bytes=VMEM_LIMIT)


def _gelu(x):
    return 0.5 * x * (1.0 + lax.erf(x * math.sqrt(0.5)))


MOD_TN = 1024


def _mod_kernel(c_ref, w_ref, b_ref, o_ref):
    ca = jax.nn.silu(c_ref[...])
    o_ref[...] = jnp.dot(ca.astype(BF16), w_ref[...].astype(BF16),
                         preferred_element_type=F32) + b_ref[...]


def _modulation(c, mod_w, mod_b):
    n_cols = N_MOD * D_MODEL
    return pl.pallas_call(
        _mod_kernel,
        grid=(DEPTH, n_cols // MOD_TN),
        in_specs=[
            pl.BlockSpec((BATCH, D_MODEL), lambda l, j: (0, 0)),
            pl.BlockSpec((None, D_MODEL, MOD_TN), lambda l, j: (l, 0, j)),
            pl.BlockSpec((None, 1, MOD_TN), lambda l, j: (l, 0, j)),
        ],
        out_specs=pl.BlockSpec((None, BATCH, MOD_TN), lambda l, j: (l, 0, j)),
        out_shape=jax.ShapeDtypeStruct((DEPTH, BATCH, n_cols), F32),
        compiler_params=_cparams(("arbitrary", "arbitrary")),
        name="modulation",
    )(c, mod_w, mod_b.reshape(DEPTH, 1, n_cols))


K1_TM = 1024


def _front_kernel(x_ref, mod_ref, n1g_ref, win_ref, lng_ref, lnb_ref, ws_ref, bs_ref, ga_ref,
                  a_ref, q_ref, k_ref, v_ref):
    x = x_ref[...]
    tm = x.shape[0]
    h = x * lax.rsqrt(jnp.mean(x * x, axis=-1, keepdims=True) + EPS) * n1g_ref[...]
    h = h * (1.0 + mod_ref[1:2, :]) + mod_ref[0:1, :]
    proj = jnp.dot(h.astype(BF16), win_ref[...], preferred_element_type=F32)

    q_ref[...] = proj[:, 2 * A_WIDTH:2 * A_WIDTH + B_WIDTH] * (HEAD_DIM ** -0.5 * LOG2E)
    k_ref[...] = proj[:, 2 * A_WIDTH + B_WIDTH:2 * A_WIDTH + 2 * B_WIDTH]
    v_ref[...] = proj[:, 2 * A_WIDTH + 2 * B_WIDTH:]

    u = _gelu(proj[:, :A_WIDTH])
    va = _gelu(proj[:, A_WIDTH:2 * A_WIDTH])
    mu = jnp.mean(va, axis=-1, keepdims=True)
    vc = va - mu
    vln = vc * lax.rsqrt(jnp.mean(vc * vc, axis=-1, keepdims=True) + EPS)
    vln = (vln * lng_ref[...] + lnb_ref[...]).astype(BF16)

    row = lax.broadcasted_iota(jnp.int32, (2 * CHUNK, CHUNK), 0)
    col = lax.broadcasted_iota(jnp.int32, (2 * CHUNK, CHUNK), 1)
    causal = (row % CHUNK) >= col
    first_group = lax.broadcasted_iota(jnp.int32, (CHUNK, LANES), 1) < HEAD_DIM
    wmix = [jnp.where(causal, ws_ref[p], jnp.zeros((), BF16)) for p in range(A_GROUPS // 2)]
    for c in range(tm // CHUNK):
        rows = slice(c * CHUNK, (c + 1) * CHUNK)
        parts = []
        for p in range(A_GROUPS // 2):
            vp = vln[rows, p * LANES:(p + 1) * LANES]
            r = jnp.dot(wmix[p], vp, preferred_element_type=F32)
            parts.append(jnp.where(first_group, r[:CHUNK], r[CHUNK:]))
        s = jnp.concatenate(parts, axis=-1) + bs_ref[...]
        oa = u[rows] * s
        oa = oa * lax.rsqrt(jnp.mean(oa * oa, axis=-1, keepdims=True) + EPS) * ga_ref[...]
        a_ref[rows, :] = oa.astype(BF16)


def _front(x, mod_l, n1g, win, lng, lnb, ws2, bs_exp, ga):
    tm = K1_TM
    tiles_per_seq = SEQ // tm
    row1024 = pl.BlockSpec((1, D_MODEL), lambda i: (0, 0))
    row512 = pl.BlockSpec((1, A_WIDTH), lambda i: (0, 0))
    half = pl.BlockSpec((tm, A_WIDTH), lambda i: (i, 0))
    return pl.pallas_call(
        _front_kernel,
        grid=(TOKENS // tm,),
        in_specs=[
            pl.BlockSpec((tm, D_MODEL), lambda i: (i, 0)),
            pl.BlockSpec((None, N_MOD, D_MODEL), lambda i: (i // tiles_per_seq, 0, 0)),
            row1024,
            pl.BlockSpec((D_MODEL, IN_COLS), lambda i: (0, 0)),
            row512, row512,
            pl.BlockSpec((A_GROUPS // 2, 2 * CHUNK, CHUNK), lambda i: (0, 0, 0)),
            pl.BlockSpec((CHUNK, A_WIDTH), lambda i: (0, 0)),
            row512,
        ],
        out_specs=[half, half, half, half],
        out_shape=[jax.ShapeDtypeStruct((TOKENS, A_WIDTH), BF16)] +
                  [jax.ShapeDtypeStruct((TOKENS, B_WIDTH), F32)] * 3,
        compiler_params=_cparams(("arbitrary",)),
        name="front",
    )(x, mod_l, n1g, win, lng, lnb, ws2, bs_exp, ga)


def _t5_bucket_np(dist):
    dist = np.maximum(dist, 0)
    ratio = np.log(np.maximum(dist, 1) / REL_MAX_EXACT) / np.log(REL_MAX_DISTANCE / REL_MAX_EXACT)
    large = REL_MAX_EXACT + np.floor(ratio * (REL_BUCKETS - REL_MAX_EXACT)).astype(np.int64)
    large = np.minimum(large, REL_BUCKETS - 1)
    return np.where(dist < REL_MAX_EXACT, dist, large).astype(np.int32)


def _bias_tables(rel_bias):
    blk = ATTN_BLOCK
    n_rel = 3 * blk
    rel = 2 * blk - 1 - np.arange(n_rel)
    tables = []
    for window, d in DILATED_CONFIGS:
        span = window // d
        bucket = _t5_bucket_np(np.clip(rel, 0, span) * d)
        valid = jnp.asarray((rel >= 0) & (rel <= span))
        w = jnp.where(valid[None, :], rel_bias.astype(F32)[bucket].T, NEG_INF)
        flat = jnp.tile(w, (1, blk))
        skew = flat[:, blk - 1:blk - 1 + blk * (n_rel - 1)].reshape(-1, blk, n_rel - 1)
        tab = skew[:, :, :2 * blk]
        sub = RESIDUES // d
        ln = blk // sub
        pos = np.arange(blk)
        rows = np.eye(blk, dtype=np.float32)[(pos % ln) * sub + pos // ln]
        cols = np.kron(np.eye(2, dtype=np.float32), rows)
        tab = jnp.einsum('pi,hik,qk->hpq', rows, tab * LOG2E, cols,
                         precision=lax.Precision.HIGHEST)
        tables.append(tab)
    return jnp.stack(tables, axis=0)


def _attn_kernel(q_ref, k_ref, v_ref, bias_ref, o_ref, qp, kp, vp, m_sc, l_sc, acc_sc):
    blk = ATTN_BLOCK
    res = RESIDUES
    per = SEQ // res
    sq = res * res
    seg = sq // res
    lane = lax.broadcasted_iota(jnp.int32, (blk, LANES), 1)
    head0 = lane < HEAD_DIM
    ones = jnp.ones((2 * blk, LANES), BF16)

    pa = lax.broadcasted_iota(jnp.int32, (sq, sq), 0)
    pb = lax.broadcasted_iota(jnp.int32, (sq, sq), 1)
    regroup = jnp.where(pb == res * (pa % res) + pa // res, 1.0, 0.0).astype(BF16)

    def residue_rows(g):
        return [slice(per * r + seg * g, per * r + seg * (g + 1)) for r in range(res)]

    def load(ref, slices):
        return jnp.concatenate([ref[s, :] for s in slices], axis=0) if len(slices) > 1 \
            else ref[slices[0], :]

    def store(ref, slices, val):
        ln = val.shape[0] // len(slices)
        for i, s in enumerate(slices):
            ref[s, :] = val[i * ln:(i + 1) * ln]

    for g in range(SEQ // sq):
        rows = slice(sq * g, sq * (g + 1))
        qkv = jnp.concatenate([q_ref[rows, :], k_ref[rows, :], v_ref[rows, :]], axis=1)
        qkv = jnp.dot(regroup, qkv.astype(BF16), preferred_element_type=F32)
        store(qp, residue_rows(g), qkv[:, :LANES])
        store(kp, residue_rows(g), qkv[:, LANES:2 * LANES])
        store(vp, residue_rows(g), qkv[:, 2 * LANES:])

    def merge_heads(t):
        return jnp.where(head0, t[:blk], t[blk:])

    def scores(q, kcat, bias):
        zero = jnp.zeros_like(q)
        qs = jnp.concatenate([jnp.where(head0, q, zero), jnp.where(head0, zero, q)],
                             axis=0).astype(BF16)
        return lax.dot_general(qs, kcat, (((1,), (1,)), ((), ())),
                               preferred_element_type=F32) + bias

    def weighted_values(s, vcat):
        nk = vcat.shape[0]
        m = jnp.max(s, axis=-1, keepdims=True)
        p = jnp.exp2(s - m).astype(BF16)
        pv = jnp.dot(p, jnp.concatenate([vcat, ones[:nk]], axis=1), preferred_element_type=F32)
        return (merge_heads(jnp.broadcast_to(m, (2 * blk, LANES))),
                merge_heads(pv[:, LANES:]), merge_heads(pv[:, :LANES]))

    def update(ci, slices, m_c, l_c, o_c):
        if ci > 0:
            m_r = load(m_sc, slices)
            m_n = jnp.maximum(m_r, m_c)
            a = jnp.exp2(m_r - m_n)
            b = jnp.exp2(m_c - m_n)
            l_c = a * load(l_sc, slices) + b * l_c
            o_c = a * load(acc_sc, slices) + b * o_c
            m_c = m_n
        store(m_sc, slices, m_c)
        store(l_sc, slices, l_c)
        store(acc_sc, slices, o_c)

    blocks = []
    for ci, (window, d) in enumerate(DILATED_CONFIGS):
        sub = res // d
        ln = blk // sub
        for r in range(d):
            for n in range(SEQ // d // blk):
                slices = [slice(per * (r + d * c) + ln * n, per * (r + d * c) + ln * (n + 1))
                          for c in range(sub)]
                blocks.append((ci, slices, n == 0))

    kv_prev = [None, None]

    def score_stage(ci, slices, first):
        k_cur = load(kp, slices).astype(BF16)
        v_cur = load(vp, slices).astype(BF16)
        if first:
            bias = jnp.concatenate([bias_ref[ci, 0, :, blk:], bias_ref[ci, 1, :, blk:]], axis=0)
            out = scores(load(qp, slices), k_cur, bias), v_cur
        else:
            bias = jnp.concatenate([bias_ref[ci, 0], bias_ref[ci, 1]], axis=0)
            out = (scores(load(qp, slices), jnp.concatenate([kv_prev[0], k_cur], axis=0), bias),
                   jnp.concatenate([kv_prev[1], v_cur], axis=0))
        kv_prev[0], kv_prev[1] = k_cur, v_cur
        return out

    ahead = score_stage(*blocks[0])
    for i, (ci, slices, _) in enumerate(blocks):
        s, vcat = ahead
        if i + 1 < len(blocks):
            ahead = score_stage(*blocks[i + 1])
        update(ci, slices, *weighted_values(s, vcat))

    for g in range(SEQ // sq):
        o = load(acc_sc, residue_rows(g)) / load(l_sc, residue_rows(g))
        hi = o.astype(BF16)
        lo = (o - hi.astype(F32)).astype(BF16)
        back = jnp.dot(regroup, jnp.concatenate([hi, lo], axis=1), preferred_element_type=F32)
        o_ref[sq * g:sq * (g + 1), :] = back[:, :LANES] + back[:, LANES:]


def _attention(q, k, v, bias_tab):
    n_cfg = len(DILATED_CONFIGS)
    blk = ATTN_BLOCK
    seq_spec = pl.BlockSpec((SEQ, HEAD_PAIR), lambda b, p: (b, p))
    return pl.pallas_call(
        _attn_kernel,
        grid=(BATCH, N_PAIRS),
        in_specs=[seq_spec, seq_spec, seq_spec,
                  pl.BlockSpec((n_cfg, 2, blk, 2 * blk), lambda b, p: (0, p, 0, 0))],
        out_specs=seq_spec,
        out_shape=jax.ShapeDtypeStruct((TOKENS, B_WIDTH), F32),
        scratch_shapes=[pltpu.VMEM((SEQ, HEAD_PAIR), F32)] * 6,
        compiler_params=_cparams(("arbitrary", "arbitrary")),
        name="attention",
    )(q, k, v, bias_tab)


K4_TM = 512


def _top2_sum(a, b, c, d):
    hi1, lo1 = jnp.maximum(a, b), jnp.minimum(a, b)
    hi2, lo2 = jnp.maximum(c, d), jnp.minimum(c, d)
    return jnp.maximum(hi1, hi2) + jnp.maximum(jnp.minimum(hi1, hi2), jnp.maximum(lo1, lo2))


def _route(logits_t, rb_col):
    m = jnp.max(logits_t, axis=0, keepdims=True)
    e = jnp.exp(logits_t - m)
    probs = e / jnp.sum(e, axis=0, keepdims=True)
    sel = probs + rb_col
    sel_rows = [sel[i:i + 1, :] for i in range(N_EXPERTS)]
    prob_rows = [probs[i:i + 1, :] for i in range(N_EXPERTS)]
    gsz = EXPERTS_PER_GROUP
    score = [_top2_sum(*sel_rows[g * gsz:(g + 1) * gsz]) for g in range(N_EXPERT_GROUPS)]
    chosen = []
    for g in range(N_EXPERT_GROUPS):
        best = None
        for g2 in range(N_EXPERT_GROUPS):
            if g2 == g:
                continue
            c = (score[g] > score[g2]) if g2 < g else (score[g] >= score[g2])
            best = c if best is None else jnp.logical_and(best, c)
        for i in range(gsz):
            ei = g * gsz + i
            rank = jnp.zeros_like(sel_rows[ei])
            for j in range(gsz):
                if j == i:
                    continue
                ej = g * gsz + j
                ahead = (sel_rows[ej] >= sel_rows[ei]) if j < i else (sel_rows[ej] > sel_rows[ei])
                rank = rank + jnp.where(ahead, 1.0, 0.0)
            chosen.append(jnp.logical_and(best, rank < float(2)))
    picked = [jnp.where(chosen[i], prob_rows[i], 0.0) for i in range(N_EXPERTS)]
    denom = picked[0]
    for i in range(1, N_EXPERTS):
        denom = denom + picked[i]
    gates = jnp.concatenate([pk / denom for pk in picked], axis=0)
    onehot = jnp.concatenate([jnp.where(ch, 1.0, 0.0) for ch in chosen], axis=0)
    return gates, onehot


def _mid_kernel(a_ref, ob_ref, x_ref, mod_ref, gb_ref, wout_ref, n2g_ref, rwt_ref, rb_ref,
                x1_ref, h2_ref, gates_ref, onehot_ref, count_ref):
    ob = ob_ref[...]
    bn = ob * lax.rsqrt(jnp.mean(ob * ob, axis=-1, keepdims=True) + EPS) * gb_ref[...]
    mixed = jnp.dot(a_ref[...], wout_ref[:A_WIDTH, :], preferred_element_type=F32)
    mixed = mixed + jnp.dot(bn.astype(BF16), wout_ref[A_WIDTH:, :], preferred_element_type=F32)
    x1 = x_ref[...] + mod_ref[2:3, :] * mixed
    x1_ref[...] = x1
    h2 = x1 * lax.rsqrt(jnp.mean(x1 * x1, axis=-1, keepdims=True) + EPS) * n2g_ref[...]
    h2 = (h2 * (1.0 + mod_ref[4:5, :]) + mod_ref[3:4, :]).astype(BF16)
    h2_ref[...] = h2
    logits_t = lax.dot_general(rwt_ref[...], h2, (((1,), (1,)), ((), ())),
                               preferred_element_type=F32)
    gates_t, onehot_t = _route(logits_t, rb_ref[...])
    gates_ref[...] = gates_t
    onehot_ref[...] = onehot_t
    for s in range(count_ref.shape[0]):
        count_ref[s] = jnp.sum(onehot_t[:, s * MOE_TOK:(s + 1) * MOE_TOK], axis=1, keepdims=True)


def _mid(out_a, out_b, x, mod_l, gb, wout, n2g, rwt, rb_col):
    tm = K4_TM
    tiles_per_seq = SEQ // tm
    tok = pl.BlockSpec((tm, D_MODEL), lambda i: (i, 0))
    half = pl.BlockSpec((tm, A_WIDTH), lambda i: (i, 0))
    route = pl.BlockSpec((N_EXPERTS, tm), lambda i: (0, i))
    return pl.pallas_call(
        _mid_kernel,
        grid=(TOKENS // tm,),
        in_specs=[
            half, half, tok,
            pl.BlockSpec((None, N_MOD, D_MODEL), lambda i: (i // tiles_per_seq, 0, 0)),
            pl.BlockSpec((1, B_WIDTH), lambda i: (0, 0)),
            pl.BlockSpec((D_MODEL, D_MODEL), lambda i: (0, 0)),
            pl.BlockSpec((1, D_MODEL), lambda i: (0, 0)),
            pl.BlockSpec((N_EXPERTS, D_MODEL), lambda i: (0, 0)),
            pl.BlockSpec((N_EXPERTS, 1), lambda i: (0, 0)),
        ],
        out_specs=[tok, tok, route, route,
                   pl.BlockSpec((tm // MOE_TOK, N_EXPERTS, 1), lambda i: (i, 0, 0))],
        out_shape=[jax.ShapeDtypeStruct((TOKENS, D_MODEL), F32),
                   jax.ShapeDtypeStruct((TOKENS, D_MODEL), BF16),
                   jax.ShapeDtypeStruct((N_EXPERTS, TOKENS), F32),
                   jax.ShapeDtypeStruct((N_EXPERTS, TOKENS), F32),
                   jax.ShapeDtypeStruct((TOKENS // MOE_TOK, N_EXPERTS, 1), F32)],
        compiler_params=_cparams(("arbitrary",)),
        name="mid",
    )(out_a, out_b, x, mod_l, gb, wout, n2g, rwt, rb_col)


MOE_TOK = 256
ROW_ALIGN = 16
MOE_SUB = 4
MOE_LB = 2 * MOE_TOK + N_EXPERTS * ROW_ALIGN
FFN_TM = 1024
N_TOK_TILES = TOKENS // MOE_TOK
ROWS_MAX = -(-(2 * TOKENS + N_TOK_TILES * N_EXPERTS * ROW_ALIGN
               + N_EXPERTS * (FFN_TM - ROW_ALIGN)) // FFN_TM) * FFN_TM


def _moe_plan(count):
    n = count.reshape(N_TOK_TILES, N_EXPERTS).astype(jnp.int32)
    n8 = jnp.maximum((n + (ROW_ALIGN - 1)) // ROW_ALIGN, 1) * ROW_ALIGN
    local = jnp.cumsum(n8, axis=1) - n8
    total = jnp.sum(n8, axis=0)
    seg = (total + (FFN_TM - 1)) // FFN_TM * FFN_TM
    seg_end = jnp.cumsum(seg)
    glob = (seg_end - seg)[None, :] + jnp.cumsum(n8, axis=0) - n8
    n_ffn = ROWS_MAX // FFN_TM
    n_active = seg_end[-1] // FFN_TM
    tile_row = jnp.arange(n_ffn, dtype=jnp.int32) * FFN_TM
    texp = jnp.sum((tile_row[:, None] >= seg_end[None, :]).astype(jnp.int32), axis=1)
    texp = jnp.minimum(texp, N_EXPERTS - 1)
    texp = jnp.where(jnp.arange(n_ffn) < n_active, texp, texp[n_active - 1])
    return dict(n8=n8.reshape(-1), local=local.reshape(-1), glob=glob.reshape(-1),
                local_col=local.astype(F32).reshape(N_TOK_TILES, N_EXPERTS, 1),
                texp=texp.astype(jnp.int32), n_active=n_active.reshape(1).astype(jnp.int32))


def _run_copies(tile, n8_ref, local_ref, glob_ref, make_copy, start):
    for e in range(N_EXPERTS):
        idx = tile * N_EXPERTS + e
        cp = make_copy(pl.multiple_of(local_ref[idx], ROW_ALIGN),
                       pl.multiple_of(glob_ref[idx], ROW_ALIGN),
                       pl.multiple_of(n8_ref[idx], ROW_ALIGN))
        if start:
            cp.start()
        else:
            cp.wait()


def _fill_before(before_sc):
    @pl.when(pl.program_id(0) == 0)
    def _():
        src = lax.broadcasted_iota(jnp.int32, before_sc.shape, 0)
        dst = lax.broadcasted_iota(jnp.int32, before_sc.shape, 1)
        before_sc[...] = jnp.where(src < dst, 1.0, 0.0).astype(BF16)


def _sorted_rows(onehot, local_col, before):
    rank = jnp.dot(onehot.astype(BF16), before, preferred_element_type=F32)
    pos = local_col + rank
    chosen = onehot > 0.5
    lo = jnp.min(jnp.where(chosen, pos, float(MOE_LB)), axis=0, keepdims=True)
    hi = jnp.max(jnp.where(chosen, pos, -1.0), axis=0, keepdims=True)
    return pos, chosen, lo, hi


def _dispatch_kernel(n8_ref, local_ref, glob_ref, h_ref, oh_ref, lcol_ref, xs_ref,
                     lbuf, before_sc, sem):
    j = pl.program_id(0)
    last = pl.num_programs(0) - 1
    par = j % 2
    _fill_before(before_sc)
    row = lax.broadcasted_iota(jnp.int32, (MOE_LB, MOE_TOK), 0).astype(F32)
    for u in range(MOE_SUB):
        tok = slice(u * MOE_TOK, (u + 1) * MOE_TOK)
        _, _, lo, hi = _sorted_rows(oh_ref[:, tok], lcol_ref[u], before_sc[...])
        perm = (jnp.where(row == lo, 1.0, 0.0) + jnp.where(row == hi, 1.0, 0.0)).astype(BF16)
        lbuf[par * MOE_SUB + u] = jnp.dot(perm, h_ref[tok, :],
                                          preferred_element_type=F32).astype(BF16)

    def copy_from(slot):
        def make(lo_, go_, size):
            return pltpu.make_async_copy(lbuf.at[slot, pl.ds(lo_, size), :],
                                         xs_ref.at[pl.ds(go_, size), :], sem.at[slot])
        return make

    refs = (n8_ref, local_ref, glob_ref)
    for u in range(MOE_SUB):
        _run_copies(j * MOE_SUB + u, *refs, copy_from(par * MOE_SUB + u), True)

    @pl.when(j > 0)
    def _():
        for u in range(MOE_SUB):
            _run_copies((j - 1) * MOE_SUB + u, *refs, copy_from((1 - par) * MOE_SUB + u), False)

    @pl.when(j == last)
    def _():
        for u in range(MOE_SUB):
            _run_copies(j * MOE_SUB + u, *refs, copy_from(par * MOE_SUB + u), False)


def _dispatch(plan, h2, onehot):
    step_tok = MOE_SUB * MOE_TOK
    grid_spec = pltpu.PrefetchScalarGridSpec(
        num_scalar_prefetch=3,
        grid=(N_TOK_TILES // MOE_SUB,),
        in_specs=[
            pl.BlockSpec((step_tok, D_MODEL), lambda j, *_: (j, 0)),
            pl.BlockSpec((N_EXPERTS, step_tok), lambda j, *_: (0, j)),
            pl.BlockSpec((MOE_SUB, N_EXPERTS, 1), lambda j, *_: (j, 0, 0)),
        ],
        out_specs=pl.BlockSpec(memory_space=pl.ANY),
        scratch_shapes=[pltpu.VMEM((2 * MOE_SUB, MOE_LB, D_MODEL), BF16),
                        pltpu.VMEM((MOE_TOK, MOE_TOK), BF16),
                        pltpu.SemaphoreType.DMA((2 * MOE_SUB,))],
    )
    return pl.pallas_call(
        _dispatch_kernel,
        grid_spec=grid_spec,
        out_shape=jax.ShapeDtypeStruct((ROWS_MAX, D_MODEL), BF16),
        compiler_params=_cparams(("arbitrary",)),
        name="dispatch",
    )(plan["n8"], plan["local"], plan["glob"], h2, onehot, plan["local_col"])


def _experts_kernel(texp_ref, nact_ref, xs_ref, wg_ref, wu_ref, wd_ref, ys_ref, wg_b, wu_b, wd_b):
    i = pl.program_id(0)
    active = i < nact_ref[0]
    new_expert = jnp.logical_or(i == 0, texp_ref[i] != texp_ref[jnp.maximum(i - 1, 0)])

    @pl.when(jnp.logical_and(active, new_expert))
    def _():
        wg_b[...] = wg_ref[...].astype(BF16)
        wu_b[...] = wu_ref[...].astype(BF16)
        wd_b[...] = wd_ref[...].astype(BF16)

    @pl.when(active)
    def _():
        x = xs_ref[...]
        hg = jnp.dot(x, wg_b[...], preferred_element_type=F32)
        hu = jnp.dot(x, wu_b[...], preferred_element_type=F32)
        act = (jax.nn.silu(hg) * hu).astype(BF16)
        ys_ref[...] = jnp.dot(act, wd_b[...], preferred_element_type=F32).astype(BF16)


def _experts(plan, xs, layer, wg, wu, wd):
    def rows(i, texp, nact):
        return (jnp.minimum(i, nact[0] - 1), 0)

    def expert(i, texp, nact):
        return (layer, texp[i], 0, 0)

    grid_spec = pltpu.PrefetchScalarGridSpec(
        num_scalar_prefetch=2,
        grid=(ROWS_MAX // FFN_TM,),
        in_specs=[
            pl.BlockSpec((FFN_TM, D_MODEL), rows),
            pl.BlockSpec((None, None, D_MODEL, D_EXPERT), expert),
            pl.BlockSpec((None, None, D_MODEL, D_EXPERT), expert),
            pl.BlockSpec((None, None, D_EXPERT, D_MODEL), expert),
        ],
        out_specs=pl.BlockSpec((FFN_TM, D_MODEL), rows),
        scratch_shapes=[pltpu.VMEM((D_MODEL, D_EXPERT), BF16), pltpu.VMEM((D_MODEL, D_EXPERT), BF16),
                        pltpu.VMEM((D_EXPERT, D_MODEL), BF16)],
    )
    return pl.pallas_call(
        _experts_kernel,
        grid_spec=grid_spec,
        out_shape=jax.ShapeDtypeStruct((ROWS_MAX, D_MODEL), BF16),
        compiler_params=_cparams(("arbitrary",)),
        name="experts",
    )(plan["texp"], plan["n_active"], xs, wg, wu, wd)


def _combine_kernel(n8_ref, local_ref, glob_ref, oh_ref, g_ref, lcol_ref, x_ref, mod_ref, fg_ref,
                    ys_ref, o_ref, ybuf, before_sc, sem, *, final_norm):
    j = pl.program_id(0)
    n_steps = pl.num_programs(0)
    par = j % 2
    refs = (n8_ref, local_ref, glob_ref)
    _fill_before(before_sc)

    def copy_to(slot):
        def make(lo_, go_, size):
            return pltpu.make_async_copy(ys_ref.at[pl.ds(go_, size), :],
                                         ybuf.at[slot, pl.ds(lo_, size), :], sem.at[slot])
        return make

    @pl.when(j == 0)
    def _():
        ybuf[...] = jnp.zeros(ybuf.shape, BF16)
        for u in range(MOE_SUB):
            _run_copies(u, *refs, copy_to(u), True)

    @pl.when(j + 1 < n_steps)
    def _():
        for u in range(MOE_SUB):
            _run_copies((j + 1) * MOE_SUB + u, *refs, copy_to((1 - par) * MOE_SUB + u), True)

    row = lax.broadcasted_iota(jnp.int32, (MOE_LB, MOE_TOK), 0).astype(F32)
    for u in range(MOE_SUB):
        tok = slice(u * MOE_TOK, (u + 1) * MOE_TOK)
        pos, chosen, lo, hi = _sorted_rows(oh_ref[:, tok], lcol_ref[u], before_sc[...])
        gate = jnp.where(chosen, g_ref[:, tok], 0.0)
        g_lo = jnp.sum(jnp.where(pos == lo, gate, 0.0), axis=0, keepdims=True)
        g_hi = jnp.sum(jnp.where(pos == hi, gate, 0.0), axis=0, keepdims=True)
        perm = (jnp.where(row == lo, g_lo, 0.0) + jnp.where(row == hi, g_hi, 0.0)).astype(BF16)
        slot = par * MOE_SUB + u
        _run_copies(j * MOE_SUB + u, *refs, copy_to(slot), False)
        y = lax.dot_general(perm, ybuf[slot], (((0,), (0,)), ((), ())),
                            preferred_element_type=F32)
        x = x_ref[tok, :] + mod_ref[5:6, :] * y
        if final_norm:
            x = x * lax.rsqrt(jnp.mean(x * x, axis=-1, keepdims=True) + EPS) * fg_ref[...]
        o_ref[tok, :] = x


N_FRONT_IN = 8


def _combine_front_kernel(*refs):
    n_in = 3 + 7
    comb_in, front_in = refs[:n_in], refs[n_in:n_in + N_FRONT_IN]
    o_ref, *front_out = refs[n_in + N_FRONT_IN:n_in + N_FRONT_IN + 5]
    scratch = refs[n_in + N_FRONT_IN + 5:]
    _combine_kernel(*comb_in, o_ref, *scratch, final_norm=False)
    _front_kernel(o_ref, *front_in, *front_out)


def _combine(plan, onehot, gates, ys, x, mod_l, final_g, front_args=None):
    step_tok = MOE_SUB * MOE_TOK
    steps_per_seq = SEQ // step_tok
    route = pl.BlockSpec((N_EXPERTS, step_tok), lambda j, *_: (0, j))
    tok = pl.BlockSpec((step_tok, D_MODEL), lambda j, *_: (j, 0))
    modspec = pl.BlockSpec((None, N_MOD, D_MODEL), lambda j, *_: (j // steps_per_seq, 0, 0))
    in_specs = [
        route, route,
        pl.BlockSpec((MOE_SUB, N_EXPERTS, 1), lambda j, *_: (j, 0, 0)),
        tok, modspec,
        pl.BlockSpec((1, D_MODEL), lambda j, *_: (0, 0)),
        pl.BlockSpec(memory_space=pl.ANY),
    ]
    args = [onehot, gates, plan["local_col"], x, mod_l, final_g, ys]
    out_specs = tok
    out_shape = jax.ShapeDtypeStruct((TOKENS, D_MODEL), F32)
    body = functools.partial(_combine_kernel, final_norm=True)
    if front_args is not None:
        assert step_tok == K1_TM and len(front_args) == N_FRONT_IN
        row1024 = pl.BlockSpec((1, D_MODEL), lambda j, *_: (0, 0))
        row512 = pl.BlockSpec((1, A_WIDTH), lambda j, *_: (0, 0))
        half = pl.BlockSpec((step_tok, A_WIDTH), lambda j, *_: (j, 0))
        in_specs += [
            modspec, row1024,
            pl.BlockSpec((D_MODEL, IN_COLS), lambda j, *_: (0, 0)),
            row512, row512,
            pl.BlockSpec((A_GROUPS // 2, 2 * CHUNK, CHUNK), lambda j, *_: (0, 0, 0)),
            pl.BlockSpec((CHUNK, A_WIDTH), lambda j, *_: (0, 0)),
            row512,
        ]
        args += list(front_args)
        out_specs = [tok, half, half, half, half]
        out_shape = [out_shape, jax.ShapeDtypeStruct((TOKENS, A_WIDTH), BF16)] + \
                    [jax.ShapeDtypeStruct((TOKENS, B_WIDTH), F32)] * 3
        body = _combine_front_kernel
    grid_spec = pltpu.PrefetchScalarGridSpec(
        num_scalar_prefetch=3,
        grid=(N_TOK_TILES // MOE_SUB,),
        in_specs=in_specs,
        out_specs=out_specs,
        scratch_shapes=[pltpu.VMEM((2 * MOE_SUB, MOE_LB, D_MODEL), BF16),
                        pltpu.VMEM((MOE_TOK, MOE_TOK), BF16),
                        pltpu.SemaphoreType.DMA((2 * MOE_SUB,))],
    )
    return pl.pallas_call(
        body,
        grid_spec=grid_spec,
        out_shape=out_shape,
        compiler_params=_cparams(("arbitrary",)),
        name="combine",
    )(plan["n8"], plan["local"], plan["glob"], *args)


def kernel(x, c, rel_bias, router_w, router_b, mod_w, mod_b, norm1_g, w_in, gmlp_ln_g, gmlp_ln_b,
           gmlp_ws, gmlp_bs, out_norm_a_g, out_norm_b_g, w_out, norm2_g, moe_w_gate, moe_w_up,
           moe_w_down, final_g):
    mod = _modulation(c, mod_w, mod_b).reshape(DEPTH, BATCH, N_MOD, D_MODEL)
    bias_tab = _bias_tables(rel_bias)
    rwt = router_w.T.astype(BF16)
    rb_col = router_b.reshape(N_EXPERTS, 1)
    xt = x.reshape(TOKENS, D_MODEL)
    win_b = w_in.astype(BF16)
    wout_b = w_out.astype(BF16)

    def front_args(l):
        ws2 = gmlp_ws[l].astype(BF16).reshape(A_GROUPS // 2, 2 * CHUNK, CHUNK)
        bs_exp = jnp.repeat(gmlp_bs[l].T, HEAD_DIM, axis=1)
        return (mod[l], norm1_g[l].reshape(1, -1), win_b[l], gmlp_ln_g[l].reshape(1, -1),
                gmlp_ln_b[l].reshape(1, -1), ws2, bs_exp, out_norm_a_g[l].reshape(1, -1))

    out_a, q, k, v = _front(xt, *front_args(0))
    for l in range(DEPTH):
        out_b = _attention(q, k, v, bias_tab)
        xt, h2, gates, onehot, count = _mid(
            out_a, out_b, xt, mod[l], out_norm_b_g[l].reshape(1, -1),
            wout_b[l], norm2_g[l].reshape(1, -1), rwt, rb_col)
        plan = _moe_plan(count)
        xs = _dispatch(plan, h2, onehot)
        ys = _experts(plan, xs, l, moe_w_gate, moe_w_up, moe_w_down)
        if l + 1 < DEPTH:
            xt, out_a, q, k, v = _combine(plan, onehot, gates, ys, xt, mod[l],
                                          final_g.reshape(1, -1), front_args(l + 1))
        else:
            xt = _combine(plan, onehot, gates, ys, xt, mod[l], final_g.reshape(1, -1))
    return xt.reshape(BATCH, SEQ, D_MODEL)
```

```python
import functools
import math

import numpy as np
import jax
import jax.numpy as jnp
from jax import lax
from jax.experimental import pallas as pl
from jax.experimental.pallas import tpu as pltpu

D_MODEL = 1024
BATCH = 8
SEQ = 2048
DEPTH = 2
TOKENS = BATCH * SEQ
HEAD_DIM = 64
A_WIDTH = 512
B_WIDTH = 512
A_GROUPS = 8
IN_COLS = 2 * A_WIDTH + 3 * B_WIDTH
CHUNK = 128
DILATED_CONFIGS = ((128, 1), (512, 4), (2048, 16))
ATTN_BLOCK = 128
REL_BUCKETS = 32
REL_MAX_EXACT = REL_BUCKETS // 2
REL_MAX_DISTANCE = 2048
N_EXPERTS = 16
N_EXPERT_GROUPS = 4
EXPERTS_PER_GROUP = 4
D_EXPERT = 512
N_MOD = 6
EPS = 1e-6
NEG_INF = -1e30

LANES = 128
HEAD_PAIR = 2 * HEAD_DIM
N_PAIRS = B_WIDTH // HEAD_PAIR
RESIDUES = max(d for _, d in DILATED_CONFIGS)
LOG2E = math.log2(math.e)

F32 = jnp.float32
BF16 = jnp.bfloat16

VMEM_LIMIT = 56 * 1024 * 1024


def _cparams(sem):
    return pltpu.CompilerParams(dimension_semantics=sem, vmem_limit_bytes=VMEM_LIMIT)


def _gelu(x):
    return 0.5 * x * (1.0 + lax.erf(x * math.sqrt(0.5)))


MOD_TN = 1024


def _mod_kernel(c_ref, w_ref, b_ref, o_ref):
    ca = jax.nn.silu(c_ref[...])
    o_ref[...] = jnp.dot(ca.astype(BF16), w_ref[...].astype(BF16),
                         preferred_element_type=F32) + b_ref[...]


def _modulation(c, mod_w, mod_b):
    n_cols = N_MOD * D_MODEL
    return pl.pallas_call(
        _mod_kernel,
        grid=(DEPTH, n_cols // MOD_TN),
        in_specs=[
            pl.BlockSpec((BATCH, D_MODEL), lambda l, j: (0, 0)),
            pl.BlockSpec((None, D_MODEL, MOD_TN), lambda l, j: (l, 0, j)),
            pl.BlockSpec((None, 1, MOD_TN), lambda l, j: (l, 0, j)),
        ],
        out_specs=pl.BlockSpec((None, BATCH, MOD_TN), lambda l, j: (l, 0, j)),
        out_shape=jax.ShapeDtypeStruct((DEPTH, BATCH, n_cols), F32),
        compiler_params=_cparams(("arbitrary", "arbitrary")),
        name="modulation",
    )(c, mod_w, mod_b.reshape(DEPTH, 1, n_cols))


K1_TM = 1024


def _front_kernel(x_ref, mod_ref, n1g_ref, win_ref, lng_ref, lnb_ref, ws_ref, bs_ref, ga_ref,
                  a_ref, q_ref, k_ref, v_ref):
    x = x_ref[...]
    tm = x.shape[0]
    h = x * lax.rsqrt(jnp.mean(x * x, axis=-1, keepdims=True) + EPS) * n1g_ref[...]
    h = h * (1.0 + mod_ref[1:2, :]) + mod_ref[0:1, :]
    proj = jnp.dot(h.astype(BF16), win_ref[...], preferred_element_type=F32)

    q_ref[...] = proj[:, 2 * A_WIDTH:2 * A_WIDTH + B_WIDTH] * (HEAD_DIM ** -0.5 * LOG2E)
    k_ref[...] = proj[:, 2 * A_WIDTH + B_WIDTH:2 * A_WIDTH + 2 * B_WIDTH]
    v_ref[...] = proj[:, 2 * A_WIDTH + 2 * B_WIDTH:]

    u = _gelu(proj[:, :A_WIDTH])
    va = _gelu(proj[:, A_WIDTH:2 * A_WIDTH])
    mu = jnp.mean(va, axis=-1, keepdims=True)
    vc = va - mu
    vln = vc * lax.rsqrt(jnp.mean(vc * vc, axis=-1, keepdims=True) + EPS)
    vln = (vln * lng_ref[...] + lnb_ref[...]).astype(BF16)

    row = lax.broadcasted_iota(jnp.int32, (2 * CHUNK, CHUNK), 0)
    col = lax.broadcasted_iota(jnp.int32, (2 * CHUNK, CHUNK), 1)
    causal = (row % CHUNK) >= col
    first_group = lax.broadcasted_iota(jnp.int32, (CHUNK, LANES), 1) < HEAD_DIM
    wmix = [jnp.where(causal, ws_ref[p], jnp.zeros((), BF16)) for p in range(A_GROUPS // 2)]
    for c in range(tm // CHUNK):
        rows = slice(c * CHUNK, (c + 1) * CHUNK)
        parts = []
        for p in range(A_GROUPS // 2):
            vp = vln[rows, p * LANES:(p + 1) * LANES]
            r = jnp.dot(wmix[p], vp, preferred_element_type=F32)
            parts.append(jnp.where(first_group, r[:CHUNK], r[CHUNK:]))
        s = jnp.concatenate(parts, axis=-1) + bs_ref[...]
        oa = u[rows] * s
        oa = oa * lax.rsqrt(jnp.mean(oa * oa, axis=-1, keepdims=True) + EPS) * ga_ref[...]
        a_ref[rows, :] = oa.astype(BF16)


def _front(x, layer, mod_l, n1g, win, lng, lnb, ws2, bs_exp, ga):
    tm = K1_TM
    tiles_per_seq = SEQ // tm
    row1024 = pl.BlockSpec((1, D_MODEL), lambda i: (0, 0))
    row512 = pl.BlockSpec((1, A_WIDTH), lambda i: (0, 0))
    half = pl.BlockSpec((tm, A_WIDTH), lambda i: (i, 0))
    return pl.pallas_call(
        _front_kernel,
        grid=(TOKENS // tm,),
        in_specs=[
            pl.BlockSpec((tm, D_MODEL), lambda i: (i, 0)),
            pl.BlockSpec((None, N_MOD, D_MODEL), lambda i: (i // tiles_per_seq, 0, 0)),
            row1024,
            pl.BlockSpec((None, D_MODEL, IN_COLS), lambda i: (layer, 0, 0)),
            row512, row512,
            pl.BlockSpec((A_GROUPS // 2, 2 * CHUNK, CHUNK), lambda i: (0, 0, 0)),
            pl.BlockSpec((CHUNK, A_WIDTH), lambda i: (0, 0)),
            row512,
        ],
        out_specs=[half, half, half, half],
        out_shape=[jax.ShapeDtypeStruct((TOKENS, A_WIDTH), BF16)] +
                  [jax.ShapeDtypeStruct((TOKENS, B_WIDTH), F32)] * 3,
        compiler_params=_cparams(("arbitrary",)),
        name="front",
    )(x, mod_l, n1g, win, lng, lnb, ws2, bs_exp, ga)


def _t5_bucket_np(dist):
    dist = np.maximum(dist, 0)
    ratio = np.log(np.maximum(dist, 1) / REL_MAX_EXACT) / np.log(REL_MAX_DISTANCE / REL_MAX_EXACT)
    large = REL_MAX_EXACT + np.floor(ratio * (REL_BUCKETS - REL_MAX_EXACT)).astype(np.int64)
    large = np.minimum(large, REL_BUCKETS - 1)
    return np.where(dist < REL_MAX_EXACT, dist, large).astype(np.int32)


def _bias_tables(rel_bias):
    blk = ATTN_BLOCK
    n_rel = 3 * blk
    rel = 2 * blk - 1 - np.arange(n_rel)
    tables = []
    for window, d in DILATED_CONFIGS:
        span = window // d
        bucket = _t5_bucket_np(np.clip(rel, 0, span) * d)
        valid = jnp.asarray((rel >= 0) & (rel <= span))
        w = jnp.where(valid[None, :], rel_bias.astype(F32)[bucket].T, NEG_INF)
        flat = jnp.tile(w, (1, blk))
        skew = flat[:, blk - 1:blk - 1 + blk * (n_rel - 1)].reshape(-1, blk, n_rel - 1)
        tab = skew[:, :, :2 * blk]
        sub = RESIDUES // d
        ln = blk // sub
        pos = np.arange(blk)
        rows = np.eye(blk, dtype=np.float32)[(pos % ln) * sub + pos // ln]
        cols = np.kron(np.eye(2, dtype=np.float32), rows)
        tab = jnp.einsum('pi,hik,qk->hpq', rows, tab * LOG2E, cols,
                         precision=lax.Precision.HIGHEST)
        tables.append(tab)
    return jnp.stack(tables, axis=0)


def _attn_kernel(q_ref, k_ref, v_ref, bias_ref, o_ref, qp, kp, vp, m_sc, l_sc, acc_sc):
    blk = ATTN_BLOCK
    res = RESIDUES
    per = SEQ // res
    sq = res * res
    seg = sq // res
    lane = lax.broadcasted_iota(jnp.int32, (blk, LANES), 1)
    head0 = lane < HEAD_DIM
    ones = jnp.ones((2 * blk, LANES), BF16)

    pa = lax.broadcasted_iota(jnp.int32, (sq, sq), 0)
    pb = lax.broadcasted_iota(jnp.int32, (sq, sq), 1)
    regroup = jnp.where(pb == res * (pa % res) + pa // res, 1.0, 0.0).astype(BF16)

    def residue_rows(g):
        return [slice(per * r + seg * g, per * r + seg * (g + 1)) for r in range(res)]

    def load(ref, slices):
        return jnp.concatenate([ref[s, :] for s in slices], axis=0) if len(slices) > 1 \
            else ref[slices[0], :]

    def store(ref, slices, val):
        ln = val.shape[0] // len(slices)
        for i, s in enumerate(slices):
            ref[s, :] = val[i * ln:(i + 1) * ln]

    for g in range(SEQ // sq):
        rows = slice(sq * g, sq * (g + 1))
        qkv = jnp.concatenate([q_ref[rows, :], k_ref[rows, :], v_ref[rows, :]], axis=1)
        qkv = jnp.dot(regroup, qkv.astype(BF16), preferred_element_type=F32)
        store(qp, residue_rows(g), qkv[:, :LANES])
        store(kp, residue_rows(g), qkv[:, LANES:2 * LANES])
        store(vp, residue_rows(g), qkv[:, 2 * LANES:])

    def merge_heads(t):
        return jnp.where(head0, t[:blk], t[blk:])

    def scores(q, kcat, bias):
        zero = jnp.zeros_like(q)
        qs = jnp.concatenate([jnp.where(head0, q, zero), jnp.where(head0, zero, q)],
                             axis=0).astype(BF16)
        return lax.dot_general(qs, kcat, (((1,), (1,)), ((), ())),
                               preferred_element_type=F32) + bias

    def weighted_values(s, vcat):
        nk = vcat.shape[0]
        m = jnp.max(s, axis=-1, keepdims=True)
        p = jnp.exp2(s - m).astype(BF16)
        pv = jnp.dot(p, jnp.concatenate([vcat, ones[:nk]], axis=1), preferred_element_type=F32)
        return (merge_heads(jnp.broadcast_to(m, (2 * blk, LANES))),
                merge_heads(pv[:, LANES:]), merge_heads(pv[:, :LANES]))

    def update(ci, slices, m_c, l_c, o_c):
        if ci > 0:
            m_r = load(m_sc, slices)
            m_n = jnp.maximum(m_r, m_c)
            a = jnp.exp2(m_r - m_n)
            b = jnp.exp2(m_c - m_n)
            l_c = a * load(l_sc, slices) + b * l_c
            o_c = a * load(acc_sc, slices) + b * o_c
            m_c = m_n
        store(m_sc, slices, m_c)
        store(l_sc, slices, l_c)
        store(acc_sc, slices, o_c)

    blocks = []
    for ci, (window, d) in enumerate(DILATED_CONFIGS):
        sub = res // d
        ln = blk // sub
        for r in range(d):
            for n in range(SEQ // d // blk):
                slices = [slice(per * (r + d * c) + ln * n, per * (r + d * c) + ln * (n + 1))
                          for c in range(sub)]
                blocks.append((ci, slices, n == 0))

    kv_prev = [None, None]

    def score_stage(ci, slices, first):
        k_cur = load(kp, slices).astype(BF16)
        v_cur = load(vp, slices).astype(BF16)
        if first:
            bias = jnp.concatenate([bias_ref[ci, 0, :, blk:], bias_ref[ci, 1, :, blk:]], axis=0)
            out = scores(load(qp, slices), k_cur, bias), v_cur
        else:
            bias = jnp.concatenate([bias_ref[ci, 0], bias_ref[ci, 1]], axis=0)
            out = (scores(load(qp, slices), jnp.concatenate([kv_prev[0], k_cur], axis=0), bias),
                   jnp.concatenate([kv_prev[1], v_cur], axis=0))
        kv_prev[0], kv_prev[1] = k_cur, v_cur
        return out

    ahead = score_stage(*blocks[0])
    for i, (ci, slices, _) in enumerate(blocks):
        s, vcat = ahead
        if i + 1 < len(blocks):
            ahead = score_stage(*blocks[i + 1])
        update(ci, slices, *weighted_values(s, vcat))

    for g in range(SEQ // sq):
        o = load(acc_sc, residue_rows(g)) / load(l_sc, residue_rows(g))
        hi = o.astype(BF16)
        lo = (o - hi.astype(F32)).astype(BF16)
        back = jnp.dot(regroup, jnp.concatenate([hi, lo], axis=1), preferred_element_type=F32)
        o_ref[sq * g:sq * (g + 1), :] = back[:, :LANES] + back[:, LANES:]


def _attention(q, k, v, bias_tab):
    n_cfg = len(DILATED_CONFIGS)
    blk = ATTN_BLOCK
    seq_spec = pl.BlockSpec((SEQ, HEAD_PAIR), lambda b, p: (b, p))
    return pl.pallas_call(
        _attn_kernel,
        grid=(BATCH, N_PAIRS),
        in_specs=[seq_spec, seq_spec, seq_spec,
                  pl.BlockSpec((n_cfg, 2, blk, 2 * blk), lambda b, p: (0, p, 0, 0))],
        out_specs=seq_spec,
        out_shape=jax.ShapeDtypeStruct((TOKENS, B_WIDTH), F32),
        scratch_shapes=[pltpu.VMEM((SEQ, HEAD_PAIR), F32)] * 6,
        compiler_params=_cparams(("arbitrary", "arbitrary")),
        name="attention",
    )(q, k, v, bias_tab)


K4_TM = 512


def _top2_sum(a, b, c, d):
    hi1, lo1 = jnp.maximum(a, b), jnp.minimum(a, b)
    hi2, lo2 = jnp.maximum(c, d), jnp.minimum(c, d)
    return jnp.maximum(hi1, hi2) + jnp.maximum(jnp.minimum(hi1, hi2), jnp.maximum(lo1, lo2))


def _route(logits_t, rb_col):
    m = jnp.max(logits_t, axis=0, keepdims=True)
    e = jnp.exp(logits_t - m)
    probs = e / jnp.sum(e, axis=0, keepdims=True)
    sel = probs + rb_col
    sel_rows = [sel[i:i + 1, :] for i in range(N_EXPERTS)]
    prob_rows = [probs[i:i + 1, :] for i in range(N_EXPERTS)]
    gsz = EXPERTS_PER_GROUP
    score = [_top2_sum(*sel_rows[g * gsz:(g + 1) * gsz]) for g in range(N_EXPERT_GROUPS)]
    chosen = []
    for g in range(N_EXPERT_GROUPS):
        best = None
        for g2 in range(N_EXPERT_GROUPS):
            if g2 == g:
                continue
            c = (score[g] > score[g2]) if g2 < g else (score[g] >= score[g2])
            best = c if best is None else jnp.logical_and(best, c)
        for i in range(gsz):
            ei = g * gsz + i
            rank = jnp.zeros_like(sel_rows[ei])
            for j in range(gsz):
                if j == i:
                    continue
                ej = g * gsz + j
                ahead = (sel_rows[ej] >= sel_rows[ei]) if j < i else (sel_rows[ej] > sel_rows[ei])
                rank = rank + jnp.where(ahead, 1.0, 0.0)
            chosen.append(jnp.logical_and(best, rank < float(2)))
    picked = [jnp.where(chosen[i], prob_rows[i], 0.0) for i in range(N_EXPERTS)]
    denom = picked[0]
    for i in range(1, N_EXPERTS):
        denom = denom + picked[i]
    gates = jnp.concatenate([pk / denom for pk in picked], axis=0)
    onehot = jnp.concatenate([jnp.where(ch, 1.0, 0.0) for ch in chosen], axis=0)
    return gates, onehot


def _mid_kernel(a_ref, ob_ref, x_ref, mod_ref, gb_ref, wout_ref, n2g_ref, rwt_ref, rb_ref,
                x1_ref, h2_ref, gates_ref, onehot_ref, count_ref):
    ob = ob_ref[...]
    bn = ob * lax.rsqrt(jnp.mean(ob * ob, axis=-1, keepdims=True) + EPS) * gb_ref[...]
    mixed = jnp.dot(a_ref[...], wout_ref[:A_WIDTH, :], preferred_element_type=F32)
    mixed = mixed + jnp.dot(bn.astype(BF16), wout_ref[A_WIDTH:, :], preferred_element_type=F32)
    x1 = x_ref[...] + mod_ref[2:3, :] * mixed
    x1_ref[...] = x1
    h2 = x1 * lax.rsqrt(jnp.mean(x1 * x1, axis=-1, keepdims=True) + EPS) * n2g_ref[...]
    h2 = (h2 * (1.0 + mod_ref[4:5, :]) + mod_ref[3:4, :]).astype(BF16)
    h2_ref[...] = h2
    logits_t = lax.dot_general(rwt_ref[...], h2, (((1,), (1,)), ((), ())),
                               preferred_element_type=F32)
    gates_t, onehot_t = _route(logits_t, rb_ref[...])
    gates_ref[...] = gates_t
    onehot_ref[...] = onehot_t
    for s in range(count_ref.shape[0]):
        count_ref[s] = jnp.sum(onehot_t[:, s * MOE_TOK:(s + 1) * MOE_TOK], axis=1, keepdims=True)


def _mid(out_a, out_b, x, layer, mod_l, gb, wout, n2g, rwt, rb_col):
    tm = K4_TM
    tiles_per_seq = SEQ // tm
    tok = pl.BlockSpec((tm, D_MODEL), lambda i: (i, 0))
    half = pl.BlockSpec((tm, A_WIDTH), lambda i: (i, 0))
    route = pl.BlockSpec((N_EXPERTS, tm), lambda i: (0, i))
    return pl.pallas_call(
        _mid_kernel,
        grid=(TOKENS // tm,),
        in_specs=[
            half, half, tok,
            pl.BlockSpec((None, N_MOD, D_MODEL), lambda i: (i // tiles_per_seq, 0, 0)),
            pl.BlockSpec((1, B_WIDTH), lambda i: (0, 0)),
            pl.BlockSpec((None, D_MODEL, D_MODEL), lambda i: (layer, 0, 0)),
            pl.BlockSpec((1, D_MODEL), lambda i: (0, 0)),
            pl.BlockSpec((N_EXPERTS, D_MODEL), lambda i: (0, 0)),
            pl.BlockSpec((N_EXPERTS, 1), lambda i: (0, 0)),
        ],
        out_specs=[tok, tok, route, route,
                   pl.BlockSpec((tm // MOE_TOK, N_EXPERTS, 1), lambda i: (i, 0, 0))],
        out_shape=[jax.ShapeDtypeStruct((TOKENS, D_MODEL), F32),
                   jax.ShapeDtypeStruct((TOKENS, D_MODEL), BF16),
                   jax.ShapeDtypeStruct((N_EXPERTS, TOKENS), F32),
                   jax.ShapeDtypeStruct((N_EXPERTS, TOKENS), F32),
                   jax.ShapeDtypeStruct((TOKENS // MOE_TOK, N_EXPERTS, 1), F32)],
        compiler_params=_cparams(("arbitrary",)),
        name="mid",
    )(out_a, out_b, x, mod_l, gb, wout, n2g, rwt, rb_col)


MOE_TOK = 256
ROW_ALIGN = 16
MOE_SUB = 4
MOE_LB = 2 * MOE_TOK + N_EXPERTS * ROW_ALIGN
FFN_TM = 1024
N_TOK_TILES = TOKENS // MOE_TOK
ROWS_MAX = -(-(2 * TOKENS + N_TOK_TILES * N_EXPERTS * ROW_ALIGN
               + N_EXPERTS * (FFN_TM - ROW_ALIGN)) // FFN_TM) * FFN_TM


N_FFN_TILES = ROWS_MAX // FFN_TM


def _plan_kernel(count_ref, n8_ref, local_ref, glob_ref, texp_ref, nact_ref):
    def prefix(a, b):
        return jnp.dot(a, b, preferred_element_type=F32, precision=lax.Precision.HIGHEST)

    row = lax.broadcasted_iota(jnp.int32, (LANES, LANES), 0)
    col = lax.broadcasted_iota(jnp.int32, (LANES, LANES), 1)
    real = jnp.logical_and(row < N_TOK_TILES, col < N_EXPERTS)
    n8 = jnp.where(real, jnp.maximum(jnp.ceil(count_ref[...] / ROW_ALIGN), 1.0) * ROW_ALIGN, 0.0)
    earlier_lane = jnp.where(row < col, 1.0, 0.0)
    earlier_row = jnp.where(col < row, 1.0, 0.0)
    local = prefix(n8, earlier_lane)
    seg = jnp.ceil(jnp.sum(n8, axis=0, keepdims=True) / FFN_TM) * FFN_TM
    seg_start = prefix(jnp.broadcast_to(seg, (LANES, LANES)), earlier_lane)[0:1]
    glob = seg_start + prefix(earlier_row, n8)
    n_active = jnp.sum(seg, axis=1, keepdims=True) / FFN_TM
    tile = row[:, 0:1].astype(F32)
    ended = jnp.logical_and(tile * FFN_TM >= seg_start + seg, col < N_EXPERTS)
    texp = jnp.sum(jnp.where(ended, 1.0, 0.0), axis=1, keepdims=True)
    texp = jnp.minimum(texp, float(N_EXPERTS - 1))
    last = jnp.sum(jnp.where(tile == n_active - 1.0, texp, 0.0), axis=0, keepdims=True)
    texp = jnp.where(tile < n_active, texp, last)
    n8_ref[...] = n8.astype(jnp.int32)
    local_ref[...] = local.astype(jnp.int32)
    glob_ref[...] = glob.astype(jnp.int32)
    texp_ref[...] = jnp.broadcast_to(texp, (LANES, LANES)).astype(jnp.int32)
    nact_ref[...] = jnp.broadcast_to(n_active, (8, LANES)).astype(jnp.int32)


def _moe_plan(count):
    assert max(N_TOK_TILES, N_FFN_TILES, N_EXPERTS) <= LANES
    count = count.reshape(N_TOK_TILES, N_EXPERTS)
    count = jnp.pad(count, ((0, LANES - N_TOK_TILES), (0, LANES - N_EXPERTS)))
    full = jax.ShapeDtypeStruct((LANES, LANES), jnp.int32)
    n8, local, glob, texp, nact = pl.pallas_call(
        _plan_kernel,
        out_shape=[full, full, full, full, jax.ShapeDtypeStruct((8, LANES), jnp.int32)],
        name="plan",
    )(count)

    def runs(t):
        return t[:N_TOK_TILES, :N_EXPERTS]

    return dict(n8=runs(n8).reshape(-1), local=runs(local).reshape(-1),
                glob=runs(glob).reshape(-1),
                local_col=runs(local).astype(F32).reshape(N_TOK_TILES, N_EXPERTS, 1),
                texp=texp[:N_FFN_TILES, 0], n_active=nact[0, :1])


def _run_copies(tile, n8_ref, local_ref, glob_ref, make_copy, start):
    for e in range(N_EXPERTS):
        idx = tile * N_EXPERTS + e
        cp = make_copy(pl.multiple_of(local_ref[idx], ROW_ALIGN),
                       pl.multiple_of(glob_ref[idx], ROW_ALIGN),
                       pl.multiple_of(n8_ref[idx], ROW_ALIGN))
        if start:
            cp.start()
        else:
            cp.wait()


def _fill_before(before_sc):
    @pl.when(pl.program_id(0) == 0)
    def _():
        src = lax.broadcasted_iota(jnp.int32, before_sc.shape, 0)
        dst = lax.broadcasted_iota(jnp.int32, before_sc.shape, 1)
        before_sc[...] = jnp.where(src < dst, 1.0, 0.0).astype(BF16)


def _sorted_rows(onehot, local_col, before):
    rank = jnp.dot(onehot.astype(BF16), before, preferred_element_type=F32)
    pos = local_col + rank
    chosen = onehot > 0.5
    lo = jnp.min(jnp.where(chosen, pos, float(MOE_LB)), axis=0, keepdims=True)
    hi = jnp.max(jnp.where(chosen, pos, -1.0), axis=0, keepdims=True)
    return pos, chosen, lo, hi


def _dispatch_kernel(n8_ref, local_ref, glob_ref, h_ref, oh_ref, lcol_ref, xs_ref,
                     lbuf, before_sc, sem):
    j = pl.program_id(0)
    last = pl.num_programs(0) - 1
    par = j % 2
    _fill_before(before_sc)
    row = lax.broadcasted_iota(jnp.int32, (MOE_LB, MOE_TOK), 0).astype(F32)
    for u in range(MOE_SUB):
        tok = slice(u * MOE_TOK, (u + 1) * MOE_TOK)
        _, _, lo, hi = _sorted_rows(oh_ref[:, tok], lcol_ref[u], before_sc[...])
        perm = (jnp.where(row == lo, 1.0, 0.0) + jnp.where(row == hi, 1.0, 0.0)).astype(BF16)
        lbuf[par * MOE_SUB + u] = jnp.dot(perm, h_ref[tok, :],
                                          preferred_element_type=F32).astype(BF16)

    def copy_from(slot):
        def make(lo_, go_, size):
            return pltpu.make_async_copy(lbuf.at[slot, pl.ds(lo_, size), :],
                                         xs_ref.at[pl.ds(go_, size), :], sem.at[slot])
        return make

    refs = (n8_ref, local_ref, glob_ref)
    for u in range(MOE_SUB):
        _run_copies(j * MOE_SUB + u, *refs, copy_from(par * MOE_SUB + u), True)

    @pl.when(j > 0)
    def _():
        for u in range(MOE_SUB):
            _run_copies((j - 1) * MOE_SUB + u, *refs, copy_from((1 - par) * MOE_SUB + u), False)

    @pl.when(j == last)
    def _():
        for u in range(MOE_SUB):
            _run_copies(j * MOE_SUB + u, *refs, copy_from(par * MOE_SUB + u), False)


def _dispatch(plan, h2, onehot):
    step_tok = MOE_SUB * MOE_TOK
    grid_spec = pltpu.PrefetchScalarGridSpec(
        num_scalar_prefetch=3,
        grid=(N_TOK_TILES // MOE_SUB,),
        in_specs=[
            pl.BlockSpec((step_tok, D_MODEL), lambda j, *_: (j, 0)),
            pl.BlockSpec((N_EXPERTS, step_tok), lambda j, *_: (0, j)),
            pl.BlockSpec((MOE_SUB, N_EXPERTS, 1), lambda j, *_: (j, 0, 0)),
        ],
        out_specs=pl.BlockSpec(memory_space=pl.ANY),
        scratch_shapes=[pltpu.VMEM((2 * MOE_SUB, MOE_LB, D_MODEL), BF16),
                        pltpu.VMEM((MOE_TOK, MOE_TOK), BF16),
                        pltpu.SemaphoreType.DMA((2 * MOE_SUB,))],
    )
    return pl.pallas_call(
        _dispatch_kernel,
        grid_spec=grid_spec,
        out_shape=jax.ShapeDtypeStruct((ROWS_MAX, D_MODEL), BF16),
        compiler_params=_cparams(("arbitrary",)),
        name="dispatch",
    )(plan["n8"], plan["local"], plan["glob"], h2, onehot, plan["local_col"])


def _experts_kernel(texp_ref, nact_ref, xs_ref, wg_ref, wu_ref, wd_ref, ys_ref, wg_b, wu_b, wd_b):
    i = pl.program_id(0)
    active = i < nact_ref[0]
    new_expert = jnp.logical_or(i == 0, texp_ref[i] != texp_ref[jnp.maximum(i - 1, 0)])

    @pl.when(jnp.logical_and(active, new_expert))
    def _():
        wg_b[...] = wg_ref[...].astype(BF16)
        wu_b[...] = wu_ref[...].astype(BF16)
        wd_b[...] = wd_ref[...].astype(BF16)

    @pl.when(active)
    def _():
        x = xs_ref[...]
        hg = jnp.dot(x, wg_b[...], preferred_element_type=F32)
        hu = jnp.dot(x, wu_b[...], preferred_element_type=F32)
        act = (jax.nn.silu(hg) * hu).astype(BF16)
        ys_ref[...] = jnp.dot(act, wd_b[...], preferred_element_type=F32).astype(BF16)


def _experts(plan, xs, layer, wg, wu, wd):
    def rows(i, texp, nact):
        return (jnp.minimum(i, nact[0] - 1), 0)

    def expert(i, texp, nact):
        return (layer, texp[i], 0, 0)

    grid_spec = pltpu.PrefetchScalarGridSpec(
        num_scalar_prefetch=2,
        grid=(ROWS_MAX // FFN_TM,),
        in_specs=[
            pl.BlockSpec((FFN_TM, D_MODEL), rows),
            pl.BlockSpec((None, None, D_MODEL, D_EXPERT), expert),
            pl.BlockSpec((None, None, D_MODEL, D_EXPERT), expert),
            pl.BlockSpec((None, None, D_EXPERT, D_MODEL), expert),
        ],
        out_specs=pl.BlockSpec((FFN_TM, D_MODEL), rows),
        scratch_shapes=[pltpu.VMEM((D_MODEL, D_EXPERT), BF16), pltpu.VMEM((D_MODEL, D_EXPERT), BF16),
                        pltpu.VMEM((D_EXPERT, D_MODEL), BF16)],
    )
    return pl.pallas_call(
        _experts_kernel,
        grid_spec=grid_spec,
        out_shape=jax.ShapeDtypeStruct((ROWS_MAX, D_MODEL), BF16),
        compiler_params=_cparams(("arbitrary",)),
        name="experts",
    )(plan["texp"], plan["n_active"], xs, wg, wu, wd)


def _combine_kernel(n8_ref, local_ref, glob_ref, oh_ref, g_ref, lcol_ref, x_ref, mod_ref, fg_ref,
                    ys_ref, o_ref, ybuf, before_sc, sem, *, final_norm):
    j = pl.program_id(0)
    n_steps = pl.num_programs(0)
    par = j % 2
    refs = (n8_ref, local_ref, glob_ref)
    _fill_before(before_sc)

    def copy_to(slot):
        def make(lo_, go_, size):
            return pltpu.make_async_copy(ys_ref.at[pl.ds(go_, size), :],
                                         ybuf.at[slot, pl.ds(lo_, size), :], sem.at[slot])
        return make

    @pl.when(j == 0)
    def _():
        ybuf[...] = jnp.zeros(ybuf.shape, BF16)
        for u in range(MOE_SUB):
            _run_copies(u, *refs, copy_to(u), True)

    @pl.when(j + 1 < n_steps)
    def _():
        for u in range(MOE_SUB):
            _run_copies((j + 1) * MOE_SUB + u, *refs, copy_to((1 - par) * MOE_SUB + u), True)

    row = lax.broadcasted_iota(jnp.int32, (MOE_LB, MOE_TOK), 0).astype(F32)
    for u in range(MOE_SUB):
        tok = slice(u * MOE_TOK, (u + 1) * MOE_TOK)
        pos, chosen, lo, hi = _sorted_rows(oh_ref[:, tok], lcol_ref[u], before_sc[...])
        gate = jnp.where(chosen, g_ref[:, tok], 0.0)
        g_lo = jnp.sum(jnp.where(pos == lo, gate, 0.0), axis=0, keepdims=True)
        g_hi = jnp.sum(jnp.where(pos == hi, gate, 0.0), axis=0, keepdims=True)
        perm = (jnp.where(row == lo, g_lo, 0.0) + jnp.where(row == hi, g_hi, 0.0)).astype(BF16)
        slot = par * MOE_SUB + u
        _run_copies(j * MOE_SUB + u, *refs, copy_to(slot), False)
        y = lax.dot_general(perm, ybuf[slot], (((0,), (0,)), ((), ())),
                            preferred_element_type=F32)
        x = x_ref[tok, :] + mod_ref[5:6, :] * y
        if final_norm:
            x = x * lax.rsqrt(jnp.mean(x * x, axis=-1, keepdims=True) + EPS) * fg_ref[...]
        o_ref[tok, :] = x


N_FRONT_IN = 8


def _combine_front_kernel(*refs):
    n_in = 3 + 7
    comb_in, front_in = refs[:n_in], refs[n_in:n_in + N_FRONT_IN]
    o_ref, *front_out = refs[n_in + N_FRONT_IN:n_in + N_FRONT_IN + 5]
    scratch = refs[n_in + N_FRONT_IN + 5:]
    _combine_kernel(*comb_in, o_ref, *scratch, final_norm=False)
    _front_kernel(o_ref, *front_in, *front_out)


def _combine(plan, onehot, gates, ys, x, mod_l, final_g, front_layer=None, front_args=None):
    step_tok = MOE_SUB * MOE_TOK
    steps_per_seq = SEQ // step_tok
    route = pl.BlockSpec((N_EXPERTS, step_tok), lambda j, *_: (0, j))
    tok = pl.BlockSpec((step_tok, D_MODEL), lambda j, *_: (j, 0))
    modspec = pl.BlockSpec((None, N_MOD, D_MODEL), lambda j, *_: (j // steps_per_seq, 0, 0))
    in_specs = [
        route, route,
        pl.BlockSpec((MOE_SUB, N_EXPERTS, 1), lambda j, *_: (j, 0, 0)),
        tok, modspec,
        pl.BlockSpec((1, D_MODEL), lambda j, *_: (0, 0)),
        pl.BlockSpec(memory_space=pl.ANY),
    ]
    args = [onehot, gates, plan["local_col"], x, mod_l, final_g, ys]
    out_specs = tok
    out_shape = jax.ShapeDtypeStruct((TOKENS, D_MODEL), F32)
    body = functools.partial(_combine_kernel, final_norm=True)
    if front_args is not None:
        assert step_tok == K1_TM and len(front_args) == N_FRONT_IN
        row1024 = pl.BlockSpec((1, D_MODEL), lambda j, *_: (0, 0))
        row512 = pl.BlockSpec((1, A_WIDTH), lambda j, *_: (0, 0))
        half = pl.BlockSpec((step_tok, A_WIDTH), lambda j, *_: (j, 0))
        in_specs += [
            modspec, row1024,
            pl.BlockSpec((None, D_MODEL, IN_COLS), lambda j, *_: (front_layer, 0, 0)),
            row512, row512,
            pl.BlockSpec((A_GROUPS // 2, 2 * CHUNK, CHUNK), lambda j, *_: (0, 0, 0)),
            pl.BlockSpec((CHUNK, A_WIDTH), lambda j, *_: (0, 0)),
            row512,
        ]
        args += list(front_args)
        out_specs = [tok, half, half, half, half]
        out_shape = [out_shape, jax.ShapeDtypeStruct((TOKENS, A_WIDTH), BF16)] + \
                    [jax.ShapeDtypeStruct((TOKENS, B_WIDTH), F32)] * 3
        body = _combine_front_kernel
    grid_spec = pltpu.PrefetchScalarGridSpec(
        num_scalar_prefetch=3,
        grid=(N_TOK_TILES // MOE_SUB,),
        in_specs=in_specs,
        out_specs=out_specs,
        scratch_shapes=[pltpu.VMEM((2 * MOE_SUB, MOE_LB, D_MODEL), BF16),
                        pltpu.VMEM((MOE_TOK, MOE_TOK), BF16),
                        pltpu.SemaphoreType.DMA((2 * MOE_SUB,))],
    )
    return pl.pallas_call(
        body,
        grid_spec=grid_spec,
        out_shape=out_shape,
        compiler_params=_cparams(("arbitrary",)),
        name="combine",
    )(plan["n8"], plan["local"], plan["glob"], *args)


def kernel(x, c, rel_bias, router_w, router_b, mod_w, mod_b, norm1_g, w_in, gmlp_ln_g, gmlp_ln_b,
           gmlp_ws, gmlp_bs, out_norm_a_g, out_norm_b_g, w_out, norm2_g, moe_w_gate, moe_w_up,
           moe_w_down, final_g):
    mod = _modulation(c, mod_w, mod_b).reshape(DEPTH, BATCH, N_MOD, D_MODEL)
    bias_tab = _bias_tables(rel_bias)
    rwt = router_w.T.astype(BF16)
    rb_col = router_b.reshape(N_EXPERTS, 1)
    xt = x.reshape(TOKENS, D_MODEL)
    win_b = w_in.astype(BF16)
    wout_b = w_out.astype(BF16)

    def front_args(l):
        ws2 = gmlp_ws[l].astype(BF16).reshape(A_GROUPS // 2, 2 * CHUNK, CHUNK)
        bs_exp = jnp.repeat(gmlp_bs[l].T, HEAD_DIM, axis=1)
        return (mod[l], norm1_g[l].reshape(1, -1), win_b, gmlp_ln_g[l].reshape(1, -1),
                gmlp_ln_b[l].reshape(1, -1), ws2, bs_exp, out_norm_a_g[l].reshape(1, -1))

    out_a, q, k, v = _front(xt, 0, *front_args(0))
    for l in range(DEPTH):
        out_b = _attention(q, k, v, bias_tab)
        xt, h2, gates, onehot, count = _mid(
            out_a, out_b, xt, l, mod[l], out_norm_b_g[l].reshape(1, -1),
            wout_b, norm2_g[l].reshape(1, -1), rwt, rb_col)
        plan = _moe_plan(count)
        xs = _dispatch(plan, h2, onehot)
        ys = _experts(plan, xs, l, moe_w_gate, moe_w_up, moe_w_down)
        if l + 1 < DEPTH:
            xt, out_a, q, k, v = _combine(plan, onehot, gates, ys, xt, mod[l],
                                          final_g.reshape(1, -1), l + 1, front_args(l + 1))
        else:
            xt = _combine(plan, onehot, gates, ys, xt, mod[l], final_g.reshape(1, -1))
    return xt.reshape(BATCH, SEQ, D_MODEL)
```

```python
import functools
import math

import numpy as np
import jax
import jax.numpy as jnp
from jax import lax
from jax.experimental import pallas as pl
from jax.experimental.pallas import tpu as pltpu

D_MODEL = 1024
BATCH = 8
SEQ = 2048
DEPTH = 2
TOKENS = BATCH * SEQ
HEAD_DIM = 64
A_WIDTH = 512
B_WIDTH = 512
A_GROUPS = 8
IN_COLS = 2 * A_WIDTH + 3 * B_WIDTH
CHUNK = 128
DILATED_CONFIGS = ((128, 1), (512, 4), (2048, 16))
ATTN_BLOCK = 128
REL_BUCKETS = 32
REL_MAX_EXACT = REL_BUCKETS // 2
REL_MAX_DISTANCE = 2048
N_EXPERTS = 16
N_EXPERT_GROUPS = 4
EXPERTS_PER_GROUP = 4
D_EXPERT = 512
N_MOD = 6
EPS = 1e-6
NEG_INF = -1e30

LANES = 128
HEAD_PAIR = 2 * HEAD_DIM
N_PAIRS = B_WIDTH // HEAD_PAIR
RESIDUES = max(d for _, d in DILATED_CONFIGS)
LOG2E = math.log2(math.e)

F32 = jnp.float32
BF16 = jnp.bfloat16

VMEM_LIMIT = 56 * 1024 * 1024


def _cparams(sem):
    return pltpu.CompilerParams(dimension_semantics=sem, vmem_limit_bytes=VMEM_LIMIT)


def _gelu(x):
    return 0.5 * x * (1.0 + lax.erf(x * math.sqrt(0.5)))


MOD_TN = 1024


def _mod_kernel(c_ref, w_ref, b_ref, o_ref):
    ca = jax.nn.silu(c_ref[...])
    o_ref[...] = jnp.dot(ca.astype(BF16), w_ref[...].astype(BF16),
                         preferred_element_type=F32) + b_ref[...]


def _modulation(c, mod_w, mod_b):
    n_cols = N_MOD * D_MODEL
    return pl.pallas_call(
        _mod_kernel,
        grid=(DEPTH, n_cols // MOD_TN),
        in_specs=[
            pl.BlockSpec((BATCH, D_MODEL), lambda l, j: (0, 0)),
            pl.BlockSpec((None, D_MODEL, MOD_TN), lambda l, j: (l, 0, j)),
            pl.BlockSpec((None, 1, MOD_TN), lambda l, j: (l, 0, j)),
        ],
        out_specs=pl.BlockSpec((None, BATCH, MOD_TN), lambda l, j: (l, 0, j)),
        out_shape=jax.ShapeDtypeStruct((DEPTH, BATCH, n_cols), F32),
        compiler_params=_cparams(("arbitrary", "arbitrary")),
        name="modulation",
    )(c, mod_w, mod_b.reshape(DEPTH, 1, n_cols))


K1_TM = 1024


def _front_kernel(x_ref, mod_ref, n1g_ref, win_ref, lng_ref, lnb_ref, ws_ref, bs_ref, ga_ref,
                  a_ref, q_ref, k_ref, v_ref):
    x = x_ref[...]
    tm = x.shape[0]
    h = x * lax.rsqrt(jnp.mean(x * x, axis=-1, keepdims=True) + EPS) * n1g_ref[...]
    h = h * (1.0 + mod_ref[1:2, :]) + mod_ref[0:1, :]
    proj = jnp.dot(h.astype(BF16), win_ref[...], preferred_element_type=F32)

    q_ref[...] = proj[:, 2 * A_WIDTH:2 * A_WIDTH + B_WIDTH] * (HEAD_DIM ** -0.5 * LOG2E)
    k_ref[...] = proj[:, 2 * A_WIDTH + B_WIDTH:2 * A_WIDTH + 2 * B_WIDTH]
    v_ref[...] = proj[:, 2 * A_WIDTH + 2 * B_WIDTH:]

    u = _gelu(proj[:, :A_WIDTH])
    va = _gelu(proj[:, A_WIDTH:2 * A_WIDTH])
    mu = jnp.mean(va, axis=-1, keepdims=True)
    vc = va - mu
    vln = vc * lax.rsqrt(jnp.mean(vc * vc, axis=-1, keepdims=True) + EPS)
    vln = (vln * lng_ref[...] + lnb_ref[...]).astype(BF16)

    row = lax.broadcasted_iota(jnp.int32, (2 * CHUNK, CHUNK), 0)
    col = lax.broadcasted_iota(jnp.int32, (2 * CHUNK, CHUNK), 1)
    causal = (row % CHUNK) >= col
    first_group = lax.broadcasted_iota(jnp.int32, (CHUNK, LANES), 1) < HEAD_DIM
    wmix = [jnp.where(causal, ws_ref[p], jnp.zeros((), BF16)) for p in range(A_GROUPS // 2)]
    for c in range(tm // CHUNK):
        rows = slice(c * CHUNK, (c + 1) * CHUNK)
        parts = []
        for p in range(A_GROUPS // 2):
            vp = vln[rows, p * LANES:(p + 1) * LANES]
            r = jnp.dot(wmix[p], vp, preferred_element_type=F32)
            parts.append(jnp.where(first_group, r[:CHUNK], r[CHUNK:]))
        s = jnp.concatenate(parts, axis=-1) + bs_ref[...]
        oa = u[rows] * s
        oa = oa * lax.rsqrt(jnp.mean(oa * oa, axis=-1, keepdims=True) + EPS) * ga_ref[...]
        a_ref[rows, :] = oa.astype(BF16)


def _of_layer(layer, *tail):
    return pl.BlockSpec((None, *tail), lambda *_: (layer,) + (0,) * len(tail))


def _mod_of(layer, tokens_per_step):
    steps_per_seq = SEQ // tokens_per_step
    return pl.BlockSpec((None, None, N_MOD, D_MODEL),
                        lambda i, *_: (layer, i // steps_per_seq, 0, 0))


def _front_specs(layer, tm):
    return [
        _mod_of(layer, tm),
        _of_layer(layer, 1, D_MODEL),
        _of_layer(layer, D_MODEL, IN_COLS),
        _of_layer(layer, 1, A_WIDTH), _of_layer(layer, 1, A_WIDTH),
        _of_layer(layer, A_GROUPS // 2, 2 * CHUNK, CHUNK),
        _of_layer(layer, CHUNK, A_WIDTH),
        _of_layer(layer, 1, A_WIDTH),
    ]


def _front(x, layer, front_params):
    tm = K1_TM
    half = pl.BlockSpec((tm, A_WIDTH), lambda i: (i, 0))
    return pl.pallas_call(
        _front_kernel,
        grid=(TOKENS // tm,),
        in_specs=[pl.BlockSpec((tm, D_MODEL), lambda i: (i, 0))] + _front_specs(layer, tm),
        out_specs=[half, half, half, half],
        out_shape=[jax.ShapeDtypeStruct((TOKENS, A_WIDTH), BF16)] +
                  [jax.ShapeDtypeStruct((TOKENS, B_WIDTH), F32)] * 3,
        compiler_params=_cparams(("arbitrary",)),
        name="front",
    )(x, *front_params)


def _t5_bucket_np(dist):
    dist = np.maximum(dist, 0)
    ratio = np.log(np.maximum(dist, 1) / REL_MAX_EXACT) / np.log(REL_MAX_DISTANCE / REL_MAX_EXACT)
    large = REL_MAX_EXACT + np.floor(ratio * (REL_BUCKETS - REL_MAX_EXACT)).astype(np.int64)
    large = np.minimum(large, REL_BUCKETS - 1)
    return np.where(dist < REL_MAX_EXACT, dist, large).astype(np.int32)


def _bias_tables(rel_bias):
    blk = ATTN_BLOCK
    n_cfg = len(DILATED_CONFIGS)
    n_rel = 3 * blk
    rel = 2 * blk - 1 - np.arange(n_rel)
    bucket, valid, rows, cols = [], [], [], []
    for window, d in DILATED_CONFIGS:
        span = window // d
        bucket.append(_t5_bucket_np(np.clip(rel, 0, span) * d))
        valid.append((rel >= 0) & (rel <= span))
        sub = RESIDUES // d
        ln = blk // sub
        pos = np.arange(blk)
        rows.append(np.eye(blk, dtype=np.float32)[(pos % ln) * sub + pos // ln])
        cols.append(np.kron(np.eye(2, dtype=np.float32), rows[-1]))
    w = jnp.transpose(rel_bias.astype(F32)[np.stack(bucket)], (0, 2, 1)) * LOG2E
    w = jnp.where(np.stack(valid)[:, None, :], w, NEG_INF * LOG2E)
    flat = jnp.tile(w, (1, 1, blk))
    skew = flat[:, :, blk - 1:blk - 1 + blk * (n_rel - 1)].reshape(n_cfg, -1, blk, n_rel - 1)
    tab = skew[..., :2 * blk]
    return jnp.einsum('cpi,chik,cqk->chpq', np.stack(rows), tab, np.stack(cols),
                      precision=lax.Precision.HIGHEST)


def _attn_kernel(q_ref, k_ref, v_ref, bias_ref, o_ref, qp, kp, vp, m_sc, l_sc, acc_sc):
    blk = ATTN_BLOCK
    res = RESIDUES
    per = SEQ // res
    sq = res * res
    seg = sq // res
    lane = lax.broadcasted_iota(jnp.int32, (blk, LANES), 1)
    head0 = lane < HEAD_DIM
    ones = jnp.ones((2 * blk, LANES), BF16)

    pa = lax.broadcasted_iota(jnp.int32, (sq, sq), 0)
    pb = lax.broadcasted_iota(jnp.int32, (sq, sq), 1)
    regroup = jnp.where(pb == res * (pa % res) + pa // res, 1.0, 0.0).astype(BF16)

    def residue_rows(g):
        return [slice(per * r + seg * g, per * r + seg * (g + 1)) for r in range(res)]

    def load(ref, slices):
        return jnp.concatenate([ref[s, :] for s in slices], axis=0) if len(slices) > 1 \
            else ref[slices[0], :]

    def store(ref, slices, val):
        ln = val.shape[0] // len(slices)
        for i, s in enumerate(slices):
            ref[s, :] = val[i * ln:(i + 1) * ln]

    for g in range(SEQ // sq):
        rows = slice(sq * g, sq * (g + 1))
        qkv = jnp.concatenate([q_ref[rows, :], k_ref[rows, :], v_ref[rows, :]], axis=1)
        qkv = jnp.dot(regroup, qkv.astype(BF16), preferred_element_type=F32)
        store(qp, residue_rows(g), qkv[:, :LANES])
        store(kp, residue_rows(g), qkv[:, LANES:2 * LANES])
        store(vp, residue_rows(g), qkv[:, 2 * LANES:])

    def merge_heads(t):
        return jnp.where(head0, t[:blk], t[blk:])

    def scores(q, kcat, bias):
        zero = jnp.zeros_like(q)
        qs = jnp.concatenate([jnp.where(head0, q, zero), jnp.where(head0, zero, q)],
                             axis=0).astype(BF16)
        return lax.dot_general(qs, kcat, (((1,), (1,)), ((), ())),
                               preferred_element_type=F32) + bias

    def weighted_values(s, vcat):
        nk = vcat.shape[0]
        m = jnp.max(s, axis=-1, keepdims=True)
        p = jnp.exp2(s - m).astype(BF16)
        pv = jnp.dot(p, jnp.concatenate([vcat, ones[:nk]], axis=1), preferred_element_type=F32)
        return (merge_heads(jnp.broadcast_to(m, (2 * blk, LANES))),
                merge_heads(pv[:, LANES:]), merge_heads(pv[:, :LANES]))

    def update(ci, slices, m_c, l_c, o_c):
        if ci > 0:
            m_r = load(m_sc, slices)
            m_n = jnp.maximum(m_r, m_c)
            a = jnp.exp2(m_r - m_n)
            b = jnp.exp2(m_c - m_n)
            l_c = a * load(l_sc, slices) + b * l_c
            o_c = a * load(acc_sc, slices) + b * o_c
            m_c = m_n
        store(m_sc, slices, m_c)
        store(l_sc, slices, l_c)
        store(acc_sc, slices, o_c)

    blocks = []
    for ci, (window, d) in enumerate(DILATED_CONFIGS):
        sub = res // d
        ln = blk // sub
        for r in range(d):
            for n in range(SEQ // d // blk):
                slices = [slice(per * (r + d * c) + ln * n, per * (r + d * c) + ln * (n + 1))
                          for c in range(sub)]
                blocks.append((ci, slices, n == 0))

    kv_prev = [None, None]

    def score_stage(ci, slices, first):
        k_cur = load(kp, slices).astype(BF16)
        v_cur = load(vp, slices).astype(BF16)
        if first:
            bias = jnp.concatenate([bias_ref[ci, 0, :, blk:], bias_ref[ci, 1, :, blk:]], axis=0)
            out = scores(load(qp, slices), k_cur, bias), v_cur
        else:
            bias = jnp.concatenate([bias_ref[ci, 0], bias_ref[ci, 1]], axis=0)
            out = (scores(load(qp, slices), jnp.concatenate([kv_prev[0], k_cur], axis=0), bias),
                   jnp.concatenate([kv_prev[1], v_cur], axis=0))
        kv_prev[0], kv_prev[1] = k_cur, v_cur
        return out

    ahead = score_stage(*blocks[0])
    for i, (ci, slices, _) in enumerate(blocks):
        s, vcat = ahead
        if i + 1 < len(blocks):
            ahead = score_stage(*blocks[i + 1])
        update(ci, slices, *weighted_values(s, vcat))

    for g in range(SEQ // sq):
        o = load(acc_sc, residue_rows(g)) / load(l_sc, residue_rows(g))
        hi = o.astype(BF16)
        lo = (o - hi.astype(F32)).astype(BF16)
        back = jnp.dot(regroup, jnp.concatenate([hi, lo], axis=1), preferred_element_type=F32)
        o_ref[sq * g:sq * (g + 1), :] = back[:, :LANES] + back[:, LANES:]


def _attention(q, k, v, bias_tab):
    n_cfg = len(DILATED_CONFIGS)
    blk = ATTN_BLOCK
    seq_spec = pl.BlockSpec((SEQ, HEAD_PAIR), lambda b, p: (b, p))
    return pl.pallas_call(
        _attn_kernel,
        grid=(BATCH, N_PAIRS),
        in_specs=[seq_spec, seq_spec, seq_spec,
                  pl.BlockSpec((n_cfg, 2, blk, 2 * blk), lambda b, p: (0, p, 0, 0))],
        out_specs=seq_spec,
        out_shape=jax.ShapeDtypeStruct((TOKENS, B_WIDTH), F32),
        scratch_shapes=[pltpu.VMEM((SEQ, HEAD_PAIR), F32)] * 6,
        compiler_params=_cparams(("arbitrary", "arbitrary")),
        name="attention",
    )(q, k, v, bias_tab)


K4_TM = 512


def _top2_sum(a, b, c, d):
    hi1, lo1 = jnp.maximum(a, b), jnp.minimum(a, b)
    hi2, lo2 = jnp.maximum(c, d), jnp.minimum(c, d)
    return jnp.maximum(hi1, hi2) + jnp.maximum(jnp.minimum(hi1, hi2), jnp.maximum(lo1, lo2))


def _route(logits_t, rb_col):
    m = jnp.max(logits_t, axis=0, keepdims=True)
    e = jnp.exp(logits_t - m)
    probs = e / jnp.sum(e, axis=0, keepdims=True)
    sel = probs + rb_col
    sel_rows = [sel[i:i + 1, :] for i in range(N_EXPERTS)]
    prob_rows = [probs[i:i + 1, :] for i in range(N_EXPERTS)]
    gsz = EXPERTS_PER_GROUP
    score = [_top2_sum(*sel_rows[g * gsz:(g + 1) * gsz]) for g in range(N_EXPERT_GROUPS)]
    chosen = []
    for g in range(N_EXPERT_GROUPS):
        best = None
        for g2 in range(N_EXPERT_GROUPS):
            if g2 == g:
                continue
            c = (score[g] > score[g2]) if g2 < g else (score[g] >= score[g2])
            best = c if best is None else jnp.logical_and(best, c)
        for i in range(gsz):
            ei = g * gsz + i
            rank = jnp.zeros_like(sel_rows[ei])
            for j in range(gsz):
                if j == i:
                    continue
                ej = g * gsz + j
                ahead = (sel_rows[ej] >= sel_rows[ei]) if j < i else (sel_rows[ej] > sel_rows[ei])
                rank = rank + jnp.where(ahead, 1.0, 0.0)
            chosen.append(jnp.logical_and(best, rank < float(2)))
    picked = [jnp.where(chosen[i], prob_rows[i], 0.0) for i in range(N_EXPERTS)]
    denom = picked[0]
    for i in range(1, N_EXPERTS):
        denom = denom + picked[i]
    gates = jnp.concatenate([pk / denom for pk in picked], axis=0)
    onehot = jnp.concatenate([jnp.where(ch, 1.0, 0.0) for ch in chosen], axis=0)
    return gates, onehot


def _mid_kernel(a_ref, ob_ref, x_ref, mod_ref, gb_ref, wout_ref, n2g_ref, rwt_ref, rb_ref,
                x1_ref, h2_ref, gates_ref, onehot_ref, count_ref):
    ob = ob_ref[...]
    bn = ob * lax.rsqrt(jnp.mean(ob * ob, axis=-1, keepdims=True) + EPS) * gb_ref[...]
    mixed = jnp.dot(a_ref[...], wout_ref[:A_WIDTH, :], preferred_element_type=F32)
    mixed = mixed + jnp.dot(bn.astype(BF16), wout_ref[A_WIDTH:, :], preferred_element_type=F32)
    x1 = x_ref[...] + mod_ref[2:3, :] * mixed
    x1_ref[...] = x1
    h2 = x1 * lax.rsqrt(jnp.mean(x1 * x1, axis=-1, keepdims=True) + EPS) * n2g_ref[...]
    h2 = (h2 * (1.0 + mod_ref[4:5, :]) + mod_ref[3:4, :]).astype(BF16)
    h2_ref[...] = h2
    logits_t = lax.dot_general(rwt_ref[...], h2, (((1,), (1,)), ((), ())),
                               preferred_element_type=F32)
    gates_t, onehot_t = _route(logits_t, rb_ref[...])
    gates_ref[...] = gates_t
    onehot_ref[...] = onehot_t
    for s in range(count_ref.shape[0]):
        count_ref[s] = jnp.sum(onehot_t[:, s * MOE_TOK:(s + 1) * MOE_TOK], axis=1, keepdims=True)


def _mid(out_a, out_b, x, layer, mod, gb, wout, n2g, rwt, rb_col):
    tm = K4_TM
    tok = pl.BlockSpec((tm, D_MODEL), lambda i: (i, 0))
    half = pl.BlockSpec((tm, A_WIDTH), lambda i: (i, 0))
    route = pl.BlockSpec((N_EXPERTS, tm), lambda i: (0, i))
    return pl.pallas_call(
        _mid_kernel,
        grid=(TOKENS // tm,),
        in_specs=[
            half, half, tok,
            _mod_of(layer, tm),
            _of_layer(layer, 1, B_WIDTH),
            _of_layer(layer, D_MODEL, D_MODEL),
            _of_layer(layer, 1, D_MODEL),
            pl.BlockSpec((N_EXPERTS, D_MODEL), lambda i: (0, 0)),
            pl.BlockSpec((N_EXPERTS, 1), lambda i: (0, 0)),
        ],
        out_specs=[tok, tok, route, route,
                   pl.BlockSpec((tm // MOE_TOK, N_EXPERTS, 1), lambda i: (i, 0, 0))],
        out_shape=[jax.ShapeDtypeStruct((TOKENS, D_MODEL), F32),
                   jax.ShapeDtypeStruct((TOKENS, D_MODEL), BF16),
                   jax.ShapeDtypeStruct((N_EXPERTS, TOKENS), F32),
                   jax.ShapeDtypeStruct((N_EXPERTS, TOKENS), F32),
                   jax.ShapeDtypeStruct((TOKENS // MOE_TOK, N_EXPERTS, 1), F32)],
        compiler_params=_cparams(("arbitrary",)),
        name="mid",
    )(out_a, out_b, x, mod, gb, wout, n2g, rwt, rb_col)


MOE_TOK = 256
ROW_ALIGN = 16
MOE_SUB = 4
MOE_LB = 2 * MOE_TOK + N_EXPERTS * ROW_ALIGN
FFN_TM = 1024
N_TOK_TILES = TOKENS // MOE_TOK
ROWS_MAX = -(-(2 * TOKENS + N_TOK_TILES * N_EXPERTS * ROW_ALIGN
               + N_EXPERTS * (FFN_TM - ROW_ALIGN)) // FFN_TM) * FFN_TM


N_FFN_TILES = ROWS_MAX // FFN_TM


def _plan_kernel(count_ref, n8_ref, local_ref, glob_ref, texp_ref, nact_ref):
    def prefix(a, b):
        return jnp.dot(a, b, preferred_element_type=F32, precision=lax.Precision.HIGHEST)

    row = lax.broadcasted_iota(jnp.int32, (LANES, LANES), 0)
    col = lax.broadcasted_iota(jnp.int32, (LANES, LANES), 1)
    real = jnp.logical_and(row < N_TOK_TILES, col < N_EXPERTS)
    n8 = jnp.where(real, jnp.maximum(jnp.ceil(count_ref[...] / ROW_ALIGN), 1.0) * ROW_ALIGN, 0.0)
    earlier_lane = jnp.where(row < col, 1.0, 0.0)
    earlier_row = jnp.where(col < row, 1.0, 0.0)
    local = prefix(n8, earlier_lane)
    seg = jnp.ceil(jnp.sum(n8, axis=0, keepdims=True) / FFN_TM) * FFN_TM
    seg_start = prefix(jnp.broadcast_to(seg, (LANES, LANES)), earlier_lane)[0:1]
    glob = seg_start + prefix(earlier_row, n8)
    n_active = jnp.sum(seg, axis=1, keepdims=True) / FFN_TM
    tile = row[:, 0:1].astype(F32)
    ended = jnp.logical_and(tile * FFN_TM >= seg_start + seg, col < N_EXPERTS)
    texp = jnp.sum(jnp.where(ended, 1.0, 0.0), axis=1, keepdims=True)
    texp = jnp.minimum(texp, float(N_EXPERTS - 1))
    last = jnp.sum(jnp.where(tile == n_active - 1.0, texp, 0.0), axis=0, keepdims=True)
    texp = jnp.where(tile < n_active, texp, last)
    n8_ref[...] = n8.astype(jnp.int32)
    local_ref[...] = local.astype(jnp.int32)
    glob_ref[...] = glob.astype(jnp.int32)
    texp_ref[...] = jnp.broadcast_to(texp, (LANES, LANES)).astype(jnp.int32)
    nact_ref[...] = jnp.broadcast_to(n_active, (8, LANES)).astype(jnp.int32)


def _moe_plan(count):
    assert max(N_TOK_TILES, N_FFN_TILES, N_EXPERTS) <= LANES
    count = count.reshape(N_TOK_TILES, N_EXPERTS)
    count = jnp.pad(count, ((0, LANES - N_TOK_TILES), (0, LANES - N_EXPERTS)))
    full = jax.ShapeDtypeStruct((LANES, LANES), jnp.int32)
    n8, local, glob, texp, nact = pl.pallas_call(
        _plan_kernel,
        out_shape=[full, full, full, full, jax.ShapeDtypeStruct((8, LANES), jnp.int32)],
        name="plan",
    )(count)

    def runs(t):
        return t[:N_TOK_TILES, :N_EXPERTS]

    return dict(n8=runs(n8).reshape(-1), local=runs(local).reshape(-1),
                glob=runs(glob).reshape(-1),
                local_col=runs(local).astype(F32).reshape(N_TOK_TILES, N_EXPERTS, 1),
                texp=texp[:N_FFN_TILES, 0], n_active=nact[0, :1])


def _run_copies(tile, n8_ref, local_ref, glob_ref, make_copy, start):
    for e in range(N_EXPERTS):
        idx = tile * N_EXPERTS + e
        cp = make_copy(pl.multiple_of(local_ref[idx], ROW_ALIGN),
                       pl.multiple_of(glob_ref[idx], ROW_ALIGN),
                       pl.multiple_of(n8_ref[idx], ROW_ALIGN))
        if start:
            cp.start()
        else:
            cp.wait()


def _fill_before(before_sc):
    @pl.when(pl.program_id(0) == 0)
    def _():
        src = lax.broadcasted_iota(jnp.int32, before_sc.shape, 0)
        dst = lax.broadcasted_iota(jnp.int32, before_sc.shape, 1)
        before_sc[...] = jnp.where(src < dst, 1.0, 0.0).astype(BF16)


def _sorted_rows(onehot, local_col, before):
    rank = jnp.dot(onehot.astype(BF16), before, preferred_element_type=F32)
    pos = local_col + rank
    chosen = onehot > 0.5
    lo = jnp.min(jnp.where(chosen, pos, float(MOE_LB)), axis=0, keepdims=True)
    hi = jnp.max(jnp.where(chosen, pos, -1.0), axis=0, keepdims=True)
    return pos, chosen, lo, hi


def _dispatch_kernel(n8_ref, local_ref, glob_ref, h_ref, oh_ref, lcol_ref, xs_ref,
                     lbuf, before_sc, sem):
    j = pl.program_id(0)
    last = pl.num_programs(0) - 1
    par = j % 2
    _fill_before(before_sc)
    row = lax.broadcasted_iota(jnp.int32, (MOE_LB, MOE_TOK), 0).astype(F32)
    for u in range(MOE_SUB):
        tok = slice(u * MOE_TOK, (u + 1) * MOE_TOK)
        _, _, lo, hi = _sorted_rows(oh_ref[:, tok], lcol_ref[u], before_sc[...])
        perm = (jnp.where(row == lo, 1.0, 0.0) + jnp.where(row == hi, 1.0, 0.0)).astype(BF16)
        lbuf[par * MOE_SUB + u] = jnp.dot(perm, h_ref[tok, :],
                                          preferred_element_type=F32).astype(BF16)

    def copy_from(slot):
        def make(lo_, go_, size):
            return pltpu.make_async_copy(lbuf.at[slot, pl.ds(lo_, size), :],
                                         xs_ref.at[pl.ds(go_, size), :], sem.at[slot])
        return make

    refs = (n8_ref, local_ref, glob_ref)
    for u in range(MOE_SUB):
        _run_copies(j * MOE_SUB + u, *refs, copy_from(par * MOE_SUB + u), True)

    @pl.when(j > 0)
    def _():
        for u in range(MOE_SUB):
            _run_copies((j - 1) * MOE_SUB + u, *refs, copy_from((1 - par) * MOE_SUB + u), False)

    @pl.when(j == last)
    def _():
        for u in range(MOE_SUB):
            _run_copies(j * MOE_SUB + u, *refs, copy_from(par * MOE_SUB + u), False)


def _dispatch(plan, h2, onehot):
    step_tok = MOE_SUB * MOE_TOK
    grid_spec = pltpu.PrefetchScalarGridSpec(
        num_scalar_prefetch=3,
        grid=(N_TOK_TILES // MOE_SUB,),
        in_specs=[
            pl.BlockSpec((step_tok, D_MODEL), lambda j, *_: (j, 0)),
            pl.BlockSpec((N_EXPERTS, step_tok), lambda j, *_: (0, j)),
            pl.BlockSpec((MOE_SUB, N_EXPERTS, 1), lambda j, *_: (j, 0, 0)),
        ],
        out_specs=pl.BlockSpec(memory_space=pl.ANY),
        scratch_shapes=[pltpu.VMEM((2 * MOE_SUB, MOE_LB, D_MODEL), BF16),
                        pltpu.VMEM((MOE_TOK, MOE_TOK), BF16),
                        pltpu.SemaphoreType.DMA((2 * MOE_SUB,))],
    )
    return pl.pallas_call(
        _dispatch_kernel,
        grid_spec=grid_spec,
        out_shape=jax.ShapeDtypeStruct((ROWS_MAX, D_MODEL), BF16),
        compiler_params=_cparams(("arbitrary",)),
        name="dispatch",
    )(plan["n8"], plan["local"], plan["glob"], h2, onehot, plan["local_col"])


def _experts_kernel(texp_ref, nact_ref, xs_ref, wg_ref, wu_ref, wd_ref, ys_ref, wg_b, wu_b, wd_b):
    i = pl.program_id(0)
    active = i < nact_ref[0]
    new_expert = jnp.logical_or(i == 0, texp_ref[i] != texp_ref[jnp.maximum(i - 1, 0)])

    @pl.when(jnp.logical_and(active, new_expert))
    def _():
        wg_b[...] = wg_ref[...].astype(BF16)
        wu_b[...] = wu_ref[...].astype(BF16)
        wd_b[...] = wd_ref[...].astype(BF16)

    @pl.when(active)
    def _():
        x = xs_ref[...]
        hg = jnp.dot(x, wg_b[...], preferred_element_type=F32)
        hu = jnp.dot(x, wu_b[...], preferred_element_type=F32)
        act = (jax.nn.silu(hg) * hu).astype(BF16)
        ys_ref[...] = jnp.dot(act, wd_b[...], preferred_element_type=F32).astype(BF16)


def _experts(plan, xs, layer, wg, wu, wd):
    def rows(i, texp, nact):
        return (jnp.minimum(i, nact[0] - 1), 0)

    def expert(i, texp, nact):
        return (layer, texp[i], 0, 0)

    grid_spec = pltpu.PrefetchScalarGridSpec(
        num_scalar_prefetch=2,
        grid=(ROWS_MAX // FFN_TM,),
        in_specs=[
            pl.BlockSpec((FFN_TM, D_MODEL), rows),
            pl.BlockSpec((None, None, D_MODEL, D_EXPERT), expert),
            pl.BlockSpec((None, None, D_MODEL, D_EXPERT), expert),
            pl.BlockSpec((None, None, D_EXPERT, D_MODEL), expert),
        ],
        out_specs=pl.BlockSpec((FFN_TM, D_MODEL), rows),
        scratch_shapes=[pltpu.VMEM((D_MODEL, D_EXPERT), BF16), pltpu.VMEM((D_MODEL, D_EXPERT), BF16),
                        pltpu.VMEM((D_EXPERT, D_MODEL), BF16)],
    )
    return pl.pallas_call(
        _experts_kernel,
        grid_spec=grid_spec,
        out_shape=jax.ShapeDtypeStruct((ROWS_MAX, D_MODEL), BF16),
        compiler_params=_cparams(("arbitrary",)),
        name="experts",
    )(plan["texp"], plan["n_active"], xs, wg, wu, wd)


def _combine_kernel(n8_ref, local_ref, glob_ref, oh_ref, g_ref, lcol_ref, x_ref, mod_ref, fg_ref,
                    ys_ref, o_ref, ybuf, before_sc, sem, *, final_norm):
    j = pl.program_id(0)
    n_steps = pl.num_programs(0)
    par = j % 2
    refs = (n8_ref, local_ref, glob_ref)
    _fill_before(before_sc)

    def copy_to(slot):
        def make(lo_, go_, size):
            return pltpu.make_async_copy(ys_ref.at[pl.ds(go_, size), :],
                                         ybuf.at[slot, pl.ds(lo_, size), :], sem.at[slot])
        return make

    @pl.when(j == 0)
    def _():
        ybuf[...] = jnp.zeros(ybuf.shape, BF16)
        for u in range(MOE_SUB):
            _run_copies(u, *refs, copy_to(u), True)

    @pl.when(j + 1 < n_steps)
    def _():
        for u in range(MOE_SUB):
            _run_copies((j + 1) * MOE_SUB + u, *refs, copy_to((1 - par) * MOE_SUB + u), True)

    row = lax.broadcasted_iota(jnp.int32, (MOE_LB, MOE_TOK), 0).astype(F32)
    for u in range(MOE_SUB):
        tok = slice(u * MOE_TOK, (u + 1) * MOE_TOK)
        pos, chosen, lo, hi = _sorted_rows(oh_ref[:, tok], lcol_ref[u], before_sc[...])
        gate = jnp.where(chosen, g_ref[:, tok], 0.0)
        g_lo = jnp.sum(jnp.where(pos == lo, gate, 0.0), axis=0, keepdims=True)
        g_hi = jnp.sum(jnp.where(pos == hi, gate, 0.0), axis=0, keepdims=True)
        perm = (jnp.where(row == lo, g_lo, 0.0) + jnp.where(row == hi, g_hi, 0.0)).astype(BF16)
        slot = par * MOE_SUB + u
        _run_copies(j * MOE_SUB + u, *refs, copy_to(slot), False)
        y = lax.dot_general(perm, ybuf[slot], (((0,), (0,)), ((), ())),
                            preferred_element_type=F32)
        x = x_ref[tok, :] + mod_ref[5:6, :] * y
        if final_norm:
            x = x * lax.rsqrt(jnp.mean(x * x, axis=-1, keepdims=True) + EPS) * fg_ref[...]
        o_ref[tok, :] = x


N_FRONT_IN = 8


def _combine_front_kernel(*refs):
    n_in = 3 + 7
    comb_in, front_in = refs[:n_in], refs[n_in:n_in + N_FRONT_IN]
    o_ref, *front_out = refs[n_in + N_FRONT_IN:n_in + N_FRONT_IN + 5]
    scratch = refs[n_in + N_FRONT_IN + 5:]
    _combine_kernel(*comb_in, o_ref, *scratch, final_norm=False)
    _front_kernel(o_ref, *front_in, *front_out)


def _combine(plan, onehot, gates, ys, x, layer, mod, final_g, front_params=None):
    step_tok = MOE_SUB * MOE_TOK
    route = pl.BlockSpec((N_EXPERTS, step_tok), lambda j, *_: (0, j))
    tok = pl.BlockSpec((step_tok, D_MODEL), lambda j, *_: (j, 0))
    in_specs = [
        route, route,
        pl.BlockSpec((MOE_SUB, N_EXPERTS, 1), lambda j, *_: (j, 0, 0)),
        tok, _mod_of(layer, step_tok),
        pl.BlockSpec((1, D_MODEL), lambda j, *_: (0, 0)),
        pl.BlockSpec(memory_space=pl.ANY),
    ]
    args = [onehot, gates, plan["local_col"], x, mod, final_g, ys]
    out_specs = tok
    out_shape = jax.ShapeDtypeStruct((TOKENS, D_MODEL), F32)
    body = functools.partial(_combine_kernel, final_norm=True)
    if front_params is not None:
        assert step_tok == K1_TM and len(front_params) == N_FRONT_IN
        half = pl.BlockSpec((step_tok, A_WIDTH), lambda j, *_: (j, 0))
        in_specs += _front_specs(layer + 1, step_tok)
        args += list(front_params)
        out_specs = [tok, half, half, half, half]
        out_shape = [out_shape, jax.ShapeDtypeStruct((TOKENS, A_WIDTH), BF16)] + \
                    [jax.ShapeDtypeStruct((TOKENS, B_WIDTH), F32)] * 3
        body = _combine_front_kernel
    grid_spec = pltpu.PrefetchScalarGridSpec(
        num_scalar_prefetch=3,
        grid=(N_TOK_TILES // MOE_SUB,),
        in_specs=in_specs,
        out_specs=out_specs,
        scratch_shapes=[pltpu.VMEM((2 * MOE_SUB, MOE_LB, D_MODEL), BF16),
                        pltpu.VMEM((MOE_TOK, MOE_TOK), BF16),
                        pltpu.SemaphoreType.DMA((2 * MOE_SUB,))],
    )
    return pl.pallas_call(
        body,
        grid_spec=grid_spec,
        out_shape=out_shape,
        compiler_params=_cparams(("arbitrary",)),
        name="combine",
    )(plan["n8"], plan["local"], plan["glob"], *args)


def kernel(x, c, rel_bias, router_w, router_b, mod_w, mod_b, norm1_g, w_in, gmlp_ln_g, gmlp_ln_b,
           gmlp_ws, gmlp_bs, out_norm_a_g, out_norm_b_g, w_out, norm2_g, moe_w_gate, moe_w_up,
           moe_w_down, final_g):
    mod = _modulation(c, mod_w, mod_b).reshape(DEPTH, BATCH, N_MOD, D_MODEL)
    bias_tab = _bias_tables(rel_bias)
    rwt = router_w.T.astype(BF16)
    rb_col = router_b.reshape(N_EXPERTS, 1)
    xt = x.reshape(TOKENS, D_MODEL)
    front_params = (
        mod, norm1_g.reshape(DEPTH, 1, D_MODEL), w_in.astype(BF16),
        gmlp_ln_g.reshape(DEPTH, 1, A_WIDTH), gmlp_ln_b.reshape(DEPTH, 1, A_WIDTH),
        gmlp_ws.astype(BF16).reshape(DEPTH, A_GROUPS // 2, 2 * CHUNK, CHUNK),
        jnp.repeat(jnp.swapaxes(gmlp_bs, 1, 2), HEAD_DIM, axis=2),
        out_norm_a_g.reshape(DEPTH, 1, A_WIDTH))
    gb = out_norm_b_g.reshape(DEPTH, 1, B_WIDTH)
    wout_b = w_out.astype(BF16)
    n2g = norm2_g.reshape(DEPTH, 1, D_MODEL)
    fg = final_g.reshape(1, D_MODEL)

    out_a, q, k, v = _front(xt, 0, front_params)
    for l in range(DEPTH):
        out_b = _attention(q, k, v, bias_tab)
        xt, h2, gates, onehot, count = _mid(out_a, out_b, xt, l, mod, gb, wout_b, n2g, rwt, rb_col)
        plan = _moe_plan(count)
        xs = _dispatch(plan, h2, onehot)
        ys = _experts(plan, xs, l, moe_w_gate, moe_w_up, moe_w_down)
        if l + 1 < DEPTH:
            xt, out_a, q, k, v = _combine(plan, onehot, gates, ys, xt, l, mod, fg, front_params)
        else:
            xt = _combine(plan, onehot, gates, ys, xt, l, mod, fg)
    return xt.reshape(BATCH, SEQ, D_MODEL)
```

```python
import functools
import math

import numpy as np
import jax
import jax.numpy as jnp
from jax import lax
from jax.experimental import pallas as pl
from jax.experimental.pallas import tpu as pltpu

D_MODEL = 1024
BATCH = 8
SEQ = 2048
DEPTH = 2
TOKENS = BATCH * SEQ
HEAD_DIM = 64
A_WIDTH = 512
B_WIDTH = 512
A_GROUPS = 8
IN_COLS = 2 * A_WIDTH + 3 * B_WIDTH
CHUNK = 128
DILATED_CONFIGS = ((128, 1), (512, 4), (2048, 16))
ATTN_BLOCK = 128
REL_BUCKETS = 32
REL_MAX_EXACT = REL_BUCKETS // 2
REL_MAX_DISTANCE = 2048
N_EXPERTS = 16
N_EXPERT_GROUPS = 4
EXPERTS_PER_GROUP = 4
D_EXPERT = 512
N_MOD = 6
EPS = 1e-6
NEG_INF = -1e30

LANES = 128
HEAD_PAIR = 2 * HEAD_DIM
N_PAIRS = B_WIDTH // HEAD_PAIR
RESIDUES = max(d for _, d in DILATED_CONFIGS)
LOG2E = math.log2(math.e)

F32 = jnp.float32
BF16 = jnp.bfloat16

VMEM_LIMIT = 56 * 1024 * 1024


def _cparams(sem):
    return pltpu.CompilerParams(dimension_semantics=sem, vmem_limit_bytes=VMEM_LIMIT)


def _gelu(x):
    return 0.5 * x * (1.0 + lax.erf(x * math.sqrt(0.5)))


MOD_TN = 1024


def _mod_kernel(c_ref, w_ref, b_ref, o_ref):
    ca = jax.nn.silu(c_ref[...])
    o_ref[...] = jnp.dot(ca.astype(BF16), w_ref[...].astype(BF16),
                         preferred_element_type=F32) + b_ref[...]


def _modulation(c, mod_w, mod_b):
    n_cols = N_MOD * D_MODEL
    return pl.pallas_call(
        _mod_kernel,
        grid=(DEPTH, n_cols // MOD_TN),
        in_specs=[
            pl.BlockSpec((BATCH, D_MODEL), lambda l, j: (0, 0)),
            pl.BlockSpec((None, D_MODEL, MOD_TN), lambda l, j: (l, 0, j)),
            pl.BlockSpec((None, 1, MOD_TN), lambda l, j: (l, 0, j)),
        ],
        out_specs=pl.BlockSpec((None, BATCH, MOD_TN), lambda l, j: (l, 0, j)),
        out_shape=jax.ShapeDtypeStruct((DEPTH, BATCH, n_cols), F32),
        compiler_params=_cparams(("arbitrary", "arbitrary")),
        name="modulation",
    )(c, mod_w, mod_b.reshape(DEPTH, 1, n_cols))


K1_TM = 1024


def _front_kernel(x_ref, mod_ref, n1g_ref, win_ref, lng_ref, lnb_ref, ws_ref, bs_ref, ga_ref,
                  a_ref, q_ref, k_ref, v_ref):
    x = x_ref[...]
    tm = x.shape[0]
    h = x * lax.rsqrt(jnp.mean(x * x, axis=-1, keepdims=True) + EPS) * n1g_ref[...]
    h = h * (1.0 + mod_ref[1:2, :]) + mod_ref[0:1, :]
    proj = jnp.dot(h.astype(BF16), win_ref[...], preferred_element_type=F32)

    q_ref[...] = proj[:, 2 * A_WIDTH:2 * A_WIDTH + B_WIDTH] * (HEAD_DIM ** -0.5 * LOG2E)
    k_ref[...] = proj[:, 2 * A_WIDTH + B_WIDTH:2 * A_WIDTH + 2 * B_WIDTH]
    v_ref[...] = proj[:, 2 * A_WIDTH + 2 * B_WIDTH:]

    u = _gelu(proj[:, :A_WIDTH])
    va = _gelu(proj[:, A_WIDTH:2 * A_WIDTH])
    mu = jnp.mean(va, axis=-1, keepdims=True)
    vc = va - mu
    vln = vc * lax.rsqrt(jnp.mean(vc * vc, axis=-1, keepdims=True) + EPS)
    vln = (vln * lng_ref[...] + lnb_ref[...]).astype(BF16)

    row = lax.broadcasted_iota(jnp.int32, (2 * CHUNK, CHUNK), 0)
    col = lax.broadcasted_iota(jnp.int32, (2 * CHUNK, CHUNK), 1)
    causal = (row % CHUNK) >= col
    first_group = lax.broadcasted_iota(jnp.int32, (CHUNK, LANES), 1) < HEAD_DIM
    wmix = [jnp.where(causal, ws_ref[p], jnp.zeros((), BF16)) for p in range(A_GROUPS // 2)]
    for c in range(tm // CHUNK):
        rows = slice(c * CHUNK, (c + 1) * CHUNK)
        parts = []
        for p in range(A_GROUPS // 2):
            vp = vln[rows, p * LANES:(p + 1) * LANES]
            r = jnp.dot(wmix[p], vp, preferred_element_type=F32)
            parts.append(jnp.where(first_group, r[:CHUNK], r[CHUNK:]))
        s = jnp.concatenate(parts, axis=-1) + bs_ref[...]
        oa = u[rows] * s
        oa = oa * lax.rsqrt(jnp.mean(oa * oa, axis=-1, keepdims=True) + EPS) * ga_ref[...]
        a_ref[rows, :] = oa.astype(BF16)


def _of_layer(layer, *tail):
    return pl.BlockSpec((None, *tail), lambda *_: (layer,) + (0,) * len(tail))


def _mod_of(layer, tokens_per_step):
    steps_per_seq = SEQ // tokens_per_step
    return pl.BlockSpec((None, None, N_MOD, D_MODEL),
                        lambda i, *_: (layer, i // steps_per_seq, 0, 0))


def _front_specs(layer, tm):
    return [
        _mod_of(layer, tm),
        _of_layer(layer, 1, D_MODEL),
        _of_layer(layer, D_MODEL, IN_COLS),
        _of_layer(layer, 1, A_WIDTH), _of_layer(layer, 1, A_WIDTH),
        _of_layer(layer, A_GROUPS // 2, 2 * CHUNK, CHUNK),
        _of_layer(layer, CHUNK, A_WIDTH),
        _of_layer(layer, 1, A_WIDTH),
    ]


def _front(x, layer, front_params):
    tm = K1_TM
    half = pl.BlockSpec((tm, A_WIDTH), lambda i: (i, 0))
    return pl.pallas_call(
        _front_kernel,
        grid=(TOKENS // tm,),
        in_specs=[pl.BlockSpec((tm, D_MODEL), lambda i: (i, 0))] + _front_specs(layer, tm),
        out_specs=[half, half, half, half],
        out_shape=[jax.ShapeDtypeStruct((TOKENS, A_WIDTH), BF16)] +
                  [jax.ShapeDtypeStruct((TOKENS, B_WIDTH), F32)] * 3,
        compiler_params=_cparams(("arbitrary",)),
        name="front",
    )(x, *front_params)


def _t5_bucket_np(dist):
    dist = np.maximum(dist, 0)
    ratio = np.log(np.maximum(dist, 1) / REL_MAX_EXACT) / np.log(REL_MAX_DISTANCE / REL_MAX_EXACT)
    large = REL_MAX_EXACT + np.floor(ratio * (REL_BUCKETS - REL_MAX_EXACT)).astype(np.int64)
    large = np.minimum(large, REL_BUCKETS - 1)
    return np.where(dist < REL_MAX_EXACT, dist, large).astype(np.int32)


def _bias_tables(rel_bias):
    blk = ATTN_BLOCK
    n_cfg = len(DILATED_CONFIGS)
    n_rel = 3 * blk
    rel = 2 * blk - 1 - np.arange(n_rel)
    bucket, valid, rows, cols = [], [], [], []
    for window, d in DILATED_CONFIGS:
        span = window // d
        bucket.append(_t5_bucket_np(np.clip(rel, 0, span) * d))
        valid.append((rel >= 0) & (rel <= span))
        sub = RESIDUES // d
        ln = blk // sub
        pos = np.arange(blk)
        rows.append(np.eye(blk, dtype=np.float32)[(pos % ln) * sub + pos // ln])
        cols.append(np.kron(np.eye(2, dtype=np.float32), rows[-1]))
    w = jnp.transpose(rel_bias.astype(F32)[np.stack(bucket)], (0, 2, 1)) * LOG2E
    w = jnp.where(np.stack(valid)[:, None, :], w, NEG_INF * LOG2E)
    flat = jnp.tile(w, (1, 1, blk))
    skew = flat[:, :, blk - 1:blk - 1 + blk * (n_rel - 1)].reshape(n_cfg, -1, blk, n_rel - 1)
    tab = skew[..., :2 * blk]
    return jnp.einsum('cpi,chik,cqk->chpq', np.stack(rows), tab, np.stack(cols),
                      precision=lax.Precision.HIGHEST)


def _attn_kernel(q_ref, k_ref, v_ref, bias_ref, o_ref, qp, kp, vp, m_sc, l_sc, acc_sc):
    blk = ATTN_BLOCK
    res = RESIDUES
    per = SEQ // res
    sq = res * res
    seg = sq // res
    lane = lax.broadcasted_iota(jnp.int32, (blk, LANES), 1)
    head0 = lane < HEAD_DIM
    ones = jnp.ones((2 * blk, LANES), BF16)

    pa = lax.broadcasted_iota(jnp.int32, (sq, sq), 0)
    pb = lax.broadcasted_iota(jnp.int32, (sq, sq), 1)
    regroup = jnp.where(pb == res * (pa % res) + pa // res, 1.0, 0.0).astype(BF16)

    def residue_rows(g):
        return [slice(per * r + seg * g, per * r + seg * (g + 1)) for r in range(res)]

    def load(ref, slices):
        return jnp.concatenate([ref[s, :] for s in slices], axis=0) if len(slices) > 1 \
            else ref[slices[0], :]

    def store(ref, slices, val):
        ln = val.shape[0] // len(slices)
        for i, s in enumerate(slices):
            ref[s, :] = val[i * ln:(i + 1) * ln]

    for g in range(SEQ // sq):
        rows = slice(sq * g, sq * (g + 1))
        qkv = jnp.concatenate([q_ref[rows, :], k_ref[rows, :], v_ref[rows, :]], axis=1)
        qkv = jnp.dot(regroup, qkv.astype(BF16), preferred_element_type=F32)
        store(qp, residue_rows(g), qkv[:, :LANES])
        store(kp, residue_rows(g), qkv[:, LANES:2 * LANES])
        store(vp, residue_rows(g), qkv[:, 2 * LANES:])

    def merge_heads(t):
        return jnp.where(head0, t[:blk], t[blk:])

    def scores(q, kcat, bias):
        zero = jnp.zeros_like(q)
        qs = jnp.concatenate([jnp.where(head0, q, zero), jnp.where(head0, zero, q)],
                             axis=0).astype(BF16)
        return lax.dot_general(qs, kcat, (((1,), (1,)), ((), ())),
                               preferred_element_type=F32) + bias

    def weighted_values(s, vcat):
        nk = vcat.shape[0]
        m = jnp.max(s, axis=-1, keepdims=True)
        p = jnp.exp2(s - m).astype(BF16)
        pv = jnp.dot(p, jnp.concatenate([vcat, ones[:nk]], axis=1), preferred_element_type=F32)
        return (merge_heads(jnp.broadcast_to(m, (2 * blk, LANES))),
                merge_heads(pv[:, LANES:]), merge_heads(pv[:, :LANES]))

    def update(ci, slices, m_c, l_c, o_c):
        if ci > 0:
            m_r = load(m_sc, slices)
            m_n = jnp.maximum(m_r, m_c)
            a = jnp.exp2(m_r - m_n)
            b = jnp.exp2(m_c - m_n)
            l_c = a * load(l_sc, slices) + b * l_c
            o_c = a * load(acc_sc, slices) + b * o_c
            m_c = m_n
        store(m_sc, slices, m_c)
        store(l_sc, slices, l_c)
        store(acc_sc, slices, o_c)

    blocks = []
    for ci, (window, d) in enumerate(DILATED_CONFIGS):
        sub = res // d
        ln = blk // sub
        for r in range(d):
            for n in range(SEQ // d // blk):
                slices = [slice(per * (r + d * c) + ln * n, per * (r + d * c) + ln * (n + 1))
                          for c in range(sub)]
                blocks.append((ci, slices, n == 0))

    kv_prev = [None, None]

    def score_stage(ci, slices, first):
        k_cur = load(kp, slices).astype(BF16)
        v_cur = load(vp, slices).astype(BF16)
        if first:
            bias = jnp.concatenate([bias_ref[ci, 0, :, blk:], bias_ref[ci, 1, :, blk:]], axis=0)
            out = scores(load(qp, slices), k_cur, bias), v_cur
        else:
            bias = jnp.concatenate([bias_ref[ci, 0], bias_ref[ci, 1]], axis=0)
            out = (scores(load(qp, slices), jnp.concatenate([kv_prev[0], k_cur], axis=0), bias),
                   jnp.concatenate([kv_prev[1], v_cur], axis=0))
        kv_prev[0], kv_prev[1] = k_cur, v_cur
        return out

    ahead = score_stage(*blocks[0])
    for i, (ci, slices, _) in enumerate(blocks):
        s, vcat = ahead
        if i + 1 < len(blocks):
            ahead = score_stage(*blocks[i + 1])
        update(ci, slices, *weighted_values(s, vcat))

    for g in range(SEQ // sq):
        o = load(acc_sc, residue_rows(g)) / load(l_sc, residue_rows(g))
        hi = o.astype(BF16)
        lo = (o - hi.astype(F32)).astype(BF16)
        back = jnp.dot(regroup, jnp.concatenate([hi, lo], axis=1), preferred_element_type=F32)
        o_ref[sq * g:sq * (g + 1), :] = back[:, :LANES] + back[:, LANES:]


def _attention(q, k, v, bias_tab):
    n_cfg = len(DILATED_CONFIGS)
    blk = ATTN_BLOCK
    seq_spec = pl.BlockSpec((SEQ, HEAD_PAIR), lambda b, p: (b, p))
    return pl.pallas_call(
        _attn_kernel,
        grid=(BATCH, N_PAIRS),
        in_specs=[seq_spec, seq_spec, seq_spec,
                  pl.BlockSpec((n_cfg, 2, blk, 2 * blk), lambda b, p: (0, p, 0, 0))],
        out_specs=seq_spec,
        out_shape=jax.ShapeDtypeStruct((TOKENS, B_WIDTH), F32),
        scratch_shapes=[pltpu.VMEM((SEQ, HEAD_PAIR), F32)] * 6,
        compiler_params=_cparams(("arbitrary", "arbitrary")),
        name="attention",
    )(q, k, v, bias_tab)


K4_TM = 1024
K4_PART = 256


def _top2_sum(a, b, c, d):
    hi1, lo1 = jnp.maximum(a, b), jnp.minimum(a, b)
    hi2, lo2 = jnp.maximum(c, d), jnp.minimum(c, d)
    return jnp.maximum(hi1, hi2) + jnp.maximum(jnp.minimum(hi1, hi2), jnp.maximum(lo1, lo2))


def _route(logits_t, rb_col):
    m = jnp.max(logits_t, axis=0, keepdims=True)
    e = jnp.exp(logits_t - m)
    probs = e / jnp.sum(e, axis=0, keepdims=True)
    sel = probs + rb_col
    sel_rows = [sel[i:i + 1, :] for i in range(N_EXPERTS)]
    prob_rows = [probs[i:i + 1, :] for i in range(N_EXPERTS)]
    gsz = EXPERTS_PER_GROUP
    score = [_top2_sum(*sel_rows[g * gsz:(g + 1) * gsz]) for g in range(N_EXPERT_GROUPS)]
    chosen = []
    for g in range(N_EXPERT_GROUPS):
        best = None
        for g2 in range(N_EXPERT_GROUPS):
            if g2 == g:
                continue
            c = (score[g] > score[g2]) if g2 < g else (score[g] >= score[g2])
            best = c if best is None else jnp.logical_and(best, c)
        for i in range(gsz):
            ei = g * gsz + i
            rank = jnp.zeros_like(sel_rows[ei])
            for j in range(gsz):
                if j == i:
                    continue
                ej = g * gsz + j
                ahead = (sel_rows[ej] >= sel_rows[ei]) if j < i else (sel_rows[ej] > sel_rows[ei])
                rank = rank + jnp.where(ahead, 1.0, 0.0)
            chosen.append(jnp.logical_and(best, rank < float(2)))
    picked = [jnp.where(chosen[i], prob_rows[i], 0.0) for i in range(N_EXPERTS)]
    denom = picked[0]
    for i in range(1, N_EXPERTS):
        denom = denom + picked[i]
    gates = jnp.concatenate([pk / denom for pk in picked], axis=0)
    onehot = jnp.concatenate([jnp.where(ch, 1.0, 0.0) for ch in chosen], axis=0)
    return gates, onehot


def _mid_kernel(a_ref, ob_ref, x_ref, mod_ref, gb_ref, wout_ref, n2g_ref, rwt_ref, rb_ref,
                x1_ref, h2_ref, gates_ref, onehot_ref, count_ref):
    for s in range(x_ref.shape[0] // K4_PART):
        rows = slice(s * K4_PART, (s + 1) * K4_PART)
        ob = ob_ref[rows, :]
        bn = ob * lax.rsqrt(jnp.mean(ob * ob, axis=-1, keepdims=True) + EPS) * gb_ref[...]
        mixed = jnp.dot(a_ref[rows, :], wout_ref[:A_WIDTH, :], preferred_element_type=F32)
        mixed = mixed + jnp.dot(bn.astype(BF16), wout_ref[A_WIDTH:, :],
                                preferred_element_type=F32)
        x1 = x_ref[rows, :] + mod_ref[2:3, :] * mixed
        x1_ref[rows, :] = x1
        h2 = x1 * lax.rsqrt(jnp.mean(x1 * x1, axis=-1, keepdims=True) + EPS) * n2g_ref[...]
        h2 = (h2 * (1.0 + mod_ref[4:5, :]) + mod_ref[3:4, :]).astype(BF16)
        h2_ref[rows, :] = h2
        logits_t = lax.dot_general(rwt_ref[...], h2, (((1,), (1,)), ((), ())),
                                   preferred_element_type=F32)
        gates_t, onehot_t = _route(logits_t, rb_ref[...])
        gates_ref[:, rows] = gates_t
        onehot_ref[:, rows] = onehot_t
        for t in range(K4_PART // MOE_TOK):
            count_ref[s * (K4_PART // MOE_TOK) + t] = jnp.sum(
                onehot_t[:, t * MOE_TOK:(t + 1) * MOE_TOK], axis=1, keepdims=True)


def _mid(out_a, out_b, x, layer, mod, gb, wout, n2g, rwt, rb_col):
    tm = K4_TM
    tok = pl.BlockSpec((tm, D_MODEL), lambda i: (i, 0))
    half = pl.BlockSpec((tm, A_WIDTH), lambda i: (i, 0))
    route = pl.BlockSpec((N_EXPERTS, tm), lambda i: (0, i))
    return pl.pallas_call(
        _mid_kernel,
        grid=(TOKENS // tm,),
        in_specs=[
            half, half, tok,
            _mod_of(layer, tm),
            _of_layer(layer, 1, B_WIDTH),
            _of_layer(layer, D_MODEL, D_MODEL),
            _of_layer(layer, 1, D_MODEL),
            pl.BlockSpec((N_EXPERTS, D_MODEL), lambda i: (0, 0)),
            pl.BlockSpec((N_EXPERTS, 1), lambda i: (0, 0)),
        ],
        out_specs=[tok, tok, route, route,
                   pl.BlockSpec((tm // MOE_TOK, N_EXPERTS, 1), lambda i: (i, 0, 0))],
        out_shape=[jax.ShapeDtypeStruct((TOKENS, D_MODEL), F32),
                   jax.ShapeDtypeStruct((TOKENS, D_MODEL), BF16),
                   jax.ShapeDtypeStruct((N_EXPERTS, TOKENS), F32),
                   jax.ShapeDtypeStruct((N_EXPERTS, TOKENS), F32),
                   jax.ShapeDtypeStruct((TOKENS // MOE_TOK, N_EXPERTS, 1), F32)],
        compiler_params=_cparams(("arbitrary",)),
        name="mid",
    )(out_a, out_b, x, mod, gb, wout, n2g, rwt, rb_col)


MOE_TOK = 256
ROW_ALIGN = 16
MOE_SUB = 4
MOE_LB = 2 * MOE_TOK + N_EXPERTS * ROW_ALIGN
FFN_TM = 512
N_TOK_TILES = TOKENS // MOE_TOK
ROWS_MAX = -(-(2 * TOKENS + N_TOK_TILES * N_EXPERTS * ROW_ALIGN
               + N_EXPERTS * (FFN_TM - ROW_ALIGN)) // FFN_TM) * FFN_TM


def _plan_kernel(count_ref, n8_ref, local_ref, glob_ref, seg_ref):
    def prefix(a, b):
        return jnp.dot(a, b, preferred_element_type=F32, precision=lax.Precision.HIGHEST)

    row = lax.broadcasted_iota(jnp.int32, (LANES, LANES), 0)
    col = lax.broadcasted_iota(jnp.int32, (LANES, LANES), 1)
    real = jnp.logical_and(row < N_TOK_TILES, col < N_EXPERTS)
    n8 = jnp.where(real, jnp.maximum(jnp.ceil(count_ref[...] / ROW_ALIGN), 1.0) * ROW_ALIGN, 0.0)
    earlier_lane = jnp.where(row < col, 1.0, 0.0)
    earlier_row = jnp.where(col < row, 1.0, 0.0)
    local = prefix(n8, earlier_lane)
    seg = jnp.ceil(jnp.sum(n8, axis=0, keepdims=True) / FFN_TM) * FFN_TM
    seg_start = prefix(jnp.broadcast_to(seg, (LANES, LANES)), earlier_lane)[0:1]
    glob = seg_start + prefix(earlier_row, n8)
    n8_ref[...] = n8.astype(jnp.int32)
    local_ref[...] = local.astype(jnp.int32)
    glob_ref[...] = glob.astype(jnp.int32)
    seg_ref[...] = jnp.where(row[:8] == 0, seg_start, seg / FFN_TM).astype(jnp.int32)


def _moe_plan(count):
    assert max(N_TOK_TILES, N_EXPERTS) <= LANES
    count = count.reshape(N_TOK_TILES, N_EXPERTS)
    count = jnp.pad(count, ((0, LANES - N_TOK_TILES), (0, LANES - N_EXPERTS)))
    full = jax.ShapeDtypeStruct((LANES, LANES), jnp.int32)
    n8, local, glob, seg = pl.pallas_call(
        _plan_kernel,
        out_shape=[full, full, full, jax.ShapeDtypeStruct((8, LANES), jnp.int32)],
        name="plan",
    )(count)

    def runs(t):
        return t[:N_TOK_TILES, :N_EXPERTS]

    return dict(n8=runs(n8).reshape(-1), local=runs(local).reshape(-1),
                glob=runs(glob).reshape(-1),
                local_col=runs(local).astype(F32).reshape(N_TOK_TILES, N_EXPERTS, 1),
                seg_start=seg[0, :N_EXPERTS], seg_tiles=seg[1, :N_EXPERTS])


def _run_copies(tile, n8_ref, local_ref, glob_ref, make_copy, start):
    for e in range(N_EXPERTS):
        idx = tile * N_EXPERTS + e
        cp = make_copy(pl.multiple_of(local_ref[idx], ROW_ALIGN),
                       pl.multiple_of(glob_ref[idx], ROW_ALIGN),
                       pl.multiple_of(n8_ref[idx], ROW_ALIGN))
        if start:
            cp.start()
        else:
            cp.wait()


def _fill_before(before_sc):
    @pl.when(pl.program_id(0) == 0)
    def _():
        src = lax.broadcasted_iota(jnp.int32, before_sc.shape, 0)
        dst = lax.broadcasted_iota(jnp.int32, before_sc.shape, 1)
        before_sc[...] = jnp.where(src < dst, 1.0, 0.0).astype(BF16)


def _sorted_rows(onehot, local_col, before):
    rank = jnp.dot(onehot.astype(BF16), before, preferred_element_type=F32)
    pos = local_col + rank
    chosen = onehot > 0.5
    lo = jnp.min(jnp.where(chosen, pos, float(MOE_LB)), axis=0, keepdims=True)
    hi = jnp.max(jnp.where(chosen, pos, -1.0), axis=0, keepdims=True)
    return pos, chosen, lo, hi


def _dispatch_kernel(n8_ref, local_ref, glob_ref, h_ref, oh_ref, lcol_ref, xs_ref,
                     lbuf, before_sc, sem):
    j = pl.program_id(0)
    last = pl.num_programs(0) - 1
    par = j % 2
    _fill_before(before_sc)
    row = lax.broadcasted_iota(jnp.int32, (MOE_LB, MOE_TOK), 0).astype(F32)
    for u in range(MOE_SUB):
        tok = slice(u * MOE_TOK, (u + 1) * MOE_TOK)
        _, _, lo, hi = _sorted_rows(oh_ref[:, tok], lcol_ref[u], before_sc[...])
        perm = (jnp.where(row == lo, 1.0, 0.0) + jnp.where(row == hi, 1.0, 0.0)).astype(BF16)
        lbuf[par * MOE_SUB + u] = jnp.dot(perm, h_ref[tok, :],
                                          preferred_element_type=F32).astype(BF16)

    def copy_from(slot):
        def make(lo_, go_, size):
            return pltpu.make_async_copy(lbuf.at[slot, pl.ds(lo_, size), :],
                                         xs_ref.at[pl.ds(go_, size), :], sem.at[slot])
        return make

    refs = (n8_ref, local_ref, glob_ref)
    for u in range(MOE_SUB):
        _run_copies(j * MOE_SUB + u, *refs, copy_from(par * MOE_SUB + u), True)

    @pl.when(j > 0)
    def _():
        for u in range(MOE_SUB):
            _run_copies((j - 1) * MOE_SUB + u, *refs, copy_from((1 - par) * MOE_SUB + u), False)

    @pl.when(j == last)
    def _():
        for u in range(MOE_SUB):
            _run_copies(j * MOE_SUB + u, *refs, copy_from(par * MOE_SUB + u), False)


def _dispatch(plan, h2, onehot):
    step_tok = MOE_SUB * MOE_TOK
    grid_spec = pltpu.PrefetchScalarGridSpec(
        num_scalar_prefetch=3,
        grid=(N_TOK_TILES // MOE_SUB,),
        in_specs=[
            pl.BlockSpec((step_tok, D_MODEL), lambda j, *_: (j, 0)),
            pl.BlockSpec((N_EXPERTS, step_tok), lambda j, *_: (0, j)),
            pl.BlockSpec((MOE_SUB, N_EXPERTS, 1), lambda j, *_: (j, 0, 0)),
        ],
        out_specs=pl.BlockSpec(memory_space=pl.ANY),
        scratch_shapes=[pltpu.VMEM((2 * MOE_SUB, MOE_LB, D_MODEL), BF16),
                        pltpu.VMEM((MOE_TOK, MOE_TOK), BF16),
                        pltpu.SemaphoreType.DMA((2 * MOE_SUB,))],
    )
    return pl.pallas_call(
        _dispatch_kernel,
        grid_spec=grid_spec,
        out_shape=jax.ShapeDtypeStruct((ROWS_MAX, D_MODEL), BF16),
        compiler_params=_cparams(("arbitrary",)),
        name="dispatch",
    )(plan["n8"], plan["local"], plan["glob"], h2, onehot, plan["local_col"])


def _experts_kernel(start_ref, tiles_ref, xs_ref, wg_ref, wu_ref, wd_ref, ys_ref,
                    wg_b, wu_b, wd_b, xbuf, ybuf, sem_in, sem_out):
    e = pl.program_id(0)
    n = tiles_ref[e]
    base = start_ref[e]

    def in_copy(expert_base, k, slot):
        rows = pl.ds(pl.multiple_of(expert_base + k * FFN_TM, FFN_TM), FFN_TM)
        return pltpu.make_async_copy(xs_ref.at[rows, :], xbuf.at[slot], sem_in.at[slot])

    def out_copy(k, slot):
        rows = pl.ds(pl.multiple_of(base + k * FFN_TM, FFN_TM), FFN_TM)
        return pltpu.make_async_copy(ybuf.at[slot], ys_ref.at[rows, :], sem_out.at[slot])

    @pl.when(e == 0)
    def _():
        in_copy(base, 0, 0).start()

    wg_b[...] = wg_ref[...].astype(BF16)
    wu_b[...] = wu_ref[...].astype(BF16)
    wd_b[...] = wd_ref[...].astype(BF16)

    def tile(k, carry):
        slot = k % 2
        in_copy(base, k, slot).wait()

        @pl.when(k + 1 < n)
        def _():
            in_copy(base, k + 1, 1 - slot).start()

        @pl.when(k >= 2)
        def _():
            out_copy(k - 2, slot).wait()

        x = xbuf[slot]
        hg = jnp.dot(x, wg_b[...], preferred_element_type=F32)
        hu = jnp.dot(x, wu_b[...], preferred_element_type=F32)
        act = (jax.nn.silu(hg) * hu).astype(BF16)
        ybuf[slot] = jnp.dot(act, wd_b[...], preferred_element_type=F32).astype(BF16)
        out_copy(k, slot).start()
        return carry

    lax.fori_loop(0, n, tile, 0)

    @pl.when(e + 1 < pl.num_programs(0))
    def _():
        in_copy(start_ref[jnp.minimum(e + 1, N_EXPERTS - 1)], 0, 0).start()

    @pl.when(n >= 2)
    def _():
        out_copy(n - 2, n % 2).wait()

    out_copy(n - 1, (n - 1) % 2).wait()


def _experts(plan, xs, layer, wg, wu, wd):
    def expert(e, start, tiles):
        return (layer, e, 0, 0)

    grid_spec = pltpu.PrefetchScalarGridSpec(
        num_scalar_prefetch=2,
        grid=(N_EXPERTS,),
        in_specs=[
            pl.BlockSpec(memory_space=pl.ANY),
            pl.BlockSpec((None, None, D_MODEL, D_EXPERT), expert),
            pl.BlockSpec((None, None, D_MODEL, D_EXPERT), expert),
            pl.BlockSpec((None, None, D_EXPERT, D_MODEL), expert),
        ],
        out_specs=pl.BlockSpec(memory_space=pl.ANY),
        scratch_shapes=[pltpu.VMEM((D_MODEL, D_EXPERT), BF16), pltpu.VMEM((D_MODEL, D_EXPERT), BF16),
                        pltpu.VMEM((D_EXPERT, D_MODEL), BF16),
                        pltpu.VMEM((2, FFN_TM, D_MODEL), BF16), pltpu.VMEM((2, FFN_TM, D_MODEL), BF16),
                        pltpu.SemaphoreType.DMA((2,)), pltpu.SemaphoreType.DMA((2,))],
    )
    return pl.pallas_call(
        _experts_kernel,
        grid_spec=grid_spec,
        out_shape=jax.ShapeDtypeStruct((ROWS_MAX, D_MODEL), BF16),
        compiler_params=_cparams(("arbitrary",)),
        name="experts",
    )(plan["seg_start"], plan["seg_tiles"], xs, wg, wu, wd)


def _combine_kernel(n8_ref, local_ref, glob_ref, oh_ref, g_ref, lcol_ref, x_ref, mod_ref, fg_ref,
                    ys_ref, o_ref, ybuf, before_sc, sem, *, final_norm):
    j = pl.program_id(0)
    n_steps = pl.num_programs(0)
    par = j % 2
    refs = (n8_ref, local_ref, glob_ref)
    _fill_before(before_sc)

    def copy_to(slot):
        def make(lo_, go_, size):
            return pltpu.make_async_copy(ys_ref.at[pl.ds(go_, size), :],
                                         ybuf.at[slot, pl.ds(lo_, size), :], sem.at[slot])
        return make

    @pl.when(j == 0)
    def _():
        ybuf[...] = jnp.zeros(ybuf.shape, BF16)
        for u in range(MOE_SUB):
            _run_copies(u, *refs, copy_to(u), True)

    @pl.when(j + 1 < n_steps)
    def _():
        for u in range(MOE_SUB):
            _run_copies((j + 1) * MOE_SUB + u, *refs, copy_to((1 - par) * MOE_SUB + u), True)

    row = lax.broadcasted_iota(jnp.int32, (MOE_LB, MOE_TOK), 0).astype(F32)
    for u in range(MOE_SUB):
        tok = slice(u * MOE_TOK, (u + 1) * MOE_TOK)
        pos, chosen, lo, hi = _sorted_rows(oh_ref[:, tok], lcol_ref[u], before_sc[...])
        gate = jnp.where(chosen, g_ref[:, tok], 0.0)
        g_lo = jnp.sum(jnp.where(pos == lo, gate, 0.0), axis=0, keepdims=True)
        g_hi = jnp.sum(jnp.where(pos == hi, gate, 0.0), axis=0, keepdims=True)
        perm = (jnp.where(row == lo, g_lo, 0.0) + jnp.where(row == hi, g_hi, 0.0)).astype(BF16)
        slot = par * MOE_SUB + u
        _run_copies(j * MOE_SUB + u, *refs, copy_to(slot), False)
        y = lax.dot_general(perm, ybuf[slot], (((0,), (0,)), ((), ())),
                            preferred_element_type=F32)
        x = x_ref[tok, :] + mod_ref[5:6, :] * y
        if final_norm:
            x = x * lax.rsqrt(jnp.mean(x * x, axis=-1, keepdims=True) + EPS) * fg_ref[...]
        o_ref[tok, :] = x


N_FRONT_IN = 8


def _combine_front_kernel(*refs):
    n_in = 3 + 7
    comb_in, front_in = refs[:n_in], refs[n_in:n_in + N_FRONT_IN]
    o_ref, *front_out = refs[n_in + N_FRONT_IN:n_in + N_FRONT_IN + 5]
    scratch = refs[n_in + N_FRONT_IN + 5:]
    _combine_kernel(*comb_in, o_ref, *scratch, final_norm=False)
    _front_kernel(o_ref, *front_in, *front_out)


def _combine(plan, onehot, gates, ys, x, layer, mod, final_g, front_params=None):
    step_tok = MOE_SUB * MOE_TOK
    route = pl.BlockSpec((N_EXPERTS, step_tok), lambda j, *_: (0, j))
    tok = pl.BlockSpec((step_tok, D_MODEL), lambda j, *_: (j, 0))
    in_specs = [
        route, route,
        pl.BlockSpec((MOE_SUB, N_EXPERTS, 1), lambda j, *_: (j, 0, 0)),
        tok, _mod_of(layer, step_tok),
        pl.BlockSpec((1, D_MODEL), lambda j, *_: (0, 0)),
        pl.BlockSpec(memory_space=pl.ANY),
    ]
    args = [onehot, gates, plan["local_col"], x, mod, final_g, ys]
    out_specs = tok
    out_shape = jax.ShapeDtypeStruct((TOKENS, D_MODEL), F32)
    body = functools.partial(_combine_kernel, final_norm=True)
    if front_params is not None:
        assert step_tok == K1_TM and len(front_params) == N_FRONT_IN
        half = pl.BlockSpec((step_tok, A_WIDTH), lambda j, *_: (j, 0))
        in_specs += _front_specs(layer + 1, step_tok)
        args += list(front_params)
        out_specs = [tok, half, half, half, half]
        out_shape = [out_shape, jax.ShapeDtypeStruct((TOKENS, A_WIDTH), BF16)] + \
                    [jax.ShapeDtypeStruct((TOKENS, B_WIDTH), F32)] * 3
        body = _combine_front_kernel
    grid_spec = pltpu.PrefetchScalarGridSpec(
        num_scalar_prefetch=3,
        grid=(N_TOK_TILES // MOE_SUB,),
        in_specs=in_specs,
        out_specs=out_specs,
        scratch_shapes=[pltpu.VMEM((2 * MOE_SUB, MOE_LB, D_MODEL), BF16),
                        pltpu.VMEM((MOE_TOK, MOE_TOK), BF16),
                        pltpu.SemaphoreType.DMA((2 * MOE_SUB,))],
    )
    return pl.pallas_call(
        body,
        grid_spec=grid_spec,
        out_shape=out_shape,
        compiler_params=_cparams(("arbitrary",)),
        name="combine",
    )(plan["n8"], plan["local"], plan["glob"], *args)


def kernel(x, c, rel_bias, router_w, router_b, mod_w, mod_b, norm1_g, w_in, gmlp_ln_g, gmlp_ln_b,
           gmlp_ws, gmlp_bs, out_norm_a_g, out_norm_b_g, w_out, norm2_g, moe_w_gate, moe_w_up,
           moe_w_down, final_g):
    mod = _modulation(c, mod_w, mod_b).reshape(DEPTH, BATCH, N_MOD, D_MODEL)
    bias_tab = _bias_tables(rel_bias)
    rwt = router_w.T.astype(BF16)
    rb_col = router_b.reshape(N_EXPERTS, 1)
    xt = x.reshape(TOKENS, D_MODEL)
    front_params = (
        mod, norm1_g.reshape(DEPTH, 1, D_MODEL), w_in.astype(BF16),
        gmlp_ln_g.reshape(DEPTH, 1, A_WIDTH), gmlp_ln_b.reshape(DEPTH, 1, A_WIDTH),
        gmlp_ws.astype(BF16).reshape(DEPTH, A_GROUPS // 2, 2 * CHUNK, CHUNK),
        jnp.repeat(jnp.swapaxes(gmlp_bs, 1, 2), HEAD_DIM, axis=2),
        out_norm_a_g.reshape(DEPTH, 1, A_WIDTH))
    gb = out_norm_b_g.reshape(DEPTH, 1, B_WIDTH)
    wout_b = w_out.astype(BF16)
    n2g = norm2_g.reshape(DEPTH, 1, D_MODEL)
    fg = final_g.reshape(1, D_MODEL)

    out_a, q, k, v = _front(xt, 0, front_params)
    for l in range(DEPTH):
        out_b = _attention(q, k, v, bias_tab)
        xt, h2, gates, onehot, count = _mid(out_a, out_b, xt, l, mod, gb, wout_b, n2g, rwt, rb_col)
        plan = _moe_plan(count)
        xs = _dispatch(plan, h2, onehot)
        ys = _experts(plan, xs, l, moe_w_gate, moe_w_up, moe_w_down)
        if l + 1 < DEPTH:
            xt, out_a, q, k, v = _combine(plan, onehot, gates, ys, xt, l, mod, fg, front_params)
        else:
            xt = _combine(plan, onehot, gates, ys, xt, l, mod, fg)
    return xt.reshape(BATCH, SEQ, D_MODEL)
```

```python
import functools
import math

import numpy as np
import jax
import jax.numpy as jnp
from jax import lax
from jax.experimental import pallas as pl
from jax.experimental.pallas import tpu as pltpu

D_MODEL = 1024
BATCH = 8
SEQ = 2048
DEPTH = 2
TOKENS = BATCH * SEQ
HEAD_DIM = 64
A_WIDTH = 512
B_WIDTH = 512
A_GROUPS = 8
IN_COLS = 2 * A_WIDTH + 3 * B_WIDTH
CHUNK = 128
DILATED_CONFIGS = ((128, 1), (512, 4), (2048, 16))
ATTN_BLOCK = 128
REL_BUCKETS = 32
REL_MAX_EXACT = REL_BUCKETS // 2
REL_MAX_DISTANCE = 2048
N_EXPERTS = 16
N_EXPERT_GROUPS = 4
EXPERTS_PER_GROUP = 4
D_EXPERT = 512
N_MOD = 6
EPS = 1e-6
NEG_INF = -1e30

LANES = 128
HEAD_PAIR = 2 * HEAD_DIM
N_PAIRS = B_WIDTH // HEAD_PAIR
RESIDUES = max(d for _, d in DILATED_CONFIGS)
LOG2E = math.log2(math.e)

F32 = jnp.float32
BF16 = jnp.bfloat16

VMEM_LIMIT = 56 * 1024 * 1024


def _cparams(sem):
    return pltpu.CompilerParams(dimension_semantics=sem, vmem_limit_bytes=VMEM_LIMIT)


def _gelu(x):
    return 0.5 * x * (1.0 + lax.erf(x * math.sqrt(0.5)))


MOD_TN = 1024


def _mod_kernel(c_ref, w_ref, b_ref, o_ref):
    ca = jax.nn.silu(c_ref[...])
    o_ref[...] = jnp.dot(ca.astype(BF16), w_ref[...].astype(BF16),
                         preferred_element_type=F32) + b_ref[...]


def _modulation(c, mod_w, mod_b):
    n_cols = N_MOD * D_MODEL
    return pl.pallas_call(
        _mod_kernel,
        grid=(DEPTH, n_cols // MOD_TN),
        in_specs=[
            pl.BlockSpec((BATCH, D_MODEL), lambda l, j: (0, 0)),
            pl.BlockSpec((None, D_MODEL, MOD_TN), lambda l, j: (l, 0, j)),
            pl.BlockSpec((None, 1, MOD_TN), lambda l, j: (l, 0, j)),
        ],
        out_specs=pl.BlockSpec((None, BATCH, MOD_TN), lambda l, j: (l, 0, j)),
        out_shape=jax.ShapeDtypeStruct((DEPTH, BATCH, n_cols), F32),
        compiler_params=_cparams(("arbitrary", "arbitrary")),
        name="modulation",
    )(c, mod_w, mod_b.reshape(DEPTH, 1, n_cols))


K1_TM = 1024


def _front_kernel(x_ref, mod_ref, n1g_ref, win_ref, lng_ref, lnb_ref, ws_ref, bs_ref, ga_ref,
                  a_ref, q_ref, k_ref, v_ref):
    x = x_ref[...]
    tm = x.shape[0]
    h = x * lax.rsqrt(jnp.mean(x * x, axis=-1, keepdims=True) + EPS) * n1g_ref[...]
    h = h * (1.0 + mod_ref[1:2, :]) + mod_ref[0:1, :]
    proj = jnp.dot(h.astype(BF16), win_ref[...], preferred_element_type=F32)

    q_ref[...] = proj[:, 2 * A_WIDTH:2 * A_WIDTH + B_WIDTH] * (HEAD_DIM ** -0.5 * LOG2E)
    k_ref[...] = proj[:, 2 * A_WIDTH + B_WIDTH:2 * A_WIDTH + 2 * B_WIDTH]
    v_ref[...] = proj[:, 2 * A_WIDTH + 2 * B_WIDTH:]

    u = _gelu(proj[:, :A_WIDTH])
    va = _gelu(proj[:, A_WIDTH:2 * A_WIDTH])
    mu = jnp.mean(va, axis=-1, keepdims=True)
    vc = va - mu
    vln = vc * lax.rsqrt(jnp.mean(vc * vc, axis=-1, keepdims=True) + EPS)
    vln = (vln * lng_ref[...] + lnb_ref[...]).astype(BF16)

    row = lax.broadcasted_iota(jnp.int32, (2 * CHUNK, CHUNK), 0)
    col = lax.broadcasted_iota(jnp.int32, (2 * CHUNK, CHUNK), 1)
    causal = (row % CHUNK) >= col
    first_group = lax.broadcasted_iota(jnp.int32, (CHUNK, LANES), 1) < HEAD_DIM
    wmix = [jnp.where(causal, ws_ref[p], jnp.zeros((), BF16)) for p in range(A_GROUPS // 2)]
    for c in range(tm // CHUNK):
        rows = slice(c * CHUNK, (c + 1) * CHUNK)
        parts = []
        for p in range(A_GROUPS // 2):
            vp = vln[rows, p * LANES:(p + 1) * LANES]
            r = jnp.dot(wmix[p], vp, preferred_element_type=F32)
            parts.append(jnp.where(first_group, r[:CHUNK], r[CHUNK:]))
        s = jnp.concatenate(parts, axis=-1) + bs_ref[...]
        oa = u[rows] * s
        oa = oa * lax.rsqrt(jnp.mean(oa * oa, axis=-1, keepdims=True) + EPS) * ga_ref[...]
        a_ref[rows, :] = oa.astype(BF16)


def _of_layer(layer, *tail):
    return pl.BlockSpec((None, *tail), lambda *_: (layer,) + (0,) * len(tail))


def _mod_of(layer, tokens_per_step):
    steps_per_seq = SEQ // tokens_per_step
    return pl.BlockSpec((None, None, N_MOD, D_MODEL),
                        lambda i, *_: (layer, i // steps_per_seq, 0, 0))


def _front_specs(layer, tm):
    return [
        _mod_of(layer, tm),
        _of_layer(layer, 1, D_MODEL),
        _of_layer(layer, D_MODEL, IN_COLS),
        _of_layer(layer, 1, A_WIDTH), _of_layer(layer, 1, A_WIDTH),
        _of_layer(layer, A_GROUPS // 2, 2 * CHUNK, CHUNK),
        _of_layer(layer, CHUNK, A_WIDTH),
        _of_layer(layer, 1, A_WIDTH),
    ]


def _front(x, layer, front_params):
    tm = K1_TM
    half = pl.BlockSpec((tm, A_WIDTH), lambda i: (i, 0))
    return pl.pallas_call(
        _front_kernel,
        grid=(TOKENS // tm,),
        in_specs=[pl.BlockSpec((tm, D_MODEL), lambda i: (i, 0))] + _front_specs(layer, tm),
        out_specs=[half, half, half, half],
        out_shape=[jax.ShapeDtypeStruct((TOKENS, A_WIDTH), BF16)] +
                  [jax.ShapeDtypeStruct((TOKENS, B_WIDTH), F32)] * 3,
        compiler_params=_cparams(("arbitrary",)),
        name="front",
    )(x, *front_params)


def _t5_bucket_np(dist):
    dist = np.maximum(dist, 0)
    ratio = np.log(np.maximum(dist, 1) / REL_MAX_EXACT) / np.log(REL_MAX_DISTANCE / REL_MAX_EXACT)
    large = REL_MAX_EXACT + np.floor(ratio * (REL_BUCKETS - REL_MAX_EXACT)).astype(np.int64)
    large = np.minimum(large, REL_BUCKETS - 1)
    return np.where(dist < REL_MAX_EXACT, dist, large).astype(np.int32)


def _bias_tables(rel_bias):
    blk = ATTN_BLOCK
    n_cfg = len(DILATED_CONFIGS)
    n_rel = 3 * blk
    rel = 2 * blk - 1 - np.arange(n_rel)
    bucket, valid, rows, cols = [], [], [], []
    for window, d in DILATED_CONFIGS:
        span = window // d
        bucket.append(_t5_bucket_np(np.clip(rel, 0, span) * d))
        valid.append((rel >= 0) & (rel <= span))
        sub = RESIDUES // d
        ln = blk // sub
        pos = np.arange(blk)
        rows.append(np.eye(blk, dtype=np.float32)[(pos % ln) * sub + pos // ln])
        cols.append(np.kron(np.eye(2, dtype=np.float32), rows[-1]))
    w = jnp.transpose(rel_bias.astype(F32)[np.stack(bucket)], (0, 2, 1)) * LOG2E
    w = jnp.where(np.stack(valid)[:, None, :], w, NEG_INF * LOG2E)
    flat = jnp.tile(w, (1, 1, blk))
    skew = flat[:, :, blk - 1:blk - 1 + blk * (n_rel - 1)].reshape(n_cfg, -1, blk, n_rel - 1)
    tab = skew[..., :2 * blk]
    return jnp.einsum('cpi,chik,cqk->chpq', np.stack(rows), tab, np.stack(cols),
                      precision=lax.Precision.HIGHEST)


def _attn_kernel(q_ref, k_ref, v_ref, bias_ref, o_ref, qp, kp, vp, m_sc, l_sc, acc_sc):
    blk = ATTN_BLOCK
    res = RESIDUES
    per = SEQ // res
    sq = res * res
    seg = sq // res
    lane = lax.broadcasted_iota(jnp.int32, (blk, LANES), 1)
    head0 = lane < HEAD_DIM
    ones = jnp.ones((2 * blk, LANES), BF16)

    pa = lax.broadcasted_iota(jnp.int32, (sq, sq), 0)
    pb = lax.broadcasted_iota(jnp.int32, (sq, sq), 1)
    regroup = jnp.where(pb == res * (pa % res) + pa // res, 1.0, 0.0).astype(BF16)

    def residue_rows(g):
        return [slice(per * r + seg * g, per * r + seg * (g + 1)) for r in range(res)]

    def load(ref, slices):
        return jnp.concatenate([ref[s, :] for s in slices], axis=0) if len(slices) > 1 \
            else ref[slices[0], :]

    def store(ref, slices, val):
        ln = val.shape[0] // len(slices)
        for i, s in enumerate(slices):
            ref[s, :] = val[i * ln:(i + 1) * ln]

    for g in range(SEQ // sq):
        rows = slice(sq * g, sq * (g + 1))
        qkv = jnp.concatenate([q_ref[rows, :], k_ref[rows, :], v_ref[rows, :]], axis=1)
        qkv = jnp.dot(regroup, qkv.astype(BF16), preferred_element_type=F32)
        store(qp, residue_rows(g), qkv[:, :LANES])
        store(kp, residue_rows(g), qkv[:, LANES:2 * LANES])
        store(vp, residue_rows(g), qkv[:, 2 * LANES:])

    def merge_heads(t):
        return jnp.where(head0, t[:blk], t[blk:])

    def scores(q, kcat, bias):
        zero = jnp.zeros_like(q)
        qs = jnp.concatenate([jnp.where(head0, q, zero), jnp.where(head0, zero, q)],
                             axis=0).astype(BF16)
        return lax.dot_general(qs, kcat, (((1,), (1,)), ((), ())),
                               preferred_element_type=F32) + bias

    def weighted_values(s, vcat):
        nk = vcat.shape[0]
        m = jnp.max(s, axis=-1, keepdims=True)
        p = jnp.exp2(s - m).astype(BF16)
        pv = jnp.dot(p, jnp.concatenate([vcat, ones[:nk]], axis=1), preferred_element_type=F32)
        return (merge_heads(jnp.broadcast_to(m, (2 * blk, LANES))),
                merge_heads(pv[:, LANES:]), merge_heads(pv[:, :LANES]))

    def update(ci, slices, m_c, l_c, o_c):
        if ci > 0:
            m_r = load(m_sc, slices)
            m_n = jnp.maximum(m_r, m_c)
            a = jnp.exp2(m_r - m_n)
            b = jnp.exp2(m_c - m_n)
            l_c = a * load(l_sc, slices) + b * l_c
            o_c = a * load(acc_sc, slices) + b * o_c
            m_c = m_n
        store(m_sc, slices, m_c)
        store(l_sc, slices, l_c)
        store(acc_sc, slices, o_c)

    blocks = []
    for ci, (window, d) in enumerate(DILATED_CONFIGS):
        sub = res // d
        ln = blk // sub
        for r in range(d):
            for n in range(SEQ // d // blk):
                slices = [slice(per * (r + d * c) + ln * n, per * (r + d * c) + ln * (n + 1))
                          for c in range(sub)]
                blocks.append((ci, slices, n == 0))

    kv_prev = [None, None]

    def score_stage(ci, slices, first):
        k_cur = load(kp, slices).astype(BF16)
        v_cur = load(vp, slices).astype(BF16)
        if first:
            bias = jnp.concatenate([bias_ref[ci, 0, :, blk:], bias_ref[ci, 1, :, blk:]], axis=0)
            out = scores(load(qp, slices), k_cur, bias), v_cur
        else:
            bias = jnp.concatenate([bias_ref[ci, 0], bias_ref[ci, 1]], axis=0)
            out = (scores(load(qp, slices), jnp.concatenate([kv_prev[0], k_cur], axis=0), bias),
                   jnp.concatenate([kv_prev[1], v_cur], axis=0))
        kv_prev[0], kv_prev[1] = k_cur, v_cur
        return out

    ahead = score_stage(*blocks[0])
    for i, (ci, slices, _) in enumerate(blocks):
        s, vcat = ahead
        if i + 1 < len(blocks):
            ahead = score_stage(*blocks[i + 1])
        update(ci, slices, *weighted_values(s, vcat))

    for g in range(SEQ // sq):
        o = load(acc_sc, residue_rows(g)) / load(l_sc, residue_rows(g))
        hi = o.astype(BF16)
        lo = (o - hi.astype(F32)).astype(BF16)
        back = jnp.dot(regroup, jnp.concatenate([hi, lo], axis=1), preferred_element_type=F32)
        o_ref[sq * g:sq * (g + 1), :] = back[:, :LANES] + back[:, LANES:]


def _attention(q, k, v, bias_tab):
    n_cfg = len(DILATED_CONFIGS)
    blk = ATTN_BLOCK
    seq_spec = pl.BlockSpec((SEQ, HEAD_PAIR), lambda b, p: (b, p))
    return pl.pallas_call(
        _attn_kernel,
        grid=(BATCH, N_PAIRS),
        in_specs=[seq_spec, seq_spec, seq_spec,
                  pl.BlockSpec((n_cfg, 2, blk, 2 * blk), lambda b, p: (0, p, 0, 0))],
        out_specs=seq_spec,
        out_shape=jax.ShapeDtypeStruct((TOKENS, B_WIDTH), F32),
        scratch_shapes=[pltpu.VMEM((SEQ, HEAD_PAIR), F32)] * 6,
        compiler_params=_cparams(("arbitrary", "arbitrary")),
        name="attention",
    )(q, k, v, bias_tab)


K4_TM = 1024
K4_PART = 256


def _top2_sum(a, b, c, d):
    hi1, lo1 = jnp.maximum(a, b), jnp.minimum(a, b)
    hi2, lo2 = jnp.maximum(c, d), jnp.minimum(c, d)
    return jnp.maximum(hi1, hi2) + jnp.maximum(jnp.minimum(hi1, hi2), jnp.maximum(lo1, lo2))


def _route(logits_t, rb_col):
    m = jnp.max(logits_t, axis=0, keepdims=True)
    e = jnp.exp(logits_t - m)
    probs = e / jnp.sum(e, axis=0, keepdims=True)
    sel = probs + rb_col
    sel_rows = [sel[i:i + 1, :] for i in range(N_EXPERTS)]
    prob_rows = [probs[i:i + 1, :] for i in range(N_EXPERTS)]
    gsz = EXPERTS_PER_GROUP
    score = [_top2_sum(*sel_rows[g * gsz:(g + 1) * gsz]) for g in range(N_EXPERT_GROUPS)]
    chosen = []
    for g in range(N_EXPERT_GROUPS):
        best = None
        for g2 in range(N_EXPERT_GROUPS):
            if g2 == g:
                continue
            c = (score[g] > score[g2]) if g2 < g else (score[g] >= score[g2])
            best = c if best is None else jnp.logical_and(best, c)
        for i in range(gsz):
            ei = g * gsz + i
            rank = jnp.zeros_like(sel_rows[ei])
            for j in range(gsz):
                if j == i:
                    continue
                ej = g * gsz + j
                ahead = (sel_rows[ej] >= sel_rows[ei]) if j < i else (sel_rows[ej] > sel_rows[ei])
                rank = rank + jnp.where(ahead, 1.0, 0.0)
            chosen.append(jnp.logical_and(best, rank < float(2)))
    picked = [jnp.where(chosen[i], prob_rows[i], 0.0) for i in range(N_EXPERTS)]
    denom = picked[0]
    for i in range(1, N_EXPERTS):
        denom = denom + picked[i]
    gates = jnp.concatenate([pk / denom for pk in picked], axis=0)
    onehot = jnp.concatenate([jnp.where(ch, 1.0, 0.0) for ch in chosen], axis=0)
    return gates, onehot


def _mid_kernel(a_ref, ob_ref, x_ref, mod_ref, gb_ref, wout_ref, n2g_ref, rwt_ref, rb_ref,
                x1_ref, h2_ref, gates_ref, onehot_ref, count_ref):
    for s in range(x_ref.shape[0] // K4_PART):
        rows = slice(s * K4_PART, (s + 1) * K4_PART)
        ob = ob_ref[rows, :]
        bn = ob * lax.rsqrt(jnp.mean(ob * ob, axis=-1, keepdims=True) + EPS) * gb_ref[...]
        mixed = jnp.dot(a_ref[rows, :], wout_ref[:A_WIDTH, :], preferred_element_type=F32)
        mixed = mixed + jnp.dot(bn.astype(BF16), wout_ref[A_WIDTH:, :],
                                preferred_element_type=F32)
        x1 = x_ref[rows, :] + mod_ref[2:3, :] * mixed
        x1_ref[rows, :] = x1
        h2 = x1 * lax.rsqrt(jnp.mean(x1 * x1, axis=-1, keepdims=True) + EPS) * n2g_ref[...]
        h2 = (h2 * (1.0 + mod_ref[4:5, :]) + mod_ref[3:4, :]).astype(BF16)
        h2_ref[rows, :] = h2
        logits_t = lax.dot_general(rwt_ref[...], h2, (((1,), (1,)), ((), ())),
                                   preferred_element_type=F32)
        gates_t, onehot_t = _route(logits_t, rb_ref[...])
        gates_ref[:, rows] = gates_t
        onehot_ref[:, rows] = onehot_t
        for t in range(K4_PART // MOE_TOK):
            count_ref[s * (K4_PART // MOE_TOK) + t] = jnp.sum(
                onehot_t[:, t * MOE_TOK:(t + 1) * MOE_TOK], axis=1, keepdims=True)


def _mid(out_a, out_b, x, layer, mod, gb, wout, n2g, rwt, rb_col):
    tm = K4_TM
    tok = pl.BlockSpec((tm, D_MODEL), lambda i: (i, 0))
    half = pl.BlockSpec((tm, A_WIDTH), lambda i: (i, 0))
    route = pl.BlockSpec((N_EXPERTS, tm), lambda i: (0, i))
    return pl.pallas_call(
        _mid_kernel,
        grid=(TOKENS // tm,),
        in_specs=[
            half, half, tok,
            _mod_of(layer, tm),
            _of_layer(layer, 1, B_WIDTH),
            _of_layer(layer, D_MODEL, D_MODEL),
            _of_layer(layer, 1, D_MODEL),
            pl.BlockSpec((N_EXPERTS, D_MODEL), lambda i: (0, 0)),
            pl.BlockSpec((N_EXPERTS, 1), lambda i: (0, 0)),
        ],
        out_specs=[tok, tok, route, route,
                   pl.BlockSpec((tm // MOE_TOK, N_EXPERTS, 1), lambda i: (i, 0, 0))],
        out_shape=[jax.ShapeDtypeStruct((TOKENS, D_MODEL), F32),
                   jax.ShapeDtypeStruct((TOKENS, D_MODEL), BF16),
                   jax.ShapeDtypeStruct((N_EXPERTS, TOKENS), F32),
                   jax.ShapeDtypeStruct((N_EXPERTS, TOKENS), F32),
                   jax.ShapeDtypeStruct((TOKENS // MOE_TOK, N_EXPERTS, 1), F32)],
        compiler_params=_cparams(("arbitrary",)),
        name="mid",
    )(out_a, out_b, x, mod, gb, wout, n2g, rwt, rb_col)


MOE_TOK = 256
ROW_ALIGN = 16
MOE_SUB = 4
MOE_LB = 2 * MOE_TOK + N_EXPERTS * ROW_ALIGN
FFN_TM = 1024
N_TOK_TILES = TOKENS // MOE_TOK
ROWS_MAX = -(-(2 * TOKENS + N_TOK_TILES * N_EXPERTS * ROW_ALIGN
               + N_EXPERTS * (FFN_TM - ROW_ALIGN)) // FFN_TM) * FFN_TM


def _plan_kernel(count_ref, n8_ref, local_ref, glob_ref, seg_ref):
    def prefix(a, b):
        return jnp.dot(a, b, preferred_element_type=F32, precision=lax.Precision.HIGHEST)

    row = lax.broadcasted_iota(jnp.int32, (LANES, LANES), 0)
    col = lax.broadcasted_iota(jnp.int32, (LANES, LANES), 1)
    real = jnp.logical_and(row < N_TOK_TILES, col < N_EXPERTS)
    n8 = jnp.where(real, jnp.maximum(jnp.ceil(count_ref[...] / ROW_ALIGN), 1.0) * ROW_ALIGN, 0.0)
    earlier_lane = jnp.where(row < col, 1.0, 0.0)
    earlier_row = jnp.where(col < row, 1.0, 0.0)
    local = prefix(n8, earlier_lane)
    seg = jnp.ceil(jnp.sum(n8, axis=0, keepdims=True) / FFN_TM) * FFN_TM
    seg_start = prefix(jnp.broadcast_to(seg, (LANES, LANES)), earlier_lane)[0:1]
    glob = seg_start + prefix(earlier_row, n8)
    n8_ref[...] = n8.astype(jnp.int32)
    local_ref[...] = local.astype(jnp.int32)
    glob_ref[...] = glob.astype(jnp.int32)
    seg_ref[...] = jnp.where(row[:8] == 0, seg_start, seg / FFN_TM).astype(jnp.int32)


def _moe_plan(count):
    assert max(N_TOK_TILES, N_EXPERTS) <= LANES
    count = count.reshape(N_TOK_TILES, N_EXPERTS)
    count = jnp.pad(count, ((0, LANES - N_TOK_TILES), (0, LANES - N_EXPERTS)))
    full = jax.ShapeDtypeStruct((LANES, LANES), jnp.int32)
    n8, local, glob, seg = pl.pallas_call(
        _plan_kernel,
        out_shape=[full, full, full, jax.ShapeDtypeStruct((8, LANES), jnp.int32)],
        name="plan",
    )(count)

    def runs(t):
        return t[:N_TOK_TILES, :N_EXPERTS]

    return dict(n8=runs(n8).reshape(-1), local=runs(local).reshape(-1),
                glob=runs(glob).reshape(-1),
                local_col=runs(local).astype(F32).reshape(N_TOK_TILES, N_EXPERTS, 1),
                seg_start=seg[0, :N_EXPERTS], seg_tiles=seg[1, :N_EXPERTS])


def _run_copies(tile, n8_ref, local_ref, glob_ref, make_copy, start):
    for e in range(N_EXPERTS):
        idx = tile * N_EXPERTS + e
        cp = make_copy(pl.multiple_of(local_ref[idx], ROW_ALIGN),
                       pl.multiple_of(glob_ref[idx], ROW_ALIGN),
                       pl.multiple_of(n8_ref[idx], ROW_ALIGN))
        if start:
            cp.start()
        else:
            cp.wait()


def _fill_before(before_sc):
    @pl.when(pl.program_id(0) == 0)
    def _():
        src = lax.broadcasted_iota(jnp.int32, before_sc.shape, 0)
        dst = lax.broadcasted_iota(jnp.int32, before_sc.shape, 1)
        before_sc[...] = jnp.where(src < dst, 1.0, 0.0).astype(BF16)


def _sorted_rows(onehot, local_col, before):
    rank = jnp.dot(onehot.astype(BF16), before, preferred_element_type=F32)
    pos = local_col + rank
    chosen = onehot > 0.5
    lo = jnp.min(jnp.where(chosen, pos, float(MOE_LB)), axis=0, keepdims=True)
    hi = jnp.max(jnp.where(chosen, pos, -1.0), axis=0, keepdims=True)
    return pos, chosen, lo, hi


def _dispatch_kernel(n8_ref, local_ref, glob_ref, h_ref, oh_ref, lcol_ref, xs_ref,
                     lbuf, before_sc, sem):
    j = pl.program_id(0)
    last = pl.num_programs(0) - 1
    par = j % 2
    _fill_before(before_sc)
    row = lax.broadcasted_iota(jnp.int32, (MOE_LB, MOE_TOK), 0).astype(F32)
    for u in range(MOE_SUB):
        tok = slice(u * MOE_TOK, (u + 1) * MOE_TOK)
        _, _, lo, hi = _sorted_rows(oh_ref[:, tok], lcol_ref[u], before_sc[...])
        perm = (jnp.where(row == lo, 1.0, 0.0) + jnp.where(row == hi, 1.0, 0.0)).astype(BF16)
        lbuf[par * MOE_SUB + u] = jnp.dot(perm, h_ref[tok, :],
                                          preferred_element_type=F32).astype(BF16)

    def copy_from(slot):
        def make(lo_, go_, size):
            return pltpu.make_async_copy(lbuf.at[slot, pl.ds(lo_, size), :],
                                         xs_ref.at[pl.ds(go_, size), :], sem.at[slot])
        return make

    refs = (n8_ref, local_ref, glob_ref)
    for u in range(MOE_SUB):
        _run_copies(j * MOE_SUB + u, *refs, copy_from(par * MOE_SUB + u), True)

    @pl.when(j > 0)
    def _():
        for u in range(MOE_SUB):
            _run_copies((j - 1) * MOE_SUB + u, *refs, copy_from((1 - par) * MOE_SUB + u), False)

    @pl.when(j == last)
    def _():
        for u in range(MOE_SUB):
            _run_copies(j * MOE_SUB + u, *refs, copy_from(par * MOE_SUB + u), False)


def _dispatch(plan, h2, onehot):
    step_tok = MOE_SUB * MOE_TOK
    grid_spec = pltpu.PrefetchScalarGridSpec(
        num_scalar_prefetch=3,
        grid=(N_TOK_TILES // MOE_SUB,),
        in_specs=[
            pl.BlockSpec((step_tok, D_MODEL), lambda j, *_: (j, 0)),
            pl.BlockSpec((N_EXPERTS, step_tok), lambda j, *_: (0, j)),
            pl.BlockSpec((MOE_SUB, N_EXPERTS, 1), lambda j, *_: (j, 0, 0)),
        ],
        out_specs=pl.BlockSpec(memory_space=pl.ANY),
        scratch_shapes=[pltpu.VMEM((2 * MOE_SUB, MOE_LB, D_MODEL), BF16),
                        pltpu.VMEM((MOE_TOK, MOE_TOK), BF16),
                        pltpu.SemaphoreType.DMA((2 * MOE_SUB,))],
    )
    return pl.pallas_call(
        _dispatch_kernel,
        grid_spec=grid_spec,
        out_shape=jax.ShapeDtypeStruct((ROWS_MAX, D_MODEL), BF16),
        compiler_params=_cparams(("arbitrary",)),
        name="dispatch",
    )(plan["n8"], plan["local"], plan["glob"], h2, onehot, plan["local_col"])


def _experts_kernel(start_ref, tiles_ref, xs_ref, wg_ref, wu_ref, wd_ref, ys_ref,
                    wgu_b, wd_b, xbuf, ybuf, sem_in, sem_out):
    e = pl.program_id(0)
    n = tiles_ref[e]
    base = start_ref[e]

    def in_copy(expert_base, k, slot):
        rows = pl.ds(pl.multiple_of(expert_base + k * FFN_TM, FFN_TM), FFN_TM)
        return pltpu.make_async_copy(xs_ref.at[rows, :], xbuf.at[slot], sem_in.at[slot])

    def out_copy(k, slot):
        rows = pl.ds(pl.multiple_of(base + k * FFN_TM, FFN_TM), FFN_TM)
        return pltpu.make_async_copy(ybuf.at[slot], ys_ref.at[rows, :], sem_out.at[slot])

    @pl.when(e == 0)
    def _():
        in_copy(base, 0, 0).start()

    wgu_b[:, :D_EXPERT] = wg_ref[...].astype(BF16)
    wgu_b[:, D_EXPERT:] = wu_ref[...].astype(BF16)
    wd_b[...] = wd_ref[...].astype(BF16)

    def tile(k, carry):
        slot = k % 2
        in_copy(base, k, slot).wait()

        @pl.when(k + 1 < n)
        def _():
            in_copy(base, k + 1, 1 - slot).start()

        @pl.when(k >= 2)
        def _():
            out_copy(k - 2, slot).wait()

        x = xbuf[slot]
        hgu = jnp.dot(x, wgu_b[...], preferred_element_type=F32)
        act = (jax.nn.silu(hgu[:, :D_EXPERT]) * hgu[:, D_EXPERT:]).astype(BF16)
        ybuf[slot] = jnp.dot(act, wd_b[...], preferred_element_type=F32).astype(BF16)
        out_copy(k, slot).start()
        return carry

    lax.fori_loop(0, n, tile, 0)

    @pl.when(e + 1 < pl.num_programs(0))
    def _():
        in_copy(start_ref[jnp.minimum(e + 1, N_EXPERTS - 1)], 0, 0).start()

    @pl.when(n >= 2)
    def _():
        out_copy(n - 2, n % 2).wait()

    out_copy(n - 1, (n - 1) % 2).wait()


def _experts(plan, xs, layer, wg, wu, wd):
    def expert(e, start, tiles):
        return (layer, e, 0, 0)

    grid_spec = pltpu.PrefetchScalarGridSpec(
        num_scalar_prefetch=2,
        grid=(N_EXPERTS,),
        in_specs=[
            pl.BlockSpec(memory_space=pl.ANY),
            pl.BlockSpec((None, None, D_MODEL, D_EXPERT), expert),
            pl.BlockSpec((None, None, D_MODEL, D_EXPERT), expert),
            pl.BlockSpec((None, None, D_EXPERT, D_MODEL), expert),
        ],
        out_specs=pl.BlockSpec(memory_space=pl.ANY),
        scratch_shapes=[pltpu.VMEM((D_MODEL, 2 * D_EXPERT), BF16),
                        pltpu.VMEM((D_EXPERT, D_MODEL), BF16),
                        pltpu.VMEM((2, FFN_TM, D_MODEL), BF16), pltpu.VMEM((2, FFN_TM, D_MODEL), BF16),
                        pltpu.SemaphoreType.DMA((2,)), pltpu.SemaphoreType.DMA((2,))],
    )
    return pl.pallas_call(
        _experts_kernel,
        grid_spec=grid_spec,
        out_shape=jax.ShapeDtypeStruct((ROWS_MAX, D_MODEL), BF16),
        compiler_params=_cparams(("arbitrary",)),
        name="experts",
    )(plan["seg_start"], plan["seg_tiles"], xs, wg, wu, wd)


def _combine_kernel(n8_ref, local_ref, glob_ref, oh_ref, g_ref, lcol_ref, x_ref, mod_ref, fg_ref,
                    ys_ref, o_ref, ybuf, before_sc, sem, *, final_norm):
    j = pl.program_id(0)
    n_steps = pl.num_programs(0)
    par = j % 2
    refs = (n8_ref, local_ref, glob_ref)
    _fill_before(before_sc)

    def copy_to(slot):
        def make(lo_, go_, size):
            return pltpu.make_async_copy(ys_ref.at[pl.ds(go_, size), :],
                                         ybuf.at[slot, pl.ds(lo_, size), :], sem.at[slot])
        return make

    @pl.when(j == 0)
    def _():
        ybuf[...] = jnp.zeros(ybuf.shape, BF16)
        for u in range(MOE_SUB):
            _run_copies(u, *refs, copy_to(u), True)

    @pl.when(j + 1 < n_steps)
    def _():
        for u in range(MOE_SUB):
            _run_copies((j + 1) * MOE_SUB + u, *refs, copy_to((1 - par) * MOE_SUB + u), True)

    row = lax.broadcasted_iota(jnp.int32, (MOE_LB, MOE_TOK), 0).astype(F32)
    for u in range(MOE_SUB):
        tok = slice(u * MOE_TOK, (u + 1) * MOE_TOK)
        pos, chosen, lo, hi = _sorted_rows(oh_ref[:, tok], lcol_ref[u], before_sc[...])
        gate = jnp.where(chosen, g_ref[:, tok], 0.0)
        g_lo = jnp.sum(jnp.where(pos == lo, gate, 0.0), axis=0, keepdims=True)
        g_hi = jnp.sum(jnp.where(pos == hi, gate, 0.0), axis=0, keepdims=True)
        perm = (jnp.where(row == lo, g_lo, 0.0) + jnp.where(row == hi, g_hi, 0.0)).astype(BF16)
        slot = par * MOE_SUB + u
        _run_copies(j * MOE_SUB + u, *refs, copy_to(slot), False)
        y = lax.dot_general(perm, ybuf[slot], (((0,), (0,)), ((), ())),
                            preferred_element_type=F32)
        x = x_ref[tok, :] + mod_ref[5:6, :] * y
        if final_norm:
            x = x * lax.rsqrt(jnp.mean(x * x, axis=-1, keepdims=True) + EPS) * fg_ref[...]
        o_ref[tok, :] = x


N_FRONT_IN = 8


def _combine_front_kernel(*refs):
    n_in = 3 + 7
    comb_in, front_in = refs[:n_in], refs[n_in:n_in + N_FRONT_IN]
    o_ref, *front_out = refs[n_in + N_FRONT_IN:n_in + N_FRONT_IN + 5]
    scratch = refs[n_in + N_FRONT_IN + 5:]
    _combine_kernel(*comb_in, o_ref, *scratch, final_norm=False)
    _front_kernel(o_ref, *front_in, *front_out)


def _combine(plan, onehot, gates, ys, x, layer, mod, final_g, front_params=None):
    step_tok = MOE_SUB * MOE_TOK
    route = pl.BlockSpec((N_EXPERTS, step_tok), lambda j, *_: (0, j))
    tok = pl.BlockSpec((step_tok, D_MODEL), lambda j, *_: (j, 0))
    in_specs = [
        route, route,
        pl.BlockSpec((MOE_SUB, N_EXPERTS, 1), lambda j, *_: (j, 0, 0)),
        tok, _mod_of(layer, step_tok),
        pl.BlockSpec((1, D_MODEL), lambda j, *_: (0, 0)),
        pl.BlockSpec(memory_space=pl.ANY),
    ]
    args = [onehot, gates, plan["local_col"], x, mod, final_g, ys]
    out_specs = tok
    out_shape = jax.ShapeDtypeStruct((TOKENS, D_MODEL), F32)
    body = functools.partial(_combine_kernel, final_norm=True)
    if front_params is not None:
        assert step_tok == K1_TM and len(front_params) == N_FRONT_IN
        half = pl.BlockSpec((step_tok, A_WIDTH), lambda j, *_: (j, 0))
        in_specs += _front_specs(layer + 1, step_tok)
        args += list(front_params)
        out_specs = [tok, half, half, half, half]
        out_shape = [out_shape, jax.ShapeDtypeStruct((TOKENS, A_WIDTH), BF16)] + \
                    [jax.ShapeDtypeStruct((TOKENS, B_WIDTH), F32)] * 3
        body = _combine_front_kernel
    grid_spec = pltpu.PrefetchScalarGridSpec(
        num_scalar_prefetch=3,
        grid=(N_TOK_TILES // MOE_SUB,),
        in_specs=in_specs,
        out_specs=out_specs,
        scratch_shapes=[pltpu.VMEM((2 * MOE_SUB, MOE_LB, D_MODEL), BF16),
                        pltpu.VMEM((MOE_TOK, MOE_TOK), BF16),
                        pltpu.SemaphoreType.DMA((2 * MOE_SUB,))],
    )
    return pl.pallas_call(
        body,
        grid_spec=grid_spec,
        out_shape=out_shape,
        compiler_params=_cparams(("arbitrary",)),
        name="combine",
    )(plan["n8"], plan["local"], plan["glob"], *args)


def kernel(x, c, rel_bias, router_w, router_b, mod_w, mod_b, norm1_g, w_in, gmlp_ln_g, gmlp_ln_b,
           gmlp_ws, gmlp_bs, out_norm_a_g, out_norm_b_g, w_out, norm2_g, moe_w_gate, moe_w_up,
           moe_w_down, final_g):
    mod = _modulation(c, mod_w, mod_b).reshape(DEPTH, BATCH, N_MOD, D_MODEL)
    bias_tab = _bias_tables(rel_bias)
    rwt = router_w.T.astype(BF16)
    rb_col = router_b.reshape(N_EXPERTS, 1)
    xt = x.reshape(TOKENS, D_MODEL)
    front_params = (
        mod, norm1_g.reshape(DEPTH, 1, D_MODEL), w_in.astype(BF16),
        gmlp_ln_g.reshape(DEPTH, 1, A_WIDTH), gmlp_ln_b.reshape(DEPTH, 1, A_WIDTH),
        gmlp_ws.astype(BF16).reshape(DEPTH, A_GROUPS // 2, 2 * CHUNK, CHUNK),
        jnp.repeat(jnp.swapaxes(gmlp_bs, 1, 2), HEAD_DIM, axis=2),
        out_norm_a_g.reshape(DEPTH, 1, A_WIDTH))
    gb = out_norm_b_g.reshape(DEPTH, 1, B_WIDTH)
    wout_b = w_out.astype(BF16)
    n2g = norm2_g.reshape(DEPTH, 1, D_MODEL)
    fg = final_g.reshape(1, D_MODEL)

    out_a, q, k, v = _front(xt, 0, front_params)
    for l in range(DEPTH):
        out_b = _attention(q, k, v, bias_tab)
        xt, h2, gates, onehot, count = _mid(out_a, out_b, xt, l, mod, gb, wout_b, n2g, rwt, rb_col)
        plan = _moe_plan(count)
        xs = _dispatch(plan, h2, onehot)
        ys = _experts(plan, xs, l, moe_w_gate, moe_w_up, moe_w_down)
        if l + 1 < DEPTH:
            xt, out_a, q, k, v = _combine(plan, onehot, gates, ys, xt, l, mod, fg, front_params)
        else:
            xt = _combine(plan, onehot, gates, ys, xt, l, mod, fg)
    return xt.reshape(BATCH, SEQ, D_MODEL)
```

```python
import functools
import math

import numpy as np
import jax
import jax.numpy as jnp
from jax import lax
from jax.experimental import pallas as pl
from jax.experimental.pallas import tpu as pltpu

D_MODEL = 1024
BATCH = 8
SEQ = 2048
DEPTH = 2
TOKENS = BATCH * SEQ
HEAD_DIM = 64
A_WIDTH = 512
B_WIDTH = 512
A_GROUPS = 8
IN_COLS = 2 * A_WIDTH + 3 * B_WIDTH
CHUNK = 128
DILATED_CONFIGS = ((128, 1), (512, 4), (2048, 16))
ATTN_BLOCK = 128
REL_BUCKETS = 32
REL_MAX_EXACT = REL_BUCKETS // 2
REL_MAX_DISTANCE = 2048
N_EXPERTS = 16
N_EXPERT_GROUPS = 4
EXPERTS_PER_GROUP = 4
D_EXPERT = 512
N_MOD = 6
EPS = 1e-6
NEG_INF = -1e30

LANES = 128
HEAD_PAIR = 2 * HEAD_DIM
N_PAIRS = B_WIDTH // HEAD_PAIR
RESIDUES = max(d for _, d in DILATED_CONFIGS)
LOG2E = math.log2(math.e)

F32 = jnp.float32
BF16 = jnp.bfloat16

VMEM_LIMIT = 56 * 1024 * 1024


def _cparams(sem):
    return pltpu.CompilerParams(dimension_semantics=sem, vmem_limit_bytes=VMEM_LIMIT)


def _gelu(x):
    return 0.5 * x * (1.0 + lax.erf(x * math.sqrt(0.5)))


MOD_TN = 1024


def _mod_kernel(c_ref, w_ref, b_ref, o_ref):
    ca = jax.nn.silu(c_ref[...])
    o_ref[...] = jnp.dot(ca.astype(BF16), w_ref[...].astype(BF16),
                         preferred_element_type=F32) + b_ref[...]


def _modulation(c, mod_w, mod_b):
    n_cols = N_MOD * D_MODEL
    return pl.pallas_call(
        _mod_kernel,
        grid=(DEPTH, n_cols // MOD_TN),
        in_specs=[
            pl.BlockSpec((BATCH, D_MODEL), lambda l, j: (0, 0)),
            pl.BlockSpec((None, D_MODEL, MOD_TN), lambda l, j: (l, 0, j)),
            pl.BlockSpec((None, 1, MOD_TN), lambda l, j: (l, 0, j)),
        ],
        out_specs=pl.BlockSpec((None, BATCH, MOD_TN), lambda l, j: (l, 0, j)),
        out_shape=jax.ShapeDtypeStruct((DEPTH, BATCH, n_cols), F32),
        compiler_params=_cparams(("arbitrary", "arbitrary")),
        name="modulation",
    )(c, mod_w, mod_b.reshape(DEPTH, 1, n_cols))


K1_TM = 1024


def _front_kernel(x_ref, mod_ref, n1g_ref, win_ref, lng_ref, lnb_ref, ws_ref, bs_ref, ga_ref,
                  a_ref, q_ref, k_ref, v_ref):
    x = x_ref[...]
    tm = x.shape[0]
    h = x * lax.rsqrt(jnp.mean(x * x, axis=-1, keepdims=True) + EPS) * n1g_ref[...]
    h = h * (1.0 + mod_ref[1:2, :]) + mod_ref[0:1, :]
    proj = jnp.dot(h.astype(BF16), win_ref[...], preferred_element_type=F32)

    q_ref[...] = proj[:, 2 * A_WIDTH:2 * A_WIDTH + B_WIDTH] * (HEAD_DIM ** -0.5 * LOG2E)
    k_ref[...] = proj[:, 2 * A_WIDTH + B_WIDTH:2 * A_WIDTH + 2 * B_WIDTH]
    v_ref[...] = proj[:, 2 * A_WIDTH + 2 * B_WIDTH:]

    u = _gelu(proj[:, :A_WIDTH])
    va = _gelu(proj[:, A_WIDTH:2 * A_WIDTH])
    mu = jnp.mean(va, axis=-1, keepdims=True)
    vc = va - mu
    vln = vc * lax.rsqrt(jnp.mean(vc * vc, axis=-1, keepdims=True) + EPS)
    vln = (vln * lng_ref[...] + lnb_ref[...]).astype(BF16)

    row = lax.broadcasted_iota(jnp.int32, (2 * CHUNK, CHUNK), 0)
    col = lax.broadcasted_iota(jnp.int32, (2 * CHUNK, CHUNK), 1)
    causal = (row % CHUNK) >= col
    first_group = lax.broadcasted_iota(jnp.int32, (CHUNK, LANES), 1) < HEAD_DIM
    wmix = [jnp.where(causal, ws_ref[p], jnp.zeros((), BF16)) for p in range(A_GROUPS // 2)]
    for c in range(tm // CHUNK):
        rows = slice(c * CHUNK, (c + 1) * CHUNK)
        parts = []
        for p in range(A_GROUPS // 2):
            vp = vln[rows, p * LANES:(p + 1) * LANES]
            r = jnp.dot(wmix[p], vp, preferred_element_type=F32)
            parts.append(jnp.where(first_group, r[:CHUNK], r[CHUNK:]))
        s = jnp.concatenate(parts, axis=-1) + bs_ref[...]
        oa = u[rows] * s
        oa = oa * lax.rsqrt(jnp.mean(oa * oa, axis=-1, keepdims=True) + EPS) * ga_ref[...]
        a_ref[rows, :] = oa.astype(BF16)


def _of_layer(layer, *tail):
    return pl.BlockSpec((None, *tail), lambda *_: (layer,) + (0,) * len(tail))


def _mod_of(layer, tokens_per_step):
    steps_per_seq = SEQ // tokens_per_step
    return pl.BlockSpec((None, None, N_MOD, D_MODEL),
                        lambda i, *_: (layer, i // steps_per_seq, 0, 0))


def _front_specs(layer, tm):
    return [
        _mod_of(layer, tm),
        _of_layer(layer, 1, D_MODEL),
        _of_layer(layer, D_MODEL, IN_COLS),
        _of_layer(layer, 1, A_WIDTH), _of_layer(layer, 1, A_WIDTH),
        _of_layer(layer, A_GROUPS // 2, 2 * CHUNK, CHUNK),
        _of_layer(layer, CHUNK, A_WIDTH),
        _of_layer(layer, 1, A_WIDTH),
    ]


def _front(x, layer, front_params):
    tm = K1_TM
    half = pl.BlockSpec((tm, A_WIDTH), lambda i: (i, 0))
    return pl.pallas_call(
        _front_kernel,
        grid=(TOKENS // tm,),
        in_specs=[pl.BlockSpec((tm, D_MODEL), lambda i: (i, 0))] + _front_specs(layer, tm),
        out_specs=[half, half, half, half],
        out_shape=[jax.ShapeDtypeStruct((TOKENS, A_WIDTH), BF16)] +
                  [jax.ShapeDtypeStruct((TOKENS, B_WIDTH), F32)] * 3,
        compiler_params=_cparams(("arbitrary",)),
        name="front",
    )(x, *front_params)


def _t5_bucket_np(dist):
    dist = np.maximum(dist, 0)
    ratio = np.log(np.maximum(dist, 1) / REL_MAX_EXACT) / np.log(REL_MAX_DISTANCE / REL_MAX_EXACT)
    large = REL_MAX_EXACT + np.floor(ratio * (REL_BUCKETS - REL_MAX_EXACT)).astype(np.int64)
    large = np.minimum(large, REL_BUCKETS - 1)
    return np.where(dist < REL_MAX_EXACT, dist, large).astype(np.int32)


def _bias_tables(rel_bias):
    blk = ATTN_BLOCK
    n_cfg = len(DILATED_CONFIGS)
    n_rel = 3 * blk
    rel = 2 * blk - 1 - np.arange(n_rel)
    bucket, valid, rows, cols = [], [], [], []
    for window, d in DILATED_CONFIGS:
        span = window // d
        bucket.append(_t5_bucket_np(np.clip(rel, 0, span) * d))
        valid.append((rel >= 0) & (rel <= span))
        sub = RESIDUES // d
        ln = blk // sub
        pos = np.arange(blk)
        rows.append(np.eye(blk, dtype=np.float32)[(pos % ln) * sub + pos // ln])
        cols.append(np.kron(np.eye(2, dtype=np.float32), rows[-1]))
    w = jnp.transpose(rel_bias.astype(F32)[np.stack(bucket)], (0, 2, 1)) * LOG2E
    w = jnp.where(np.stack(valid)[:, None, :], w, NEG_INF * LOG2E)
    flat = jnp.tile(w, (1, 1, blk))
    skew = flat[:, :, blk - 1:blk - 1 + blk * (n_rel - 1)].reshape(n_cfg, -1, blk, n_rel - 1)
    tab = skew[..., :2 * blk]
    return jnp.einsum('cpi,chik,cqk->chpq', np.stack(rows), tab, np.stack(cols),
                      precision=lax.Precision.HIGHEST)


def _attn_kernel(q_ref, k_ref, v_ref, bias_ref, o_ref, qp, kp, vp, m_sc, l_sc, acc_sc):
    blk = ATTN_BLOCK
    res = RESIDUES
    per = SEQ // res
    sq = res * res
    seg = sq // res
    lane = lax.broadcasted_iota(jnp.int32, (blk, LANES), 1)
    head0 = lane < HEAD_DIM
    ones = jnp.ones((2 * blk, LANES), BF16)

    pa = lax.broadcasted_iota(jnp.int32, (sq, sq), 0)
    pb = lax.broadcasted_iota(jnp.int32, (sq, sq), 1)
    regroup = jnp.where(pb == res * (pa % res) + pa // res, 1.0, 0.0).astype(BF16)

    def residue_rows(g):
        return [slice(per * r + seg * g, per * r + seg * (g + 1)) for r in range(res)]

    def load(ref, slices):
        return jnp.concatenate([ref[s, :] for s in slices], axis=0) if len(slices) > 1 \
            else ref[slices[0], :]

    def store(ref, slices, val):
        ln = val.shape[0] // len(slices)
        for i, s in enumerate(slices):
            ref[s, :] = val[i * ln:(i + 1) * ln]

    for g in range(SEQ // sq):
        rows = slice(sq * g, sq * (g + 1))
        qkv = jnp.concatenate([q_ref[rows, :], k_ref[rows, :], v_ref[rows, :]], axis=1)
        qkv = jnp.dot(regroup, qkv.astype(BF16), preferred_element_type=F32)
        store(qp, residue_rows(g), qkv[:, :LANES])
        store(kp, residue_rows(g), qkv[:, LANES:2 * LANES])
        store(vp, residue_rows(g), qkv[:, 2 * LANES:])

    def merge_heads(t):
        return jnp.where(head0, t[:blk], t[blk:])

    def scores(q, kcat, bias):
        zero = jnp.zeros_like(q)
        qs = jnp.concatenate([jnp.where(head0, q, zero), jnp.where(head0, zero, q)],
                             axis=0).astype(BF16)
        return lax.dot_general(qs, kcat, (((1,), (1,)), ((), ())),
                               preferred_element_type=F32) + bias

    def weighted_values(s, vcat):
        nk = vcat.shape[0]
        m = jnp.max(s, axis=-1, keepdims=True)
        p = jnp.exp2(s - m).astype(BF16)
        pv = jnp.dot(p, jnp.concatenate([vcat, ones[:nk]], axis=1), preferred_element_type=F32)
        return (merge_heads(jnp.broadcast_to(m, (2 * blk, LANES))),
                merge_heads(pv[:, LANES:]), merge_heads(pv[:, :LANES]))

    def update(ci, slices, m_c, l_c, o_c):
        if ci > 0:
            m_r = load(m_sc, slices)
            m_n = jnp.maximum(m_r, m_c)
            a = jnp.exp2(m_r - m_n)
            b = jnp.exp2(m_c - m_n)
            l_c = a * load(l_sc, slices) + b * l_c
            o_c = a * load(acc_sc, slices) + b * o_c
            m_c = m_n
        store(m_sc, slices, m_c)
        store(l_sc, slices, l_c)
        store(acc_sc, slices, o_c)

    blocks = []
    for ci, (window, d) in enumerate(DILATED_CONFIGS):
        sub = res // d
        ln = blk // sub
        for r in range(d):
            for n in range(SEQ // d // blk):
                slices = [slice(per * (r + d * c) + ln * n, per * (r + d * c) + ln * (n + 1))
                          for c in range(sub)]
                blocks.append((ci, slices, n == 0))

    kv_prev = [None, None]

    def score_stage(ci, slices, first):
        k_cur = load(kp, slices).astype(BF16)
        v_cur = load(vp, slices).astype(BF16)
        if first:
            bias = jnp.concatenate([bias_ref[ci, 0, :, blk:], bias_ref[ci, 1, :, blk:]], axis=0)
            out = scores(load(qp, slices), k_cur, bias), v_cur
        else:
            bias = jnp.concatenate([bias_ref[ci, 0], bias_ref[ci, 1]], axis=0)
            out = (scores(load(qp, slices), jnp.concatenate([kv_prev[0], k_cur], axis=0), bias),
                   jnp.concatenate([kv_prev[1], v_cur], axis=0))
        kv_prev[0], kv_prev[1] = k_cur, v_cur
        return out

    ahead = score_stage(*blocks[0])
    for i, (ci, slices, _) in enumerate(blocks):
        s, vcat = ahead
        if i + 1 < len(blocks):
            ahead = score_stage(*blocks[i + 1])
        update(ci, slices, *weighted_values(s, vcat))

    for g in range(SEQ // sq):
        o = load(acc_sc, residue_rows(g)) / load(l_sc, residue_rows(g))
        hi = o.astype(BF16)
        lo = (o - hi.astype(F32)).astype(BF16)
        back = jnp.dot(regroup, jnp.concatenate([hi, lo], axis=1), preferred_element_type=F32)
        o_ref[sq * g:sq * (g + 1), :] = back[:, :LANES] + back[:, LANES:]


def _attention(q, k, v, bias_tab):
    n_cfg = len(DILATED_CONFIGS)
    blk = ATTN_BLOCK
    seq_spec = pl.BlockSpec((SEQ, HEAD_PAIR), lambda b, p: (b, p))
    return pl.pallas_call(
        _attn_kernel,
        grid=(BATCH, N_PAIRS),
        in_specs=[seq_spec, seq_spec, seq_spec,
                  pl.BlockSpec((n_cfg, 2, blk, 2 * blk), lambda b, p: (0, p, 0, 0))],
        out_specs=seq_spec,
        out_shape=jax.ShapeDtypeStruct((TOKENS, B_WIDTH), F32),
        scratch_shapes=[pltpu.VMEM((SEQ, HEAD_PAIR), F32)] * 6,
        compiler_params=_cparams(("arbitrary", "arbitrary")),
        name="attention",
    )(q, k, v, bias_tab)


K4_TM = 1024
K4_PART = 256


def _top2_sum(a, b, c, d):
    hi1, lo1 = jnp.maximum(a, b), jnp.minimum(a, b)
    hi2, lo2 = jnp.maximum(c, d), jnp.minimum(c, d)
    return jnp.maximum(hi1, hi2) + jnp.maximum(jnp.minimum(hi1, hi2), jnp.maximum(lo1, lo2))


def _route(logits_t, rb_col):
    m = jnp.max(logits_t, axis=0, keepdims=True)
    e = jnp.exp(logits_t - m)
    probs = e / jnp.sum(e, axis=0, keepdims=True)
    sel = probs + rb_col
    sel_rows = [sel[i:i + 1, :] for i in range(N_EXPERTS)]
    prob_rows = [probs[i:i + 1, :] for i in range(N_EXPERTS)]
    gsz = EXPERTS_PER_GROUP
    score = [_top2_sum(*sel_rows[g * gsz:(g + 1) * gsz]) for g in range(N_EXPERT_GROUPS)]
    chosen = []
    for g in range(N_EXPERT_GROUPS):
        best = None
        for g2 in range(N_EXPERT_GROUPS):
            if g2 == g:
                continue
            c = (score[g] > score[g2]) if g2 < g else (score[g] >= score[g2])
            best = c if best is None else jnp.logical_and(best, c)
        for i in range(gsz):
            ei = g * gsz + i
            rank = jnp.zeros_like(sel_rows[ei])
            for j in range(gsz):
                if j == i:
                    continue
                ej = g * gsz + j
                ahead = (sel_rows[ej] >= sel_rows[ei]) if j < i else (sel_rows[ej] > sel_rows[ei])
                rank = rank + jnp.where(ahead, 1.0, 0.0)
            chosen.append(jnp.logical_and(best, rank < float(2)))
    picked = [jnp.where(chosen[i], prob_rows[i], 0.0) for i in range(N_EXPERTS)]
    denom = picked[0]
    for i in range(1, N_EXPERTS):
        denom = denom + picked[i]
    gates = jnp.concatenate([pk / denom for pk in picked], axis=0)
    onehot = jnp.concatenate([jnp.where(ch, 1.0, 0.0) for ch in chosen], axis=0)
    return gates, onehot


def _mid_kernel(a_ref, ob_ref, x_ref, mod_ref, gb_ref, wout_ref, n2g_ref, rwt_ref, rb_ref,
                x1_ref, h2_ref, gates_ref, onehot_ref, count_ref):
    for s in range(x_ref.shape[0] // K4_PART):
        rows = slice(s * K4_PART, (s + 1) * K4_PART)
        ob = ob_ref[rows, :]
        bn = ob * lax.rsqrt(jnp.mean(ob * ob, axis=-1, keepdims=True) + EPS) * gb_ref[...]
        mixed = jnp.dot(a_ref[rows, :], wout_ref[:A_WIDTH, :], preferred_element_type=F32)
        mixed = mixed + jnp.dot(bn.astype(BF16), wout_ref[A_WIDTH:, :],
                                preferred_element_type=F32)
        x1 = x_ref[rows, :] + mod_ref[2:3, :] * mixed
        x1_ref[rows, :] = x1
        h2 = x1 * lax.rsqrt(jnp.mean(x1 * x1, axis=-1, keepdims=True) + EPS) * n2g_ref[...]
        h2 = (h2 * (1.0 + mod_ref[4:5, :]) + mod_ref[3:4, :]).astype(BF16)
        h2_ref[rows, :] = h2
        logits_t = lax.dot_general(rwt_ref[...], h2, (((1,), (1,)), ((), ())),
                                   preferred_element_type=F32)
        gates_t, onehot_t = _route(logits_t, rb_ref[...])
        gates_ref[:, rows] = gates_t
        onehot_ref[:, rows] = onehot_t
        for t in range(K4_PART // MOE_TOK):
            count_ref[s * (K4_PART // MOE_TOK) + t] = jnp.sum(
                onehot_t[:, t * MOE_TOK:(t + 1) * MOE_TOK], axis=1, keepdims=True)


def _mid(out_a, out_b, x, layer, mod, gb, wout, n2g, rwt, rb_col):
    tm = K4_TM
    tok = pl.BlockSpec((tm, D_MODEL), lambda i: (i, 0))
    half = pl.BlockSpec((tm, A_WIDTH), lambda i: (i, 0))
    route = pl.BlockSpec((N_EXPERTS, tm), lambda i: (0, i))
    return pl.pallas_call(
        _mid_kernel,
        grid=(TOKENS // tm,),
        in_specs=[
            half, half, tok,
            _mod_of(layer, tm),
            _of_layer(layer, 1, B_WIDTH),
            _of_layer(layer, D_MODEL, D_MODEL),
            _of_layer(layer, 1, D_MODEL),
            pl.BlockSpec((N_EXPERTS, D_MODEL), lambda i: (0, 0)),
            pl.BlockSpec((N_EXPERTS, 1), lambda i: (0, 0)),
        ],
        out_specs=[tok, tok, route, route,
                   pl.BlockSpec((tm // MOE_TOK, N_EXPERTS, 1), lambda i: (i, 0, 0))],
        out_shape=[jax.ShapeDtypeStruct((TOKENS, D_MODEL), F32),
                   jax.ShapeDtypeStruct((TOKENS, D_MODEL), BF16),
                   jax.ShapeDtypeStruct((N_EXPERTS, TOKENS), F32),
                   jax.ShapeDtypeStruct((N_EXPERTS, TOKENS), F32),
                   jax.ShapeDtypeStruct((TOKENS // MOE_TOK, N_EXPERTS, 1), F32)],
        compiler_params=_cparams(("arbitrary",)),
        name="mid",
    )(out_a, out_b, x, mod, gb, wout, n2g, rwt, rb_col)


MOE_TOK = 256
ROW_ALIGN = 16
MOE_SUB = 4
MOE_LB = 2 * MOE_TOK + N_EXPERTS * ROW_ALIGN
FFN_TM = 1024
N_TOK_TILES = TOKENS // MOE_TOK
ROWS_MAX = -(-(2 * TOKENS + N_TOK_TILES * N_EXPERTS * ROW_ALIGN
               + N_EXPERTS * (FFN_TM - ROW_ALIGN)) // FFN_TM) * FFN_TM


N_FFN_TILES = ROWS_MAX // FFN_TM


def _plan_kernel(count_ref, n8_ref, local_ref, glob_ref, texp_ref, nact_ref):
    def prefix(a, b):
        return jnp.dot(a, b, preferred_element_type=F32, precision=lax.Precision.HIGHEST)

    row = lax.broadcasted_iota(jnp.int32, (LANES, LANES), 0)
    col = lax.broadcasted_iota(jnp.int32, (LANES, LANES), 1)
    real = jnp.logical_and(row < N_TOK_TILES, col < N_EXPERTS)
    n8 = jnp.where(real, jnp.maximum(jnp.ceil(count_ref[...] / ROW_ALIGN), 1.0) * ROW_ALIGN, 0.0)
    earlier_lane = jnp.where(row < col, 1.0, 0.0)
    earlier_row = jnp.where(col < row, 1.0, 0.0)
    local = prefix(n8, earlier_lane)
    seg = jnp.ceil(jnp.sum(n8, axis=0, keepdims=True) / FFN_TM) * FFN_TM
    seg_start = prefix(jnp.broadcast_to(seg, (LANES, LANES)), earlier_lane)[0:1]
    glob = seg_start + prefix(earlier_row, n8)
    n_active = jnp.sum(seg, axis=1, keepdims=True) / FFN_TM
    tile = row[:, 0:1].astype(F32)
    ended = jnp.logical_and(tile * FFN_TM >= seg_start + seg, col < N_EXPERTS)
    texp = jnp.sum(jnp.where(ended, 1.0, 0.0), axis=1, keepdims=True)
    texp = jnp.minimum(texp, float(N_EXPERTS - 1))
    last = jnp.sum(jnp.where(tile == n_active - 1.0, texp, 0.0), axis=0, keepdims=True)
    texp = jnp.where(tile < n_active, texp, last)
    n8_ref[...] = n8.astype(jnp.int32)
    local_ref[...] = local.astype(jnp.int32)
    glob_ref[...] = glob.astype(jnp.int32)
    texp_ref[...] = jnp.broadcast_to(texp, (LANES, LANES)).astype(jnp.int32)
    nact_ref[...] = jnp.broadcast_to(n_active, (8, LANES)).astype(jnp.int32)


def _moe_plan(count):
    assert max(N_TOK_TILES, N_FFN_TILES, N_EXPERTS) <= LANES
    count = count.reshape(N_TOK_TILES, N_EXPERTS)
    count = jnp.pad(count, ((0, LANES - N_TOK_TILES), (0, LANES - N_EXPERTS)))
    full = jax.ShapeDtypeStruct((LANES, LANES), jnp.int32)
    n8, local, glob, texp, nact = pl.pallas_call(
        _plan_kernel,
        out_shape=[full, full, full, full, jax.ShapeDtypeStruct((8, LANES), jnp.int32)],
        name="plan",
    )(count)

    def runs(t):
        return t[:N_TOK_TILES, :N_EXPERTS]

    return dict(n8=runs(n8).reshape(-1), local=runs(local).reshape(-1),
                glob=runs(glob).reshape(-1),
                local_col=runs(local).astype(F32).reshape(N_TOK_TILES, N_EXPERTS, 1),
                texp=texp[:N_FFN_TILES, 0], n_active=nact[0, :1])


def _run_copies(tile, n8_ref, local_ref, glob_ref, make_copy, start):
    for e in range(N_EXPERTS):
        idx = tile * N_EXPERTS + e
        cp = make_copy(pl.multiple_of(local_ref[idx], ROW_ALIGN),
                       pl.multiple_of(glob_ref[idx], ROW_ALIGN),
                       pl.multiple_of(n8_ref[idx], ROW_ALIGN))
        if start:
            cp.start()
        else:
            cp.wait()


def _fill_before(before_sc):
    @pl.when(pl.program_id(0) == 0)
    def _():
        src = lax.broadcasted_iota(jnp.int32, before_sc.shape, 0)
        dst = lax.broadcasted_iota(jnp.int32, before_sc.shape, 1)
        before_sc[...] = jnp.where(src < dst, 1.0, 0.0).astype(BF16)


def _sorted_rows(onehot, local_col, before):
    rank = jnp.dot(onehot.astype(BF16), before, preferred_element_type=F32)
    pos = local_col + rank
    chosen = onehot > 0.5
    lo = jnp.min(jnp.where(chosen, pos, float(MOE_LB)), axis=0, keepdims=True)
    hi = jnp.max(jnp.where(chosen, pos, -1.0), axis=0, keepdims=True)
    return pos, chosen, lo, hi


def _dispatch_kernel(n8_ref, local_ref, glob_ref, h_ref, oh_ref, lcol_ref, xs_ref,
                     lbuf, before_sc, sem):
    j = pl.program_id(0)
    last = pl.num_programs(0) - 1
    par = j % 2
    _fill_before(before_sc)
    row = lax.broadcasted_iota(jnp.int32, (MOE_LB, MOE_TOK), 0).astype(F32)
    for u in range(MOE_SUB):
        tok = slice(u * MOE_TOK, (u + 1) * MOE_TOK)
        _, _, lo, hi = _sorted_rows(oh_ref[:, tok], lcol_ref[u], before_sc[...])
        perm = (jnp.where(row == lo, 1.0, 0.0) + jnp.where(row == hi, 1.0, 0.0)).astype(BF16)
        lbuf[par * MOE_SUB + u] = jnp.dot(perm, h_ref[tok, :],
                                          preferred_element_type=F32).astype(BF16)

    def copy_from(slot):
        def make(lo_, go_, size):
            return pltpu.make_async_copy(lbuf.at[slot, pl.ds(lo_, size), :],
                                         xs_ref.at[pl.ds(go_, size), :], sem.at[slot])
        return make

    refs = (n8_ref, local_ref, glob_ref)
    for u in range(MOE_SUB):
        _run_copies(j * MOE_SUB + u, *refs, copy_from(par * MOE_SUB + u), True)

    @pl.when(j > 0)
    def _():
        for u in range(MOE_SUB):
            _run_copies((j - 1) * MOE_SUB + u, *refs, copy_from((1 - par) * MOE_SUB + u), False)

    @pl.when(j == last)
    def _():
        for u in range(MOE_SUB):
            _run_copies(j * MOE_SUB + u, *refs, copy_from(par * MOE_SUB + u), False)


def _dispatch(plan, h2, onehot):
    step_tok = MOE_SUB * MOE_TOK
    grid_spec = pltpu.PrefetchScalarGridSpec(
        num_scalar_prefetch=3,
        grid=(N_TOK_TILES // MOE_SUB,),
        in_specs=[
            pl.BlockSpec((step_tok, D_MODEL), lambda j, *_: (j, 0)),
            pl.BlockSpec((N_EXPERTS, step_tok), lambda j, *_: (0, j)),
            pl.BlockSpec((MOE_SUB, N_EXPERTS, 1), lambda j, *_: (j, 0, 0)),
        ],
        out_specs=pl.BlockSpec(memory_space=pl.ANY),
        scratch_shapes=[pltpu.VMEM((2 * MOE_SUB, MOE_LB, D_MODEL), BF16),
                        pltpu.VMEM((MOE_TOK, MOE_TOK), BF16),
                        pltpu.SemaphoreType.DMA((2 * MOE_SUB,))],
    )
    return pl.pallas_call(
        _dispatch_kernel,
        grid_spec=grid_spec,
        out_shape=jax.ShapeDtypeStruct((ROWS_MAX, D_MODEL), BF16),
        compiler_params=_cparams(("arbitrary",)),
        name="dispatch",
    )(plan["n8"], plan["local"], plan["glob"], h2, onehot, plan["local_col"])


def _experts_kernel(texp_ref, nact_ref, xs_ref, wg_ref, wu_ref, wd_ref, ys_ref, wg_b, wu_b, wd_b):
    i = pl.program_id(0)
    active = i < nact_ref[0]
    new_expert = jnp.logical_or(i == 0, texp_ref[i] != texp_ref[jnp.maximum(i - 1, 0)])

    @pl.when(jnp.logical_and(active, new_expert))
    def _():
        wg_b[...] = wg_ref[...].astype(BF16)
        wu_b[...] = wu_ref[...].astype(BF16)
        wd_b[...] = wd_ref[...].astype(BF16)

    @pl.when(active)
    def _():
        x = xs_ref[...]
        hg = jnp.dot(x, wg_b[...], preferred_element_type=F32)
        hu = jnp.dot(x, wu_b[...], preferred_element_type=F32)
        act = (jax.nn.silu(hg) * hu).astype(BF16)
        ys_ref[...] = jnp.dot(act, wd_b[...], preferred_element_type=F32).astype(BF16)


def _experts(plan, xs, layer, wg, wu, wd):
    def rows(i, texp, nact):
        return (jnp.minimum(i, nact[0] - 1), 0)

    def expert(i, texp, nact):
        return (layer, texp[i], 0, 0)

    grid_spec = pltpu.PrefetchScalarGridSpec(
        num_scalar_prefetch=2,
        grid=(ROWS_MAX // FFN_TM,),
        in_specs=[
            pl.BlockSpec((FFN_TM, D_MODEL), rows),
            pl.BlockSpec((None, None, D_MODEL, D_EXPERT), expert),
            pl.BlockSpec((None, None, D_MODEL, D_EXPERT), expert),
            pl.BlockSpec((None, None, D_EXPERT, D_MODEL), expert),
        ],
        out_specs=pl.BlockSpec((FFN_TM, D_MODEL), rows),
        scratch_shapes=[pltpu.VMEM((D_MODEL, D_EXPERT), BF16), pltpu.VMEM((D_MODEL, D_EXPERT), BF16),
                        pltpu.VMEM((D_EXPERT, D_MODEL), BF16)],
    )
    return pl.pallas_call(
        _experts_kernel,
        grid_spec=grid_spec,
        out_shape=jax.ShapeDtypeStruct((ROWS_MAX, D_MODEL), BF16),
        compiler_params=_cparams(("arbitrary",)),
        name="experts",
    )(plan["texp"], plan["n_active"], xs, wg, wu, wd)


def _combine_kernel(n8_ref, local_ref, glob_ref, oh_ref, g_ref, lcol_ref, x_ref, mod_ref, fg_ref,
                    ys_ref, o_ref, ybuf, before_sc, sem, *, final_norm):
    j = pl.program_id(0)
    n_steps = pl.num_programs(0)
    par = j % 2
    refs = (n8_ref, local_ref, glob_ref)
    _fill_before(before_sc)

    def copy_to(slot):
        def make(lo_, go_, size):
            return pltpu.make_async_copy(ys_ref.at[pl.ds(go_, size), :],
                                         ybuf.at[slot, pl.ds(lo_, size), :], sem.at[slot])
        return make

    @pl.when(j == 0)
    def _():
        ybuf[...] = jnp.zeros(ybuf.shape, BF16)
        for u in range(MOE_SUB):
            _run_copies(u, *refs, copy_to(u), True)

    @pl.when(j + 1 < n_steps)
    def _():
        for u in range(MOE_SUB):
            _run_copies((j + 1) * MOE_SUB + u, *refs, copy_to((1 - par) * MOE_SUB + u), True)

    row = lax.broadcasted_iota(jnp.int32, (MOE_LB, MOE_TOK), 0).astype(F32)
    for u in range(MOE_SUB):
        tok = slice(u * MOE_TOK, (u + 1) * MOE_TOK)
        pos, chosen, lo, hi = _sorted_rows(oh_ref[:, tok], lcol_ref[u], before_sc[...])
        gate = jnp.where(chosen, g_ref[:, tok], 0.0)
        g_lo = jnp.sum(jnp.where(pos == lo, gate, 0.0), axis=0, keepdims=True)
        g_hi = jnp.sum(jnp.where(pos == hi, gate, 0.0), axis=0, keepdims=True)
        perm = (jnp.where(row == lo, g_lo, 0.0) + jnp.where(row == hi, g_hi, 0.0)).astype(BF16)
        slot = par * MOE_SUB + u
        _run_copies(j * MOE_SUB + u, *refs, copy_to(slot), False)
        y = lax.dot_general(perm, ybuf[slot], (((0,), (0,)), ((), ())),
                            preferred_element_type=F32)
        x = x_ref[tok, :] + mod_ref[5:6, :] * y
        if final_norm:
            x = x * lax.rsqrt(jnp.mean(x * x, axis=-1, keepdims=True) + EPS) * fg_ref[...]
        o_ref[tok, :] = x


N_FRONT_IN = 8


def _combine_front_kernel(*refs):
    n_in = 3 + 7
    comb_in, front_in = refs[:n_in], refs[n_in:n_in + N_FRONT_IN]
    o_ref, *front_out = refs[n_in + N_FRONT_IN:n_in + N_FRONT_IN + 5]
    scratch = refs[n_in + N_FRONT_IN + 5:]
    _combine_kernel(*comb_in, o_ref, *scratch, final_norm=False)
    _front_kernel(o_ref, *front_in, *front_out)


def _combine(plan, onehot, gates, ys, x, layer, mod, final_g, front_params=None):
    step_tok = MOE_SUB * MOE_TOK
    route = pl.BlockSpec((N_EXPERTS, step_tok), lambda j, *_: (0, j))
    tok = pl.BlockSpec((step_tok, D_MODEL), lambda j, *_: (j, 0))
    in_specs = [
        route, route,
        pl.BlockSpec((MOE_SUB, N_EXPERTS, 1), lambda j, *_: (j, 0, 0)),
        tok, _mod_of(layer, step_tok),
        pl.BlockSpec((1, D_MODEL), lambda j, *_: (0, 0)),
        pl.BlockSpec(memory_space=pl.ANY),
    ]
    args = [onehot, gates, plan["local_col"], x, mod, final_g, ys]
    out_specs = tok
    out_shape = jax.ShapeDtypeStruct((TOKENS, D_MODEL), F32)
    body = functools.partial(_combine_kernel, final_norm=True)
    if front_params is not None:
        assert step_tok == K1_TM and len(front_params) == N_FRONT_IN
        half = pl.BlockSpec((step_tok, A_WIDTH), lambda j, *_: (j, 0))
        in_specs += _front_specs(layer + 1, step_tok)
        args += list(front_params)
        out_specs = [tok, half, half, half, half]
        out_shape = [out_shape, jax.ShapeDtypeStruct((TOKENS, A_WIDTH), BF16)] + \
                    [jax.ShapeDtypeStruct((TOKENS, B_WIDTH), F32)] * 3
        body = _combine_front_kernel
    grid_spec = pltpu.PrefetchScalarGridSpec(
        num_scalar_prefetch=3,
        grid=(N_TOK_TILES // MOE_SUB,),
        in_specs=in_specs,
        out_specs=out_specs,
        scratch_shapes=[pltpu.VMEM((2 * MOE_SUB, MOE_LB, D_MODEL), BF16),
                        pltpu.VMEM((MOE_TOK, MOE_TOK), BF16),
                        pltpu.SemaphoreType.DMA((2 * MOE_SUB,))],
    )
    return pl.pallas_call(
        body,
        grid_spec=grid_spec,
        out_shape=out_shape,
        compiler_params=_cparams(("arbitrary",)),
        name="combine",
    )(plan["n8"], plan["local"], plan["glob"], *args)


def kernel(x, c, rel_bias, router_w, router_b, mod_w, mod_b, norm1_g, w_in, gmlp_ln_g, gmlp_ln_b,
           gmlp_ws, gmlp_bs, out_norm_a_g, out_norm_b_g, w_out, norm2_g, moe_w_gate, moe_w_up,
           moe_w_down, final_g):
    mod = _modulation(c, mod_w, mod_b).reshape(DEPTH, BATCH, N_MOD, D_MODEL)
    bias_tab = _bias_tables(rel_bias)
    rwt = router_w.T.astype(BF16)
    rb_col = router_b.reshape(N_EXPERTS, 1)
    xt = x.reshape(TOKENS, D_MODEL)
    front_params = (
        mod, norm1_g.reshape(DEPTH, 1, D_MODEL), w_in.astype(BF16),
        gmlp_ln_g.reshape(DEPTH, 1, A_WIDTH), gmlp_ln_b.reshape(DEPTH, 1, A_WIDTH),
        gmlp_ws.astype(BF16).reshape(DEPTH, A_GROUPS // 2, 2 * CHUNK, CHUNK),
        jnp.repeat(jnp.swapaxes(gmlp_bs, 1, 2), HEAD_DIM, axis=2),
        out_norm_a_g.reshape(DEPTH, 1, A_WIDTH))
    gb = out_norm_b_g.reshape(DEPTH, 1, B_WIDTH)
    wout_b = w_out.astype(BF16)
    n2g = norm2_g.reshape(DEPTH, 1, D_MODEL)
    fg = final_g.reshape(1, D_MODEL)

    out_a, q, k, v = _front(xt, 0, front_params)
    for l in range(DEPTH):
        out_b = _attention(q, k, v, bias_tab)
        xt, h2, gates, onehot, count = _mid(out_a, out_b, xt, l, mod, gb, wout_b, n2g, rwt, rb_col)
        plan = _moe_plan(count)
        xs = _dispatch(plan, h2, onehot)
        ys = _experts(plan, xs, l, moe_w_gate, moe_w_up, moe_w_down)
        if l + 1 < DEPTH:
            xt, out_a, q, k, v = _combine(plan, onehot, gates, ys, xt, l, mod, fg, front_params)
        else:
            xt = _combine(plan, onehot, gates, ys, xt, l, mod, fg)
    return xt.reshape(BATCH, SEQ, D_MODEL)
```

```python
import functools
import math

import numpy as np
import jax
import jax.numpy as jnp
from jax import lax
from jax.experimental import pallas as pl
from jax.experimental.pallas import tpu as pltpu

D_MODEL = 1024
BATCH = 8
SEQ = 2048
DEPTH = 2
TOKENS = BATCH * SEQ
HEAD_DIM = 64
A_WIDTH = 512
B_WIDTH = 512
A_GROUPS = 8
IN_COLS = 2 * A_WIDTH + 3 * B_WIDTH
CHUNK = 128
DILATED_CONFIGS = ((128, 1), (512, 4), (2048, 16))
ATTN_BLOCK = 128
REL_BUCKETS = 32
REL_MAX_EXACT = REL_BUCKETS // 2
REL_MAX_DISTANCE = 2048
N_EXPERTS = 16
N_EXPERT_GROUPS = 4
EXPERTS_PER_GROUP = 4
D_EXPERT = 512
N_MOD = 6
EPS = 1e-6
NEG_INF = -1e30

LANES = 128
HEAD_PAIR = 2 * HEAD_DIM
N_PAIRS = B_WIDTH // HEAD_PAIR
RESIDUES = max(d for _, d in DILATED_CONFIGS)
LOG2E = math.log2(math.e)

F32 = jnp.float32
BF16 = jnp.bfloat16

VMEM_LIMIT = 56 * 1024 * 1024


def _cparams(sem):
    return pltpu.CompilerParams(dimension_semantics=sem, vmem_limit_bytes=VMEM_LIMIT)


def _gelu(x):
    return 0.5 * x * (1.0 + lax.erf(x * math.sqrt(0.5)))


MOD_TN = 1024


def _mod_kernel(c_ref, w_ref, b_ref, o_ref):
    ca = jax.nn.silu(c_ref[...])
    o_ref[...] = jnp.dot(ca.astype(BF16), w_ref[...].astype(BF16),
                         preferred_element_type=F32) + b_ref[...]


def _modulation(c, mod_w, mod_b):
    n_cols = N_MOD * D_MODEL
    return pl.pallas_call(
        _mod_kernel,
        grid=(DEPTH, n_cols // MOD_TN),
        in_specs=[
            pl.BlockSpec((BATCH, D_MODEL), lambda l, j: (0, 0)),
            pl.BlockSpec((None, D_MODEL, MOD_TN), lambda l, j: (l, 0, j)),
            pl.BlockSpec((None, 1, MOD_TN), lambda l, j: (l, 0, j)),
        ],
        out_specs=pl.BlockSpec((None, BATCH, MOD_TN), lambda l, j: (l, 0, j)),
        out_shape=jax.ShapeDtypeStruct((DEPTH, BATCH, n_cols), F32),
        compiler_params=_cparams(("arbitrary", "arbitrary")),
        name="modulation",
    )(c, mod_w, mod_b.reshape(DEPTH, 1, n_cols))


K1_TM = 1024


def _front_kernel(x_ref, mod_ref, n1g_ref, win_ref, lng_ref, lnb_ref, ws_ref, bs_ref, ga_ref,
                  a_ref, q_ref, k_ref, v_ref):
    x = x_ref[...]
    tm = x.shape[0]
    h = x * lax.rsqrt(jnp.mean(x * x, axis=-1, keepdims=True) + EPS) * n1g_ref[...]
    h = h * (1.0 + mod_ref[1:2, :]) + mod_ref[0:1, :]
    proj = jnp.dot(h.astype(BF16), win_ref[...], preferred_element_type=F32)

    q_ref[...] = proj[:, 2 * A_WIDTH:2 * A_WIDTH + B_WIDTH] * (HEAD_DIM ** -0.5 * LOG2E)
    k_ref[...] = proj[:, 2 * A_WIDTH + B_WIDTH:2 * A_WIDTH + 2 * B_WIDTH]
    v_ref[...] = proj[:, 2 * A_WIDTH + 2 * B_WIDTH:]

    u = _gelu(proj[:, :A_WIDTH])
    va = _gelu(proj[:, A_WIDTH:2 * A_WIDTH])
    mu = jnp.mean(va, axis=-1, keepdims=True)
    vc = va - mu
    vln = vc * lax.rsqrt(jnp.mean(vc * vc, axis=-1, keepdims=True) + EPS)
    vln = (vln * lng_ref[...] + lnb_ref[...]).astype(BF16)

    row = lax.broadcasted_iota(jnp.int32, (2 * CHUNK, CHUNK), 0)
    col = lax.broadcasted_iota(jnp.int32, (2 * CHUNK, CHUNK), 1)
    causal = (row % CHUNK) >= col
    first_group = lax.broadcasted_iota(jnp.int32, (CHUNK, LANES), 1) < HEAD_DIM
    wmix = [jnp.where(causal, ws_ref[p], jnp.zeros((), BF16)) for p in range(A_GROUPS // 2)]
    for c in range(tm // CHUNK):
        rows = slice(c * CHUNK, (c + 1) * CHUNK)
        parts = []
        for p in range(A_GROUPS // 2):
            vp = vln[rows, p * LANES:(p + 1) * LANES]
            r = jnp.dot(wmix[p], vp, preferred_element_type=F32)
            parts.append(jnp.where(first_group, r[:CHUNK], r[CHUNK:]))
        s = jnp.concatenate(parts, axis=-1) + bs_ref[...]
        oa = u[rows] * s
        oa = oa * lax.rsqrt(jnp.mean(oa * oa, axis=-1, keepdims=True) + EPS) * ga_ref[...]
        a_ref[rows, :] = oa.astype(BF16)


def _of_layer(layer, *tail):
    return pl.BlockSpec((None, *tail), lambda *_: (layer,) + (0,) * len(tail))


def _mod_of(layer, tokens_per_step):
    steps_per_seq = SEQ // tokens_per_step
    return pl.BlockSpec((None, None, N_MOD, D_MODEL),
                        lambda i, *_: (layer, i // steps_per_seq, 0, 0))


def _front_specs(layer, tm):
    return [
        _mod_of(layer, tm),
        _of_layer(layer, 1, D_MODEL),
        _of_layer(layer, D_MODEL, IN_COLS),
        _of_layer(layer, 1, A_WIDTH), _of_layer(layer, 1, A_WIDTH),
        _of_layer(layer, A_GROUPS // 2, 2 * CHUNK, CHUNK),
        _of_layer(layer, CHUNK, A_WIDTH),
        _of_layer(layer, 1, A_WIDTH),
    ]


def _front(x, layer, front_params):
    tm = K1_TM
    half = pl.BlockSpec((tm, A_WIDTH), lambda i: (i, 0))
    return pl.pallas_call(
        _front_kernel,
        grid=(TOKENS // tm,),
        in_specs=[pl.BlockSpec((tm, D_MODEL), lambda i: (i, 0))] + _front_specs(layer, tm),
        out_specs=[half, half, half, half],
        out_shape=[jax.ShapeDtypeStruct((TOKENS, A_WIDTH), BF16)] +
                  [jax.ShapeDtypeStruct((TOKENS, B_WIDTH), F32)] * 3,
        compiler_params=_cparams(("arbitrary",)),
        name="front",
    )(x, *front_params)


def _t5_bucket_np(dist):
    dist = np.maximum(dist, 0)
    ratio = np.log(np.maximum(dist, 1) / REL_MAX_EXACT) / np.log(REL_MAX_DISTANCE / REL_MAX_EXACT)
    large = REL_MAX_EXACT + np.floor(ratio * (REL_BUCKETS - REL_MAX_EXACT)).astype(np.int64)
    large = np.minimum(large, REL_BUCKETS - 1)
    return np.where(dist < REL_MAX_EXACT, dist, large).astype(np.int32)


def _bias_tables(rel_bias):
    blk = ATTN_BLOCK
    n_cfg = len(DILATED_CONFIGS)
    n_rel = 3 * blk
    rel = 2 * blk - 1 - np.arange(n_rel)
    bucket, valid, rows, cols = [], [], [], []
    for window, d in DILATED_CONFIGS:
        span = window // d
        bucket.append(_t5_bucket_np(np.clip(rel, 0, span) * d))
        valid.append((rel >= 0) & (rel <= span))
        sub = RESIDUES // d
        ln = blk // sub
        pos = np.arange(blk)
        rows.append(np.eye(blk, dtype=np.float32)[(pos % ln) * sub + pos // ln])
        cols.append(np.kron(np.eye(2, dtype=np.float32), rows[-1]))
    w = jnp.transpose(rel_bias.astype(F32)[np.stack(bucket)], (0, 2, 1)) * LOG2E
    w = jnp.where(np.stack(valid)[:, None, :], w, NEG_INF * LOG2E)
    flat = jnp.tile(w, (1, 1, blk))
    skew = flat[:, :, blk - 1:blk - 1 + blk * (n_rel - 1)].reshape(n_cfg, -1, blk, n_rel - 1)
    tab = skew[..., :2 * blk]
    return jnp.einsum('cpi,chik,cqk->chpq', np.stack(rows), tab, np.stack(cols),
                      precision=lax.Precision.HIGHEST)


ATTN_LEAD = 3
ATTN_LAG = 1


def _attn_kernel(q_ref, k_ref, v_ref, bias_ref, o_ref, qp, kp, vp, m_sc, l_sc, acc_sc):
    blk = ATTN_BLOCK
    res = RESIDUES
    per = SEQ // res
    sq = res * res
    seg = sq // res
    lane = lax.broadcasted_iota(jnp.int32, (blk, LANES), 1)
    head0 = lane < HEAD_DIM
    ones = jnp.ones((2 * blk, LANES), BF16)

    pa = lax.broadcasted_iota(jnp.int32, (sq, sq), 0)
    pb = lax.broadcasted_iota(jnp.int32, (sq, sq), 1)
    regroup = jnp.where(pb == res * (pa % res) + pa // res, 1.0, 0.0).astype(BF16)

    def residue_rows(g):
        return [slice(per * r + seg * g, per * r + seg * (g + 1)) for r in range(res)]

    def load(ref, slices):
        return jnp.concatenate([ref[s, :] for s in slices], axis=0) if len(slices) > 1 \
            else ref[slices[0], :]

    def store(ref, slices, val):
        ln = val.shape[0] // len(slices)
        for i, s in enumerate(slices):
            ref[s, :] = val[i * ln:(i + 1) * ln]

    for g in range(SEQ // sq):
        rows = slice(sq * g, sq * (g + 1))
        qkv = jnp.concatenate([q_ref[rows, :], k_ref[rows, :], v_ref[rows, :]], axis=1)
        qkv = jnp.dot(regroup, qkv.astype(BF16), preferred_element_type=F32)
        store(qp, residue_rows(g), qkv[:, :LANES])
        store(kp, residue_rows(g), qkv[:, LANES:2 * LANES])
        store(vp, residue_rows(g), qkv[:, 2 * LANES:])

    def merge_heads(t):
        return jnp.where(head0, t[:blk], t[blk:])

    def scores(q, kcat, bias):
        zero = jnp.zeros_like(q)
        qs = jnp.concatenate([jnp.where(head0, q, zero), jnp.where(head0, zero, q)],
                             axis=0).astype(BF16)
        return lax.dot_general(qs, kcat, (((1,), (1,)), ((), ())),
                               preferred_element_type=F32) + bias

    def softmax_stage(s):
        m = jnp.max(s, axis=-1, keepdims=True)
        return m, jnp.exp2(s - m).astype(BF16)

    def value_stage(m, p, vcat):
        nk = vcat.shape[0]
        pv = jnp.dot(p, jnp.concatenate([vcat, ones[:nk]], axis=1), preferred_element_type=F32)
        return (merge_heads(jnp.broadcast_to(m, (2 * blk, LANES))),
                merge_heads(pv[:, LANES:]), merge_heads(pv[:, :LANES]))

    def update(ci, slices, m_c, l_c, o_c):
        if ci > 0:
            m_r = load(m_sc, slices)
            m_n = jnp.maximum(m_r, m_c)
            a = jnp.exp2(m_r - m_n)
            b = jnp.exp2(m_c - m_n)
            l_c = a * load(l_sc, slices) + b * l_c
            o_c = a * load(acc_sc, slices) + b * o_c
            m_c = m_n
        store(m_sc, slices, m_c)
        store(l_sc, slices, l_c)
        store(acc_sc, slices, o_c)

    blocks = []
    for ci, (window, d) in enumerate(DILATED_CONFIGS):
        sub = res // d
        ln = blk // sub
        for r in range(d):
            for n in range(SEQ // d // blk):
                slices = [slice(per * (r + d * c) + ln * n, per * (r + d * c) + ln * (n + 1))
                          for c in range(sub)]
                blocks.append((ci, slices, n == 0))

    kv_prev = [None, None]

    def score_stage(ci, slices, first):
        k_cur = load(kp, slices).astype(BF16)
        v_cur = load(vp, slices).astype(BF16)
        if first:
            bias = jnp.concatenate([bias_ref[ci, 0, :, blk:], bias_ref[ci, 1, :, blk:]], axis=0)
            out = scores(load(qp, slices), k_cur, bias), v_cur
        else:
            bias = jnp.concatenate([bias_ref[ci, 0], bias_ref[ci, 1]], axis=0)
            out = (scores(load(qp, slices), jnp.concatenate([kv_prev[0], k_cur], axis=0), bias),
                   jnp.concatenate([kv_prev[1], v_cur], axis=0))
        kv_prev[0], kv_prev[1] = k_cur, v_cur
        return out

    lead, lag = ATTN_LEAD, ATTN_LAG
    ahead = [score_stage(*blocks[j]) for j in range(lead)]
    behind = []
    for i, (ci, slices, _) in enumerate(blocks):
        s, vcat = ahead.pop(0)
        if i + lead < len(blocks):
            ahead.append(score_stage(*blocks[i + lead]))
        m, p = softmax_stage(s)
        behind.append((ci, slices, m, p, vcat))
        if len(behind) > lag:
            done = behind.pop(0)
            update(done[0], done[1], *value_stage(*done[2:]))
    for done in behind:
        update(done[0], done[1], *value_stage(*done[2:]))

    for g in range(SEQ // sq):
        o = load(acc_sc, residue_rows(g)) / load(l_sc, residue_rows(g))
        hi = o.astype(BF16)
        lo = (o - hi.astype(F32)).astype(BF16)
        back = jnp.dot(regroup, jnp.concatenate([hi, lo], axis=1), preferred_element_type=F32)
        o_ref[sq * g:sq * (g + 1), :] = back[:, :LANES] + back[:, LANES:]


def _attention(q, k, v, bias_tab):
    n_cfg = len(DILATED_CONFIGS)
    blk = ATTN_BLOCK
    seq_spec = pl.BlockSpec((SEQ, HEAD_PAIR), lambda b, p: (b, p))
    return pl.pallas_call(
        _attn_kernel,
        grid=(BATCH, N_PAIRS),
        in_specs=[seq_spec, seq_spec, seq_spec,
                  pl.BlockSpec((n_cfg, 2, blk, 2 * blk), lambda b, p: (0, p, 0, 0))],
        out_specs=seq_spec,
        out_shape=jax.ShapeDtypeStruct((TOKENS, B_WIDTH), F32),
        scratch_shapes=[pltpu.VMEM((SEQ, HEAD_PAIR), F32)] * 6,
        compiler_params=_cparams(("arbitrary", "arbitrary")),
        name="attention",
    )(q, k, v, bias_tab)


K4_TM = 1024
K4_PART = 256


def _top2_sum(a, b, c, d):
    hi1, lo1 = jnp.maximum(a, b), jnp.minimum(a, b)
    hi2, lo2 = jnp.maximum(c, d), jnp.minimum(c, d)
    return jnp.maximum(hi1, hi2) + jnp.maximum(jnp.minimum(hi1, hi2), jnp.maximum(lo1, lo2))


def _route(logits_t, rb_col):
    m = jnp.max(logits_t, axis=0, keepdims=True)
    e = jnp.exp(logits_t - m)
    probs = e / jnp.sum(e, axis=0, keepdims=True)
    sel = probs + rb_col
    sel_rows = [sel[i:i + 1, :] for i in range(N_EXPERTS)]
    prob_rows = [probs[i:i + 1, :] for i in range(N_EXPERTS)]
    gsz = EXPERTS_PER_GROUP
    score = [_top2_sum(*sel_rows[g * gsz:(g + 1) * gsz]) for g in range(N_EXPERT_GROUPS)]
    chosen = []
    for g in range(N_EXPERT_GROUPS):
        best = None
        for g2 in range(N_EXPERT_GROUPS):
            if g2 == g:
                continue
            c = (score[g] > score[g2]) if g2 < g else (score[g] >= score[g2])
            best = c if best is None else jnp.logical_and(best, c)
        for i in range(gsz):
            ei = g * gsz + i
            rank = jnp.zeros_like(sel_rows[ei])
            for j in range(gsz):
                if j == i:
                    continue
                ej = g * gsz + j
                ahead = (sel_rows[ej] >= sel_rows[ei]) if j < i else (sel_rows[ej] > sel_rows[ei])
                rank = rank + jnp.where(ahead, 1.0, 0.0)
            chosen.append(jnp.logical_and(best, rank < float(2)))
    picked = [jnp.where(chosen[i], prob_rows[i], 0.0) for i in range(N_EXPERTS)]
    denom = picked[0]
    for i in range(1, N_EXPERTS):
        denom = denom + picked[i]
    gates = jnp.concatenate([pk / denom for pk in picked], axis=0)
    onehot = jnp.concatenate([jnp.where(ch, 1.0, 0.0) for ch in chosen], axis=0)
    return gates, onehot


def _mid_kernel(a_ref, ob_ref, x_ref, mod_ref, gb_ref, wout_ref, n2g_ref, rwt_ref, rb_ref,
                x1_ref, h2_ref, gates_ref, onehot_ref, count_ref):
    for s in range(x_ref.shape[0] // K4_PART):
        rows = slice(s * K4_PART, (s + 1) * K4_PART)
        ob = ob_ref[rows, :]
        bn = ob * lax.rsqrt(jnp.mean(ob * ob, axis=-1, keepdims=True) + EPS) * gb_ref[...]
        mixed = jnp.dot(a_ref[rows, :], wout_ref[:A_WIDTH, :], preferred_element_type=F32)
        mixed = mixed + jnp.dot(bn.astype(BF16), wout_ref[A_WIDTH:, :],
                                preferred_element_type=F32)
        x1 = x_ref[rows, :] + mod_ref[2:3, :] * mixed
        x1_ref[rows, :] = x1
        h2 = x1 * lax.rsqrt(jnp.mean(x1 * x1, axis=-1, keepdims=True) + EPS) * n2g_ref[...]
        h2 = (h2 * (1.0 + mod_ref[4:5, :]) + mod_ref[3:4, :]).astype(BF16)
        h2_ref[rows, :] = h2
        logits_t = lax.dot_general(rwt_ref[...], h2, (((1,), (1,)), ((), ())),
                                   preferred_element_type=F32)
        gates_t, onehot_t = _route(logits_t, rb_ref[...])
        gates_ref[:, rows] = gates_t
        onehot_ref[:, rows] = onehot_t
        for t in range(K4_PART // MOE_TOK):
            count_ref[s * (K4_PART // MOE_TOK) + t] = jnp.sum(
                onehot_t[:, t * MOE_TOK:(t + 1) * MOE_TOK], axis=1, keepdims=True)


def _mid(out_a, out_b, x, layer, mod, gb, wout, n2g, rwt, rb_col):
    tm = K4_TM
    tok = pl.BlockSpec((tm, D_MODEL), lambda i: (i, 0))
    half = pl.BlockSpec((tm, A_WIDTH), lambda i: (i, 0))
    route = pl.BlockSpec((N_EXPERTS, tm), lambda i: (0, i))
    return pl.pallas_call(
        _mid_kernel,
        grid=(TOKENS // tm,),
        in_specs=[
            half, half, tok,
            _mod_of(layer, tm),
            _of_layer(layer, 1, B_WIDTH),
            _of_layer(layer, D_MODEL, D_MODEL),
            _of_layer(layer, 1, D_MODEL),
            pl.BlockSpec((N_EXPERTS, D_MODEL), lambda i: (0, 0)),
            pl.BlockSpec((N_EXPERTS, 1), lambda i: (0, 0)),
        ],
        out_specs=[tok, tok, route, route,
                   pl.BlockSpec((tm // MOE_TOK, N_EXPERTS, 1), lambda i: (i, 0, 0))],
        out_shape=[jax.ShapeDtypeStruct((TOKENS, D_MODEL), F32),
                   jax.ShapeDtypeStruct((TOKENS, D_MODEL), BF16),
                   jax.ShapeDtypeStruct((N_EXPERTS, TOKENS), F32),
                   jax.ShapeDtypeStruct((N_EXPERTS, TOKENS), F32),
                   jax.ShapeDtypeStruct((TOKENS // MOE_TOK, N_EXPERTS, 1), F32)],
        compiler_params=_cparams(("arbitrary",)),
        name="mid",
    )(out_a, out_b, x, mod, gb, wout, n2g, rwt, rb_col)


MOE_TOK = 256
ROW_ALIGN = 16
MOE_SUB = 4
MOE_LB = 2 * MOE_TOK + N_EXPERTS * ROW_ALIGN
FFN_TM = 1024
N_TOK_TILES = TOKENS // MOE_TOK
ROWS_MAX = -(-(2 * TOKENS + N_TOK_TILES * N_EXPERTS * ROW_ALIGN
               + N_EXPERTS * (FFN_TM - ROW_ALIGN)) // FFN_TM) * FFN_TM


N_FFN_TILES = ROWS_MAX // FFN_TM


def _plan_kernel(count_ref, n8_ref, local_ref, glob_ref, texp_ref, nact_ref):
    def prefix(a, b):
        return jnp.dot(a, b, preferred_element_type=F32, precision=lax.Precision.HIGHEST)

    row = lax.broadcasted_iota(jnp.int32, (LANES, LANES), 0)
    col = lax.broadcasted_iota(jnp.int32, (LANES, LANES), 1)
    real = jnp.logical_and(row < N_TOK_TILES, col < N_EXPERTS)
    n8 = jnp.where(real, jnp.maximum(jnp.ceil(count_ref[...] / ROW_ALIGN), 1.0) * ROW_ALIGN, 0.0)
    earlier_lane = jnp.where(row < col, 1.0, 0.0)
    earlier_row = jnp.where(col < row, 1.0, 0.0)
    local = prefix(n8, earlier_lane)
    seg = jnp.ceil(jnp.sum(n8, axis=0, keepdims=True) / FFN_TM) * FFN_TM
    seg_start = prefix(jnp.broadcast_to(seg, (LANES, LANES)), earlier_lane)[0:1]
    glob = seg_start + prefix(earlier_row, n8)
    n_active = jnp.sum(seg, axis=1, keepdims=True) / FFN_TM
    tile = row[:, 0:1].astype(F32)
    ended = jnp.logical_and(tile * FFN_TM >= seg_start + seg, col < N_EXPERTS)
    texp = jnp.sum(jnp.where(ended, 1.0, 0.0), axis=1, keepdims=True)
    texp = jnp.minimum(texp, float(N_EXPERTS - 1))
    last = jnp.sum(jnp.where(tile == n_active - 1.0, texp, 0.0), axis=0, keepdims=True)
    texp = jnp.where(tile < n_active, texp, last)
    n8_ref[...] = n8.astype(jnp.int32)
    local_ref[...] = local.astype(jnp.int32)
    glob_ref[...] = glob.astype(jnp.int32)
    texp_ref[...] = jnp.broadcast_to(texp, (LANES, LANES)).astype(jnp.int32)
    nact_ref[...] = jnp.broadcast_to(n_active, (8, LANES)).astype(jnp.int32)


def _moe_plan(count):
    assert max(N_TOK_TILES, N_FFN_TILES, N_EXPERTS) <= LANES
    count = count.reshape(N_TOK_TILES, N_EXPERTS)
    count = jnp.pad(count, ((0, LANES - N_TOK_TILES), (0, LANES - N_EXPERTS)))
    full = jax.ShapeDtypeStruct((LANES, LANES), jnp.int32)
    n8, local, glob, texp, nact = pl.pallas_call(
        _plan_kernel,
        out_shape=[full, full, full, full, jax.ShapeDtypeStruct((8, LANES), jnp.int32)],
        name="plan",
    )(count)

    def runs(t):
        return t[:N_TOK_TILES, :N_EXPERTS]

    return dict(n8=runs(n8).reshape(-1), local=runs(local).reshape(-1),
                glob=runs(glob).reshape(-1),
                local_col=runs(local).astype(F32).reshape(N_TOK_TILES, N_EXPERTS, 1),
                texp=texp[:N_FFN_TILES, 0], n_active=nact[0, :1])


def _run_copies(tile, n8_ref, local_ref, glob_ref, make_copy, start):
    for e in range(N_EXPERTS):
        idx = tile * N_EXPERTS + e
        cp = make_copy(pl.multiple_of(local_ref[idx], ROW_ALIGN),
                       pl.multiple_of(glob_ref[idx], ROW_ALIGN),
                       pl.multiple_of(n8_ref[idx], ROW_ALIGN))
        if start:
            cp.start()
        else:
            cp.wait()


def _fill_before(before_sc):
    @pl.when(pl.program_id(0) == 0)
    def _():
        src = lax.broadcasted_iota(jnp.int32, before_sc.shape, 0)
        dst = lax.broadcasted_iota(jnp.int32, before_sc.shape, 1)
        before_sc[...] = jnp.where(src < dst, 1.0, 0.0).astype(BF16)


def _sorted_rows(onehot, local_col, before):
    rank = jnp.dot(onehot.astype(BF16), before, preferred_element_type=F32)
    pos = local_col + rank
    chosen = onehot > 0.5
    lo = jnp.min(jnp.where(chosen, pos, float(MOE_LB)), axis=0, keepdims=True)
    hi = jnp.max(jnp.where(chosen, pos, -1.0), axis=0, keepdims=True)
    return pos, chosen, lo, hi


def _dispatch_kernel(n8_ref, local_ref, glob_ref, h_ref, oh_ref, lcol_ref, xs_ref,
                     lbuf, before_sc, sem):
    j = pl.program_id(0)
    last = pl.num_programs(0) - 1
    par = j % 2
    _fill_before(before_sc)
    row = lax.broadcasted_iota(jnp.int32, (MOE_LB, MOE_TOK), 0).astype(F32)
    toks = [slice(u * MOE_TOK, (u + 1) * MOE_TOK) for u in range(MOE_SUB)]
    spots = [_sorted_rows(oh_ref[:, tok], lcol_ref[u], before_sc[...])[2:]
             for u, tok in enumerate(toks)]
    perms = [(jnp.where(row == lo, 1.0, 0.0) + jnp.where(row == hi, 1.0, 0.0)).astype(BF16)
             for lo, hi in spots]
    for u, tok in enumerate(toks):
        lbuf[par * MOE_SUB + u] = jnp.dot(perms[u], h_ref[tok, :],
                                          preferred_element_type=F32).astype(BF16)

    def copy_from(slot):
        def make(lo_, go_, size):
            return pltpu.make_async_copy(lbuf.at[slot, pl.ds(lo_, size), :],
                                         xs_ref.at[pl.ds(go_, size), :], sem.at[slot])
        return make

    refs = (n8_ref, local_ref, glob_ref)
    for u in range(MOE_SUB):
        _run_copies(j * MOE_SUB + u, *refs, copy_from(par * MOE_SUB + u), True)

    @pl.when(j > 0)
    def _():
        for u in range(MOE_SUB):
            _run_copies((j - 1) * MOE_SUB + u, *refs, copy_from((1 - par) * MOE_SUB + u), False)

    @pl.when(j == last)
    def _():
        for u in range(MOE_SUB):
            _run_copies(j * MOE_SUB + u, *refs, copy_from(par * MOE_SUB + u), False)


def _dispatch(plan, h2, onehot):
    step_tok = MOE_SUB * MOE_TOK
    grid_spec = pltpu.PrefetchScalarGridSpec(
        num_scalar_prefetch=3,
        grid=(N_TOK_TILES // MOE_SUB,),
        in_specs=[
            pl.BlockSpec((step_tok, D_MODEL), lambda j, *_: (j, 0)),
            pl.BlockSpec((N_EXPERTS, step_tok), lambda j, *_: (0, j)),
            pl.BlockSpec((MOE_SUB, N_EXPERTS, 1), lambda j, *_: (j, 0, 0)),
        ],
        out_specs=pl.BlockSpec(memory_space=pl.ANY),
        scratch_shapes=[pltpu.VMEM((2 * MOE_SUB, MOE_LB, D_MODEL), BF16),
                        pltpu.VMEM((MOE_TOK, MOE_TOK), BF16),
                        pltpu.SemaphoreType.DMA((2 * MOE_SUB,))],
    )
    return pl.pallas_call(
        _dispatch_kernel,
        grid_spec=grid_spec,
        out_shape=jax.ShapeDtypeStruct((ROWS_MAX, D_MODEL), BF16),
        compiler_params=_cparams(("arbitrary",)),
        name="dispatch",
    )(plan["n8"], plan["local"], plan["glob"], h2, onehot, plan["local_col"])


def _experts_kernel(texp_ref, nact_ref, xs_ref, wg_ref, wu_ref, wd_ref, ys_ref, wg_b, wu_b, wd_b):
    i = pl.program_id(0)
    active = i < nact_ref[0]
    new_expert = jnp.logical_or(i == 0, texp_ref[i] != texp_ref[jnp.maximum(i - 1, 0)])

    @pl.when(jnp.logical_and(active, new_expert))
    def _():
        wg_b[...] = wg_ref[...].astype(BF16)
        wu_b[...] = wu_ref[...].astype(BF16)
        wd_b[...] = wd_ref[...].astype(BF16)

    @pl.when(active)
    def _():
        x = xs_ref[...]
        hg = jnp.dot(x, wg_b[...], preferred_element_type=F32)
        hu = jnp.dot(x, wu_b[...], preferred_element_type=F32)
        act = (jax.nn.silu(hg) * hu).astype(BF16)
        ys_ref[...] = jnp.dot(act, wd_b[...], preferred_element_type=F32).astype(BF16)


def _experts(plan, xs, layer, wg, wu, wd):
    def rows(i, texp, nact):
        return (jnp.minimum(i, nact[0] - 1), 0)

    def expert(i, texp, nact):
        return (layer, texp[i], 0, 0)

    grid_spec = pltpu.PrefetchScalarGridSpec(
        num_scalar_prefetch=2,
        grid=(ROWS_MAX // FFN_TM,),
        in_specs=[
            pl.BlockSpec((FFN_TM, D_MODEL), rows),
            pl.BlockSpec((None, None, D_MODEL, D_EXPERT), expert),
            pl.BlockSpec((None, None, D_MODEL, D_EXPERT), expert),
            pl.BlockSpec((None, None, D_EXPERT, D_MODEL), expert),
        ],
        out_specs=pl.BlockSpec((FFN_TM, D_MODEL), rows),
        scratch_shapes=[pltpu.VMEM((D_MODEL, D_EXPERT), BF16), pltpu.VMEM((D_MODEL, D_EXPERT), BF16),
                        pltpu.VMEM((D_EXPERT, D_MODEL), BF16)],
    )
    return pl.pallas_call(
        _experts_kernel,
        grid_spec=grid_spec,
        out_shape=jax.ShapeDtypeStruct((ROWS_MAX, D_MODEL), BF16),
        compiler_params=_cparams(("arbitrary",)),
        name="experts",
    )(plan["texp"], plan["n_active"], xs, wg, wu, wd)


def _combine_kernel(n8_ref, local_ref, glob_ref, oh_ref, g_ref, lcol_ref, x_ref, mod_ref, fg_ref,
                    ys_ref, o_ref, ybuf, before_sc, sem, *, final_norm):
    j = pl.program_id(0)
    n_steps = pl.num_programs(0)
    par = j % 2
    refs = (n8_ref, local_ref, glob_ref)
    _fill_before(before_sc)

    def copy_to(slot):
        def make(lo_, go_, size):
            return pltpu.make_async_copy(ys_ref.at[pl.ds(go_, size), :],
                                         ybuf.at[slot, pl.ds(lo_, size), :], sem.at[slot])
        return make

    @pl.when(j == 0)
    def _():
        ybuf[...] = jnp.zeros(ybuf.shape, BF16)
        for u in range(MOE_SUB):
            _run_copies(u, *refs, copy_to(u), True)

    @pl.when(j + 1 < n_steps)
    def _():
        for u in range(MOE_SUB):
            _run_copies((j + 1) * MOE_SUB + u, *refs, copy_to((1 - par) * MOE_SUB + u), True)

    row = lax.broadcasted_iota(jnp.int32, (MOE_LB, MOE_TOK), 0).astype(F32)
    toks = [slice(u * MOE_TOK, (u + 1) * MOE_TOK) for u in range(MOE_SUB)]
    perms = []
    for u, tok in enumerate(toks):
        pos, chosen, lo, hi = _sorted_rows(oh_ref[:, tok], lcol_ref[u], before_sc[...])
        gate = jnp.where(chosen, g_ref[:, tok], 0.0)
        g_lo = jnp.sum(jnp.where(pos == lo, gate, 0.0), axis=0, keepdims=True)
        g_hi = jnp.sum(jnp.where(pos == hi, gate, 0.0), axis=0, keepdims=True)
        perms.append((jnp.where(row == lo, g_lo, 0.0)
                      + jnp.where(row == hi, g_hi, 0.0)).astype(BF16))
    for u in range(MOE_SUB):
        _run_copies(j * MOE_SUB + u, *refs, copy_to(par * MOE_SUB + u), False)
    for u, tok in enumerate(toks):
        y = lax.dot_general(perms[u], ybuf[par * MOE_SUB + u], (((0,), (0,)), ((), ())),
                            preferred_element_type=F32)
        x = x_ref[tok, :] + mod_ref[5:6, :] * y
        if final_norm:
            x = x * lax.rsqrt(jnp.mean(x * x, axis=-1, keepdims=True) + EPS) * fg_ref[...]
        o_ref[tok, :] = x


N_FRONT_IN = 8


def _combine_front_kernel(*refs):
    n_in = 3 + 7
    comb_in, front_in = refs[:n_in], refs[n_in:n_in + N_FRONT_IN]
    o_ref, *front_out = refs[n_in + N_FRONT_IN:n_in + N_FRONT_IN + 5]
    scratch = refs[n_in + N_FRONT_IN + 5:]
    _combine_kernel(*comb_in, o_ref, *scratch, final_norm=False)
    _front_kernel(o_ref, *front_in, *front_out)


def _combine(plan, onehot, gates, ys, x, layer, mod, final_g, front_params=None):
    step_tok = MOE_SUB * MOE_TOK
    route = pl.BlockSpec((N_EXPERTS, step_tok), lambda j, *_: (0, j))
    tok = pl.BlockSpec((step_tok, D_MODEL), lambda j, *_: (j, 0))
    in_specs = [
        route, route,
        pl.BlockSpec((MOE_SUB, N_EXPERTS, 1), lambda j, *_: (j, 0, 0)),
        tok, _mod_of(layer, step_tok),
        pl.BlockSpec((1, D_MODEL), lambda j, *_: (0, 0)),
        pl.BlockSpec(memory_space=pl.ANY),
    ]
    args = [onehot, gates, plan["local_col"], x, mod, final_g, ys]
    out_specs = tok
    out_shape = jax.ShapeDtypeStruct((TOKENS, D_MODEL), F32)
    body = functools.partial(_combine_kernel, final_norm=True)
    if front_params is not None:
        assert step_tok == K1_TM and len(front_params) == N_FRONT_IN
        half = pl.BlockSpec((step_tok, A_WIDTH), lambda j, *_: (j, 0))
        in_specs += _front_specs(layer + 1, step_tok)
        args += list(front_params)
        out_specs = [tok, half, half, half, half]
        out_shape = [out_shape, jax.ShapeDtypeStruct((TOKENS, A_WIDTH), BF16)] + \
                    [jax.ShapeDtypeStruct((TOKENS, B_WIDTH), F32)] * 3
        body = _combine_front_kernel
    grid_spec = pltpu.PrefetchScalarGridSpec(
        num_scalar_prefetch=3,
        grid=(N_TOK_TILES // MOE_SUB,),
        in_specs=in_specs,
        out_specs=out_specs,
        scratch_shapes=[pltpu.VMEM((2 * MOE_SUB, MOE_LB, D_MODEL), BF16),
                        pltpu.VMEM((MOE_TOK, MOE_TOK), BF16),
                        pltpu.SemaphoreType.DMA((2 * MOE_SUB,))],
    )
    return pl.pallas_call(
        body,
        grid_spec=grid_spec,
        out_shape=out_shape,
        compiler_params=_cparams(("arbitrary",)),
        name="combine",
    )(plan["n8"], plan["local"], plan["glob"], *args)


def kernel(x, c, rel_bias, router_w, router_b, mod_w, mod_b, norm1_g, w_in, gmlp_ln_g, gmlp_ln_b,
           gmlp_ws, gmlp_bs, out_norm_a_g, out_norm_b_g, w_out, norm2_g, moe_w_gate, moe_w_up,
           moe_w_down, final_g):
    mod = _modulation(c, mod_w, mod_b).reshape(DEPTH, BATCH, N_MOD, D_MODEL)
    bias_tab = _bias_tables(rel_bias)
    rwt = router_w.T.astype(BF16)
    rb_col = router_b.reshape(N_EXPERTS, 1)
    xt = x.reshape(TOKENS, D_MODEL)
    front_params = (
        mod, norm1_g.reshape(DEPTH, 1, D_MODEL), w_in.astype(BF16),
        gmlp_ln_g.reshape(DEPTH, 1, A_WIDTH), gmlp_ln_b.reshape(DEPTH, 1, A_WIDTH),
        gmlp_ws.astype(BF16).reshape(DEPTH, A_GROUPS // 2, 2 * CHUNK, CHUNK),
        jnp.repeat(jnp.swapaxes(gmlp_bs, 1, 2), HEAD_DIM, axis=2),
        out_norm_a_g.reshape(DEPTH, 1, A_WIDTH))
    gb = out_norm_b_g.reshape(DEPTH, 1, B_WIDTH)
    wout_b = w_out.astype(BF16)
    n2g = norm2_g.reshape(DEPTH, 1, D_MODEL)
    fg = final_g.reshape(1, D_MODEL)

    out_a, q, k, v = _front(xt, 0, front_params)
    for l in range(DEPTH):
        out_b = _attention(q, k, v, bias_tab)
        xt, h2, gates, onehot, count = _mid(out_a, out_b, xt, l, mod, gb, wout_b, n2g, rwt, rb_col)
        plan = _moe_plan(count)
        xs = _dispatch(plan, h2, onehot)
        ys = _experts(plan, xs, l, moe_w_gate, moe_w_up, moe_w_down)
        if l + 1 < DEPTH:
            xt, out_a, q, k, v = _combine(plan, onehot, gates, ys, xt, l, mod, fg, front_params)
        else:
            xt = _combine(plan, onehot, gates, ys, xt, l, mod, fg)
    return xt.reshape(BATCH, SEQ, D_MODEL)
```

```python
import functools
import math

import numpy as np
import jax
import jax.numpy as jnp
from jax import lax
from jax.experimental import pallas as pl
from jax.experimental.pallas import tpu as pltpu

D_MODEL = 1024
BATCH = 8
SEQ = 2048
DEPTH = 2
TOKENS = BATCH * SEQ
HEAD_DIM = 64
A_WIDTH = 512
B_WIDTH = 512
A_GROUPS = 8
IN_COLS = 2 * A_WIDTH + 3 * B_WIDTH
CHUNK = 128
DILATED_CONFIGS = ((128, 1), (512, 4), (2048, 16))
ATTN_BLOCK = 128
REL_BUCKETS = 32
REL_MAX_EXACT = REL_BUCKETS // 2
REL_MAX_DISTANCE = 2048
N_EXPERTS = 16
N_EXPERT_GROUPS = 4
EXPERTS_PER_GROUP = 4
D_EXPERT = 512
N_MOD = 6
EPS = 1e-6
NEG_INF = -1e30

LANES = 128
HEAD_PAIR = 2 * HEAD_DIM
N_PAIRS = B_WIDTH // HEAD_PAIR
RESIDUES = max(d for _, d in DILATED_CONFIGS)
LOG2E = math.log2(math.e)

F32 = jnp.float32
BF16 = jnp.bfloat16

VMEM_LIMIT = 56 * 1024 * 1024


def _cparams(sem):
    return pltpu.CompilerParams(dimension_semantics=sem, vmem_limit_bytes=VMEM_LIMIT)


def _gelu(x):
    return 0.5 * x * (1.0 + lax.erf(x * math.sqrt(0.5)))


MOD_TN = 1024


def _mod_kernel(c_ref, w_ref, b_ref, o_ref):
    ca = jax.nn.silu(c_ref[...])
    o_ref[...] = jnp.dot(ca.astype(BF16), w_ref[...].astype(BF16),
                         preferred_element_type=F32) + b_ref[...]


def _modulation(c, mod_w, mod_b):
    n_cols = N_MOD * D_MODEL
    return pl.pallas_call(
        _mod_kernel,
        grid=(DEPTH, n_cols // MOD_TN),
        in_specs=[
            pl.BlockSpec((BATCH, D_MODEL), lambda l, j: (0, 0)),
            pl.BlockSpec((None, D_MODEL, MOD_TN), lambda l, j: (l, 0, j)),
            pl.BlockSpec((None, 1, MOD_TN), lambda l, j: (l, 0, j)),
        ],
        out_specs=pl.BlockSpec((None, BATCH, MOD_TN), lambda l, j: (l, 0, j)),
        out_shape=jax.ShapeDtypeStruct((DEPTH, BATCH, n_cols), F32),
        compiler_params=_cparams(("arbitrary", "arbitrary")),
        name="modulation",
    )(c, mod_w, mod_b.reshape(DEPTH, 1, n_cols))


K1_TM = 1024


def _front_kernel(x_ref, mod_ref, n1g_ref, win_ref, lng_ref, lnb_ref, ws_ref, bs_ref, ga_ref,
                  a_ref, q_ref, k_ref, v_ref):
    x = x_ref[...]
    tm = x.shape[0]
    h = x * lax.rsqrt(jnp.mean(x * x, axis=-1, keepdims=True) + EPS) * n1g_ref[...]
    h = h * (1.0 + mod_ref[1:2, :]) + mod_ref[0:1, :]
    proj = jnp.dot(h.astype(BF16), win_ref[...], preferred_element_type=F32)

    q_ref[...] = proj[:, 2 * A_WIDTH:2 * A_WIDTH + B_WIDTH] * (HEAD_DIM ** -0.5 * LOG2E)
    k_ref[...] = proj[:, 2 * A_WIDTH + B_WIDTH:2 * A_WIDTH + 2 * B_WIDTH]
    v_ref[...] = proj[:, 2 * A_WIDTH + 2 * B_WIDTH:]

    u = _gelu(proj[:, :A_WIDTH])
    va = _gelu(proj[:, A_WIDTH:2 * A_WIDTH])
    mu = jnp.mean(va, axis=-1, keepdims=True)
    vc = va - mu
    vln = vc * lax.rsqrt(jnp.mean(vc * vc, axis=-1, keepdims=True) + EPS)
    vln = (vln * lng_ref[...] + lnb_ref[...]).astype(BF16)

    row = lax.broadcasted_iota(jnp.int32, (2 * CHUNK, CHUNK), 0)
    col = lax.broadcasted_iota(jnp.int32, (2 * CHUNK, CHUNK), 1)
    causal = (row % CHUNK) >= col
    first_group = lax.broadcasted_iota(jnp.int32, (CHUNK, LANES), 1) < HEAD_DIM
    wmix = [jnp.where(causal, ws_ref[p], jnp.zeros((), BF16)) for p in range(A_GROUPS // 2)]
    for c in range(tm // CHUNK):
        rows = slice(c * CHUNK, (c + 1) * CHUNK)
        parts = []
        for p in range(A_GROUPS // 2):
            vp = vln[rows, p * LANES:(p + 1) * LANES]
            r = jnp.dot(wmix[p], vp, preferred_element_type=F32)
            parts.append(jnp.where(first_group, r[:CHUNK], r[CHUNK:]))
        s = jnp.concatenate(parts, axis=-1) + bs_ref[...]
        oa = u[rows] * s
        oa = oa * lax.rsqrt(jnp.mean(oa * oa, axis=-1, keepdims=True) + EPS) * ga_ref[...]
        a_ref[rows, :] = oa.astype(BF16)


def _of_layer(layer, *tail):
    return pl.BlockSpec((None, *tail), lambda *_: (layer,) + (0,) * len(tail))


def _mod_of(layer, tokens_per_step):
    steps_per_seq = SEQ // tokens_per_step
    return pl.BlockSpec((None, None, N_MOD, D_MODEL),
                        lambda i, *_: (layer, i // steps_per_seq, 0, 0))


def _front_specs(layer, tm):
    return [
        _mod_of(layer, tm),
        _of_layer(layer, 1, D_MODEL),
        _of_layer(layer, D_MODEL, IN_COLS),
        _of_layer(layer, 1, A_WIDTH), _of_layer(layer, 1, A_WIDTH),
        _of_layer(layer, A_GROUPS // 2, 2 * CHUNK, CHUNK),
        _of_layer(layer, CHUNK, A_WIDTH),
        _of_layer(layer, 1, A_WIDTH),
    ]


def _front(x, layer, front_params):
    tm = K1_TM
    half = pl.BlockSpec((tm, A_WIDTH), lambda i: (i, 0))
    return pl.pallas_call(
        _front_kernel,
        grid=(TOKENS // tm,),
        in_specs=[pl.BlockSpec((tm, D_MODEL), lambda i: (i, 0))] + _front_specs(layer, tm),
        out_specs=[half, half, half, half],
        out_shape=[jax.ShapeDtypeStruct((TOKENS, A_WIDTH), BF16)] +
                  [jax.ShapeDtypeStruct((TOKENS, B_WIDTH), F32)] * 3,
        compiler_params=_cparams(("arbitrary",)),
        name="front",
    )(x, *front_params)


def _t5_bucket_np(dist):
    dist = np.maximum(dist, 0)
    ratio = np.log(np.maximum(dist, 1) / REL_MAX_EXACT) / np.log(REL_MAX_DISTANCE / REL_MAX_EXACT)
    large = REL_MAX_EXACT + np.floor(ratio * (REL_BUCKETS - REL_MAX_EXACT)).astype(np.int64)
    large = np.minimum(large, REL_BUCKETS - 1)
    return np.where(dist < REL_MAX_EXACT, dist, large).astype(np.int32)


def _bias_tables(rel_bias):
    blk = ATTN_BLOCK
    n_cfg = len(DILATED_CONFIGS)
    n_rel = 3 * blk
    rel = 2 * blk - 1 - np.arange(n_rel)
    bucket, valid, rows, cols = [], [], [], []
    for window, d in DILATED_CONFIGS:
        span = window // d
        bucket.append(_t5_bucket_np(np.clip(rel, 0, span) * d))
        valid.append((rel >= 0) & (rel <= span))
        sub = RESIDUES // d
        ln = blk // sub
        pos = np.arange(blk)
        rows.append(np.eye(blk, dtype=np.float32)[(pos % ln) * sub + pos // ln])
        cols.append(np.kron(np.eye(2, dtype=np.float32), rows[-1]))
    w = jnp.transpose(rel_bias.astype(F32)[np.stack(bucket)], (0, 2, 1)) * LOG2E
    w = jnp.where(np.stack(valid)[:, None, :], w, NEG_INF * LOG2E)
    flat = jnp.tile(w, (1, 1, blk))
    skew = flat[:, :, blk - 1:blk - 1 + blk * (n_rel - 1)].reshape(n_cfg, -1, blk, n_rel - 1)
    tab = skew[..., :2 * blk]
    return jnp.einsum('cpi,chik,cqk->chpq', np.stack(rows), tab, np.stack(cols),
                      precision=lax.Precision.HIGHEST)


ATTN_LEAD = 3
ATTN_LAG = 1


def _attn_kernel(q_ref, k_ref, v_ref, bias_ref, o_ref, qp, kp, vp, m_sc, l_sc, acc_sc):
    blk = ATTN_BLOCK
    res = RESIDUES
    per = SEQ // res
    sq = res * res
    seg = sq // res
    lane = lax.broadcasted_iota(jnp.int32, (blk, LANES), 1)
    head0 = lane < HEAD_DIM
    ones = jnp.ones((2 * blk, LANES), BF16)

    pa = lax.broadcasted_iota(jnp.int32, (sq, sq), 0)
    pb = lax.broadcasted_iota(jnp.int32, (sq, sq), 1)
    regroup = jnp.where(pb == res * (pa % res) + pa // res, 1.0, 0.0).astype(BF16)

    def residue_rows(g):
        return [slice(per * r + seg * g, per * r + seg * (g + 1)) for r in range(res)]

    def load(ref, slices):
        return jnp.concatenate([ref[s, :] for s in slices], axis=0) if len(slices) > 1 \
            else ref[slices[0], :]

    def store(ref, slices, val):
        ln = val.shape[0] // len(slices)
        for i, s in enumerate(slices):
            ref[s, :] = val[i * ln:(i + 1) * ln]

    for g in range(SEQ // sq):
        rows = slice(sq * g, sq * (g + 1))
        qkv = jnp.concatenate([q_ref[rows, :], k_ref[rows, :], v_ref[rows, :]], axis=1)
        qkv = jnp.dot(regroup, qkv.astype(BF16), preferred_element_type=F32)
        store(qp, residue_rows(g), qkv[:, :LANES])
        store(kp, residue_rows(g), qkv[:, LANES:2 * LANES])
        store(vp, residue_rows(g), qkv[:, 2 * LANES:])

    def merge_heads(t):
        return jnp.where(head0, t[:blk], t[blk:])

    def scores(q, kcat, bias):
        zero = jnp.zeros_like(q)
        qs = jnp.concatenate([jnp.where(head0, q, zero), jnp.where(head0, zero, q)],
                             axis=0).astype(BF16)
        return lax.dot_general(qs, kcat, (((1,), (1,)), ((), ())),
                               preferred_element_type=F32) + bias

    def softmax_stage(s):
        m = jnp.max(s, axis=-1, keepdims=True)
        return m, jnp.exp2(s - m).astype(BF16)

    def value_stage(m, p, vcat):
        nk = vcat.shape[0]
        pv = jnp.dot(p, jnp.concatenate([vcat, ones[:nk]], axis=1), preferred_element_type=F32)
        return (merge_heads(jnp.broadcast_to(m, (2 * blk, LANES))),
                merge_heads(pv[:, LANES:]), merge_heads(pv[:, :LANES]))

    def update(ci, slices, m_c, l_c, o_c):
        if ci > 0:
            m_r = load(m_sc, slices)
            m_n = jnp.maximum(m_r, m_c)
            a = jnp.exp2(m_r - m_n)
            b = jnp.exp2(m_c - m_n)
            l_c = a * load(l_sc, slices) + b * l_c
            o_c = a * load(acc_sc, slices) + b * o_c
            m_c = m_n
        store(m_sc, slices, m_c)
        store(l_sc, slices, l_c)
        store(acc_sc, slices, o_c)

    blocks = []
    for ci, (window, d) in enumerate(DILATED_CONFIGS):
        sub = res // d
        ln = blk // sub
        for r in range(d):
            for n in range(SEQ // d // blk):
                slices = [slice(per * (r + d * c) + ln * n, per * (r + d * c) + ln * (n + 1))
                          for c in range(sub)]
                blocks.append((ci, slices, n == 0))

    kv_prev = [None, None]

    def score_stage(ci, slices, first):
        k_cur = load(kp, slices).astype(BF16)
        v_cur = load(vp, slices).astype(BF16)
        if first:
            bias = jnp.concatenate([bias_ref[ci, 0, :, blk:], bias_ref[ci, 1, :, blk:]], axis=0)
            out = scores(load(qp, slices), k_cur, bias), v_cur
        else:
            bias = jnp.concatenate([bias_ref[ci, 0], bias_ref[ci, 1]], axis=0)
            out = (scores(load(qp, slices), jnp.concatenate([kv_prev[0], k_cur], axis=0), bias),
                   jnp.concatenate([kv_prev[1], v_cur], axis=0))
        kv_prev[0], kv_prev[1] = k_cur, v_cur
        return out

    lead, lag = ATTN_LEAD, ATTN_LAG
    ahead = [score_stage(*blocks[j]) for j in range(lead)]
    behind = []
    for i, (ci, slices, _) in enumerate(blocks):
        s, vcat = ahead.pop(0)
        if i + lead < len(blocks):
            ahead.append(score_stage(*blocks[i + lead]))
        m, p = softmax_stage(s)
        behind.append((ci, slices, m, p, vcat))
        if len(behind) > lag:
            done = behind.pop(0)
            update(done[0], done[1], *value_stage(*done[2:]))
    for done in behind:
        update(done[0], done[1], *value_stage(*done[2:]))

    for g in range(SEQ // sq):
        o = load(acc_sc, residue_rows(g)) / load(l_sc, residue_rows(g))
        hi = o.astype(BF16)
        lo = (o - hi.astype(F32)).astype(BF16)
        back = jnp.dot(regroup, jnp.concatenate([hi, lo], axis=1), preferred_element_type=F32)
        o_ref[sq * g:sq * (g + 1), :] = back[:, :LANES] + back[:, LANES:]


def _attention(q, k, v, bias_tab):
    n_cfg = len(DILATED_CONFIGS)
    blk = ATTN_BLOCK
    seq_spec = pl.BlockSpec((SEQ, HEAD_PAIR), lambda b, p: (b, p))
    return pl.pallas_call(
        _attn_kernel,
        grid=(BATCH, N_PAIRS),
        in_specs=[seq_spec, seq_spec, seq_spec,
                  pl.BlockSpec((n_cfg, 2, blk, 2 * blk), lambda b, p: (0, p, 0, 0))],
        out_specs=seq_spec,
        out_shape=jax.ShapeDtypeStruct((TOKENS, B_WIDTH), F32),
        scratch_shapes=[pltpu.VMEM((SEQ, HEAD_PAIR), F32)] * 6,
        compiler_params=_cparams(("arbitrary", "arbitrary")),
        name="attention",
    )(q, k, v, bias_tab)


K4_TM = 1024
K4_PART = 256


def _top2_sum(a, b, c, d):
    hi1, lo1 = jnp.maximum(a, b), jnp.minimum(a, b)
    hi2, lo2 = jnp.maximum(c, d), jnp.minimum(c, d)
    return jnp.maximum(hi1, hi2) + jnp.maximum(jnp.minimum(hi1, hi2), jnp.maximum(lo1, lo2))


def _route(logits_t, rb_col):
    m = jnp.max(logits_t, axis=0, keepdims=True)
    e = jnp.exp(logits_t - m)
    probs = e / jnp.sum(e, axis=0, keepdims=True)
    sel = probs + rb_col
    sel_rows = [sel[i:i + 1, :] for i in range(N_EXPERTS)]
    prob_rows = [probs[i:i + 1, :] for i in range(N_EXPERTS)]
    gsz = EXPERTS_PER_GROUP
    score = [_top2_sum(*sel_rows[g * gsz:(g + 1) * gsz]) for g in range(N_EXPERT_GROUPS)]
    chosen = []
    for g in range(N_EXPERT_GROUPS):
        best = None
        for g2 in range(N_EXPERT_GROUPS):
            if g2 == g:
                continue
            c = (score[g] > score[g2]) if g2 < g else (score[g] >= score[g2])
            best = c if best is None else jnp.logical_and(best, c)
        for i in range(gsz):
            ei = g * gsz + i
            rank = jnp.zeros_like(sel_rows[ei])
            for j in range(gsz):
                if j == i:
                    continue
                ej = g * gsz + j
                ahead = (sel_rows[ej] >= sel_rows[ei]) if j < i else (sel_rows[ej] > sel_rows[ei])
                rank = rank + jnp.where(ahead, 1.0, 0.0)
            chosen.append(jnp.logical_and(best, rank < float(2)))
    picked = [jnp.where(chosen[i], prob_rows[i], 0.0) for i in range(N_EXPERTS)]
    denom = picked[0]
    for i in range(1, N_EXPERTS):
        denom = denom + picked[i]
    gates = jnp.concatenate([pk / denom for pk in picked], axis=0)
    onehot = jnp.concatenate([jnp.where(ch, 1.0, 0.0) for ch in chosen], axis=0)
    return gates, onehot


def _mid_kernel(a_ref, ob_ref, x_ref, mod_ref, gb_ref, wout_ref, n2g_ref, rwt_ref, rb_ref,
                x1_ref, h2_ref, gates_ref, onehot_ref, count_ref):
    for s in range(x_ref.shape[0] // K4_PART):
        rows = slice(s * K4_PART, (s + 1) * K4_PART)
        ob = ob_ref[rows, :]
        bn = ob * lax.rsqrt(jnp.mean(ob * ob, axis=-1, keepdims=True) + EPS) * gb_ref[...]
        mixed = jnp.dot(a_ref[rows, :], wout_ref[:A_WIDTH, :], preferred_element_type=F32)
        mixed = mixed + jnp.dot(bn.astype(BF16), wout_ref[A_WIDTH:, :],
                                preferred_element_type=F32)
        x1 = x_ref[rows, :] + mod_ref[2:3, :] * mixed
        x1_ref[rows, :] = x1
        h2 = x1 * lax.rsqrt(jnp.mean(x1 * x1, axis=-1, keepdims=True) + EPS) * n2g_ref[...]
        h2 = (h2 * (1.0 + mod_ref[4:5, :]) + mod_ref[3:4, :]).astype(BF16)
        h2_ref[rows, :] = h2
        logits_t = lax.dot_general(rwt_ref[...], h2, (((1,), (1,)), ((), ())),
                                   preferred_element_type=F32)
        gates_t, onehot_t = _route(logits_t, rb_ref[...])
        gates_ref[:, rows] = gates_t
        onehot_ref[:, rows] = onehot_t
        for t in range(K4_PART // MOE_TOK):
            count_ref[s * (K4_PART // MOE_TOK) + t] = jnp.sum(
                onehot_t[:, t * MOE_TOK:(t + 1) * MOE_TOK], axis=1, keepdims=True)


def _mid(out_a, out_b, x, layer, mod, gb, wout, n2g, rwt, rb_col):
    tm = K4_TM
    tok = pl.BlockSpec((tm, D_MODEL), lambda i: (i, 0))
    half = pl.BlockSpec((tm, A_WIDTH), lambda i: (i, 0))
    route = pl.BlockSpec((N_EXPERTS, tm), lambda i: (0, i))
    return pl.pallas_call(
        _mid_kernel,
        grid=(TOKENS // tm,),
        in_specs=[
            half, half, tok,
            _mod_of(layer, tm),
            _of_layer(layer, 1, B_WIDTH),
            _of_layer(layer, D_MODEL, D_MODEL),
            _of_layer(layer, 1, D_MODEL),
            pl.BlockSpec((N_EXPERTS, D_MODEL), lambda i: (0, 0)),
            pl.BlockSpec((N_EXPERTS, 1), lambda i: (0, 0)),
        ],
        out_specs=[tok, tok, route, route,
                   pl.BlockSpec((tm // MOE_TOK, N_EXPERTS, 1), lambda i: (i, 0, 0))],
        out_shape=[jax.ShapeDtypeStruct((TOKENS, D_MODEL), F32),
                   jax.ShapeDtypeStruct((TOKENS, D_MODEL), BF16),
                   jax.ShapeDtypeStruct((N_EXPERTS, TOKENS), F32),
                   jax.ShapeDtypeStruct((N_EXPERTS, TOKENS), F32),
                   jax.ShapeDtypeStruct((TOKENS // MOE_TOK, N_EXPERTS, 1), F32)],
        compiler_params=_cparams(("arbitrary",)),
        name="mid",
    )(out_a, out_b, x, mod, gb, wout, n2g, rwt, rb_col)


MOE_TOK = 256
ROW_ALIGN = 16
MOE_SUB = 4
MOE_LB = 2 * MOE_TOK + N_EXPERTS * ROW_ALIGN
FFN_TM = 512
N_TOK_TILES = TOKENS // MOE_TOK
ROWS_MAX = -(-(2 * TOKENS + N_TOK_TILES * N_EXPERTS * ROW_ALIGN
               + N_EXPERTS * (FFN_TM - ROW_ALIGN)) // FFN_TM) * FFN_TM


N_FFN_TILES = ROWS_MAX // FFN_TM


def _plan_kernel(count_ref, n8_ref, local_ref, glob_ref, texp_ref, nact_ref):
    def prefix(a, b):
        return jnp.dot(a, b, preferred_element_type=F32, precision=lax.Precision.HIGHEST)

    row = lax.broadcasted_iota(jnp.int32, (LANES, LANES), 0)
    col = lax.broadcasted_iota(jnp.int32, (LANES, LANES), 1)
    real = jnp.logical_and(row < N_TOK_TILES, col < N_EXPERTS)
    n8 = jnp.where(real, jnp.maximum(jnp.ceil(count_ref[...] / ROW_ALIGN), 1.0) * ROW_ALIGN, 0.0)
    earlier_lane = jnp.where(row < col, 1.0, 0.0)
    earlier_row = jnp.where(col < row, 1.0, 0.0)
    local = prefix(n8, earlier_lane)
    seg = jnp.ceil(jnp.sum(n8, axis=0, keepdims=True) / FFN_TM) * FFN_TM
    seg_start = prefix(jnp.broadcast_to(seg, (LANES, LANES)), earlier_lane)[0:1]
    glob = seg_start + prefix(earlier_row, n8)
    n_active = jnp.sum(seg, axis=1, keepdims=True) / FFN_TM
    tile = row[:, 0:1].astype(F32)
    ended = jnp.logical_and(tile * FFN_TM >= seg_start + seg, col < N_EXPERTS)
    texp = jnp.sum(jnp.where(ended, 1.0, 0.0), axis=1, keepdims=True)
    texp = jnp.minimum(texp, float(N_EXPERTS - 1))
    last = jnp.sum(jnp.where(tile == n_active - 1.0, texp, 0.0), axis=0, keepdims=True)
    texp = jnp.where(tile < n_active, texp, last)
    n8_ref[...] = n8.astype(jnp.int32)
    local_ref[...] = local.astype(jnp.int32)
    glob_ref[...] = glob.astype(jnp.int32)
    texp_ref[...] = jnp.broadcast_to(texp, (LANES, LANES)).astype(jnp.int32)
    nact_ref[...] = jnp.broadcast_to(n_active, (8, LANES)).astype(jnp.int32)


def _moe_plan(count):
    assert max(N_TOK_TILES, N_FFN_TILES, N_EXPERTS) <= LANES
    count = count.reshape(N_TOK_TILES, N_EXPERTS)
    count = jnp.pad(count, ((0, LANES - N_TOK_TILES), (0, LANES - N_EXPERTS)))
    full = jax.ShapeDtypeStruct((LANES, LANES), jnp.int32)
    n8, local, glob, texp, nact = pl.pallas_call(
        _plan_kernel,
        out_shape=[full, full, full, full, jax.ShapeDtypeStruct((8, LANES), jnp.int32)],
        name="plan",
    )(count)

    def runs(t):
        return t[:N_TOK_TILES, :N_EXPERTS]

    return dict(n8=runs(n8).reshape(-1), local=runs(local).reshape(-1),
                glob=runs(glob).reshape(-1),
                local_col=runs(local).astype(F32).reshape(N_TOK_TILES, N_EXPERTS, 1),
                texp=texp[:N_FFN_TILES, 0], n_active=nact[0, :1])


def _run_copies(tile, n8_ref, local_ref, glob_ref, make_copy, start):
    for e in range(N_EXPERTS):
        idx = tile * N_EXPERTS + e
        cp = make_copy(pl.multiple_of(local_ref[idx], ROW_ALIGN),
                       pl.multiple_of(glob_ref[idx], ROW_ALIGN),
                       pl.multiple_of(n8_ref[idx], ROW_ALIGN))
        if start:
            cp.start()
        else:
            cp.wait()


def _fill_before(before_sc):
    @pl.when(pl.program_id(0) == 0)
    def _():
        src = lax.broadcasted_iota(jnp.int32, before_sc.shape, 0)
        dst = lax.broadcasted_iota(jnp.int32, before_sc.shape, 1)
        before_sc[...] = jnp.where(src < dst, 1.0, 0.0).astype(BF16)


def _sorted_rows(onehot, local_col, before):
    rank = jnp.dot(onehot.astype(BF16), before, preferred_element_type=F32)
    pos = local_col + rank
    chosen = onehot > 0.5
    lo = jnp.min(jnp.where(chosen, pos, float(MOE_LB)), axis=0, keepdims=True)
    hi = jnp.max(jnp.where(chosen, pos, -1.0), axis=0, keepdims=True)
    return pos, chosen, lo, hi


def _dispatch_kernel(n8_ref, local_ref, glob_ref, h_ref, oh_ref, lcol_ref, xs_ref,
                     lbuf, before_sc, sem):
    j = pl.program_id(0)
    last = pl.num_programs(0) - 1
    par = j % 2
    _fill_before(before_sc)
    row = lax.broadcasted_iota(jnp.int32, (MOE_LB, MOE_TOK), 0).astype(F32)
    toks = [slice(u * MOE_TOK, (u + 1) * MOE_TOK) for u in range(MOE_SUB)]
    spots = [_sorted_rows(oh_ref[:, tok], lcol_ref[u], before_sc[...])[2:]
             for u, tok in enumerate(toks)]
    perms = [(jnp.where(row == lo, 1.0, 0.0) + jnp.where(row == hi, 1.0, 0.0)).astype(BF16)
             for lo, hi in spots]
    for u, tok in enumerate(toks):
        lbuf[par * MOE_SUB + u] = jnp.dot(perms[u], h_ref[tok, :],
                                          preferred_element_type=F32).astype(BF16)

    def copy_from(slot):
        def make(lo_, go_, size):
            return pltpu.make_async_copy(lbuf.at[slot, pl.ds(lo_, size), :],
                                         xs_ref.at[pl.ds(go_, size), :], sem.at[slot])
        return make

    refs = (n8_ref, local_ref, glob_ref)
    for u in range(MOE_SUB):
        _run_copies(j * MOE_SUB + u, *refs, copy_from(par * MOE_SUB + u), True)

    @pl.when(j > 0)
    def _():
        for u in range(MOE_SUB):
            _run_copies((j - 1) * MOE_SUB + u, *refs, copy_from((1 - par) * MOE_SUB + u), False)

    @pl.when(j == last)
    def _():
        for u in range(MOE_SUB):
            _run_copies(j * MOE_SUB + u, *refs, copy_from(par * MOE_SUB + u), False)


def _dispatch(plan, h2, onehot):
    step_tok = MOE_SUB * MOE_TOK
    grid_spec = pltpu.PrefetchScalarGridSpec(
        num_scalar_prefetch=3,
        grid=(N_TOK_TILES // MOE_SUB,),
        in_specs=[
            pl.BlockSpec((step_tok, D_MODEL), lambda j, *_: (j, 0)),
            pl.BlockSpec((N_EXPERTS, step_tok), lambda j, *_: (0, j)),
            pl.BlockSpec((MOE_SUB, N_EXPERTS, 1), lambda j, *_: (j, 0, 0)),
        ],
        out_specs=pl.BlockSpec(memory_space=pl.ANY),
        scratch_shapes=[pltpu.VMEM((2 * MOE_SUB, MOE_LB, D_MODEL), BF16),
                        pltpu.VMEM((MOE_TOK, MOE_TOK), BF16),
                        pltpu.SemaphoreType.DMA((2 * MOE_SUB,))],
    )
    return pl.pallas_call(
        _dispatch_kernel,
        grid_spec=grid_spec,
        out_shape=jax.ShapeDtypeStruct((ROWS_MAX, D_MODEL), BF16),
        compiler_params=_cparams(("arbitrary",)),
        name="dispatch",
    )(plan["n8"], plan["local"], plan["glob"], h2, onehot, plan["local_col"])


def _experts_kernel(texp_ref, nact_ref, xs_ref, wg_ref, wu_ref, wd_ref, ys_ref,
                    wg_b, wu_b, wd_b, wg_f, wu_f, wd_f, sem, *, layer):
    i = pl.program_id(0)
    active = i < nact_ref[0]
    e = texp_ref[i]
    new_expert = jnp.logical_or(i == 0, e != texp_ref[jnp.maximum(i - 1, 0)])

    def fetch(expert, slot):
        return [pltpu.make_async_copy(src.at[layer, expert], dst.at[slot], sem.at[slot, n])
                for n, (src, dst) in enumerate(((wg_ref, wg_f), (wu_ref, wu_f), (wd_ref, wd_f)))]

    @pl.when(i == 0)
    def _():
        for cp in fetch(e, e % 2):
            cp.start()

    @pl.when(jnp.logical_and(active, new_expert))
    def _():
        slot = e % 2
        for cp in fetch(e, slot):
            cp.wait()

        @pl.when(e + 1 < N_EXPERTS)
        def _():
            for cp in fetch(e + 1, 1 - slot):
                cp.start()

        wg_b[...] = wg_f[slot].astype(BF16)
        wu_b[...] = wu_f[slot].astype(BF16)
        wd_b[...] = wd_f[slot].astype(BF16)

    @pl.when(active)
    def _():
        x = xs_ref[...]
        hg = jnp.dot(x, wg_b[...], preferred_element_type=F32)
        hu = jnp.dot(x, wu_b[...], preferred_element_type=F32)
        act = (jax.nn.silu(hg) * hu).astype(BF16)
        ys_ref[...] = jnp.dot(act, wd_b[...], preferred_element_type=F32).astype(BF16)


def _experts(plan, xs, layer, wg, wu, wd):
    def rows(i, texp, nact):
        return (jnp.minimum(i, nact[0] - 1), 0)

    anywhere = pl.BlockSpec(memory_space=pl.ANY)
    grid_spec = pltpu.PrefetchScalarGridSpec(
        num_scalar_prefetch=2,
        grid=(ROWS_MAX // FFN_TM,),
        in_specs=[pl.BlockSpec((FFN_TM, D_MODEL), rows), anywhere, anywhere, anywhere],
        out_specs=pl.BlockSpec((FFN_TM, D_MODEL), rows),
        scratch_shapes=[pltpu.VMEM((D_MODEL, D_EXPERT), BF16), pltpu.VMEM((D_MODEL, D_EXPERT), BF16),
                        pltpu.VMEM((D_EXPERT, D_MODEL), BF16),
                        pltpu.VMEM((2, D_MODEL, D_EXPERT), F32), pltpu.VMEM((2, D_MODEL, D_EXPERT), F32),
                        pltpu.VMEM((2, D_EXPERT, D_MODEL), F32),
                        pltpu.SemaphoreType.DMA((2, 3))],
    )
    return pl.pallas_call(
        functools.partial(_experts_kernel, layer=layer),
        grid_spec=grid_spec,
        out_shape=jax.ShapeDtypeStruct((ROWS_MAX, D_MODEL), BF16),
        compiler_params=_cparams(("arbitrary",)),
        name="experts",
    )(plan["texp"], plan["n_active"], xs, wg, wu, wd)


def _combine_kernel(n8_ref, local_ref, glob_ref, oh_ref, g_ref, lcol_ref, x_ref, mod_ref, fg_ref,
                    ys_ref, o_ref, ybuf, before_sc, sem, *, final_norm):
    j = pl.program_id(0)
    n_steps = pl.num_programs(0)
    par = j % 2
    refs = (n8_ref, local_ref, glob_ref)
    _fill_before(before_sc)

    def copy_to(slot):
        def make(lo_, go_, size):
            return pltpu.make_async_copy(ys_ref.at[pl.ds(go_, size), :],
                                         ybuf.at[slot, pl.ds(lo_, size), :], sem.at[slot])
        return make

    @pl.when(j == 0)
    def _():
        ybuf[...] = jnp.zeros(ybuf.shape, BF16)
        for u in range(MOE_SUB):
            _run_copies(u, *refs, copy_to(u), True)

    @pl.when(j + 1 < n_steps)
    def _():
        for u in range(MOE_SUB):
            _run_copies((j + 1) * MOE_SUB + u, *refs, copy_to((1 - par) * MOE_SUB + u), True)

    row = lax.broadcasted_iota(jnp.int32, (MOE_LB, MOE_TOK), 0).astype(F32)
    toks = [slice(u * MOE_TOK, (u + 1) * MOE_TOK) for u in range(MOE_SUB)]
    perms = []
    for u, tok in enumerate(toks):
        pos, chosen, lo, hi = _sorted_rows(oh_ref[:, tok], lcol_ref[u], before_sc[...])
        gate = jnp.where(chosen, g_ref[:, tok], 0.0)
        g_lo = jnp.sum(jnp.where(pos == lo, gate, 0.0), axis=0, keepdims=True)
        g_hi = jnp.sum(jnp.where(pos == hi, gate, 0.0), axis=0, keepdims=True)
        perms.append((jnp.where(row == lo, g_lo, 0.0)
                      + jnp.where(row == hi, g_hi, 0.0)).astype(BF16))
    for u in range(MOE_SUB):
        _run_copies(j * MOE_SUB + u, *refs, copy_to(par * MOE_SUB + u), False)
    for u, tok in enumerate(toks):
        y = lax.dot_general(perms[u], ybuf[par * MOE_SUB + u], (((0,), (0,)), ((), ())),
                            preferred_element_type=F32)
        x = x_ref[tok, :] + mod_ref[5:6, :] * y
        if final_norm:
            x = x * lax.rsqrt(jnp.mean(x * x, axis=-1, keepdims=True) + EPS) * fg_ref[...]
        o_ref[tok, :] = x


N_FRONT_IN = 8


def _combine_front_kernel(*refs):
    n_in = 3 + 7
    comb_in, front_in = refs[:n_in], refs[n_in:n_in + N_FRONT_IN]
    o_ref, *front_out = refs[n_in + N_FRONT_IN:n_in + N_FRONT_IN + 5]
    scratch = refs[n_in + N_FRONT_IN + 5:]
    _combine_kernel(*comb_in, o_ref, *scratch, final_norm=False)
    _front_kernel(o_ref, *front_in, *front_out)


def _combine(plan, onehot, gates, ys, x, layer, mod, final_g, front_params=None):
    step_tok = MOE_SUB * MOE_TOK
    route = pl.BlockSpec((N_EXPERTS, step_tok), lambda j, *_: (0, j))
    tok = pl.BlockSpec((step_tok, D_MODEL), lambda j, *_: (j, 0))
    in_specs = [
        route, route,
        pl.BlockSpec((MOE_SUB, N_EXPERTS, 1), lambda j, *_: (j, 0, 0)),
        tok, _mod_of(layer, step_tok),
        pl.BlockSpec((1, D_MODEL), lambda j, *_: (0, 0)),
        pl.BlockSpec(memory_space=pl.ANY),
    ]
    args = [onehot, gates, plan["local_col"], x, mod, final_g, ys]
    out_specs = tok
    out_shape = jax.ShapeDtypeStruct((TOKENS, D_MODEL), F32)
    body = functools.partial(_combine_kernel, final_norm=True)
    if front_params is not None:
        assert step_tok == K1_TM and len(front_params) == N_FRONT_IN
        half = pl.BlockSpec((step_tok, A_WIDTH), lambda j, *_: (j, 0))
        in_specs += _front_specs(layer + 1, step_tok)
        args += list(front_params)
        out_specs = [tok, half, half, half, half]
        out_shape = [out_shape, jax.ShapeDtypeStruct((TOKENS, A_WIDTH), BF16)] + \
                    [jax.ShapeDtypeStruct((TOKENS, B_WIDTH), F32)] * 3
        body = _combine_front_kernel
    grid_spec = pltpu.PrefetchScalarGridSpec(
        num_scalar_prefetch=3,
        grid=(N_TOK_TILES // MOE_SUB,),
        in_specs=in_specs,
        out_specs=out_specs,
        scratch_shapes=[pltpu.VMEM((2 * MOE_SUB, MOE_LB, D_MODEL), BF16),
                        pltpu.VMEM((MOE_TOK, MOE_TOK), BF16),
                        pltpu.SemaphoreType.DMA((2 * MOE_SUB,))],
    )
    return pl.pallas_call(
        body,
        grid_spec=grid_spec,
        out_shape=out_shape,
        compiler_params=_cparams(("arbitrary",)),
        name="combine",
    )(plan["n8"], plan["local"], plan["glob"], *args)


def kernel(x, c, rel_bias, router_w, router_b, mod_w, mod_b, norm1_g, w_in, gmlp_ln_g, gmlp_ln_b,
           gmlp_ws, gmlp_bs, out_norm_a_g, out_norm_b_g, w_out, norm2_g, moe_w_gate, moe_w_up,
           moe_w_down, final_g):
    mod = _modulation(c, mod_w, mod_b).reshape(DEPTH, BATCH, N_MOD, D_MODEL)
    bias_tab = _bias_tables(rel_bias)
    rwt = router_w.T.astype(BF16)
    rb_col = router_b.reshape(N_EXPERTS, 1)
    xt = x.reshape(TOKENS, D_MODEL)
    front_params = (
        mod, norm1_g.reshape(DEPTH, 1, D_MODEL), w_in.astype(BF16),
        gmlp_ln_g.reshape(DEPTH, 1, A_WIDTH), gmlp_ln_b.reshape(DEPTH, 1, A_WIDTH),
        gmlp_ws.astype(BF16).reshape(DEPTH, A_GROUPS // 2, 2 * CHUNK, CHUNK),
        jnp.repeat(jnp.swapaxes(gmlp_bs, 1, 2), HEAD_DIM, axis=2),
        out_norm_a_g.reshape(DEPTH, 1, A_WIDTH))
    gb = out_norm_b_g.reshape(DEPTH, 1, B_WIDTH)
    wout_b = w_out.astype(BF16)
    n2g = norm2_g.reshape(DEPTH, 1, D_MODEL)
    fg = final_g.reshape(1, D_MODEL)

    out_a, q, k, v = _front(xt, 0, front_params)
    for l in range(DEPTH):
        out_b = _attention(q, k, v, bias_tab)
        xt, h2, gates, onehot, count = _mid(out_a, out_b, xt, l, mod, gb, wout_b, n2g, rwt, rb_col)
        plan = _moe_plan(count)
        xs = _dispatch(plan, h2, onehot)
        ys = _experts(plan, xs, l, moe_w_gate, moe_w_up, moe_w_down)
        if l + 1 < DEPTH:
            xt, out_a, q, k, v = _combine(plan, onehot, gates, ys, xt, l, mod, fg, front_params)
        else:
            xt = _combine(plan, onehot, gates, ys, xt, l, mod, fg)
    return xt.reshape(BATCH, SEQ, D_MODEL)
```

```python
import functools
import math

import numpy as np
import jax
import jax.numpy as jnp
from jax import lax
from jax.experimental import pallas as pl
from jax.experimental.pallas import tpu as pltpu

D_MODEL = 1024
BATCH = 8
SEQ = 2048
DEPTH = 2
TOKENS = BATCH * SEQ
HEAD_DIM = 64
A_WIDTH = 512
B_WIDTH = 512
A_GROUPS = 8
IN_COLS = 2 * A_WIDTH + 3 * B_WIDTH
CHUNK = 128
DILATED_CONFIGS = ((128, 1), (512, 4), (2048, 16))
ATTN_BLOCK = 128
REL_BUCKETS = 32
REL_MAX_EXACT = REL_BUCKETS // 2
REL_MAX_DISTANCE = 2048
N_EXPERTS = 16
N_EXPERT_GROUPS = 4
EXPERTS_PER_GROUP = 4
D_EXPERT = 512
N_MOD = 6
EPS = 1e-6
NEG_INF = -1e30

LANES = 128
HEAD_PAIR = 2 * HEAD_DIM
N_PAIRS = B_WIDTH // HEAD_PAIR
RESIDUES = max(d for _, d in DILATED_CONFIGS)
LOG2E = math.log2(math.e)

F32 = jnp.float32
BF16 = jnp.bfloat16

VMEM_LIMIT = 56 * 1024 * 1024


def _cparams(sem):
    return pltpu.CompilerParams(dimension_semantics=sem, vmem_limit_bytes=VMEM_LIMIT)


def _gelu(x):
    return 0.5 * x * (1.0 + lax.erf(x * math.sqrt(0.5)))


MOD_TN = 1024


def _mod_kernel(c_ref, w_ref, b_ref, o_ref):
    ca = jax.nn.silu(c_ref[...])
    o_ref[...] = jnp.dot(ca.astype(BF16), w_ref[...].astype(BF16),
                         preferred_element_type=F32) + b_ref[...]


def _modulation(c, mod_w, mod_b):
    n_cols = N_MOD * D_MODEL
    return pl.pallas_call(
        _mod_kernel,
        grid=(DEPTH, n_cols // MOD_TN),
        in_specs=[
            pl.BlockSpec((BATCH, D_MODEL), lambda l, j: (0, 0)),
            pl.BlockSpec((None, D_MODEL, MOD_TN), lambda l, j: (l, 0, j)),
            pl.BlockSpec((None, 1, MOD_TN), lambda l, j: (l, 0, j)),
        ],
        out_specs=pl.BlockSpec((None, BATCH, MOD_TN), lambda l, j: (l, 0, j)),
        out_shape=jax.ShapeDtypeStruct((DEPTH, BATCH, n_cols), F32),
        compiler_params=_cparams(("arbitrary", "arbitrary")),
        name="modulation",
    )(c, mod_w, mod_b.reshape(DEPTH, 1, n_cols))


K1_TM = 1024


def _front_kernel(x_ref, mod_ref, n1g_ref, win_ref, lng_ref, lnb_ref, ws_ref, bs_ref, ga_ref,
                  a_ref, q_ref, k_ref, v_ref):
    x = x_ref[...]
    tm = x.shape[0]
    h = x * lax.rsqrt(jnp.mean(x * x, axis=-1, keepdims=True) + EPS) * n1g_ref[...]
    h = h * (1.0 + mod_ref[1:2, :]) + mod_ref[0:1, :]
    proj = jnp.dot(h.astype(BF16), win_ref[...], preferred_element_type=F32)

    q_ref[...] = proj[:, 2 * A_WIDTH:2 * A_WIDTH + B_WIDTH] * (HEAD_DIM ** -0.5 * LOG2E)
    k_ref[...] = proj[:, 2 * A_WIDTH + B_WIDTH:2 * A_WIDTH + 2 * B_WIDTH]
    v_ref[...] = proj[:, 2 * A_WIDTH + 2 * B_WIDTH:]

    u = _gelu(proj[:, :A_WIDTH])
    va = _gelu(proj[:, A_WIDTH:2 * A_WIDTH])
    mu = jnp.mean(va, axis=-1, keepdims=True)
    vc = va - mu
    vln = vc * lax.rsqrt(jnp.mean(vc * vc, axis=-1, keepdims=True) + EPS)
    vln = (vln * lng_ref[...] + lnb_ref[...]).astype(BF16)

    row = lax.broadcasted_iota(jnp.int32, (2 * CHUNK, CHUNK), 0)
    col = lax.broadcasted_iota(jnp.int32, (2 * CHUNK, CHUNK), 1)
    causal = (row % CHUNK) >= col
    first_group = lax.broadcasted_iota(jnp.int32, (CHUNK, LANES), 1) < HEAD_DIM
    wmix = [jnp.where(causal, ws_ref[p], jnp.zeros((), BF16)) for p in range(A_GROUPS // 2)]
    for c in range(tm // CHUNK):
        rows = slice(c * CHUNK, (c + 1) * CHUNK)
        parts = []
        for p in range(A_GROUPS // 2):
            vp = vln[rows, p * LANES:(p + 1) * LANES]
            r = jnp.dot(wmix[p], vp, preferred_element_type=F32)
            parts.append(jnp.where(first_group, r[:CHUNK], r[CHUNK:]))
        s = jnp.concatenate(parts, axis=-1) + bs_ref[...]
        oa = u[rows] * s
        oa = oa * lax.rsqrt(jnp.mean(oa * oa, axis=-1, keepdims=True) + EPS) * ga_ref[...]
        a_ref[rows, :] = oa.astype(BF16)


def _of_layer(layer, *tail):
    return pl.BlockSpec((None, *tail), lambda *_: (layer,) + (0,) * len(tail))


def _mod_of(layer, tokens_per_step):
    steps_per_seq = SEQ // tokens_per_step
    return pl.BlockSpec((None, None, N_MOD, D_MODEL),
                        lambda i, *_: (layer, i // steps_per_seq, 0, 0))


def _front_specs(layer, tm):
    return [
        _mod_of(layer, tm),
        _of_layer(layer, 1, D_MODEL),
        _of_layer(layer, D_MODEL, IN_COLS),
        _of_layer(layer, 1, A_WIDTH), _of_layer(layer, 1, A_WIDTH),
        _of_layer(layer, A_GROUPS // 2, 2 * CHUNK, CHUNK),
        _of_layer(layer, CHUNK, A_WIDTH),
        _of_layer(layer, 1, A_WIDTH),
    ]


def _front(x, layer, front_params):
    tm = K1_TM
    half = pl.BlockSpec((tm, A_WIDTH), lambda i: (i, 0))
    return pl.pallas_call(
        _front_kernel,
        grid=(TOKENS // tm,),
        in_specs=[pl.BlockSpec((tm, D_MODEL), lambda i: (i, 0))] + _front_specs(layer, tm),
        out_specs=[half, half, half, half],
        out_shape=[jax.ShapeDtypeStruct((TOKENS, A_WIDTH), BF16)] +
                  [jax.ShapeDtypeStruct((TOKENS, B_WIDTH), F32)] * 3,
        compiler_params=_cparams(("arbitrary",)),
        name="front",
    )(x, *front_params)


def _t5_bucket_np(dist):
    dist = np.maximum(dist, 0)
    ratio = np.log(np.maximum(dist, 1) / REL_MAX_EXACT) / np.log(REL_MAX_DISTANCE / REL_MAX_EXACT)
    large = REL_MAX_EXACT + np.floor(ratio * (REL_BUCKETS - REL_MAX_EXACT)).astype(np.int64)
    large = np.minimum(large, REL_BUCKETS - 1)
    return np.where(dist < REL_MAX_EXACT, dist, large).astype(np.int32)


def _bias_tables(rel_bias):
    blk = ATTN_BLOCK
    n_cfg = len(DILATED_CONFIGS)
    n_rel = 3 * blk
    rel = 2 * blk - 1 - np.arange(n_rel)
    bucket, valid, rows, cols = [], [], [], []
    for window, d in DILATED_CONFIGS:
        span = window // d
        bucket.append(_t5_bucket_np(np.clip(rel, 0, span) * d))
        valid.append((rel >= 0) & (rel <= span))
        sub = RESIDUES // d
        ln = blk // sub
        pos = np.arange(blk)
        rows.append(np.eye(blk, dtype=np.float32)[(pos % ln) * sub + pos // ln])
        cols.append(np.kron(np.eye(2, dtype=np.float32), rows[-1]))
    w = jnp.transpose(rel_bias.astype(F32)[np.stack(bucket)], (0, 2, 1)) * LOG2E
    w = jnp.where(np.stack(valid)[:, None, :], w, NEG_INF * LOG2E)
    flat = jnp.tile(w, (1, 1, blk))
    skew = flat[:, :, blk - 1:blk - 1 + blk * (n_rel - 1)].reshape(n_cfg, -1, blk, n_rel - 1)
    tab = skew[..., :2 * blk]
    return jnp.einsum('cpi,chik,cqk->chpq', np.stack(rows), tab, np.stack(cols),
                      precision=lax.Precision.HIGHEST)


ATTN_LEAD = 3
ATTN_LAG = 1


def _attn_kernel(q_ref, k_ref, v_ref, bias_ref, o_ref, qp, kp, vp, m_sc, l_sc, acc_sc):
    blk = ATTN_BLOCK
    res = RESIDUES
    per = SEQ // res
    sq = res * res
    seg = sq // res
    lane = lax.broadcasted_iota(jnp.int32, (blk, LANES), 1)
    head0 = lane < HEAD_DIM
    ones = jnp.ones((2 * blk, LANES), BF16)

    pa = lax.broadcasted_iota(jnp.int32, (sq, sq), 0)
    pb = lax.broadcasted_iota(jnp.int32, (sq, sq), 1)
    regroup = jnp.where(pb == res * (pa % res) + pa // res, 1.0, 0.0).astype(BF16)

    def residue_rows(g):
        return [slice(per * r + seg * g, per * r + seg * (g + 1)) for r in range(res)]

    def load(ref, slices):
        return jnp.concatenate([ref[s, :] for s in slices], axis=0) if len(slices) > 1 \
            else ref[slices[0], :]

    def store(ref, slices, val):
        ln = val.shape[0] // len(slices)
        for i, s in enumerate(slices):
            ref[s, :] = val[i * ln:(i + 1) * ln]

    for g in range(SEQ // sq):
        rows = slice(sq * g, sq * (g + 1))
        qkv = jnp.concatenate([q_ref[rows, :], k_ref[rows, :], v_ref[rows, :]], axis=1)
        qkv = jnp.dot(regroup, qkv.astype(BF16), preferred_element_type=F32)
        store(qp, residue_rows(g), qkv[:, :LANES])
        store(kp, residue_rows(g), qkv[:, LANES:2 * LANES])
        store(vp, residue_rows(g), qkv[:, 2 * LANES:])

    def merge_heads(t):
        return jnp.where(head0, t[:blk], t[blk:])

    def scores(q, kcat, bias):
        zero = jnp.zeros_like(q)
        qs = jnp.concatenate([jnp.where(head0, q, zero), jnp.where(head0, zero, q)],
                             axis=0).astype(BF16)
        return lax.dot_general(qs, kcat, (((1,), (1,)), ((), ())),
                               preferred_element_type=F32) + bias

    def softmax_stage(s):
        m = jnp.max(s, axis=-1, keepdims=True)
        return m, jnp.exp2(s - m).astype(BF16)

    def value_stage(m, p, vcat):
        nk = vcat.shape[0]
        pv = jnp.dot(p, jnp.concatenate([vcat, ones[:nk]], axis=1), preferred_element_type=F32)
        return (merge_heads(jnp.broadcast_to(m, (2 * blk, LANES))),
                merge_heads(pv[:, LANES:]), merge_heads(pv[:, :LANES]))

    def update(ci, slices, m_c, l_c, o_c):
        if ci > 0:
            m_r = load(m_sc, slices)
            m_n = jnp.maximum(m_r, m_c)
            a = jnp.exp2(m_r - m_n)
            b = jnp.exp2(m_c - m_n)
            l_c = a * load(l_sc, slices) + b * l_c
            o_c = a * load(acc_sc, slices) + b * o_c
            m_c = m_n
        store(m_sc, slices, m_c)
        store(l_sc, slices, l_c)
        store(acc_sc, slices, o_c)

    blocks = []
    for ci, (window, d) in enumerate(DILATED_CONFIGS):
        sub = res // d
        ln = blk // sub
        for r in range(d):
            for n in range(SEQ // d // blk):
                slices = [slice(per * (r + d * c) + ln * n, per * (r + d * c) + ln * (n + 1))
                          for c in range(sub)]
                blocks.append((ci, slices, n == 0))

    kv_prev = [None, None]

    def score_stage(ci, slices, first):
        k_cur = load(kp, slices).astype(BF16)
        v_cur = load(vp, slices).astype(BF16)
        if first:
            bias = jnp.concatenate([bias_ref[ci, 0, :, blk:], bias_ref[ci, 1, :, blk:]], axis=0)
            out = scores(load(qp, slices), k_cur, bias), v_cur
        else:
            bias = jnp.concatenate([bias_ref[ci, 0], bias_ref[ci, 1]], axis=0)
            out = (scores(load(qp, slices), jnp.concatenate([kv_prev[0], k_cur], axis=0), bias),
                   jnp.concatenate([kv_prev[1], v_cur], axis=0))
        kv_prev[0], kv_prev[1] = k_cur, v_cur
        return out

    lead, lag = ATTN_LEAD, ATTN_LAG
    ahead = [score_stage(*blocks[j]) for j in range(lead)]
    behind = []
    for i, (ci, slices, _) in enumerate(blocks):
        s, vcat = ahead.pop(0)
        if i + lead < len(blocks):
            ahead.append(score_stage(*blocks[i + lead]))
        m, p = softmax_stage(s)
        behind.append((ci, slices, m, p, vcat))
        if len(behind) > lag:
            done = behind.pop(0)
            update(done[0], done[1], *value_stage(*done[2:]))
    for done in behind:
        update(done[0], done[1], *value_stage(*done[2:]))

    for g in range(SEQ // sq):
        o = load(acc_sc, residue_rows(g)) / load(l_sc, residue_rows(g))
        hi = o.astype(BF16)
        lo = (o - hi.astype(F32)).astype(BF16)
        back = jnp.dot(regroup, jnp.concatenate([hi, lo], axis=1), preferred_element_type=F32)
        o_ref[sq * g:sq * (g + 1), :] = back[:, :LANES] + back[:, LANES:]


def _attention(q, k, v, bias_tab):
    n_cfg = len(DILATED_CONFIGS)
    blk = ATTN_BLOCK
    seq_spec = pl.BlockSpec((SEQ, HEAD_PAIR), lambda b, p: (b, p))
    return pl.pallas_call(
        _attn_kernel,
        grid=(BATCH, N_PAIRS),
        in_specs=[seq_spec, seq_spec, seq_spec,
                  pl.BlockSpec((n_cfg, 2, blk, 2 * blk), lambda b, p: (0, p, 0, 0))],
        out_specs=seq_spec,
        out_shape=jax.ShapeDtypeStruct((TOKENS, B_WIDTH), F32),
        scratch_shapes=[pltpu.VMEM((SEQ, HEAD_PAIR), F32)] * 6,
        compiler_params=_cparams(("arbitrary", "arbitrary")),
        name="attention",
    )(q, k, v, bias_tab)


K4_TM = 1024
K4_PART = 256


def _top2_sum(a, b, c, d):
    hi1, lo1 = jnp.maximum(a, b), jnp.minimum(a, b)
    hi2, lo2 = jnp.maximum(c, d), jnp.minimum(c, d)
    return jnp.maximum(hi1, hi2) + jnp.maximum(jnp.minimum(hi1, hi2), jnp.maximum(lo1, lo2))


def _route(logits_t, rb_col):
    m = jnp.max(logits_t, axis=0, keepdims=True)
    e = jnp.exp(logits_t - m)
    probs = e / jnp.sum(e, axis=0, keepdims=True)
    sel = probs + rb_col
    sel_rows = [sel[i:i + 1, :] for i in range(N_EXPERTS)]
    prob_rows = [probs[i:i + 1, :] for i in range(N_EXPERTS)]
    gsz = EXPERTS_PER_GROUP
    score = [_top2_sum(*sel_rows[g * gsz:(g + 1) * gsz]) for g in range(N_EXPERT_GROUPS)]
    chosen = []
    for g in range(N_EXPERT_GROUPS):
        best = None
        for g2 in range(N_EXPERT_GROUPS):
            if g2 == g:
                continue
            c = (score[g] > score[g2]) if g2 < g else (score[g] >= score[g2])
            best = c if best is None else jnp.logical_and(best, c)
        for i in range(gsz):
            ei = g * gsz + i
            rank = jnp.zeros_like(sel_rows[ei])
            for j in range(gsz):
                if j == i:
                    continue
                ej = g * gsz + j
                ahead = (sel_rows[ej] >= sel_rows[ei]) if j < i else (sel_rows[ej] > sel_rows[ei])
                rank = rank + jnp.where(ahead, 1.0, 0.0)
            chosen.append(jnp.logical_and(best, rank < float(2)))
    picked = [jnp.where(chosen[i], prob_rows[i], 0.0) for i in range(N_EXPERTS)]
    denom = picked[0]
    for i in range(1, N_EXPERTS):
        denom = denom + picked[i]
    gates = jnp.concatenate([pk / denom for pk in picked], axis=0)
    onehot = jnp.concatenate([jnp.where(ch, 1.0, 0.0) for ch in chosen], axis=0)
    return gates, onehot


def _mid_kernel(a_ref, ob_ref, x_ref, mod_ref, gb_ref, wout_ref, n2g_ref, rwt_ref, rb_ref,
                x1_ref, h2_ref, gates_ref, onehot_ref, count_ref):
    for s in range(x_ref.shape[0] // K4_PART):
        rows = slice(s * K4_PART, (s + 1) * K4_PART)
        ob = ob_ref[rows, :]
        bn = ob * lax.rsqrt(jnp.mean(ob * ob, axis=-1, keepdims=True) + EPS) * gb_ref[...]
        mixed = jnp.dot(a_ref[rows, :], wout_ref[:A_WIDTH, :], preferred_element_type=F32)
        mixed = mixed + jnp.dot(bn.astype(BF16), wout_ref[A_WIDTH:, :],
                                preferred_element_type=F32)
        x1 = x_ref[rows, :] + mod_ref[2:3, :] * mixed
        x1_ref[rows, :] = x1
        h2 = x1 * lax.rsqrt(jnp.mean(x1 * x1, axis=-1, keepdims=True) + EPS) * n2g_ref[...]
        h2 = (h2 * (1.0 + mod_ref[4:5, :]) + mod_ref[3:4, :]).astype(BF16)
        h2_ref[rows, :] = h2
        logits_t = lax.dot_general(rwt_ref[...], h2, (((1,), (1,)), ((), ())),
                                   preferred_element_type=F32)
        gates_t, onehot_t = _route(logits_t, rb_ref[...])
        gates_ref[:, rows] = gates_t
        onehot_ref[:, rows] = onehot_t
        for t in range(K4_PART // MOE_TOK):
            count_ref[s * (K4_PART // MOE_TOK) + t] = jnp.sum(
                onehot_t[:, t * MOE_TOK:(t + 1) * MOE_TOK], axis=1, keepdims=True)


def _mid(out_a, out_b, x, layer, mod, gb, wout, n2g, rwt, rb_col):
    tm = K4_TM
    tok = pl.BlockSpec((tm, D_MODEL), lambda i: (i, 0))
    half = pl.BlockSpec((tm, A_WIDTH), lambda i: (i, 0))
    route = pl.BlockSpec((N_EXPERTS, tm), lambda i: (0, i))
    return pl.pallas_call(
        _mid_kernel,
        grid=(TOKENS // tm,),
        in_specs=[
            half, half, tok,
            _mod_of(layer, tm),
            _of_layer(layer, 1, B_WIDTH),
            _of_layer(layer, D_MODEL, D_MODEL),
            _of_layer(layer, 1, D_MODEL),
            pl.BlockSpec((N_EXPERTS, D_MODEL), lambda i: (0, 0)),
            pl.BlockSpec((N_EXPERTS, 1), lambda i: (0, 0)),
        ],
        out_specs=[tok, tok, route, route,
                   pl.BlockSpec((tm // MOE_TOK, N_EXPERTS, 1), lambda i: (i, 0, 0))],
        out_shape=[jax.ShapeDtypeStruct((TOKENS, D_MODEL), F32),
                   jax.ShapeDtypeStruct((TOKENS, D_MODEL), BF16),
                   jax.ShapeDtypeStruct((N_EXPERTS, TOKENS), F32),
                   jax.ShapeDtypeStruct((N_EXPERTS, TOKENS), F32),
                   jax.ShapeDtypeStruct((TOKENS // MOE_TOK, N_EXPERTS, 1), F32)],
        compiler_params=_cparams(("arbitrary",)),
        name="mid",
    )(out_a, out_b, x, mod, gb, wout, n2g, rwt, rb_col)


MOE_TOK = 256
ROW_ALIGN = 16
MOE_SUB = 4
MOE_LB = 2 * MOE_TOK + N_EXPERTS * ROW_ALIGN
FFN_TM = 512
N_TOK_TILES = TOKENS // MOE_TOK
ROWS_MAX = -(-(2 * TOKENS + N_TOK_TILES * N_EXPERTS * ROW_ALIGN
               + N_EXPERTS * (FFN_TM - ROW_ALIGN)) // FFN_TM) * FFN_TM


N_FFN_TILES = ROWS_MAX // FFN_TM


def _plan_kernel(count_ref, run_len_ref, local_ref, glob_ref, rows_ref, texp_ref, nact_ref):
    def prefix(a, b):
        return jnp.dot(a, b, preferred_element_type=F32, precision=lax.Precision.HIGHEST)

    row = lax.broadcasted_iota(jnp.int32, (LANES, LANES), 0)
    col = lax.broadcasted_iota(jnp.int32, (LANES, LANES), 1)
    real = jnp.logical_and(row < N_TOK_TILES, col < N_EXPERTS)
    run_len = jnp.where(real, jnp.maximum(jnp.ceil(count_ref[...] / ROW_ALIGN), 1.0) * ROW_ALIGN, 0.0)
    earlier_lane = jnp.where(row < col, 1.0, 0.0)
    earlier_row = jnp.where(col < row, 1.0, 0.0)
    local = prefix(run_len, earlier_lane)
    seg = jnp.ceil(jnp.sum(run_len, axis=0, keepdims=True) / FFN_TM) * FFN_TM
    seg_start = prefix(jnp.broadcast_to(seg, (LANES, LANES)), earlier_lane)[0:1]
    glob = seg_start + prefix(earlier_row, run_len)
    n_active = jnp.sum(seg, axis=1, keepdims=True) / FFN_TM
    tile = row[:, 0:1].astype(F32)
    ended = jnp.logical_and(tile * FFN_TM >= seg_start + seg, col < N_EXPERTS)
    texp = jnp.sum(jnp.where(ended, 1.0, 0.0), axis=1, keepdims=True)
    texp = jnp.minimum(texp, float(N_EXPERTS - 1))
    last = jnp.sum(jnp.where(tile == n_active - 1.0, texp, 0.0), axis=0, keepdims=True)
    texp = jnp.where(tile < n_active, texp, last)
    run_len_ref[...] = run_len.astype(jnp.int32)
    local_ref[...] = local.astype(jnp.int32)
    glob_ref[...] = glob.astype(jnp.int32)
    rows_ref[...] = jnp.broadcast_to(jnp.sum(run_len, axis=1, keepdims=True),
                                     (LANES, LANES)).astype(jnp.int32)
    texp_ref[...] = jnp.broadcast_to(texp, (LANES, LANES)).astype(jnp.int32)
    nact_ref[...] = jnp.broadcast_to(n_active, (8, LANES)).astype(jnp.int32)


def _moe_plan(count):
    assert max(N_TOK_TILES, N_FFN_TILES, N_EXPERTS) <= LANES
    count = count.reshape(N_TOK_TILES, N_EXPERTS)
    count = jnp.pad(count, ((0, LANES - N_TOK_TILES), (0, LANES - N_EXPERTS)))
    full = jax.ShapeDtypeStruct((LANES, LANES), jnp.int32)
    run_len, local, glob, rows, texp, nact = pl.pallas_call(
        _plan_kernel,
        out_shape=[full, full, full, full, full, jax.ShapeDtypeStruct((8, LANES), jnp.int32)],
        name="plan",
    )(count)

    def runs(t):
        return t[:N_TOK_TILES, :N_EXPERTS]

    return dict(run_len=runs(run_len).reshape(-1), local=runs(local).reshape(-1),
                glob=runs(glob).reshape(-1), rows=rows[:N_TOK_TILES, 0],
                local_col=runs(local).astype(F32).reshape(N_TOK_TILES, N_EXPERTS, 1),
                texp=texp[:N_FFN_TILES, 0], n_active=nact[0, :1])


def _run_copies(tile, run_len_ref, local_ref, glob_ref, rows_ref, make_copy, start):
    if not start:
        make_copy(0, 0, pl.multiple_of(rows_ref[tile], ROW_ALIGN)).wait()
        return
    for e in range(N_EXPERTS):
        idx = tile * N_EXPERTS + e
        make_copy(pl.multiple_of(local_ref[idx], ROW_ALIGN),
                  pl.multiple_of(glob_ref[idx], ROW_ALIGN),
                  pl.multiple_of(run_len_ref[idx], ROW_ALIGN)).start()


def _fill_before(before_sc):
    @pl.when(pl.program_id(0) == 0)
    def _():
        src = lax.broadcasted_iota(jnp.int32, before_sc.shape, 0)
        dst = lax.broadcasted_iota(jnp.int32, before_sc.shape, 1)
        before_sc[...] = jnp.where(src < dst, 1.0, 0.0).astype(BF16)


def _sorted_rows(onehot, local_col, before):
    rank = jnp.dot(onehot.astype(BF16), before, preferred_element_type=F32)
    pos = local_col + rank
    chosen = onehot > 0.5
    lo = jnp.min(jnp.where(chosen, pos, float(MOE_LB)), axis=0, keepdims=True)
    hi = jnp.max(jnp.where(chosen, pos, -1.0), axis=0, keepdims=True)
    return pos, chosen, lo, hi


def _dispatch_kernel(run_len_ref, local_ref, glob_ref, rows_ref, h_ref, oh_ref, lcol_ref, xs_ref,
                     lbuf, before_sc, sem):
    j = pl.program_id(0)
    last = pl.num_programs(0) - 1
    par = j % 2
    _fill_before(before_sc)
    row = lax.broadcasted_iota(jnp.int32, (MOE_LB, MOE_TOK), 0).astype(F32)
    toks = [slice(u * MOE_TOK, (u + 1) * MOE_TOK) for u in range(MOE_SUB)]
    spots = [_sorted_rows(oh_ref[:, tok], lcol_ref[u], before_sc[...])[2:]
             for u, tok in enumerate(toks)]
    perms = [(jnp.where(row == lo, 1.0, 0.0) + jnp.where(row == hi, 1.0, 0.0)).astype(BF16)
             for lo, hi in spots]
    for u, tok in enumerate(toks):
        lbuf[par * MOE_SUB + u] = jnp.dot(perms[u], h_ref[tok, :],
                                          preferred_element_type=F32).astype(BF16)

    def copy_from(slot):
        def make(lo_, go_, size):
            return pltpu.make_async_copy(lbuf.at[slot, pl.ds(lo_, size), :],
                                         xs_ref.at[pl.ds(go_, size), :], sem.at[slot])
        return make

    refs = (run_len_ref, local_ref, glob_ref, rows_ref)
    for u in range(MOE_SUB):
        _run_copies(j * MOE_SUB + u, *refs, copy_from(par * MOE_SUB + u), True)

    @pl.when(j > 0)
    def _():
        for u in range(MOE_SUB):
            _run_copies((j - 1) * MOE_SUB + u, *refs, copy_from((1 - par) * MOE_SUB + u), False)

    @pl.when(j == last)
    def _():
        for u in range(MOE_SUB):
            _run_copies(j * MOE_SUB + u, *refs, copy_from(par * MOE_SUB + u), False)


def _dispatch(plan, h2, onehot):
    step_tok = MOE_SUB * MOE_TOK
    grid_spec = pltpu.PrefetchScalarGridSpec(
        num_scalar_prefetch=4,
        grid=(N_TOK_TILES // MOE_SUB,),
        in_specs=[
            pl.BlockSpec((step_tok, D_MODEL), lambda j, *_: (j, 0)),
            pl.BlockSpec((N_EXPERTS, step_tok), lambda j, *_: (0, j)),
            pl.BlockSpec((MOE_SUB, N_EXPERTS, 1), lambda j, *_: (j, 0, 0)),
        ],
        out_specs=pl.BlockSpec(memory_space=pl.ANY),
        scratch_shapes=[pltpu.VMEM((2 * MOE_SUB, MOE_LB, D_MODEL), BF16),
                        pltpu.VMEM((MOE_TOK, MOE_TOK), BF16),
                        pltpu.SemaphoreType.DMA((2 * MOE_SUB,))],
    )
    return pl.pallas_call(
        _dispatch_kernel,
        grid_spec=grid_spec,
        out_shape=jax.ShapeDtypeStruct((ROWS_MAX, D_MODEL), BF16),
        compiler_params=_cparams(("arbitrary",)),
        name="dispatch",
    )(plan["run_len"], plan["local"], plan["glob"], plan["rows"], h2, onehot, plan["local_col"])


def _experts_kernel(texp_ref, nact_ref, xs_ref, wg_ref, wu_ref, wd_ref, ys_ref,
                    wg_b, wu_b, wd_b, wg_f, wu_f, wd_f, sem, *, layer):
    i = pl.program_id(0)
    active = i < nact_ref[0]
    e = texp_ref[i]
    new_expert = jnp.logical_or(i == 0, e != texp_ref[jnp.maximum(i - 1, 0)])

    def fetch(expert, slot):
        return [pltpu.make_async_copy(src.at[layer, expert], dst.at[slot], sem.at[slot, n])
                for n, (src, dst) in enumerate(((wg_ref, wg_f), (wu_ref, wu_f), (wd_ref, wd_f)))]

    @pl.when(i == 0)
    def _():
        for cp in fetch(e, e % 2):
            cp.start()

    @pl.when(jnp.logical_and(active, new_expert))
    def _():
        slot = e % 2
        for cp in fetch(e, slot):
            cp.wait()

        @pl.when(e + 1 < N_EXPERTS)
        def _():
            for cp in fetch(e + 1, 1 - slot):
                cp.start()

        wg_b[...] = wg_f[slot].astype(BF16)
        wu_b[...] = wu_f[slot].astype(BF16)
        wd_b[...] = wd_f[slot].astype(BF16)

    @pl.when(active)
    def _():
        x = xs_ref[...]
        hg = jnp.dot(x, wg_b[...], preferred_element_type=F32)
        hu = jnp.dot(x, wu_b[...], preferred_element_type=F32)
        act = (jax.nn.silu(hg) * hu).astype(BF16)
        ys_ref[...] = jnp.dot(act, wd_b[...], preferred_element_type=F32).astype(BF16)


def _experts(plan, xs, layer, wg, wu, wd):
    def rows(i, texp, nact):
        return (jnp.minimum(i, nact[0] - 1), 0)

    anywhere = pl.BlockSpec(memory_space=pl.ANY)
    grid_spec = pltpu.PrefetchScalarGridSpec(
        num_scalar_prefetch=2,
        grid=(ROWS_MAX // FFN_TM,),
        in_specs=[pl.BlockSpec((FFN_TM, D_MODEL), rows), anywhere, anywhere, anywhere],
        out_specs=pl.BlockSpec((FFN_TM, D_MODEL), rows),
        scratch_shapes=[pltpu.VMEM((D_MODEL, D_EXPERT), BF16), pltpu.VMEM((D_MODEL, D_EXPERT), BF16),
                        pltpu.VMEM((D_EXPERT, D_MODEL), BF16),
                        pltpu.VMEM((2, D_MODEL, D_EXPERT), F32), pltpu.VMEM((2, D_MODEL, D_EXPERT), F32),
                        pltpu.VMEM((2, D_EXPERT, D_MODEL), F32),
                        pltpu.SemaphoreType.DMA((2, 3))],
    )
    return pl.pallas_call(
        functools.partial(_experts_kernel, layer=layer),
        grid_spec=grid_spec,
        out_shape=jax.ShapeDtypeStruct((ROWS_MAX, D_MODEL), BF16),
        compiler_params=_cparams(("arbitrary",)),
        name="experts",
    )(plan["texp"], plan["n_active"], xs, wg, wu, wd)


def _combine_kernel(run_len_ref, local_ref, glob_ref, rows_ref, oh_ref, g_ref, lcol_ref, x_ref, mod_ref,
                    fg_ref,
                    ys_ref, o_ref, ybuf, before_sc, sem, *, final_norm):
    j = pl.program_id(0)
    n_steps = pl.num_programs(0)
    par = j % 2
    refs = (run_len_ref, local_ref, glob_ref, rows_ref)
    _fill_before(before_sc)

    def copy_to(slot):
        def make(lo_, go_, size):
            return pltpu.make_async_copy(ys_ref.at[pl.ds(go_, size), :],
                                         ybuf.at[slot, pl.ds(lo_, size), :], sem.at[slot])
        return make

    @pl.when(j == 0)
    def _():
        ybuf[...] = jnp.zeros(ybuf.shape, BF16)
        for u in range(MOE_SUB):
            _run_copies(u, *refs, copy_to(u), True)

    @pl.when(j + 1 < n_steps)
    def _():
        for u in range(MOE_SUB):
            _run_copies((j + 1) * MOE_SUB + u, *refs, copy_to((1 - par) * MOE_SUB + u), True)

    row = lax.broadcasted_iota(jnp.int32, (MOE_LB, MOE_TOK), 0).astype(F32)
    toks = [slice(u * MOE_TOK, (u + 1) * MOE_TOK) for u in range(MOE_SUB)]
    perms = []
    for u, tok in enumerate(toks):
        pos, chosen, lo, hi = _sorted_rows(oh_ref[:, tok], lcol_ref[u], before_sc[...])
        gate = jnp.where(chosen, g_ref[:, tok], 0.0)
        g_lo = jnp.sum(jnp.where(pos == lo, gate, 0.0), axis=0, keepdims=True)
        g_hi = jnp.sum(jnp.where(pos == hi, gate, 0.0), axis=0, keepdims=True)
        perms.append((jnp.where(row == lo, g_lo, 0.0)
                      + jnp.where(row == hi, g_hi, 0.0)).astype(BF16))
    for u in range(MOE_SUB):
        _run_copies(j * MOE_SUB + u, *refs, copy_to(par * MOE_SUB + u), False)
    for u, tok in enumerate(toks):
        y = lax.dot_general(perms[u], ybuf[par * MOE_SUB + u], (((0,), (0,)), ((), ())),
                            preferred_element_type=F32)
        x = x_ref[tok, :] + mod_ref[5:6, :] * y
        if final_norm:
            x = x * lax.rsqrt(jnp.mean(x * x, axis=-1, keepdims=True) + EPS) * fg_ref[...]
        o_ref[tok, :] = x


N_FRONT_IN = 8


def _combine_front_kernel(*refs):
    n_in = 4 + 7
    comb_in, front_in = refs[:n_in], refs[n_in:n_in + N_FRONT_IN]
    o_ref, *front_out = refs[n_in + N_FRONT_IN:n_in + N_FRONT_IN + 5]
    scratch = refs[n_in + N_FRONT_IN + 5:]
    _combine_kernel(*comb_in, o_ref, *scratch, final_norm=False)
    _front_kernel(o_ref, *front_in, *front_out)


def _combine(plan, onehot, gates, ys, x, layer, mod, final_g, front_params=None):
    step_tok = MOE_SUB * MOE_TOK
    route = pl.BlockSpec((N_EXPERTS, step_tok), lambda j, *_: (0, j))
    tok = pl.BlockSpec((step_tok, D_MODEL), lambda j, *_: (j, 0))
    in_specs = [
        route, route,
        pl.BlockSpec((MOE_SUB, N_EXPERTS, 1), lambda j, *_: (j, 0, 0)),
        tok, _mod_of(layer, step_tok),
        pl.BlockSpec((1, D_MODEL), lambda j, *_: (0, 0)),
        pl.BlockSpec(memory_space=pl.ANY),
    ]
    args = [onehot, gates, plan["local_col"], x, mod, final_g, ys]
    out_specs = tok
    out_shape = jax.ShapeDtypeStruct((TOKENS, D_MODEL), F32)
    body = functools.partial(_combine_kernel, final_norm=True)
    if front_params is not None:
        assert step_tok == K1_TM and len(front_params) == N_FRONT_IN
        half = pl.BlockSpec((step_tok, A_WIDTH), lambda j, *_: (j, 0))
        in_specs += _front_specs(layer + 1, step_tok)
        args += list(front_params)
        out_specs = [tok, half, half, half, half]
        out_shape = [out_shape, jax.ShapeDtypeStruct((TOKENS, A_WIDTH), BF16)] + \
                    [jax.ShapeDtypeStruct((TOKENS, B_WIDTH), F32)] * 3
        body = _combine_front_kernel
    grid_spec = pltpu.PrefetchScalarGridSpec(
        num_scalar_prefetch=4,
        grid=(N_TOK_TILES // MOE_SUB,),
        in_specs=in_specs,
        out_specs=out_specs,
        scratch_shapes=[pltpu.VMEM((2 * MOE_SUB, MOE_LB, D_MODEL), BF16),
                        pltpu.VMEM((MOE_TOK, MOE_TOK), BF16),
                        pltpu.SemaphoreType.DMA((2 * MOE_SUB,))],
    )
    return pl.pallas_call(
        body,
        grid_spec=grid_spec,
        out_shape=out_shape,
        compiler_params=_cparams(("arbitrary",)),
        name="combine",
    )(plan["run_len"], plan["local"], plan["glob"], plan["rows"], *args)


def kernel(x, c, rel_bias, router_w, router_b, mod_w, mod_b, norm1_g, w_in, gmlp_ln_g, gmlp_ln_b,
           gmlp_ws, gmlp_bs, out_norm_a_g, out_norm_b_g, w_out, norm2_g, moe_w_gate, moe_w_up,
           moe_w_down, final_g):
    mod = _modulation(c, mod_w, mod_b).reshape(DEPTH, BATCH, N_MOD, D_MODEL)
    bias_tab = _bias_tables(rel_bias)
    rwt = router_w.T.astype(BF16)
    rb_col = router_b.reshape(N_EXPERTS, 1)
    xt = x.reshape(TOKENS, D_MODEL)
    front_params = (
        mod, norm1_g.reshape(DEPTH, 1, D_MODEL), w_in.astype(BF16),
        gmlp_ln_g.reshape(DEPTH, 1, A_WIDTH), gmlp_ln_b.reshape(DEPTH, 1, A_WIDTH),
        gmlp_ws.astype(BF16).reshape(DEPTH, A_GROUPS // 2, 2 * CHUNK, CHUNK),
        jnp.repeat(jnp.swapaxes(gmlp_bs, 1, 2), HEAD_DIM, axis=2),
        out_norm_a_g.reshape(DEPTH, 1, A_WIDTH))
    gb = out_norm_b_g.reshape(DEPTH, 1, B_WIDTH)
    wout_b = w_out.astype(BF16)
    n2g = norm2_g.reshape(DEPTH, 1, D_MODEL)
    fg = final_g.reshape(1, D_MODEL)

    out_a, q, k, v = _front(xt, 0, front_params)
    for l in range(DEPTH):
        out_b = _attention(q, k, v, bias_tab)
        xt, h2, gates, onehot, count = _mid(out_a, out_b, xt, l, mod, gb, wout_b, n2g, rwt, rb_col)
        plan = _moe_plan(count)
        xs = _dispatch(plan, h2, onehot)
        ys = _experts(plan, xs, l, moe_w_gate, moe_w_up, moe_w_down)
        if l + 1 < DEPTH:
            xt, out_a, q, k, v = _combine(plan, onehot, gates, ys, xt, l, mod, fg, front_params)
        else:
            xt = _combine(plan, onehot, gates, ys, xt, l, mod, fg)
    return xt.reshape(BATCH, SEQ, D_MODEL)
```

```python
import functools
import math

import numpy as np
import jax
import jax.numpy as jnp
from jax import lax
from jax.experimental import pallas as pl
from jax.experimental.pallas import tpu as pltpu

D_MODEL = 1024
BATCH = 8
SEQ = 2048
DEPTH = 2
TOKENS = BATCH * SEQ
HEAD_DIM = 64
A_WIDTH = 512
B_WIDTH = 512
A_GROUPS = 8
IN_COLS = 2 * A_WIDTH + 3 * B_WIDTH
CHUNK = 128
DILATED_CONFIGS = ((128, 1), (512, 4), (2048, 16))
ATTN_BLOCK = 128
REL_BUCKETS = 32
REL_MAX_EXACT = REL_BUCKETS // 2
REL_MAX_DISTANCE = 2048
N_EXPERTS = 16
N_EXPERT_GROUPS = 4
EXPERTS_PER_GROUP = 4
D_EXPERT = 512
N_MOD = 6
EPS = 1e-6
NEG_INF = -1e30

LANES = 128
HEAD_PAIR = 2 * HEAD_DIM
N_PAIRS = B_WIDTH // HEAD_PAIR
RESIDUES = max(d for _, d in DILATED_CONFIGS)
LOG2E = math.log2(math.e)

F32 = jnp.float32
BF16 = jnp.bfloat16

V7X_VMEM_BYTES = 64 * 1024 * 1024
VMEM_LIMIT = V7X_VMEM_BYTES * 7 // 8


def _cparams(sem):
    return pltpu.CompilerParams(dimension_semantics=sem, vmem_limit_bytes=VMEM_LIMIT)


def _gelu(x):
    return 0.5 * x * (1.0 + lax.erf(x * math.sqrt(0.5)))


MOD_TN = 1024


def _mod_kernel(c_ref, w_ref, b_ref, o_ref):
    ca = jax.nn.silu(c_ref[...])
    o_ref[...] = jnp.dot(ca.astype(BF16), w_ref[...].astype(BF16),
                         preferred_element_type=F32) + b_ref[...]


def _modulation(c, mod_w, mod_b):
    n_cols = N_MOD * D_MODEL
    return pl.pallas_call(
        _mod_kernel,
        grid=(DEPTH, n_cols // MOD_TN),
        in_specs=[
            pl.BlockSpec((BATCH, D_MODEL), lambda l, j: (0, 0)),
            pl.BlockSpec((None, D_MODEL, MOD_TN), lambda l, j: (l, 0, j)),
            pl.BlockSpec((None, 1, MOD_TN), lambda l, j: (l, 0, j)),
        ],
        out_specs=pl.BlockSpec((None, BATCH, MOD_TN), lambda l, j: (l, 0, j)),
        out_shape=jax.ShapeDtypeStruct((DEPTH, BATCH, n_cols), F32),
        compiler_params=_cparams(("arbitrary", "arbitrary")),
        name="modulation",
    )(c, mod_w, mod_b.reshape(DEPTH, 1, n_cols))


K1_TM = 1024


def _front_kernel(x_ref, mod_ref, n1g_ref, win_ref, lng_ref, lnb_ref, ws_ref, bs_ref, ga_ref,
                  a_ref, q_ref, k_ref, v_ref):
    x = x_ref[...]
    tm = x.shape[0]
    h = x * lax.rsqrt(jnp.mean(x * x, axis=-1, keepdims=True) + EPS) * n1g_ref[...]
    h = h * (1.0 + mod_ref[1:2, :]) + mod_ref[0:1, :]
    proj = jnp.dot(h.astype(BF16), win_ref[...], preferred_element_type=F32)

    q = proj[:, 2 * A_WIDTH:2 * A_WIDTH + B_WIDTH] * (HEAD_DIM ** -0.5 * LOG2E)
    q_ref[...] = q.astype(BF16)
    k_ref[...] = proj[:, 2 * A_WIDTH + B_WIDTH:2 * A_WIDTH + 2 * B_WIDTH].astype(BF16)
    v_ref[...] = proj[:, 2 * A_WIDTH + 2 * B_WIDTH:].astype(BF16)

    u = _gelu(proj[:, :A_WIDTH])
    va = _gelu(proj[:, A_WIDTH:2 * A_WIDTH])
    mu = jnp.mean(va, axis=-1, keepdims=True)
    vc = va - mu
    vln = vc * lax.rsqrt(jnp.mean(vc * vc, axis=-1, keepdims=True) + EPS)
    vln = (vln * lng_ref[...] + lnb_ref[...]).astype(BF16)

    row = lax.broadcasted_iota(jnp.int32, (2 * CHUNK, CHUNK), 0)
    col = lax.broadcasted_iota(jnp.int32, (2 * CHUNK, CHUNK), 1)
    causal = (row % CHUNK) >= col
    first_group = lax.broadcasted_iota(jnp.int32, (CHUNK, LANES), 1) < HEAD_DIM
    wmix = [jnp.where(causal, ws_ref[p], jnp.zeros((), BF16)) for p in range(A_GROUPS // 2)]
    for c in range(tm // CHUNK):
        rows = slice(c * CHUNK, (c + 1) * CHUNK)
        parts = []
        for p in range(A_GROUPS // 2):
            vp = vln[rows, p * LANES:(p + 1) * LANES]
            r = jnp.dot(wmix[p], vp, preferred_element_type=F32)
            parts.append(jnp.where(first_group, r[:CHUNK], r[CHUNK:]))
        s = jnp.concatenate(parts, axis=-1) + bs_ref[...]
        oa = u[rows] * s
        oa = oa * lax.rsqrt(jnp.mean(oa * oa, axis=-1, keepdims=True) + EPS) * ga_ref[...]
        a_ref[rows, :] = oa.astype(BF16)


def _of_layer(layer, *tail):
    return pl.BlockSpec((None, *tail), lambda *_: (layer,) + (0,) * len(tail))


def _mod_of(layer, tokens_per_step):
    steps_per_seq = SEQ // tokens_per_step
    return pl.BlockSpec((None, None, N_MOD, D_MODEL),
                        lambda i, *_: (layer, i // steps_per_seq, 0, 0))


def _front_specs(layer, tm):
    return [
        _mod_of(layer, tm),
        _of_layer(layer, 1, D_MODEL),
        _of_layer(layer, D_MODEL, IN_COLS),
        _of_layer(layer, 1, A_WIDTH), _of_layer(layer, 1, A_WIDTH),
        _of_layer(layer, A_GROUPS // 2, 2 * CHUNK, CHUNK),
        _of_layer(layer, CHUNK, A_WIDTH),
        _of_layer(layer, 1, A_WIDTH),
    ]


def _front(x, layer, front_params):
    tm = K1_TM
    half = pl.BlockSpec((tm, A_WIDTH), lambda i: (i, 0))
    return pl.pallas_call(
        _front_kernel,
        grid=(TOKENS // tm,),
        in_specs=[pl.BlockSpec((tm, D_MODEL), lambda i: (i, 0))] + _front_specs(layer, tm),
        out_specs=[half, half, half, half],
        out_shape=[jax.ShapeDtypeStruct((TOKENS, A_WIDTH), BF16)] +
                  [jax.ShapeDtypeStruct((TOKENS, B_WIDTH), BF16)] * 3,
        compiler_params=_cparams(("arbitrary",)),
        name="front",
    )(x, *front_params)


def _t5_bucket_np(dist):
    dist = np.maximum(dist, 0)
    ratio = np.log(np.maximum(dist, 1) / REL_MAX_EXACT) / np.log(REL_MAX_DISTANCE / REL_MAX_EXACT)
    large = REL_MAX_EXACT + np.floor(ratio * (REL_BUCKETS - REL_MAX_EXACT)).astype(np.int64)
    large = np.minimum(large, REL_BUCKETS - 1)
    return np.where(dist < REL_MAX_EXACT, dist, large).astype(np.int32)


def _bias_tables(rel_bias):
    blk = ATTN_BLOCK
    n_cfg = len(DILATED_CONFIGS)
    n_rel = 3 * blk
    rel = 2 * blk - 1 - np.arange(n_rel)
    bucket, valid, rows, cols = [], [], [], []
    for window, d in DILATED_CONFIGS:
        span = window // d
        bucket.append(_t5_bucket_np(np.clip(rel, 0, span) * d))
        valid.append((rel >= 0) & (rel <= span))
        sub = RESIDUES // d
        ln = blk // sub
        pos = np.arange(blk)
        rows.append(np.eye(blk, dtype=np.float32)[(pos % ln) * sub + pos // ln])
        cols.append(np.kron(np.eye(2, dtype=np.float32), rows[-1]))
    w = jnp.transpose(rel_bias.astype(F32)[np.stack(bucket)], (0, 2, 1)) * LOG2E
    w = jnp.where(np.stack(valid)[:, None, :], w, NEG_INF * LOG2E)
    flat = jnp.tile(w, (1, 1, blk))
    skew = flat[:, :, blk - 1:blk - 1 + blk * (n_rel - 1)].reshape(n_cfg, -1, blk, n_rel - 1)
    tab = skew[..., :2 * blk]
    return jnp.einsum('cpi,chik,cqk->chpq', np.stack(rows), tab, np.stack(cols),
                      precision=lax.Precision.HIGHEST)


ATTN_LEAD = 3
ATTN_LAG = 1


def _attn_kernel(q_ref, k_ref, v_ref, bias_ref, o_ref, qp, kp, vp, m_sc, l_sc, acc_sc):
    blk = ATTN_BLOCK
    res = RESIDUES
    per = SEQ // res
    sq = res * res
    seg = sq // res
    lane = lax.broadcasted_iota(jnp.int32, (blk, LANES), 1)
    head0 = lane < HEAD_DIM
    ones = jnp.ones((2 * blk, LANES), BF16)

    pa = lax.broadcasted_iota(jnp.int32, (sq, sq), 0)
    pb = lax.broadcasted_iota(jnp.int32, (sq, sq), 1)
    regroup = jnp.where(pb == res * (pa % res) + pa // res, 1.0, 0.0).astype(BF16)

    def residue_rows(g):
        return [slice(per * r + seg * g, per * r + seg * (g + 1)) for r in range(res)]

    def load(ref, slices):
        return jnp.concatenate([ref[s, :] for s in slices], axis=0) if len(slices) > 1 \
            else ref[slices[0], :]

    def store(ref, slices, val):
        ln = val.shape[0] // len(slices)
        for i, s in enumerate(slices):
            ref[s, :] = val[i * ln:(i + 1) * ln]

    for g in range(SEQ // sq):
        rows = slice(sq * g, sq * (g + 1))
        qkv = jnp.concatenate([q_ref[rows, :], k_ref[rows, :], v_ref[rows, :]], axis=1)
        qkv = jnp.dot(regroup, qkv.astype(BF16), preferred_element_type=F32)
        store(qp, residue_rows(g), qkv[:, :LANES])
        store(kp, residue_rows(g), qkv[:, LANES:2 * LANES])
        store(vp, residue_rows(g), qkv[:, 2 * LANES:])

    def merge_heads(t):
        return jnp.where(head0, t[:blk], t[blk:])

    def scores(q, kcat, bias):
        zero = jnp.zeros_like(q)
        qs = jnp.concatenate([jnp.where(head0, q, zero), jnp.where(head0, zero, q)],
                             axis=0).astype(BF16)
        return lax.dot_general(qs, kcat, (((1,), (1,)), ((), ())),
                               preferred_element_type=F32) + bias

    def softmax_stage(s):
        m = jnp.max(s, axis=-1, keepdims=True)
        return m, jnp.exp2(s - m).astype(BF16)

    def value_stage(m, p, vcat):
        nk = vcat.shape[0]
        pv = jnp.dot(p, jnp.concatenate([vcat, ones[:nk]], axis=1), preferred_element_type=F32)
        return (merge_heads(jnp.broadcast_to(m, (2 * blk, LANES))),
                merge_heads(pv[:, LANES:]), merge_heads(pv[:, :LANES]))

    def update(ci, slices, m_c, l_c, o_c):
        if ci > 0:
            m_r = load(m_sc, slices)
            m_n = jnp.maximum(m_r, m_c)
            a = jnp.exp2(m_r - m_n)
            b = jnp.exp2(m_c - m_n)
            l_c = a * load(l_sc, slices) + b * l_c
            o_c = a * load(acc_sc, slices) + b * o_c
            m_c = m_n
        store(m_sc, slices, m_c)
        store(l_sc, slices, l_c)
        store(acc_sc, slices, o_c)

    blocks = []
    for ci, (window, d) in enumerate(DILATED_CONFIGS):
        sub = res // d
        ln = blk // sub
        for r in range(d):
            for n in range(SEQ // d // blk):
                slices = [slice(per * (r + d * c) + ln * n, per * (r + d * c) + ln * (n + 1))
                          for c in range(sub)]
                blocks.append((ci, slices, n == 0))

    kv_prev = [None, None]

    def score_stage(ci, slices, first):
        k_cur = load(kp, slices).astype(BF16)
        v_cur = load(vp, slices).astype(BF16)
        if first:
            bias = jnp.concatenate([bias_ref[ci, 0, :, blk:], bias_ref[ci, 1, :, blk:]], axis=0)
            out = scores(load(qp, slices), k_cur, bias), v_cur
        else:
            bias = jnp.concatenate([bias_ref[ci, 0], bias_ref[ci, 1]], axis=0)
            out = (scores(load(qp, slices), jnp.concatenate([kv_prev[0], k_cur], axis=0), bias),
                   jnp.concatenate([kv_prev[1], v_cur], axis=0))
        kv_prev[0], kv_prev[1] = k_cur, v_cur
        return out

    lead, lag = ATTN_LEAD, ATTN_LAG
    ahead = [score_stage(*blocks[j]) for j in range(lead)]
    behind = []
    for i, (ci, slices, _) in enumerate(blocks):
        s, vcat = ahead.pop(0)
        if i + lead < len(blocks):
            ahead.append(score_stage(*blocks[i + lead]))
        m, p = softmax_stage(s)
        behind.append((ci, slices, m, p, vcat))
        if len(behind) > lag:
            done = behind.pop(0)
            update(done[0], done[1], *value_stage(*done[2:]))
    for done in behind:
        update(done[0], done[1], *value_stage(*done[2:]))

    for g in range(SEQ // sq):
        o = load(acc_sc, residue_rows(g)) / load(l_sc, residue_rows(g))
        hi = o.astype(BF16)
        lo = (o - hi.astype(F32)).astype(BF16)
        back = jnp.dot(regroup, jnp.concatenate([hi, lo], axis=1), preferred_element_type=F32)
        o_ref[sq * g:sq * (g + 1), :] = back[:, :LANES] + back[:, LANES:]


def _attention(q, k, v, bias_tab):
    n_cfg = len(DILATED_CONFIGS)
    blk = ATTN_BLOCK
    seq_spec = pl.BlockSpec((SEQ, HEAD_PAIR), lambda b, p: (b, p))
    return pl.pallas_call(
        _attn_kernel,
        grid=(BATCH, N_PAIRS),
        in_specs=[seq_spec, seq_spec, seq_spec,
                  pl.BlockSpec((n_cfg, 2, blk, 2 * blk), lambda b, p: (0, p, 0, 0))],
        out_specs=seq_spec,
        out_shape=jax.ShapeDtypeStruct((TOKENS, B_WIDTH), F32),
        scratch_shapes=[pltpu.VMEM((SEQ, HEAD_PAIR), F32)] * 6,
        compiler_params=_cparams(("arbitrary", "arbitrary")),
        name="attention",
    )(q, k, v, bias_tab)


K4_TM = 1024
K4_PART = 256


def _top2_sum(a, b, c, d):
    hi1, lo1 = jnp.maximum(a, b), jnp.minimum(a, b)
    hi2, lo2 = jnp.maximum(c, d), jnp.minimum(c, d)
    return jnp.maximum(hi1, hi2) + jnp.maximum(jnp.minimum(hi1, hi2), jnp.maximum(lo1, lo2))


def _route(logits_t, rb_col):
    m = jnp.max(logits_t, axis=0, keepdims=True)
    e = jnp.exp(logits_t - m)
    probs = e / jnp.sum(e, axis=0, keepdims=True)
    sel = probs + rb_col
    sel_rows = [sel[i:i + 1, :] for i in range(N_EXPERTS)]
    prob_rows = [probs[i:i + 1, :] for i in range(N_EXPERTS)]
    gsz = EXPERTS_PER_GROUP
    score = [_top2_sum(*sel_rows[g * gsz:(g + 1) * gsz]) for g in range(N_EXPERT_GROUPS)]
    chosen = []
    for g in range(N_EXPERT_GROUPS):
        best = None
        for g2 in range(N_EXPERT_GROUPS):
            if g2 == g:
                continue
            c = (score[g] > score[g2]) if g2 < g else (score[g] >= score[g2])
            best = c if best is None else jnp.logical_and(best, c)
        for i in range(gsz):
            ei = g * gsz + i
            rank = jnp.zeros_like(sel_rows[ei])
            for j in range(gsz):
                if j == i:
                    continue
                ej = g * gsz + j
                ahead = (sel_rows[ej] >= sel_rows[ei]) if j < i else (sel_rows[ej] > sel_rows[ei])
                rank = rank + jnp.where(ahead, 1.0, 0.0)
            chosen.append(jnp.logical_and(best, rank < float(2)))
    picked = [jnp.where(chosen[i], prob_rows[i], 0.0) for i in range(N_EXPERTS)]
    denom = picked[0]
    for i in range(1, N_EXPERTS):
        denom = denom + picked[i]
    gates = jnp.concatenate([pk / denom for pk in picked], axis=0)
    onehot = jnp.concatenate([jnp.where(ch, 1.0, 0.0) for ch in chosen], axis=0)
    return gates, onehot


def _mid_kernel(a_ref, ob_ref, x_ref, mod_ref, gb_ref, wout_ref, n2g_ref, rwt_ref, rb_ref,
                x1_ref, h2_ref, gates_ref, onehot_ref, count_ref):
    for s in range(x_ref.shape[0] // K4_PART):
        rows = slice(s * K4_PART, (s + 1) * K4_PART)
        ob = ob_ref[rows, :]
        bn = ob * lax.rsqrt(jnp.mean(ob * ob, axis=-1, keepdims=True) + EPS) * gb_ref[...]
        mixed = jnp.dot(a_ref[rows, :], wout_ref[:A_WIDTH, :], preferred_element_type=F32)
        mixed = mixed + jnp.dot(bn.astype(BF16), wout_ref[A_WIDTH:, :],
                                preferred_element_type=F32)
        x1 = x_ref[rows, :] + mod_ref[2:3, :] * mixed
        x1_ref[rows, :] = x1
        h2 = x1 * lax.rsqrt(jnp.mean(x1 * x1, axis=-1, keepdims=True) + EPS) * n2g_ref[...]
        h2 = (h2 * (1.0 + mod_ref[4:5, :]) + mod_ref[3:4, :]).astype(BF16)
        h2_ref[rows, :] = h2
        logits_t = lax.dot_general(rwt_ref[...], h2, (((1,), (1,)), ((), ())),
                                   preferred_element_type=F32)
        gates_t, onehot_t = _route(logits_t, rb_ref[...])
        gates_ref[:, rows] = gates_t
        onehot_ref[:, rows] = onehot_t
        for t in range(K4_PART // MOE_TOK):
            count_ref[s * (K4_PART // MOE_TOK) + t] = jnp.sum(
                onehot_t[:, t * MOE_TOK:(t + 1) * MOE_TOK], axis=1, keepdims=True)


def _mid(out_a, out_b, x, layer, mod, gb, wout, n2g, rwt, rb_col):
    tm = K4_TM
    tok = pl.BlockSpec((tm, D_MODEL), lambda i: (i, 0))
    half = pl.BlockSpec((tm, A_WIDTH), lambda i: (i, 0))
    route = pl.BlockSpec((N_EXPERTS, tm), lambda i: (0, i))
    return pl.pallas_call(
        _mid_kernel,
        grid=(TOKENS // tm,),
        in_specs=[
            half, half, tok,
            _mod_of(layer, tm),
            _of_layer(layer, 1, B_WIDTH),
            _of_layer(layer, D_MODEL, D_MODEL),
            _of_layer(layer, 1, D_MODEL),
            pl.BlockSpec((N_EXPERTS, D_MODEL), lambda i: (0, 0)),
            pl.BlockSpec((N_EXPERTS, 1), lambda i: (0, 0)),
        ],
        out_specs=[tok, tok, route, route,
                   pl.BlockSpec((tm // MOE_TOK, N_EXPERTS, 1), lambda i: (i, 0, 0))],
        out_shape=[jax.ShapeDtypeStruct((TOKENS, D_MODEL), F32),
                   jax.ShapeDtypeStruct((TOKENS, D_MODEL), BF16),
                   jax.ShapeDtypeStruct((N_EXPERTS, TOKENS), F32),
                   jax.ShapeDtypeStruct((N_EXPERTS, TOKENS), F32),
                   jax.ShapeDtypeStruct((TOKENS // MOE_TOK, N_EXPERTS, 1), F32)],
        compiler_params=_cparams(("arbitrary",)),
        name="mid",
    )(out_a, out_b, x, mod, gb, wout, n2g, rwt, rb_col)


MOE_TOK = 256
ROW_ALIGN = 16
MOE_SUB = 4
MOE_LB = 2 * MOE_TOK + N_EXPERTS * ROW_ALIGN
FFN_TM = 512
N_TOK_TILES = TOKENS // MOE_TOK
ROWS_MAX = -(-(2 * TOKENS + N_TOK_TILES * N_EXPERTS * ROW_ALIGN
               + N_EXPERTS * (FFN_TM - ROW_ALIGN)) // FFN_TM) * FFN_TM


N_FFN_TILES = ROWS_MAX // FFN_TM


def _plan_kernel(count_ref, run_len_ref, local_ref, glob_ref, rows_ref, texp_ref, nact_ref):
    def prefix(a, b):
        return jnp.dot(a, b, preferred_element_type=F32, precision=lax.Precision.HIGHEST)

    row = lax.broadcasted_iota(jnp.int32, (LANES, LANES), 0)
    col = lax.broadcasted_iota(jnp.int32, (LANES, LANES), 1)
    real = jnp.logical_and(row < N_TOK_TILES, col < N_EXPERTS)
    run_len = jnp.where(real, jnp.maximum(jnp.ceil(count_ref[...] / ROW_ALIGN), 1.0) * ROW_ALIGN, 0.0)
    earlier_lane = jnp.where(row < col, 1.0, 0.0)
    earlier_row = jnp.where(col < row, 1.0, 0.0)
    local = prefix(run_len, earlier_lane)
    seg = jnp.ceil(jnp.sum(run_len, axis=0, keepdims=True) / FFN_TM) * FFN_TM
    seg_start = prefix(jnp.broadcast_to(seg, (LANES, LANES)), earlier_lane)[0:1]
    glob = seg_start + prefix(earlier_row, run_len)
    n_active = jnp.sum(seg, axis=1, keepdims=True) / FFN_TM
    tile = row[:, 0:1].astype(F32)
    ended = jnp.logical_and(tile * FFN_TM >= seg_start + seg, col < N_EXPERTS)
    texp = jnp.sum(jnp.where(ended, 1.0, 0.0), axis=1, keepdims=True)
    texp = jnp.minimum(texp, float(N_EXPERTS - 1))
    last = jnp.sum(jnp.where(tile == n_active - 1.0, texp, 0.0), axis=0, keepdims=True)
    texp = jnp.where(tile < n_active, texp, last)
    run_len_ref[...] = run_len.astype(jnp.int32)
    local_ref[...] = local.astype(jnp.int32)
    glob_ref[...] = glob.astype(jnp.int32)
    rows_ref[...] = jnp.broadcast_to(jnp.sum(run_len, axis=1, keepdims=True),
                                     (LANES, LANES)).astype(jnp.int32)
    texp_ref[...] = jnp.broadcast_to(texp, (LANES, LANES)).astype(jnp.int32)
    nact_ref[...] = jnp.broadcast_to(n_active, (8, LANES)).astype(jnp.int32)


def _moe_plan(count):
    assert max(N_TOK_TILES, N_FFN_TILES, N_EXPERTS) <= LANES
    count = count.reshape(N_TOK_TILES, N_EXPERTS)
    count = jnp.pad(count, ((0, LANES - N_TOK_TILES), (0, LANES - N_EXPERTS)))
    full = jax.ShapeDtypeStruct((LANES, LANES), jnp.int32)
    run_len, local, glob, rows, texp, nact = pl.pallas_call(
        _plan_kernel,
        out_shape=[full, full, full, full, full, jax.ShapeDtypeStruct((8, LANES), jnp.int32)],
        name="plan",
    )(count)

    def runs(t):
        return t[:N_TOK_TILES, :N_EXPERTS]

    return dict(run_len=runs(run_len).reshape(-1), local=runs(local).reshape(-1),
                glob=runs(glob).reshape(-1), rows=rows[:N_TOK_TILES, 0],
                local_col=runs(local).astype(F32).reshape(N_TOK_TILES, N_EXPERTS, 1),
                texp=texp[:N_FFN_TILES, 0], n_active=nact[0, :1])


def _run_copies(tile, run_len_ref, local_ref, glob_ref, rows_ref, make_copy, start):
    if not start:
        make_copy(0, 0, pl.multiple_of(rows_ref[tile], ROW_ALIGN)).wait()
        return
    for e in range(N_EXPERTS):
        idx = tile * N_EXPERTS + e
        make_copy(pl.multiple_of(local_ref[idx], ROW_ALIGN),
                  pl.multiple_of(glob_ref[idx], ROW_ALIGN),
                  pl.multiple_of(run_len_ref[idx], ROW_ALIGN)).start()


def _fill_before(before_sc):
    @pl.when(pl.program_id(0) == 0)
    def _():
        src = lax.broadcasted_iota(jnp.int32, before_sc.shape, 0)
        dst = lax.broadcasted_iota(jnp.int32, before_sc.shape, 1)
        before_sc[...] = jnp.where(src < dst, 1.0, 0.0).astype(BF16)


def _sorted_rows(onehot, local_col, before):
    rank = jnp.dot(onehot.astype(BF16), before, preferred_element_type=F32)
    pos = local_col + rank
    chosen = onehot > 0.5
    lo = jnp.min(jnp.where(chosen, pos, float(MOE_LB)), axis=0, keepdims=True)
    hi = jnp.max(jnp.where(chosen, pos, -1.0), axis=0, keepdims=True)
    return pos, chosen, lo, hi


def _dispatch_kernel(run_len_ref, local_ref, glob_ref, rows_ref, h_ref, oh_ref, lcol_ref, xs_ref,
                     lbuf, before_sc, sem):
    j = pl.program_id(0)
    last = pl.num_programs(0) - 1
    par = j % 2
    _fill_before(before_sc)
    row = lax.broadcasted_iota(jnp.int32, (MOE_LB, MOE_TOK), 0).astype(F32)
    toks = [slice(u * MOE_TOK, (u + 1) * MOE_TOK) for u in range(MOE_SUB)]
    spots = [_sorted_rows(oh_ref[:, tok], lcol_ref[u], before_sc[...])[2:]
             for u, tok in enumerate(toks)]
    perms = [(jnp.where(row == lo, 1.0, 0.0) + jnp.where(row == hi, 1.0, 0.0)).astype(BF16)
             for lo, hi in spots]
    for u, tok in enumerate(toks):
        lbuf[par * MOE_SUB + u] = jnp.dot(perms[u], h_ref[tok, :],
                                          preferred_element_type=F32).astype(BF16)

    def copy_from(slot):
        def make(lo_, go_, size):
            return pltpu.make_async_copy(lbuf.at[slot, pl.ds(lo_, size), :],
                                         xs_ref.at[pl.ds(go_, size), :], sem.at[slot])
        return make

    refs = (run_len_ref, local_ref, glob_ref, rows_ref)
    for u in range(MOE_SUB):
        _run_copies(j * MOE_SUB + u, *refs, copy_from(par * MOE_SUB + u), True)

    @pl.when(j > 0)
    def _():
        for u in range(MOE_SUB):
            _run_copies((j - 1) * MOE_SUB + u, *refs, copy_from((1 - par) * MOE_SUB + u), False)

    @pl.when(j == last)
    def _():
        for u in range(MOE_SUB):
            _run_copies(j * MOE_SUB + u, *refs, copy_from(par * MOE_SUB + u), False)


def _dispatch(plan, h2, onehot):
    step_tok = MOE_SUB * MOE_TOK
    grid_spec = pltpu.PrefetchScalarGridSpec(
        num_scalar_prefetch=4,
        grid=(N_TOK_TILES // MOE_SUB,),
        in_specs=[
            pl.BlockSpec((step_tok, D_MODEL), lambda j, *_: (j, 0)),
            pl.BlockSpec((N_EXPERTS, step_tok), lambda j, *_: (0, j)),
            pl.BlockSpec((MOE_SUB, N_EXPERTS, 1), lambda j, *_: (j, 0, 0)),
        ],
        out_specs=pl.BlockSpec(memory_space=pl.ANY),
        scratch_shapes=[pltpu.VMEM((2 * MOE_SUB, MOE_LB, D_MODEL), BF16),
                        pltpu.VMEM((MOE_TOK, MOE_TOK), BF16),
                        pltpu.SemaphoreType.DMA((2 * MOE_SUB,))],
    )
    return pl.pallas_call(
        _dispatch_kernel,
        grid_spec=grid_spec,
        out_shape=jax.ShapeDtypeStruct((ROWS_MAX, D_MODEL), BF16),
        compiler_params=_cparams(("arbitrary",)),
        name="dispatch",
    )(plan["run_len"], plan["local"], plan["glob"], plan["rows"], h2, onehot, plan["local_col"])


def _experts_kernel(texp_ref, nact_ref, xs_ref, wg_ref, wu_ref, wd_ref, ys_ref,
                    wg_b, wu_b, wd_b, wg_f, wu_f, wd_f, sem, *, layer):
    i = pl.program_id(0)
    active = i < nact_ref[0]
    e = texp_ref[i]
    new_expert = jnp.logical_or(i == 0, e != texp_ref[jnp.maximum(i - 1, 0)])

    def fetch(expert, slot):
        return [pltpu.make_async_copy(src.at[layer, expert], dst.at[slot], sem.at[slot, n])
                for n, (src, dst) in enumerate(((wg_ref, wg_f), (wu_ref, wu_f), (wd_ref, wd_f)))]

    @pl.when(i == 0)
    def _():
        for cp in fetch(e, e % 2):
            cp.start()

    @pl.when(jnp.logical_and(active, new_expert))
    def _():
        slot = e % 2
        for cp in fetch(e, slot):
            cp.wait()

        @pl.when(e + 1 < N_EXPERTS)
        def _():
            for cp in fetch(e + 1, 1 - slot):
                cp.start()

        wg_b[...] = wg_f[slot].astype(BF16)
        wu_b[...] = wu_f[slot].astype(BF16)
        wd_b[...] = wd_f[slot].astype(BF16)

    @pl.when(active)
    def _():
        x = xs_ref[...]
        hg = jnp.dot(x, wg_b[...], preferred_element_type=F32)
        hu = jnp.dot(x, wu_b[...], preferred_element_type=F32)
        act = (jax.nn.silu(hg) * hu).astype(BF16)
        ys_ref[...] = jnp.dot(act, wd_b[...], preferred_element_type=F32).astype(BF16)


def _experts(plan, xs, layer, wg, wu, wd):
    def rows(i, texp, nact):
        return (jnp.minimum(i, nact[0] - 1), 0)

    anywhere = pl.BlockSpec(memory_space=pl.ANY)
    grid_spec = pltpu.PrefetchScalarGridSpec(
        num_scalar_prefetch=2,
        grid=(ROWS_MAX // FFN_TM,),
        in_specs=[pl.BlockSpec((FFN_TM, D_MODEL), rows), anywhere, anywhere, anywhere],
        out_specs=pl.BlockSpec((FFN_TM, D_MODEL), rows),
        scratch_shapes=[pltpu.VMEM((D_MODEL, D_EXPERT), BF16), pltpu.VMEM((D_MODEL, D_EXPERT), BF16),
                        pltpu.VMEM((D_EXPERT, D_MODEL), BF16),
                        pltpu.VMEM((2, D_MODEL, D_EXPERT), F32), pltpu.VMEM((2, D_MODEL, D_EXPERT), F32),
                        pltpu.VMEM((2, D_EXPERT, D_MODEL), F32),
                        pltpu.SemaphoreType.DMA((2, 3))],
    )
    return pl.pallas_call(
        functools.partial(_experts_kernel, layer=layer),
        grid_spec=grid_spec,
        out_shape=jax.ShapeDtypeStruct((ROWS_MAX, D_MODEL), BF16),
        compiler_params=_cparams(("arbitrary",)),
        name="experts",
    )(plan["texp"], plan["n_active"], xs, wg, wu, wd)


def _combine_kernel(run_len_ref, local_ref, glob_ref, rows_ref, oh_ref, g_ref, lcol_ref, x_ref, mod_ref,
                    fg_ref,
                    ys_ref, o_ref, ybuf, before_sc, sem, *, final_norm):
    j = pl.program_id(0)
    n_steps = pl.num_programs(0)
    par = j % 2
    refs = (run_len_ref, local_ref, glob_ref, rows_ref)
    _fill_before(before_sc)

    def copy_to(slot):
        def make(lo_, go_, size):
            return pltpu.make_async_copy(ys_ref.at[pl.ds(go_, size), :],
                                         ybuf.at[slot, pl.ds(lo_, size), :], sem.at[slot])
        return make

    @pl.when(j == 0)
    def _():
        ybuf[...] = jnp.zeros(ybuf.shape, BF16)
        for u in range(MOE_SUB):
            _run_copies(u, *refs, copy_to(u), True)

    @pl.when(j + 1 < n_steps)
    def _():
        for u in range(MOE_SUB):
            _run_copies((j + 1) * MOE_SUB + u, *refs, copy_to((1 - par) * MOE_SUB + u), True)

    row = lax.broadcasted_iota(jnp.int32, (MOE_LB, MOE_TOK), 0).astype(F32)
    toks = [slice(u * MOE_TOK, (u + 1) * MOE_TOK) for u in range(MOE_SUB)]
    perms = []
    for u, tok in enumerate(toks):
        pos, chosen, lo, hi = _sorted_rows(oh_ref[:, tok], lcol_ref[u], before_sc[...])
        gate = jnp.where(chosen, g_ref[:, tok], 0.0)
        g_lo = jnp.sum(jnp.where(pos == lo, gate, 0.0), axis=0, keepdims=True)
        g_hi = jnp.sum(jnp.where(pos == hi, gate, 0.0), axis=0, keepdims=True)
        perms.append((jnp.where(row == lo, g_lo, 0.0)
                      + jnp.where(row == hi, g_hi, 0.0)).astype(BF16))
    for u in range(MOE_SUB):
        _run_copies(j * MOE_SUB + u, *refs, copy_to(par * MOE_SUB + u), False)
    for u, tok in enumerate(toks):
        y = lax.dot_general(perms[u], ybuf[par * MOE_SUB + u], (((0,), (0,)), ((), ())),
                            preferred_element_type=F32)
        x = x_ref[tok, :] + mod_ref[5:6, :] * y
        if final_norm:
            x = x * lax.rsqrt(jnp.mean(x * x, axis=-1, keepdims=True) + EPS) * fg_ref[...]
        o_ref[tok, :] = x


N_FRONT_IN = 8


def _combine_front_kernel(*refs):
    n_in = 4 + 7
    comb_in, front_in = refs[:n_in], refs[n_in:n_in + N_FRONT_IN]
    o_ref, *front_out = refs[n_in + N_FRONT_IN:n_in + N_FRONT_IN + 5]
    scratch = refs[n_in + N_FRONT_IN + 5:]
    _combine_kernel(*comb_in, o_ref, *scratch, final_norm=False)
    _front_kernel(o_ref, *front_in, *front_out)


def _combine(plan, onehot, gates, ys, x, layer, mod, final_g, front_params=None):
    step_tok = MOE_SUB * MOE_TOK
    route = pl.BlockSpec((N_EXPERTS, step_tok), lambda j, *_: (0, j))
    tok = pl.BlockSpec((step_tok, D_MODEL), lambda j, *_: (j, 0))
    in_specs = [
        route, route,
        pl.BlockSpec((MOE_SUB, N_EXPERTS, 1), lambda j, *_: (j, 0, 0)),
        tok, _mod_of(layer, step_tok),
        pl.BlockSpec((1, D_MODEL), lambda j, *_: (0, 0)),
        pl.BlockSpec(memory_space=pl.ANY),
    ]
    args = [onehot, gates, plan["local_col"], x, mod, final_g, ys]
    out_specs = tok
    out_shape = jax.ShapeDtypeStruct((TOKENS, D_MODEL), F32)
    body = functools.partial(_combine_kernel, final_norm=True)
    if front_params is not None:
        assert step_tok == K1_TM and len(front_params) == N_FRONT_IN
        half = pl.BlockSpec((step_tok, A_WIDTH), lambda j, *_: (j, 0))
        in_specs += _front_specs(layer + 1, step_tok)
        args += list(front_params)
        out_specs = [tok, half, half, half, half]
        out_shape = [out_shape, jax.ShapeDtypeStruct((TOKENS, A_WIDTH), BF16)] + \
                    [jax.ShapeDtypeStruct((TOKENS, B_WIDTH), BF16)] * 3
        body = _combine_front_kernel
    grid_spec = pltpu.PrefetchScalarGridSpec(
        num_scalar_prefetch=4,
        grid=(N_TOK_TILES // MOE_SUB,),
        in_specs=in_specs,
        out_specs=out_specs,
        scratch_shapes=[pltpu.VMEM((2 * MOE_SUB, MOE_LB, D_MODEL), BF16),
                        pltpu.VMEM((MOE_TOK, MOE_TOK), BF16),
                        pltpu.SemaphoreType.DMA((2 * MOE_SUB,))],
    )
    return pl.pallas_call(
        body,
        grid_spec=grid_spec,
        out_shape=out_shape,
        compiler_params=_cparams(("arbitrary",)),
        name="combine",
    )(plan["run_len"], plan["local"], plan["glob"], plan["rows"], *args)


def kernel(x, c, rel_bias, router_w, router_b, mod_w, mod_b, norm1_g, w_in, gmlp_ln_g, gmlp_ln_b,
           gmlp_ws, gmlp_bs, out_norm_a_g, out_norm_b_g, w_out, norm2_g, moe_w_gate, moe_w_up,
           moe_w_down, final_g):
    mod = _modulation(c, mod_w, mod_b).reshape(DEPTH, BATCH, N_MOD, D_MODEL)
    bias_tab = _bias_tables(rel_bias)
    rwt = router_w.T.astype(BF16)
    rb_col = router_b.reshape(N_EXPERTS, 1)
    xt = x.reshape(TOKENS, D_MODEL)
    front_params = (
        mod, norm1_g.reshape(DEPTH, 1, D_MODEL), w_in.astype(BF16),
        gmlp_ln_g.reshape(DEPTH, 1, A_WIDTH), gmlp_ln_b.reshape(DEPTH, 1, A_WIDTH),
        gmlp_ws.astype(BF16).reshape(DEPTH, A_GROUPS // 2, 2 * CHUNK, CHUNK),
        jnp.repeat(jnp.swapaxes(gmlp_bs, 1, 2), HEAD_DIM, axis=2),
        out_norm_a_g.reshape(DEPTH, 1, A_WIDTH))
    gb = out_norm_b_g.reshape(DEPTH, 1, B_WIDTH)
    wout_b = w_out.astype(BF16)
    n2g = norm2_g.reshape(DEPTH, 1, D_MODEL)
    fg = final_g.reshape(1, D_MODEL)

    out_a, q, k, v = _front(xt, 0, front_params)
    for l in range(DEPTH):
        out_b = _attention(q, k, v, bias_tab)
        xt, h2, gates, onehot, count = _mid(out_a, out_b, xt, l, mod, gb, wout_b, n2g, rwt, rb_col)
        plan = _moe_plan(count)
        xs = _dispatch(plan, h2, onehot)
        ys = _experts(plan, xs, l, moe_w_gate, moe_w_up, moe_w_down)
        if l + 1 < DEPTH:
            xt, out_a, q, k, v = _combine(plan, onehot, gates, ys, xt, l, mod, fg, front_params)
        else:
            xt = _combine(plan, onehot, gates, ys, xt, l, mod, fg)
    return xt.reshape(BATCH, SEQ, D_MODEL)
```
